```python
import math
import jax
import jax.numpy as jnp
from jax import lax
import numpy as np

D_MODEL = 2048
BATCH = 2
SEQ = 8192
DEPTH = 2

N_META = 16
CHUNK = 64
CONV_K = 4
MIX_WIDTH = D_MODEL

S5_WIDTH = D_MODEL // 2
S5_GROUP = 16
S5_GROUPS = S5_WIDTH // S5_GROUP
S5_STATE = 64

GLA_HEADS = 4
GLA_DV = (D_MODEL // 2) // GLA_HEADS
GLA_DK = GLA_DV // 2
GLA_RANK = 16
GLA_TAU = 16.0

GDN_HEADS = 8
GDN_DK = (D_MODEL // 2) // GDN_HEADS
GDN_DV = GDN_DK

LRU_WIDTH = D_MODEL // 2
LRU_BLOCKS = 8
LRU_BW = LRU_WIDTH // LRU_BLOCKS
LRU_C = 8.0

MOE_GROUPS = 4
MOE_EPG = 8
MOE_TOPK = 2
MOE_FF = D_MODEL // 8

AB_SPLITS = (S5_WIDTH, GLA_HEADS * GLA_DK, GLA_HEADS * GLA_DK, GLA_HEADS * GLA_DV, GLA_RANK, GLA_HEADS * GLA_DV)
AB_IN = sum(AB_SPLITS)
CD_SPLITS = (2 * GDN_HEADS * GDN_DK + GDN_HEADS * GDN_DV, GDN_HEADS * GDN_DV, GDN_HEADS, GDN_HEADS, LRU_WIDTH, LRU_WIDTH)
CD_IN = sum(CD_SPLITS)

N_AB = (DEPTH + 1) // 2
N_CD = DEPTH // 2
DN_ALPHA = (2.0 * DEPTH) ** 0.25
DN_BETA = (8.0 * DEPTH) ** -0.25
LN_EPS = 1e-5
NORM_EPS = 1e-6

kernel_name = 'hybrid_s5_gla_gdn_rglru_hmoe'


def split_cols(h, sizes):
    offs = [int(s) for s in np.cumsum(sizes)[:-1]]
    return jnp.split(h, offs, axis=-1)


def layer_norm(x, g, b):
    xf = x.astype(jnp.float32)
    mu = jnp.mean(xf, axis=-1, keepdims=True)
    xc = xf - mu
    var = jnp.mean(xc * xc, axis=-1, keepdims=True)
    y = xc * lax.rsqrt(var + LN_EPS) * g.astype(jnp.float32) + b.astype(jnp.float32)
    return y.astype(x.dtype)


def rms_norm_heads(o, g):
    return o * lax.rsqrt(jnp.mean(o * o, axis=-1, keepdims=True) + NORM_EPS) * g.astype(jnp.float32)


def l2_normalize(t):
    return t * lax.rsqrt(jnp.sum(t * t, axis=-1, keepdims=True) + NORM_EPS)


def causal_conv(x, w, b=None):
    L = x.shape[1]
    xp = jnp.pad(x, ((0, 0), (CONV_K - 1, 0), (0, 0)))
    y = xp[:, 0:L] * w[0]
    for j in range(1, CONV_K):
        y = y + xp[:, j:j + L] * w[j]
    if b is not None:
        y = y + b
    return y


def linear_combine(left, right):
    a_l, b_l = left
    a_r, b_r = right
    return a_r * a_l, a_r * b_l + b_r


def to_chunks(t):
    pad = CHUNK - N_META
    t = jnp.pad(t.astype(jnp.float32), [(0, 0), (pad, 0)] + [(0, 0)] * (t.ndim - 2))
    bsz, lp = t.shape[:2]
    t = t.reshape((bsz, lp // CHUNK, CHUNK) + t.shape[2:])
    return jnp.moveaxis(t, 3, 1)


def from_chunks(t):
    bsz, nh, nc, c = t.shape[:4]
    t = jnp.moveaxis(t, 1, 3).reshape((bsz, nc * c, nh) + t.shape[4:])
    return t[:, CHUNK - N_META:]


def s5_mixer(u, a_re, a_im, log_dt, b_re, b_im, c_re, c_im, d, w_glu, b_glu):
    f32 = jnp.float32
    bsz, L, _ = u.shape
    uf = u.astype(f32).reshape(bsz, L, S5_GROUPS, S5_GROUP)
    a = lax.complex(a_re.astype(f32), a_im.astype(f32))
    dt = jnp.exp(log_dt.astype(f32))[:, None]
    a_bar = jnp.exp(a * dt)
    zoh = (a_bar - 1.0) / a
    bu = lax.complex(jnp.einsum('blgc,gpc->blgp', uf, b_re.astype(f32)),
                     jnp.einsum('blgc,gpc->blgp', uf, b_im.astype(f32))) * zoh
    a_seq = jnp.broadcast_to(a_bar, bu.shape)
    _, state = lax.associative_scan(linear_combine, (a_seq, bu), axis=1)
    y = (jnp.einsum('blgp,gcp->blgc', state.real, c_re.astype(f32))
         - jnp.einsum('blgp,gcp->blgc', state.imag, c_im.astype(f32))
         + d.astype(f32).reshape(S5_GROUPS, S5_GROUP) * uf)
    z = jax.nn.gelu(y.reshape(bsz, L, S5_WIDTH))
    out = z * jax.nn.sigmoid(z @ w_glu.astype(f32) + b_glu.astype(f32))
    return out.astype(u.dtype)


def gla_chunked(q, k, v, g):
    q, k, v, g = to_chunks(q), to_chunks(k), to_chunks(v), to_chunks(g)
    bsz, nh, _, _, dk = q.shape
    dv = v.shape[-1]
    bcum = jnp.cumsum(g, axis=3)
    b_last = bcum[:, :, :, -1:, :]
    q_in = q * jnp.exp(bcum)
    k_in = k * jnp.exp(-bcum)
    k_st = k * jnp.exp(b_last - bcum)
    causal = jnp.tril(jnp.ones((CHUNK, CHUNK), dtype=bool))
    att = jnp.where(causal, jnp.einsum('bhncd,bhnsd->bhncs', q_in, k_in), 0.0)
    o_intra = jnp.einsum('bhncs,bhnsv->bhncv', att, v)

    def step(s, inp):
        q_n, k_n, v_n, dec_n = inp
        o_n = jnp.einsum('bhcd,bhdv->bhcv', q_n, s)
        s = s * dec_n[..., None] + jnp.einsum('bhcd,bhcv->bhdv', k_n, v_n)
        return s, o_n

    s0 = jnp.zeros((bsz, nh, dk, dv), jnp.float32)
    xs = (jnp.moveaxis(q_in, 2, 0), jnp.moveaxis(k_st, 2, 0), jnp.moveaxis(v, 2, 0),
          jnp.moveaxis(jnp.exp(b_last[:, :, :, 0]), 2, 0))
    _, o_inter = lax.scan(step, s0, xs)
    return from_chunks(o_intra + jnp.moveaxis(o_inter, 0, 2))


def gla_mixer(q, k, v, g_low, r, w_gate, b_gate, norm_g):
    bsz, L, _ = q.shape
    logf = jax.nn.log_sigmoid((g_low @ w_gate).astype(jnp.float32) + b_gate.astype(jnp.float32)) / GLA_TAU
    qh = q.astype(jnp.float32).reshape(bsz, L, GLA_HEADS, GLA_DK) * (GLA_DK ** -0.5)
    kh = k.reshape(bsz, L, GLA_HEADS, GLA_DK)
    vh = v.reshape(bsz, L, GLA_HEADS, GLA_DV)
    o = gla_chunked(qh, kh, vh, logf.reshape(bsz, L, GLA_HEADS, GLA_DK))
    o = rms_norm_heads(o, norm_g) * jax.nn.silu(r.astype(jnp.float32).reshape(bsz, L, GLA_HEADS, GLA_DV))
    return o.reshape(bsz, L, GLA_HEADS * GLA_DV).astype(q.dtype)


def gdn_chunked(q, k, v, beta, g):
    q, k, v, beta, g = to_chunks(q), to_chunks(k), to_chunks(v), to_chunks(beta), to_chunks(g)
    bsz, nh, _, _, dk = q.shape
    dv = v.shape[-1]
    gc = jnp.cumsum(g, axis=-1)
    tril = jnp.tril(jnp.ones((CHUNK, CHUNK), dtype=bool))
    strict = jnp.tril(jnp.ones((CHUNK, CHUNK), dtype=bool), -1)
    gam = jnp.exp(jnp.where(tril, gc[..., :, None] - gc[..., None, :], -jnp.inf))
    kb = k * beta[..., None]
    m = jnp.where(strict, jnp.einsum('bhncd,bhnsd->bhncs', kb, k) * gam, 0.0)
    eye = jnp.eye(CHUNK, dtype=jnp.float32)
    rhs = jnp.concatenate([v * beta[..., None], kb * jnp.exp(gc)[..., None]], axis=-1)
    sol = lax.linalg.triangular_solve(eye + m, rhs, left_side=True, lower=True, unit_diagonal=True)
    u_c, w_c = sol[..., :dv], sol[..., dv:]
    a_qk = jnp.einsum('bhncd,bhnsd->bhncs', q, k) * gam
    q_dec = q * jnp.exp(gc)[..., None]
    g_last = gc[..., -1]
    k_st = k * jnp.exp(g_last[..., None] - gc)[..., None]

    def step(s, inp):
        q_n, k_n, u_n, w_n, a_n, dec_n = inp
        v_new = u_n - jnp.einsum('bhcd,bhdv->bhcv', w_n, s)
        o_n = jnp.einsum('bhcd,bhdv->bhcv', q_n, s) + jnp.einsum('bhcs,bhsv->bhcv', a_n, v_new)
        s = s * dec_n[..., None, None] + jnp.einsum('bhcd,bhcv->bhdv', k_n, v_new)
        return s, o_n

    s0 = jnp.zeros((bsz, nh, dk, dv), jnp.float32)
    xs = tuple(jnp.moveaxis(t, 2, 0) for t in (q_dec, k_st, u_c, w_c, a_qk, jnp.exp(g_last)))
    _, o = lax.scan(step, s0, xs)
    return from_chunks(jnp.moveaxis(o, 0, 2))


def gated_deltanet(qkv, z, beta_logit, a_logit, conv_w, a_log, dt_bias, norm_g):
    f32 = jnp.float32
    bsz, L, _ = qkv.shape
    qkv = jax.nn.silu(causal_conv(qkv, conv_w).astype(f32))
    q, k, v = split_cols(qkv, (GDN_HEADS * GDN_DK, GDN_HEADS * GDN_DK, GDN_HEADS * GDN_DV))
    q = l2_normalize(q.reshape(bsz, L, GDN_HEADS, GDN_DK)) * (GDN_DK ** -0.5)
    k = l2_normalize(k.reshape(bsz, L, GDN_HEADS, GDN_DK))
    v = v.reshape(bsz, L, GDN_HEADS, GDN_DV)
    beta = jax.nn.sigmoid(beta_logit.astype(f32))
    g = -jnp.exp(a_log.astype(f32)) * jax.nn.softplus(a_logit.astype(f32) + dt_bias.astype(f32))
    o = gdn_chunked(q, k, v, beta, g)
    o = rms_norm_heads(o, norm_g) * jax.nn.silu(z.astype(f32).reshape(bsz, L, GDN_HEADS, GDN_DV))
    return o.reshape(bsz, L, GDN_HEADS * GDN_DV).astype(z.dtype)


def rglru_mixer(xb, gate, conv_w, conv_b, w_a, b_a, w_x, b_x, lam):
    f32 = jnp.float32
    bsz, L, _ = xb.shape
    xc = causal_conv(xb, conv_w, conv_b).astype(f32).reshape(bsz, L, LRU_BLOCKS, LRU_BW)
    r = jax.nn.sigmoid(jnp.einsum('blhi,hij->blhj', xc, w_a.astype(f32)) + b_a.astype(f32).reshape(LRU_BLOCKS, LRU_BW))
    i = jax.nn.sigmoid(jnp.einsum('blhi,hij->blhj', xc, w_x.astype(f32)) + b_x.astype(f32).reshape(LRU_BLOCKS, LRU_BW))
    log_a = -LRU_C * jax.nn.softplus(-lam.astype(f32)).reshape(LRU_BLOCKS, LRU_BW) * r
    a = jnp.exp(log_a)
    bx = jnp.sqrt(-jnp.expm1(2.0 * log_a)) * (i * xc)
    _, hs = lax.associative_scan(linear_combine, (a, bx), axis=1)
    y = hs.reshape(bsz, L, LRU_WIDTH) * jax.nn.gelu(gate.astype(f32))
    return y.astype(xb.dtype)


def mixer_ab(x, w_in, a_re, a_im, log_dt, b_re, b_im, c_re, c_im, d, w_glu, b_glu,
             gla_w_gate, gla_b_gate, gla_norm, w_out):
    h = x @ w_in
    u, q, k, v, g_low, r = split_cols(h, AB_SPLITS)
    y_s5 = s5_mixer(u, a_re, a_im, log_dt, b_re, b_im, c_re, c_im, d, w_glu, b_glu)
    y_gla = gla_mixer(q, k, v, g_low, r, gla_w_gate, gla_b_gate, gla_norm)
    return jnp.concatenate([y_s5, y_gla], axis=-1) @ w_out


def mixer_cd(x, w_in, conv_w, a_log, dt_bias, gdn_norm, lru_conv_w, lru_conv_b,
             lru_w_a, lru_b_a, lru_w_x, lru_b_x, lru_lambda, w_out):
    h = x @ w_in
    qkv, z, beta_logit, a_logit, lru_x, lru_gate = split_cols(h, CD_SPLITS)
    y_gdn = gated_deltanet(qkv, z, beta_logit, a_logit, conv_w, a_log, dt_bias, gdn_norm)
    y_lru = rglru_mixer(lru_x, lru_gate, lru_conv_w, lru_conv_b, lru_w_a, lru_b_a, lru_w_x, lru_b_x, lru_lambda)
    return jnp.concatenate([y_gdn, y_lru], axis=-1) @ w_out


def hier_moe(x, w_rg, b_rg, w_re, b_re, w_gate, w_up, w_down):
    f32 = jnp.float32
    bsz, L, dm = x.shape
    xt = x.reshape(bsz * L, dm)
    lg = (xt @ w_rg).astype(f32) + b_rg.astype(f32)
    pg = jax.nn.softmax(lg, axis=-1)
    p_top, g_top = lax.top_k(pg, 1)
    le = jnp.einsum('td,dge->tge', xt, w_re).astype(f32) + b_re.astype(f32)
    le_sel = jnp.take_along_axis(le, g_top[:, :, None], axis=1)[:, 0]
    v_top, e_top = lax.top_k(le_sel, MOE_TOPK)
    w_tok = p_top * jax.nn.softmax(v_top, axis=-1)
    comb_e = jnp.einsum('tk,tke->te', w_tok, jax.nn.one_hot(e_top, MOE_EPG, dtype=f32))
    comb = (jax.nn.one_hot(g_top[:, 0], MOE_GROUPS, dtype=f32)[:, :, None] * comb_e[:, None, :]).astype(x.dtype)
    out = jnp.zeros_like(xt)
    for gi in range(MOE_GROUPS):
        hg = jnp.einsum('td,edf->tef', xt, w_gate[gi])
        hu = jnp.einsum('td,edf->tef', xt, w_up[gi])
        hh = jax.nn.silu(hg) * hu * comb[:, gi, :, None]
        out = out + jnp.einsum('tef,efd->td', hh, w_down[gi])
    return out.reshape(bsz, L, dm)


def setup_inputs(seed: int = 0) -> dict:
    key = jax.random.key(seed)
    ks = iter(jax.random.split(key, 64))
    f32 = jnp.float32

    def nrm(shape, scale):
        return jax.random.normal(next(ks), shape, f32) * scale

    def unif(shape, lo, hi):
        return jax.random.uniform(next(ks), shape, f32, lo, hi)

    x = nrm((BATCH, SEQ, D_MODEL), 1.0)
    meta_tokens = nrm((N_META, D_MODEL), 1.0)
    ab_w_in = nrm((N_AB, D_MODEL, AB_IN), D_MODEL ** -0.5)
    ab_s5_a_re = -0.5 * jnp.exp(nrm((N_AB, S5_GROUPS, S5_STATE), 0.05))
    ab_s5_a_im = math.pi * jnp.arange(S5_STATE, dtype=f32) + nrm((N_AB, S5_GROUPS, S5_STATE), 0.01)
    ab_s5_log_dt = unif((N_AB, S5_GROUPS), math.log(1e-3), math.log(1e-1))
    ab_s5_b_re = nrm((N_AB, S5_GROUPS, S5_STATE, S5_GROUP), (2 * S5_GROUP) ** -0.5)
    ab_s5_b_im = nrm((N_AB, S5_GROUPS, S5_STATE, S5_GROUP), (2 * S5_GROUP) ** -0.5)
    ab_s5_c_re = nrm((N_AB, S5_GROUPS, S5_GROUP, S5_STATE), S5_STATE ** -0.5)
    ab_s5_c_im = nrm((N_AB, S5_GROUPS, S5_GROUP, S5_STATE), S5_STATE ** -0.5)
    ab_s5_d = nrm((N_AB, S5_WIDTH), 1.0)
    ab_s5_w_glu = nrm((N_AB, S5_WIDTH, S5_WIDTH), S5_WIDTH ** -0.5)
    ab_s5_b_glu = nrm((N_AB, S5_WIDTH), 0.01)
    ab_gla_w_gate = nrm((N_AB, GLA_RANK, GLA_HEADS * GLA_DK), GLA_RANK ** -0.5)
    ab_gla_b_gate = nrm((N_AB, GLA_HEADS * GLA_DK), 0.01)
    ab_gla_norm = 1.0 + nrm((N_AB, GLA_DV), 0.02)
    ab_w_out = nrm((N_AB, MIX_WIDTH, D_MODEL), (MIX_WIDTH ** -0.5) * DN_BETA)
    cd_w_in = nrm((N_CD, D_MODEL, CD_IN), D_MODEL ** -0.5)
    cd_conv_w = nrm((N_CD, CONV_K, CD_SPLITS[0]), CONV_K ** -0.5)
    cd_gdn_a_log = jnp.log(unif((N_CD, GDN_HEADS), 1.0, 16.0))
    dt0 = jnp.exp(unif((N_CD, GDN_HEADS), math.log(1e-3), math.log(1e-1)))
    cd_gdn_dt_bias = dt0 + jnp.log(-jnp.expm1(-dt0))
    cd_gdn_norm = 1.0 + nrm((N_CD, GDN_DV), 0.02)
    cd_lru_conv_w = nrm((N_CD, CONV_K, LRU_WIDTH), CONV_K ** -0.5)
    cd_lru_conv_b = nrm((N_CD, LRU_WIDTH), 0.01)
    cd_lru_w_a = nrm((N_CD, LRU_BLOCKS, LRU_BW, LRU_BW), LRU_BW ** -0.5)
    cd_lru_b_a = nrm((N_CD, LRU_WIDTH), 0.01)
    cd_lru_w_x = nrm((N_CD, LRU_BLOCKS, LRU_BW, LRU_BW), LRU_BW ** -0.5)
    cd_lru_b_x = nrm((N_CD, LRU_WIDTH), 0.01)
    a_c = unif((N_CD, LRU_WIDTH), 0.9, 0.999) ** (1.0 / LRU_C)
    cd_lru_lambda = jnp.log(a_c) - jnp.log1p(-a_c)
    cd_w_out = nrm((N_CD, MIX_WIDTH, D_MODEL), (MIX_WIDTH ** -0.5) * DN_BETA)
    moe_w_router_g = nrm((DEPTH, D_MODEL, MOE_GROUPS), D_MODEL ** -0.5)
    moe_b_router_g = nrm((DEPTH, MOE_GROUPS), 0.01)
    moe_w_router_e = nrm((DEPTH, D_MODEL, MOE_GROUPS, MOE_EPG), D_MODEL ** -0.5)
    moe_b_router_e = nrm((DEPTH, MOE_GROUPS, MOE_EPG), 0.01)
    moe_w_gate = nrm((DEPTH, MOE_GROUPS, MOE_EPG, D_MODEL, MOE_FF), D_MODEL ** -0.5)
    moe_w_up = nrm((DEPTH, MOE_GROUPS, MOE_EPG, D_MODEL, MOE_FF), D_MODEL ** -0.5)
    moe_w_down = nrm((DEPTH, MOE_GROUPS, MOE_EPG, MOE_FF, D_MODEL), (MOE_FF ** -0.5) * DN_BETA)
    ln_mix_g = 1.0 + nrm((DEPTH, D_MODEL), 0.02)
    ln_mix_b = nrm((DEPTH, D_MODEL), 0.01)
    ln_ffn_g = 1.0 + nrm((DEPTH, D_MODEL), 0.02)
    ln_ffn_b = nrm((DEPTH, D_MODEL), 0.01)
    return {'x': x, 'meta_tokens': meta_tokens, 'ab_w_in': ab_w_in,
            'ab_s5_a_re': ab_s5_a_re, 'ab_s5_a_im': ab_s5_a_im, 'ab_s5_log_dt': ab_s5_log_dt,
            'ab_s5_b_re': ab_s5_b_re, 'ab_s5_b_im': ab_s5_b_im, 'ab_s5_c_re': ab_s5_c_re,
            'ab_s5_c_im': ab_s5_c_im, 'ab_s5_d': ab_s5_d, 'ab_s5_w_glu': ab_s5_w_glu,
            'ab_s5_b_glu': ab_s5_b_glu, 'ab_gla_w_gate': ab_gla_w_gate, 'ab_gla_b_gate': ab_gla_b_gate,
            'ab_gla_norm': ab_gla_norm, 'ab_w_out': ab_w_out, 'cd_w_in': cd_w_in,
            'cd_conv_w': cd_conv_w, 'cd_gdn_a_log': cd_gdn_a_log, 'cd_gdn_dt_bias': cd_gdn_dt_bias,
            'cd_gdn_norm': cd_gdn_norm, 'cd_lru_conv_w': cd_lru_conv_w, 'cd_lru_conv_b': cd_lru_conv_b,
            'cd_lru_w_a': cd_lru_w_a, 'cd_lru_b_a': cd_lru_b_a, 'cd_lru_w_x': cd_lru_w_x,
            'cd_lru_b_x': cd_lru_b_x, 'cd_lru_lambda': cd_lru_lambda, 'cd_w_out': cd_w_out,
            'moe_w_router_g': moe_w_router_g, 'moe_b_router_g': moe_b_router_g,
            'moe_w_router_e': moe_w_router_e, 'moe_b_router_e': moe_b_router_e,
            'moe_w_gate': moe_w_gate, 'moe_w_up': moe_w_up, 'moe_w_down': moe_w_down,
            'ln_mix_g': ln_mix_g, 'ln_mix_b': ln_mix_b, 'ln_ffn_g': ln_ffn_g, 'ln_ffn_b': ln_ffn_b}


def reference(x, meta_tokens, ab_w_in, ab_s5_a_re, ab_s5_a_im, ab_s5_log_dt, ab_s5_b_re, ab_s5_b_im,
              ab_s5_c_re, ab_s5_c_im, ab_s5_d, ab_s5_w_glu, ab_s5_b_glu, ab_gla_w_gate, ab_gla_b_gate,
              ab_gla_norm, ab_w_out, cd_w_in, cd_conv_w, cd_gdn_a_log, cd_gdn_dt_bias, cd_gdn_norm,
              cd_lru_conv_w, cd_lru_conv_b, cd_lru_w_a, cd_lru_b_a, cd_lru_w_x, cd_lru_b_x, cd_lru_lambda,
              cd_w_out, moe_w_router_g, moe_b_router_g, moe_w_router_e, moe_b_router_e, moe_w_gate,
              moe_w_up, moe_w_down, ln_mix_g, ln_mix_b, ln_ffn_g, ln_ffn_b):
    bsz = x.shape[0]
    meta = jnp.broadcast_to(meta_tokens.astype(x.dtype)[None], (bsz, N_META, D_MODEL))
    h = jnp.concatenate([meta, x], axis=1)
    for layer in range(DEPTH):
        j = layer // 2
        if layer % 2 == 0:
            mix = mixer_ab(h, ab_w_in[j], ab_s5_a_re[j], ab_s5_a_im[j], ab_s5_log_dt[j], ab_s5_b_re[j],
                           ab_s5_b_im[j], ab_s5_c_re[j], ab_s5_c_im[j], ab_s5_d[j], ab_s5_w_glu[j],
                           ab_s5_b_glu[j], ab_gla_w_gate[j], ab_gla_b_gate[j], ab_gla_norm[j], ab_w_out[j])
        else:
            mix = mixer_cd(h, cd_w_in[j], cd_conv_w[j], cd_gdn_a_log[j], cd_gdn_dt_bias[j], cd_gdn_norm[j],
                           cd_lru_conv_w[j], cd_lru_conv_b[j], cd_lru_w_a[j], cd_lru_b_a[j], cd_lru_w_x[j],
                           cd_lru_b_x[j], cd_lru_lambda[j], cd_w_out[j])
        h = layer_norm(DN_ALPHA * h + mix, ln_mix_g[layer], ln_mix_b[layer])
        ffn = hier_moe(h, moe_w_router_g[layer], moe_b_router_g[layer], moe_w_router_e[layer],
                       moe_b_router_e[layer], moe_w_gate[layer], moe_w_up[layer], moe_w_down[layer])
        h = layer_norm(DN_ALPHA * h + ffn, ln_ffn_g[layer], ln_ffn_b[layer])
    return h[:, N_META:]
```

```python
import functools
import math

import jax
import jax.numpy as jnp
from jax import lax
from jax.experimental import pallas as pl
from jax.experimental.pallas import tpu as pltpu

F32 = jnp.float32
BF16 = jnp.bfloat16
HIGHEST = lax.Precision.HIGHEST

D_MODEL = 2048
N_META = 16
CONV_K = 4
DEPTH = 2
DN_ALPHA = (2.0 * DEPTH) ** 0.25
LN_EPS = 1e-5
NORM_EPS = 1e-6

S5_WIDTH = 1024
S5_GROUP = 16
S5_GROUPS = 64
S5_STATE = 64
S5_LC = 16
S5_SLABS = 8

GLA_HEADS = 4
GLA_DK = 128
GLA_DV = 256
GLA_RANK = 16
GLA_TAU = 16.0

GDN_HEADS = 8
GDN_DK = 128
GDN_DV = 128

LRU_WIDTH = 1024
LRU_BLOCKS = 8
LRU_BW = 128
LRU_C = 8.0

MOE_GROUPS = 4
MOE_EPG = 8
MOE_FF = 256

CHUNK = 64
ROW_ALIGN = 768
VMEM_LIMIT = 56 * 1024 * 1024

NT_DIMS = (((1,), (1,)), ((), ()))
TN_DIMS = (((0,), (0,)), ((), ()))


def _cparams(n_axes):
    return pltpu.CompilerParams(dimension_semantics=("arbitrary",) * n_axes,
                                vmem_limit_bytes=VMEM_LIMIT)


def _dot(a, b):
    return jnp.dot(a, b, preferred_element_type=F32)


def _dotg(a, b, dims):
    return lax.dot_general(a, b, dims, preferred_element_type=F32)


def _split(a):
    hi = a.astype(BF16)
    lo = (a - hi.astype(F32)).astype(BF16)
    return hi, lo


def _sigmoid(x):
    return 1.0 / (1.0 + jnp.exp(-x))


def _softplus(x):
    return jnp.maximum(x, 0.0) + jnp.log1p(jnp.exp(-jnp.abs(x)))


def _gelu(x):
    return 0.5 * x * (1.0 + jnp.tanh(math.sqrt(2.0 / math.pi) * (x + 0.044715 * (x * x * x))))


def _layer_norm(x, g, b):
    mu = jnp.mean(x, axis=-1, keepdims=True)
    xc = x - mu
    var = jnp.mean(xc * xc, axis=-1, keepdims=True)
    return xc * lax.rsqrt(var + LN_EPS) * g + b


def _mm_kernel(x_ref, w_ref, o_ref, xb_ref):
    @pl.when(pl.program_id(1) == 0)
    def _():
        xb_ref[...] = x_ref[...].astype(BF16)

    o_ref[...] = _dot(xb_ref[...], w_ref[...]).astype(o_ref.dtype)


def _matmul(x, w, tn, tm=768, out_dtype=F32):
    m, k = x.shape
    n = w.shape[1]
    return pl.pallas_call(
        _mm_kernel,
        grid=(m // tm, n // tn),
        in_specs=[pl.BlockSpec((tm, k), lambda i, j: (i, 0)),
                  pl.BlockSpec((k, tn), lambda i, j: (0, j))],
        out_specs=pl.BlockSpec((tm, tn), lambda i, j: (i, j)),
        out_shape=jax.ShapeDtypeStruct((m, n), out_dtype),
        scratch_shapes=[pltpu.VMEM((tm, k), BF16)],
        compiler_params=_cparams(2),
        name="in_proj",
    )(x, w)


S5_TILE_CHUNKS = 32


def _s5_uproj_kernel(hv_ref, w_ref, u_ref, lhs_ref):
    m = S5_TILE_CHUNKS
    for t in range(S5_LC):
        lhs_ref[t * m:(t + 1) * m, :] = hv_ref[:, t * D_MODEL:(t + 1) * D_MODEL].astype(BF16)
    z = _dot(lhs_ref[...], w_ref[...])
    for t in range(S5_LC):
        for s in range(S5_SLABS):
            u_ref[s, :, t * 128:(t + 1) * 128] = z[t * m:(t + 1) * m, s * 128:(s + 1) * 128]


def _s5_uproj(h, w_u):
    t_rows = h.shape[0]
    n_chunks = t_rows // S5_LC
    m = S5_TILE_CHUNKS
    hv = h.reshape(n_chunks, S5_LC * D_MODEL)
    return pl.pallas_call(
        _s5_uproj_kernel,
        grid=(n_chunks // m,),
        in_specs=[pl.BlockSpec((m, S5_LC * D_MODEL), lambda i: (i, 0)),
                  pl.BlockSpec((D_MODEL, S5_WIDTH), lambda i: (0, 0))],
        out_specs=pl.BlockSpec((S5_SLABS, m, S5_LC * 128), lambda i: (0, i, 0)),
        out_shape=jax.ShapeDtypeStruct((S5_SLABS, n_chunks, S5_LC * 128), F32),
        scratch_shapes=[pltpu.VMEM((S5_LC * m, D_MODEL), BF16)],
        compiler_params=_cparams(1),
        name="s5_uproj",
    )(hv, w_u)


def _s5_core_kernel(u_ref, stack_ref, ms_ref, mo_ref, ak_ref, pre_ref, pim_ref, d_ref, z_ref,
                    sre_ref, sim_ref, xre_ref, xim_ref):
    n = u_ref.shape[1]
    half = 8 * S5_STATE
    u = u_ref[0]
    ub = u.astype(BF16)
    s = _dot(ub, ms_ref[0])
    sre_ref[...] = s[:, :half]
    sim_ref[...] = s[:, half:]

    ak = ak_ref[0]
    pre = pre_ref[0]
    pim = pim_ref[0]
    row = lax.broadcasted_iota(jnp.int32, (8, half), 0)

    def body(rb, carry):
        cre, cim = carry
        r0 = pl.multiple_of(rb * 8, 8)
        xr = sre_ref[pl.ds(r0, 8), :]
        xi = sim_ref[pl.ds(r0, 8), :]
        for idx, k in enumerate((1, 2, 4)):
            are = ak[2 * idx:2 * idx + 1, :]
            aim = ak[2 * idx + 1:2 * idx + 2, :]
            keep = row >= k
            shr = jnp.where(keep, pltpu.roll(xr, k, 0), 0.0)
            shi = jnp.where(keep, pltpu.roll(xi, k, 0), 0.0)
            xr, xi = xr + are * shr - aim * shi, xi + are * shi + aim * shr
        xr = xr + pre * cre - pim * cim
        xi = xi + pre * cim + pim * cre
        first = row == 0
        xre_ref[pl.ds(r0, 8), :] = jnp.where(first, cre, pltpu.roll(xr, 1, 0))
        xim_ref[pl.ds(r0, 8), :] = jnp.where(first, cim, pltpu.roll(xi, 1, 0))
        return xr[7:8, :], xi[7:8, :]

    zero = jnp.zeros((1, half), F32)
    lax.fori_loop(0, n // 8, body, (zero, zero))

    xprev = jnp.concatenate([xre_ref[...], xim_ref[...]], axis=1).astype(BF16)
    y_inter = _dot(xprev, mo_ref[0])
    d = d_ref[0]
    npair = S5_LC // 2
    for q in range(npair):
        yq = _dot(ub[:, :(q + 1) * 256], stack_ref[0, (npair - 1 - q) * 256:, :])
        sl = slice(q * 256, (q + 1) * 256)
        y = yq + y_inter[:, sl] + d[:, sl] * u[:, sl]
        z_ref[0, :, sl] = _gelu(y)


def _s5_core(u, prm, batch):
    n_tot = u.shape[1]
    n = n_tot // batch
    w = S5_LC * 128
    half = 8 * S5_STATE
    slab = lambda s, b: (s, 0, 0)
    return pl.pallas_call(
        _s5_core_kernel,
        grid=(S5_SLABS, batch),
        in_specs=[pl.BlockSpec((1, n, w), lambda s, b: (s, b, 0)),
                  pl.BlockSpec((1, w, 256), slab),
                  pl.BlockSpec((1, w, 2 * half), slab),
                  pl.BlockSpec((1, 2 * half, w), slab),
                  pl.BlockSpec((1, 8, half), slab),
                  pl.BlockSpec((1, 8, half), slab),
                  pl.BlockSpec((1, 8, half), slab),
                  pl.BlockSpec((1, 1, w), slab)],
        out_specs=pl.BlockSpec((1, n, w), lambda s, b: (s, b, 0)),
        out_shape=jax.ShapeDtypeStruct(u.shape, F32),
        scratch_shapes=[pltpu.VMEM((n, half), F32)] * 4,
        compiler_params=_cparams(2),
        name="s5_core",
    )(u, prm["stack"], prm["ms"], prm["mo"], prm["ak"], prm["pre"], prm["pim"], prm["d"])


def _s5_out_kernel(z_ref, wglu_ref, bglu_ref, wout_ref, p_ref, zf_ref):
    m = S5_TILE_CHUNKS
    for t in range(S5_LC):
        for s in range(S5_SLABS):
            zf_ref[t * m:(t + 1) * m, s * 128:(s + 1) * 128] = z_ref[s, :, t * 128:(t + 1) * 128]
    zf = zf_ref[...]
    gate = _sigmoid(_dot(zf.astype(BF16), wglu_ref[...]) + bglu_ref[...])
    p = _dot((zf * gate).astype(BF16), wout_ref[...])
    for t in range(S5_LC):
        p_ref[:, t * D_MODEL:(t + 1) * D_MODEL] = p[t * m:(t + 1) * m, :]


def _s5_out(z, w_glu, b_glu, w_out_top):
    n_chunks = z.shape[1]
    m = S5_TILE_CHUNKS
    p = pl.pallas_call(
        _s5_out_kernel,
        grid=(n_chunks // m,),
        in_specs=[pl.BlockSpec((S5_SLABS, m, S5_LC * 128), lambda i: (0, i, 0)),
                  pl.BlockSpec((S5_WIDTH, S5_WIDTH), lambda i: (0, 0)),
                  pl.BlockSpec((1, S5_WIDTH), lambda i: (0, 0)),
                  pl.BlockSpec((S5_WIDTH, D_MODEL), lambda i: (0, 0))],
        out_specs=pl.BlockSpec((m, S5_LC * D_MODEL), lambda i: (i, 0)),
        out_shape=jax.ShapeDtypeStruct((n_chunks, S5_LC * D_MODEL), F32),
        scratch_shapes=[pltpu.VMEM((S5_LC * m, S5_WIDTH), F32)],
        compiler_params=_cparams(1),
        name="s5_out",
    )(z, w_glu, b_glu, w_out_top)
    return p.reshape(n_chunks * S5_LC, D_MODEL)


def _s5_params(a_re, a_im, log_dt, b_re, b_im, c_re, c_im, d):
    g, p, c, lc = S5_GROUPS, S5_STATE, S5_GROUP, S5_LC
    a = lax.complex(a_re.astype(F32), a_im.astype(F32))
    adt = a * jnp.exp(log_dt.astype(F32))[:, None]
    zoh = (jnp.exp(adt) - 1.0) / a
    bz = lax.complex(b_re.astype(F32), b_im.astype(F32)) * zoh[:, :, None]
    cc = lax.complex(c_re.astype(F32), c_im.astype(F32))
    jj = jnp.arange(lc + 1, dtype=F32)
    apow = jnp.exp(adt[None] * jj[:, None, None])
    eye8 = jnp.eye(8, dtype=F32)

    kre = jnp.einsum("gop,jgp,gpc->jgco", cc, apow[:lc], bz, precision=HIGHEST).real
    kg = kre.reshape(lc, S5_SLABS, 8, c, c)
    kblk = jnp.einsum("jsgco,gh->jsgcho", kg, eye8).reshape(lc, S5_SLABS, 128, 128)
    kpad = jnp.concatenate([jnp.zeros_like(kblk[:1]), kblk], axis=0)
    npair = lc // 2
    dd = jnp.arange(npair)[:, None, None]
    lo_in = jnp.arange(2)[None, :, None]
    lo_out = jnp.arange(2)[None, None, :]
    lag = 2 * dd + lo_out - lo_in
    tiles = kpad[lag + 1]
    tiles = tiles.transpose(3, 0, 1, 4, 2, 5).reshape(S5_SLABS, npair, 256, 256)
    stack = tiles[:, ::-1].reshape(S5_SLABS, npair * 256, 256)

    wst = apow[lc - 1 - jnp.arange(lc)][:, :, :, None] * bz[None]
    wst = wst.reshape(lc, S5_SLABS, 8, p, c)
    ms_c = jnp.einsum("tsgpc,gh->stgchp", wst, lax.complex(eye8, jnp.zeros_like(eye8)))
    ms_c = ms_c.reshape(S5_SLABS, lc * 128, 8 * p)
    ms = jnp.concatenate([ms_c.real, ms_c.imag], axis=2)

    zc = cc[None] * apow[1:lc + 1][:, :, None, :]
    zc = zc.reshape(lc, S5_SLABS, 8, c, p)
    mo_c = jnp.einsum("tsgop,gh->sgptho", zc, lax.complex(eye8, jnp.zeros_like(eye8)))
    mo_c = mo_c.reshape(S5_SLABS, 8 * p, lc * 128)
    mo = jnp.concatenate([mo_c.real, -mo_c.imag], axis=1)

    def slab_layout(x):
        return x.reshape(x.shape[:-2] + (S5_SLABS, 8 * p))

    kk = jnp.array([1.0, 2.0, 4.0], F32)
    a_k = slab_layout(jnp.exp(adt[None] * (lc * kk)[:, None, None]))
    ak = jnp.stack([a_k[0].real, a_k[0].imag, a_k[1].real, a_k[1].imag, a_k[2].real, a_k[2].imag,
                    jnp.zeros_like(a_k[0].real), jnp.zeros_like(a_k[0].real)], axis=1)
    rr = jnp.arange(1, 9, dtype=F32)
    a_r = slab_layout(jnp.exp(adt[None] * (lc * rr)[:, None, None]))
    pre = a_r.real.transpose(1, 0, 2)
    pim = a_r.imag.transpose(1, 0, 2)
    dsk = jnp.tile(d.astype(F32).reshape(S5_SLABS, 1, 128), (1, 1, lc))
    return {"stack": stack.astype(BF16), "ms": ms.astype(BF16), "mo": mo.astype(BF16),
            "ak": ak, "pre": pre, "pim": pim, "d": dsk}


GLA_TILE = 256


def _gla_kernel(q_ref, k_ref, v_ref, r_ref, gl_ref, wgh_ref, wgl_ref, bg_ref, ng_ref, o_ref, st_ref):
    @pl.when(pl.program_id(1) == 0)
    def _():
        st_ref[...] = jnp.zeros_like(st_ref)

    rowi = lax.broadcasted_iota(jnp.int32, (CHUNK, CHUNK), 0)
    coli = lax.broadcasted_iota(jnp.int32, (CHUNK, CHUNK), 1)
    tri = rowi >= coli
    tri_b = jnp.where(tri, 1.0, 0.0).astype(BF16)
    wgh = wgh_ref[...]
    wgl = wgl_ref[...]
    bg = bg_ref[...]
    ng = ng_ref[...]
    scale = GLA_DK ** -0.5

    def chunk(c, carry):
        r0 = pl.multiple_of(c * CHUNK, CHUNK)
        rows = pl.ds(r0, CHUNK)
        glh, gll = _split(gl_ref[rows, :])
        pre = _dot(glh, wgh) + _dot(gll, wgh) + _dot(glh, wgl) + bg
        logf = (jnp.minimum(pre, 0.0) - jnp.log1p(jnp.exp(-jnp.abs(pre)))) * (1.0 / GLA_TAU)
        lh, ll = _split(logf)
        bcum = _dot(tri_b, lh) + _dot(tri_b, ll)
        blast = bcum[CHUNK - 1:CHUNK, :]
        kk = k_ref[rows, :]
        q_in = q_ref[rows, :] * scale * jnp.exp(bcum)
        k_in = kk * jnp.exp(-bcum)
        k_st = kk * jnp.exp(blast - bcum)
        dec = jnp.exp(blast)
        for h in range(GLA_HEADS):
            sk = slice(h * GLA_DK, (h + 1) * GLA_DK)
            sv = slice(h * GLA_DV, (h + 1) * GLA_DV)
            qh = q_in[:, sk].astype(BF16)
            att = jnp.where(tri, _dotg(qh, k_in[:, sk].astype(BF16), NT_DIMS), 0.0)
            vh = v_ref[rows, sv].astype(BF16)
            st = st_ref[h]
            o = _dot(att.astype(BF16), vh) + _dotg(qh, st.astype(BF16), NT_DIMS)
            st_ref[h] = st * dec[:, sk] + _dotg(vh, k_st[:, sk].astype(BF16), TN_DIMS)
            on = o * lax.rsqrt(jnp.mean(o * o, axis=1, keepdims=True) + NORM_EPS) * ng
            rr = r_ref[rows, sv]
            o_ref[rows, sv] = (on * (rr * _sigmoid(rr))).astype(o_ref.dtype)
        return carry

    lax.fori_loop(0, GLA_TILE // CHUNK, chunk, 0)


def _gla(proj, wgh, wgl, bg, ng, batch, lp):
    t_rows = proj.shape[0]
    nt = lp // GLA_TILE
    hk = GLA_HEADS * GLA_DK
    hv = GLA_HEADS * GLA_DV
    rowmap = lambda col: (lambda b, t: (b * nt + t, col))
    const = lambda b, t: (0, 0)
    return pl.pallas_call(
        _gla_kernel,
        grid=(batch, nt),
        in_specs=[pl.BlockSpec((GLA_TILE, hk), rowmap(0)),
                  pl.BlockSpec((GLA_TILE, hk), rowmap(1)),
                  pl.BlockSpec((GLA_TILE, hv), rowmap(1)),
                  pl.BlockSpec((GLA_TILE, hv), rowmap(2)),
                  pl.BlockSpec((GLA_TILE, 128), rowmap(24)),
                  pl.BlockSpec((128, hk), const),
                  pl.BlockSpec((128, hk), const),
                  pl.BlockSpec((1, hk), const),
                  pl.BlockSpec((1, GLA_DV), const)],
        out_specs=pl.BlockSpec((GLA_TILE, hv), rowmap(0)),
        out_shape=jax.ShapeDtypeStruct((t_rows, hv), BF16),
        scratch_shapes=[pltpu.VMEM((GLA_HEADS, GLA_DV, GLA_DK), F32)],
        compiler_params=_cparams(2),
        name="gla",
    )(proj, proj, proj, proj, proj, wgh, wgl, bg, ng)


GDN_TILE = 256
GDN_QKV = 3 * GDN_HEADS * GDN_DK


def _gdn_kernel(qkv_ref, z_ref, ba_ref, cw_ref, hp_ref, ng_ref, o_ref,
                ext_ref, act_ref, beta_ref, g_ref, s_ref):
    tile = GDN_TILE

    @pl.when(pl.program_id(1) == 0)
    def _():
        ext_ref[0:8, :] = jnp.zeros((8, GDN_QKV), F32)
        s_ref[...] = jnp.zeros_like(s_ref)

    @pl.when(pl.program_id(1) > 0)
    def _():
        ext_ref[0:8, :] = ext_ref[tile:tile + 8, :]

    ext_ref[8:tile + 8, :] = qkv_ref[...]
    qscale = GDN_DK ** -0.5
    for cb in range(GDN_QKV // 128):
        cols = slice(cb * 128, (cb + 1) * 128)
        w = cw_ref[:, cols]
        y = (w[3:4] * ext_ref[8:tile + 8, cols] + w[2:3] * ext_ref[7:tile + 7, cols]
             + w[1:2] * ext_ref[6:tile + 6, cols] + w[0:1] * ext_ref[5:tile + 5, cols])
        a = y * _sigmoid(y)
        if cb < 2 * GDN_HEADS:
            a = a * lax.rsqrt(jnp.sum(a * a, axis=1, keepdims=True) + NORM_EPS)
            if cb < GDN_HEADS:
                a = a * qscale
        act_ref[:, cols] = a

    ba = ba_ref[...]
    beta_ref[...] = _sigmoid(ba)
    g_ref[...] = hp_ref[0:1, :] * _softplus(ba + hp_ref[1:2, :])

    rowi = lax.broadcasted_iota(jnp.int32, (CHUNK, CHUNK), 0)
    coli = lax.broadcasted_iota(jnp.int32, (CHUNK, CHUNK), 1)
    tri = rowi >= coli
    strict = rowi > coli
    tri_b = jnp.where(tri, 1.0, 0.0).astype(BF16)
    upper_b = jnp.where(rowi <= coli, 1.0, 0.0).astype(BF16)
    eye = jnp.where(rowi == coli, 1.0, 0.0)
    level_masks = []
    for lvl in range(6):
        bi = rowi >> lvl
        bj = coli >> lvl
        level_masks.append(((bi & 1) == 1) & (bj == bi - 1))
    ng = ng_ref[...]

    def chunk(c, carry):
        r0 = pl.multiple_of(c * CHUNK, CHUNK)
        rows = pl.ds(r0, CHUNK)
        gh, gl = _split(g_ref[rows, :])
        gc = _dot(tri_b, gh) + _dot(tri_b, gl)
        gct = _dotg(gh, upper_b, TN_DIMS) + _dotg(gl, upper_b, TN_DIMS)
        eg = jnp.exp(gc)
        elast = jnp.exp(gc[CHUNK - 1:CHUNK, :] - gc)
        dlast = jnp.exp(gc[CHUNK - 1:CHUNK, :])
        beta = beta_ref[rows, :]
        for h in range(GDN_HEADS):
            col = GDN_HEADS + h
            sq = slice(h * GDN_DK, (h + 1) * GDN_DK)
            sk = slice((GDN_HEADS + h) * GDN_DK, (GDN_HEADS + h + 1) * GDN_DK)
            sv = slice((2 * GDN_HEADS + h) * GDN_DK, (2 * GDN_HEADS + h + 1) * GDN_DK)
            qh = act_ref[rows, sq]
            kh = act_ref[rows, sk]
            vh = act_ref[rows, sv]
            bcol = beta[:, h:h + 1]
            diff = gc[:, col:col + 1] - gct[col:col + 1, :]
            gam = jnp.exp(jnp.where(tri, diff, -1e30))
            kb = kh * bcol
            khb = kh.astype(BF16)
            mm = jnp.where(strict, _dotg(kb.astype(BF16), khb, NT_DIMS) * gam, 0.0)
            tinv = eye - jnp.where(level_masks[0], mm, 0.0)
            for lvl in range(1, 6):
                ml = jnp.where(level_masks[lvl], mm, 0.0).astype(BF16)
                tb = tinv.astype(BF16)
                tinv = tinv - _dot(tb, _dot(ml, tb).astype(BF16))
            rhs = jnp.concatenate([vh * bcol, kb * eg[:, col:col + 1]], axis=1)
            sol = _dot(tinv.astype(BF16), rhs.astype(BF16))
            u_c = sol[:, :GDN_DV]
            w_c = sol[:, GDN_DV:]
            aqk = jnp.where(tri, _dotg(qh.astype(BF16), khb, NT_DIMS) * gam, 0.0)
            s = s_ref[h]
            sb = s.astype(BF16)
            v_new = u_c - _dot(w_c.astype(BF16), sb)
            vnb = v_new.astype(BF16)
            o = _dot((qh * eg[:, col:col + 1]).astype(BF16), sb) + _dot(aqk.astype(BF16), vnb)
            k_st = kh * elast[:, col:col + 1]
            s_ref[h] = s * dlast[:, col:col + 1] + _dotg(k_st.astype(BF16), vnb, TN_DIMS)
            on = o * lax.rsqrt(jnp.mean(o * o, axis=1, keepdims=True) + NORM_EPS) * ng
            so = slice(h * GDN_DV, (h + 1) * GDN_DV)
            zz = z_ref[rows, so]
            o_ref[rows, so] = (on * (zz * _sigmoid(zz))).astype(o_ref.dtype)
        return carry

    lax.fori_loop(0, tile // CHUNK, chunk, 0)


def _gdn(proj, cw, hp, ng, batch, lp):
    t_rows = proj.shape[0]
    nt = lp // GDN_TILE
    hv = GDN_HEADS * GDN_DV
    rowmap = lambda col: (lambda b, t: (b * nt + t, col))
    const = lambda b, t: (0, 0)
    return pl.pallas_call(
        _gdn_kernel,
        grid=(batch, nt),
        in_specs=[pl.BlockSpec((GDN_TILE, GDN_QKV), rowmap(0)),
                  pl.BlockSpec((GDN_TILE, hv), rowmap(3)),
                  pl.BlockSpec((GDN_TILE, 128), rowmap(48)),
                  pl.BlockSpec((8, GDN_QKV), const),
                  pl.BlockSpec((8, 128), const),
                  pl.BlockSpec((1, GDN_DV), const)],
        out_specs=pl.BlockSpec((GDN_TILE, hv), rowmap(0)),
        out_shape=jax.ShapeDtypeStruct((t_rows, hv), BF16),
        scratch_shapes=[pltpu.VMEM((GDN_TILE + 8, GDN_QKV), F32),
                        pltpu.VMEM((GDN_TILE, GDN_QKV), F32),
                        pltpu.VMEM((GDN_TILE, 128), F32),
                        pltpu.VMEM((GDN_TILE, 128), F32),
                        pltpu.VMEM((GDN_HEADS, GDN_DK, GDN_DV), F32)],
        compiler_params=_cparams(2),
        name="gdn",
    )(proj, proj, proj, cw, hp, ng)


LRU_TILE = 256


def _lru_kernel(x_ref, gate_ref, cw_ref, cb_ref, wax_ref, bax_ref, sp_ref, o_ref,
                ext_ref, a_ref, b_ref, h_ref):
    tile = LRU_TILE

    @pl.when(pl.program_id(1) == 0)
    def _():
        ext_ref[0:8, :] = jnp.zeros((8, LRU_WIDTH), F32)
        h_ref[...] = jnp.zeros_like(h_ref)

    @pl.when(pl.program_id(1) > 0)
    def _():
        ext_ref[0:8, :] = ext_ref[tile:tile + 8, :]

    ext_ref[8:tile + 8, :] = x_ref[...]
    for blk in range(LRU_BLOCKS):
        cols = slice(blk * LRU_BW, (blk + 1) * LRU_BW)
        w = cw_ref[:, cols]
        xc = (w[3:4] * ext_ref[8:tile + 8, cols] + w[2:3] * ext_ref[7:tile + 7, cols]
              + w[1:2] * ext_ref[6:tile + 6, cols] + w[0:1] * ext_ref[5:tile + 5, cols]
              + cb_ref[:, cols])
        pre = _dot(xc.astype(BF16), wax_ref[blk]) + bax_ref[blk]
        r = _sigmoid(pre[:, :LRU_BW])
        i = _sigmoid(pre[:, LRU_BW:])
        log_a = sp_ref[:, cols] * r
        a_ref[:, cols] = jnp.exp(log_a)
        th = jnp.tanh(log_a)
        b_ref[:, cols] = jnp.sqrt(-2.0 * th / (1.0 - th)) * (i * xc)

    row = lax.broadcasted_iota(jnp.int32, (8, LRU_WIDTH), 0)

    def body(rb, carry):
        r0 = pl.multiple_of(rb * 8, 8)
        aa = a_ref[pl.ds(r0, 8), :]
        bb = b_ref[pl.ds(r0, 8), :]
        for k in (1, 2, 4):
            keep = row >= k
            a_sh = jnp.where(keep, pltpu.roll(aa, k, 0), 1.0)
            b_sh = jnp.where(keep, pltpu.roll(bb, k, 0), 0.0)
            bb = aa * b_sh + bb
            aa = aa * a_sh
        hs = bb + aa * carry
        gt = gate_ref[pl.ds(r0, 8), :]
        o_ref[pl.ds(r0, 8), :] = (hs * _gelu(gt)).astype(o_ref.dtype)
        return hs[7:8, :]

    h_ref[...] = lax.fori_loop(0, tile // 8, body, h_ref[...])


def _lru(proj, cw, cb, wax, bax, sp, batch, lp):
    t_rows = proj.shape[0]
    nt = lp // LRU_TILE
    rowmap = lambda col: (lambda b, t: (b * nt + t, col))
    const = lambda b, t: (0, 0)
    return pl.pallas_call(
        _lru_kernel,
        grid=(batch, nt),
        in_specs=[pl.BlockSpec((LRU_TILE, LRU_WIDTH), rowmap(4)),
                  pl.BlockSpec((LRU_TILE, LRU_WIDTH), rowmap(5)),
                  pl.BlockSpec((8, LRU_WIDTH), const),
                  pl.BlockSpec((1, LRU_WIDTH), const),
                  pl.BlockSpec((LRU_BLOCKS, LRU_BW, 2 * LRU_BW), lambda b, t: (0, 0, 0)),
                  pl.BlockSpec((LRU_BLOCKS, 1, 2 * LRU_BW), lambda b, t: (0, 0, 0)),
                  pl.BlockSpec((1, LRU_WIDTH), const)],
        out_specs=pl.BlockSpec((LRU_TILE, LRU_WIDTH), rowmap(0)),
        out_shape=jax.ShapeDtypeStruct((t_rows, LRU_WIDTH), BF16),
        scratch_shapes=[pltpu.VMEM((LRU_TILE + 8, LRU_WIDTH), F32),
                        pltpu.VMEM((LRU_TILE, LRU_WIDTH), F32),
                        pltpu.VMEM((LRU_TILE, LRU_WIDTH), F32),
                        pltpu.VMEM((1, LRU_WIDTH), F32)],
        compiler_params=_cparams(2),
        name="lru",
    )(proj, proj, cw, cb, wax, bax, sp)


OUT_TILE = 256


def _out_ln_kernel(n_add, h_ref, *refs):
    adds = refs[:n_add]
    a1_ref, w1_ref, a2_ref, w2_ref, g_ref, b_ref, o_ref = refs[n_add:]
    acc = DN_ALPHA * h_ref[...]
    for r in adds:
        acc = acc + r[...]
    acc = acc + _dot(a1_ref[...], w1_ref[...])
    if a2_ref is not None:
        acc = acc + _dot(a2_ref[...], w2_ref[...])
    o_ref[...] = _layer_norm(acc, g_ref[...], b_ref[...])


def _out_ln_kernel_1(h_ref, p_ref, a1_ref, w1_ref, g_ref, b_ref, o_ref):
    _out_ln_kernel(1, h_ref, p_ref, a1_ref, w1_ref, None, None, g_ref, b_ref, o_ref)


def _out_ln_kernel_2(h_ref, a1_ref, w1_ref, a2_ref, w2_ref, g_ref, b_ref, o_ref):
    _out_ln_kernel(0, h_ref, a1_ref, w1_ref, a2_ref, w2_ref, g_ref, b_ref, o_ref)


def _out_ln(h, addend, pairs, g, b):
    t_rows = h.shape[0]
    tm = OUT_TILE
    row = lambda i: (i, 0)
    const = lambda i: (0, 0)
    ins = [h]
    specs = [pl.BlockSpec((tm, D_MODEL), row)]
    if addend is not None:
        ins.append(addend)
        specs.append(pl.BlockSpec((tm, D_MODEL), row))
    for a, w in pairs:
        ins += [a, w]
        specs += [pl.BlockSpec((tm, a.shape[1]), row), pl.BlockSpec(w.shape, const)]
    ins += [g, b]
    specs += [pl.BlockSpec((1, D_MODEL), const)] * 2
    body = _out_ln_kernel_1 if addend is not None else _out_ln_kernel_2
    return pl.pallas_call(
        body,
        grid=(t_rows // tm,),
        in_specs=specs,
        out_specs=pl.BlockSpec((tm, D_MODEL), row),
        out_shape=jax.ShapeDtypeStruct((t_rows, D_MODEL), F32),
        compiler_params=_cparams(1),
        name="out_ln",
    )(*ins)


ROUTER_TILE = 768
MOE_TILE = 512
ROW_W = D_MODEL + 128
BIG = 1e30


def _router_kernel(h_ref, wh_ref, wl_ref, b_ref, xr_ref, gid_ref):
    h = h_ref[...]
    hh, hl = _split(h)
    logits = _dot(hh, wh_ref[...]) + _dot(hl, wh_ref[...]) + _dot(hh, wl_ref[...]) + b_ref[...]
    lane = lax.broadcasted_iota(jnp.int32, logits.shape, 1).astype(F32)

    def first_argmax(vals, vmax):
        return jnp.min(jnp.where(vals == vmax, lane, 4096.0), axis=1, keepdims=True)

    is_g = lane < MOE_GROUPS
    lg = jnp.where(is_g, logits, -BIG)
    m = jnp.max(lg, axis=1, keepdims=True)
    gidx = first_argmax(lg, m)
    denom = jnp.sum(jnp.where(is_g, jnp.exp(logits - m), 0.0), axis=1, keepdims=True)
    p_top = 1.0 / denom
    base = MOE_GROUPS + MOE_EPG * gidx
    in_grp = (lane >= base) & (lane < base + MOE_EPG)
    le = jnp.where(in_grp, logits, -BIG)
    v1 = jnp.max(le, axis=1, keepdims=True)
    i1 = first_argmax(le, v1)
    le2 = jnp.where(lane == i1, -BIG, le)
    v2 = jnp.max(le2, axis=1, keepdims=True)
    i2 = first_argmax(le2, v2)
    e21 = jnp.exp(v2 - v1)
    w1 = p_top / (1.0 + e21)
    w2 = p_top * e21 / (1.0 + e21)
    comb = jnp.where(lane == i1 - base, w1, 0.0) + jnp.where(lane == i2 - base, w2, 0.0)
    xr_ref[:, :D_MODEL] = h
    xr_ref[:, D_MODEL:] = comb
    gid_ref[...] = jnp.broadcast_to(gidx, gid_ref.shape).astype(jnp.int32)


def _router(h, wh, wl, b):
    t_rows = h.shape[0]
    tm = ROUTER_TILE
    return pl.pallas_call(
        _router_kernel,
        grid=(t_rows // tm,),
        in_specs=[pl.BlockSpec((tm, D_MODEL), lambda i: (i, 0)),
                  pl.BlockSpec((D_MODEL, 128), lambda i: (0, 0)),
                  pl.BlockSpec((D_MODEL, 128), lambda i: (0, 0)),
                  pl.BlockSpec((1, 128), lambda i: (0, 0))],
        out_specs=[pl.BlockSpec((tm, ROW_W), lambda i: (i, 0)),
                   pl.BlockSpec((tm, 128), lambda i: (i, 0))],
        out_shape=[jax.ShapeDtypeStruct((t_rows, ROW_W), F32),
                   jax.ShapeDtypeStruct((t_rows, 128), jnp.int32)],
        compiler_params=_cparams(1),
        name="router",
    )(h, wh, wl, b)


DISPATCH_TILE = 256


def _dispatch_kernel(pos_ref, x_ref, init_ref, xs_ref, sem):
    del init_ref
    base = pl.program_id(0) * DISPATCH_TILE

    def start(r, carry):
        pltpu.make_async_copy(x_ref.at[pl.ds(r, 1)], xs_ref.at[pl.ds(pos_ref[base + r], 1)], sem).start()
        return carry

    lax.fori_loop(0, DISPATCH_TILE, start, 0)

    def wait(r, carry):
        pltpu.make_async_copy(x_ref.at[pl.ds(0, 1)], xs_ref.at[pl.ds(0, 1)], sem).wait()
        return carry

    lax.fori_loop(0, DISPATCH_TILE, wait, 0)


def _dispatch(pos, xr, n_sorted):
    t_rows = xr.shape[0]
    tm = DISPATCH_TILE
    return pl.pallas_call(
        _dispatch_kernel,
        grid_spec=pltpu.PrefetchScalarGridSpec(
            num_scalar_prefetch=1,
            grid=(t_rows // tm,),
            in_specs=[pl.BlockSpec((tm, ROW_W), lambda i, pos: (i, 0)),
                      pl.BlockSpec(memory_space=pl.ANY)],
            out_specs=pl.BlockSpec(memory_space=pl.ANY),
            scratch_shapes=[pltpu.SemaphoreType.DMA(())]),
        out_shape=jax.ShapeDtypeStruct((n_sorted, ROW_W), F32),
        input_output_aliases={2: 0},
        compiler_params=_cparams(1),
        name="moe_dispatch",
    )(pos, xr, jnp.zeros((n_sorted, ROW_W), F32))


def _moe_kernel(tg_ref, xs_ref, wg_ref, wu_ref, wd_ref, y_ref, xb_ref):
    i = pl.program_id(0)
    e = pl.program_id(1)

    @pl.when((tg_ref[i] < 0) & (e == 0))
    def _():
        y_ref[...] = jnp.zeros_like(y_ref)

    @pl.when(tg_ref[i] >= 0)
    def _():
        @pl.when(e == 0)
        def _():
            xb_ref[...] = xs_ref[:, :D_MODEL].astype(BF16)

        xb = xb_ref[...]
        hg = _dot(xb, wg_ref[0, 0])
        hu = _dot(xb, wu_ref[0, 0])
        comb = xs_ref[:, D_MODEL:]
        lane = lax.broadcasted_iota(jnp.int32, comb.shape, 1)
        ce = jnp.sum(jnp.where(lane == e, comb, 0.0), axis=1, keepdims=True)
        hh = (hg * _sigmoid(hg)) * hu * ce
        contrib = _dot(hh.astype(BF16), wd_ref[0, 0])

        @pl.when(e == 0)
        def _():
            y_ref[...] = contrib

        @pl.when(e > 0)
        def _():
            y_ref[...] += contrib


def _moe(tile_group, xs, wg, wu, wd):
    n_sorted = xs.shape[0]
    tm = MOE_TILE
    grp = lambda i, e, tg: (jnp.maximum(tg[i], 0), e, 0, 0)
    return pl.pallas_call(
        _moe_kernel,
        grid_spec=pltpu.PrefetchScalarGridSpec(
            num_scalar_prefetch=1,
            grid=(n_sorted // tm, MOE_EPG),
            in_specs=[pl.BlockSpec((tm, ROW_W), lambda i, e, tg: (i, 0)),
                      pl.BlockSpec((1, 1, D_MODEL, MOE_FF), grp),
                      pl.BlockSpec((1, 1, D_MODEL, MOE_FF), grp),
                      pl.BlockSpec((1, 1, MOE_FF, D_MODEL), grp)],
            out_specs=pl.BlockSpec((tm, D_MODEL), lambda i, e, tg: (i, 0)),
            scratch_shapes=[pltpu.VMEM((tm, D_MODEL), BF16)]),
        out_shape=jax.ShapeDtypeStruct((n_sorted, D_MODEL), F32),
        compiler_params=_cparams(2),
        name="moe_experts",
    )(tile_group, xs, wg, wu, wd)


COMBINE_TILE = 256


def _combine_kernel(pos_ref, h_ref, g_ref, b_ref, ys_ref, o_ref, buf_ref, sem):
    base = pl.program_id(0) * COMBINE_TILE

    def start(r, carry):
        pltpu.make_async_copy(ys_ref.at[pl.ds(pos_ref[base + r], 1)], buf_ref.at[pl.ds(r, 1)], sem).start()
        return carry

    lax.fori_loop(0, COMBINE_TILE, start, 0)

    def wait(r, carry):
        pltpu.make_async_copy(ys_ref.at[pl.ds(0, 1)], buf_ref.at[pl.ds(0, 1)], sem).wait()
        return carry

    lax.fori_loop(0, COMBINE_TILE, wait, 0)
    o_ref[...] = _layer_norm(DN_ALPHA * h_ref[...] + buf_ref[...], g_ref[...], b_ref[...])


def _combine(pos, h, ys, g, b):
    t_rows = h.shape[0]
    tm = COMBINE_TILE
    return pl.pallas_call(
        _combine_kernel,
        grid_spec=pltpu.PrefetchScalarGridSpec(
            num_scalar_prefetch=1,
            grid=(t_rows // tm,),
            in_specs=[pl.BlockSpec((tm, D_MODEL), lambda i, pos: (i, 0)),
                      pl.BlockSpec((1, D_MODEL), lambda i, pos: (0, 0)),
                      pl.BlockSpec((1, D_MODEL), lambda i, pos: (0, 0)),
                      pl.BlockSpec(memory_space=pl.ANY)],
            out_specs=pl.BlockSpec((tm, D_MODEL), lambda i, pos: (i, 0)),
            scratch_shapes=[pltpu.VMEM((tm, D_MODEL), F32), pltpu.SemaphoreType.DMA(())]),
        out_shape=jax.ShapeDtypeStruct((t_rows, D_MODEL), F32),
        compiler_params=_cparams(1),
        name="moe_combine",
    )(pos, h, g, b, ys)


def _hier_moe_ln(h, w_rg, b_rg, w_re, b_re, w_gate, w_up, w_down, ln_g, ln_b):
    t_rows = h.shape[0]
    wr = jnp.concatenate([w_rg.astype(F32), w_re.astype(F32).reshape(D_MODEL, MOE_GROUPS * MOE_EPG)], axis=1)
    wr = jnp.pad(wr, ((0, 0), (0, 128 - wr.shape[1])))
    wrh, wrl = _split(wr)
    br = jnp.concatenate([b_rg.astype(F32), b_re.astype(F32).reshape(-1)])
    br = jnp.pad(br, (0, 128 - br.shape[0])).reshape(1, 128)
    xr, gid = _router(h, wrh, wrl, br)

    gid = gid[:, 0]
    onehot = (gid[:, None] == jnp.arange(MOE_GROUPS)[None, :]).astype(jnp.int32)
    rank = jnp.cumsum(onehot, axis=0) - onehot
    counts = jnp.sum(onehot, axis=0)
    tiles = (counts + MOE_TILE - 1) // MOE_TILE
    tile_end = jnp.cumsum(tiles)
    tile_start = tile_end - tiles
    pos = jnp.sum(onehot * (rank + tile_start[None, :] * MOE_TILE), axis=1).astype(jnp.int32)
    n_tiles = t_rows // MOE_TILE + MOE_GROUPS
    tidx = jnp.arange(n_tiles)[:, None]
    in_g = (tidx >= tile_start[None, :]) & (tidx < tile_end[None, :])
    tile_group = jnp.where(jnp.any(in_g, axis=1), jnp.argmax(in_g, axis=1), -1).astype(jnp.int32)

    xs = _dispatch(pos, xr, n_tiles * MOE_TILE)
    ys = _moe(tile_group, xs, w_gate.astype(BF16), w_up.astype(BF16), w_down.astype(BF16))
    return _combine(pos, h, ys, ln_g.reshape(1, D_MODEL).astype(F32), ln_b.reshape(1, D_MODEL).astype(F32))


def _layer_ab(h, batch, lp, w_in, s5, w_glu, b_glu, gla_w_gate, gla_b_gate, gla_norm, w_out, ln_g, ln_b):
    hk = GLA_HEADS * GLA_DK
    hv = GLA_HEADS * GLA_DV
    o_q = S5_WIDTH
    o_g = o_q + 2 * hk + hv
    o_r = o_g + GLA_RANK
    w_in = w_in.astype(F32)
    w_u = w_in[:, :S5_WIDTH].astype(BF16)
    w_main = jnp.concatenate([w_in[:, o_q:o_g], w_in[:, o_r:o_r + hv],
                              jnp.pad(w_in[:, o_g:o_r], ((0, 0), (0, 128 - GLA_RANK)))], axis=1).astype(BF16)
    u = _s5_uproj(h, w_u)
    z = _s5_core(u, s5, batch)
    p_s5 = _s5_out(z, w_glu.astype(BF16), b_glu.reshape(1, -1).astype(F32), w_out[:S5_WIDTH].astype(BF16))
    proj = _matmul(h, w_main, tn=640)
    wg = jnp.pad(gla_w_gate.astype(F32), ((0, 128 - GLA_RANK), (0, 0)))
    wgh, wgl = _split(wg)
    y_gla = _gla(proj, wgh, wgl, gla_b_gate.reshape(1, -1).astype(F32), gla_norm.reshape(1, -1).astype(F32),
                 batch, lp)
    return _out_ln(h, p_s5, [(y_gla, w_out[S5_WIDTH:].astype(BF16))],
                   ln_g.reshape(1, -1).astype(F32), ln_b.reshape(1, -1).astype(F32))


def _layer_cd(h, batch, lp, w_in, conv_w, a_log, dt_bias, gdn_norm, lru_conv_w, lru_conv_b,
              lru_w_a, lru_b_a, lru_w_x, lru_b_x, lru_lambda, w_out, ln_g, ln_b):
    nq = GDN_QKV
    hv = GDN_HEADS * GDN_DV
    w_in = w_in.astype(F32)
    o_z = nq
    o_b = o_z + hv
    o_a = o_b + GDN_HEADS
    o_x = o_a + GDN_HEADS
    o_gt = o_x + LRU_WIDTH
    w_main = jnp.concatenate([w_in[:, :o_b], w_in[:, o_x:o_gt + LRU_WIDTH],
                              jnp.pad(w_in[:, o_b:o_x], ((0, 0), (0, 128 - 2 * GDN_HEADS)))], axis=1).astype(BF16)
    proj = _matmul(h, w_main, tn=896)

    cw = jnp.pad(conv_w.astype(F32), ((0, 8 - CONV_K), (0, 0)))
    hp = jnp.zeros((8, 128), F32)
    hp = hp.at[0, GDN_HEADS:2 * GDN_HEADS].set(-jnp.exp(a_log.astype(F32)))
    hp = hp.at[1, GDN_HEADS:2 * GDN_HEADS].set(dt_bias.astype(F32))
    y_gdn = _gdn(proj, cw, hp, gdn_norm.reshape(1, -1).astype(F32), batch, lp)

    lcw = jnp.pad(lru_conv_w.astype(F32), ((0, 8 - CONV_K), (0, 0)))
    wax = jnp.concatenate([lru_w_a.astype(F32), lru_w_x.astype(F32)], axis=2).astype(BF16)
    bax = jnp.concatenate([lru_b_a.astype(F32).reshape(LRU_BLOCKS, 1, LRU_BW),
                           lru_b_x.astype(F32).reshape(LRU_BLOCKS, 1, LRU_BW)], axis=2)
    sp = (-LRU_C * jax.nn.softplus(-lru_lambda.astype(F32))).reshape(1, LRU_WIDTH)
    y_lru = _lru(proj, lcw, lru_conv_b.reshape(1, -1).astype(F32), wax, bax, sp, batch, lp)

    return _out_ln(h, None, [(y_gdn, w_out[:hv].astype(BF16)), (y_lru, w_out[hv:].astype(BF16))],
                   ln_g.reshape(1, -1).astype(F32), ln_b.reshape(1, -1).astype(F32))


def kernel(x, meta_tokens, ab_w_in, ab_s5_a_re, ab_s5_a_im, ab_s5_log_dt, ab_s5_b_re, ab_s5_b_im,
           ab_s5_c_re, ab_s5_c_im, ab_s5_d, ab_s5_w_glu, ab_s5_b_glu, ab_gla_w_gate, ab_gla_b_gate,
           ab_gla_norm, ab_w_out, cd_w_in, cd_conv_w, cd_gdn_a_log, cd_gdn_dt_bias, cd_gdn_norm,
           cd_lru_conv_w, cd_lru_conv_b, cd_lru_w_a, cd_lru_b_a, cd_lru_w_x, cd_lru_b_x, cd_lru_lambda,
           cd_w_out, moe_w_router_g, moe_b_router_g, moe_w_router_e, moe_b_router_e, moe_w_gate,
           moe_w_up, moe_w_down, ln_mix_g, ln_mix_b, ln_ffn_g, ln_ffn_b):
    batch, seq, _ = x.shape
    length = seq + N_META
    lp = -(-length // ROW_ALIGN) * ROW_ALIGN
    meta = jnp.broadcast_to(meta_tokens.astype(x.dtype)[None], (batch, N_META, D_MODEL))
    h = jnp.concatenate([meta, x, jnp.zeros((batch, lp - length, D_MODEL), x.dtype)], axis=1)
    h = h.reshape(batch * lp, D_MODEL).astype(F32)

    for layer in range(DEPTH):
        j = layer // 2
        if layer % 2 == 0:
            s5 = _s5_params(ab_s5_a_re[j], ab_s5_a_im[j], ab_s5_log_dt[j], ab_s5_b_re[j], ab_s5_b_im[j],
                            ab_s5_c_re[j], ab_s5_c_im[j], ab_s5_d[j])
            h = _layer_ab(h, batch, lp, ab_w_in[j], s5, ab_s5_w_glu[j], ab_s5_b_glu[j], ab_gla_w_gate[j],
                          ab_gla_b_gate[j], ab_gla_norm[j], ab_w_out[j], ln_mix_g[layer], ln_mix_b[layer])
        else:
            h = _layer_cd(h, batch, lp, cd_w_in[j], cd_conv_w[j], cd_gdn_a_log[j], cd_gdn_dt_bias[j],
                          cd_gdn_norm[j], cd_lru_conv_w[j], cd_lru_conv_b[j], cd_lru_w_a[j], cd_lru_b_a[j],
                          cd_lru_w_x[j], cd_lru_b_x[j], cd_lru_lambda[j], cd_w_out[j],
                          ln_mix_g[layer], ln_mix_b[layer])
        h = _hier_moe_ln(h, moe_w_router_g[layer], moe_b_router_g[layer], moe_w_router_e[layer],
                         moe_b_router_e[layer], moe_w_gate[layer], moe_w_up[layer], moe_w_down[layer],
                         ln_ffn_g[layer], ln_ffn_b[layer])
    return h.reshape(batch, lp, D_MODEL)[:, N_META:length].astype(x.dtype)
```

```python
import functools
import math

import jax
import jax.numpy as jnp
from jax import lax
from jax.experimental import pallas as pl
from jax.experimental.pallas import tpu as pltpu

F32 = jnp.float32
BF16 = jnp.bfloat16
HIGHEST = lax.Precision.HIGHEST

D_MODEL = 2048
N_META = 16
CONV_K = 4
DEPTH = 2
DN_ALPHA = (2.0 * DEPTH) ** 0.25
LN_EPS = 1e-5
NORM_EPS = 1e-6

S5_WIDTH = 1024
S5_GROUP = 16
S5_GROUPS = 64
S5_STATE = 64
S5_LC = 16
S5_SLABS = 8

GLA_HEADS = 4
GLA_DK = 128
GLA_DV = 256
GLA_RANK = 16
GLA_TAU = 16.0

GDN_HEADS = 8
GDN_DK = 128
GDN_DV = 128

LRU_WIDTH = 1024
LRU_BLOCKS = 8
LRU_BW = 128
LRU_C = 8.0

MOE_GROUPS = 4
MOE_EPG = 8
MOE_FF = 256

CHUNK = 64
ROW_ALIGN = 768
VMEM_LIMIT = 56 * 1024 * 1024

NT_DIMS = (((1,), (1,)), ((), ()))
TN_DIMS = (((0,), (0,)), ((), ()))


def _cparams(n_axes):
    return pltpu.CompilerParams(dimension_semantics=("arbitrary",) * n_axes,
                                vmem_limit_bytes=VMEM_LIMIT)


def _dot(a, b):
    return jnp.dot(a, b, preferred_element_type=F32)


def _dotg(a, b, dims):
    return lax.dot_general(a, b, dims, preferred_element_type=F32)


def _split(a):
    hi = a.astype(BF16)
    lo = (a - hi.astype(F32)).astype(BF16)
    return hi, lo


def _sigmoid(x):
    return 1.0 / (1.0 + jnp.exp(-x))


def _softplus(x):
    return jnp.maximum(x, 0.0) + jnp.log1p(jnp.exp(-jnp.abs(x)))


def _gelu(x):
    return 0.5 * x * (1.0 + jnp.tanh(math.sqrt(2.0 / math.pi) * (x + 0.044715 * (x * x * x))))


def _layer_norm(x, g, b):
    mu = jnp.mean(x, axis=-1, keepdims=True)
    xc = x - mu
    var = jnp.mean(xc * xc, axis=-1, keepdims=True)
    return xc * lax.rsqrt(var + LN_EPS) * g + b


def _mm_kernel(x_ref, w_ref, o_ref, xb_ref):
    @pl.when(pl.program_id(1) == 0)
    def _():
        xb_ref[...] = x_ref[...].astype(BF16)

    o_ref[...] = _dot(xb_ref[...], w_ref[...]).astype(o_ref.dtype)


def _matmul(x, w, tn, tm=768, out_dtype=F32):
    m, k = x.shape
    n = w.shape[1]
    return pl.pallas_call(
        _mm_kernel,
        grid=(m // tm, n // tn),
        in_specs=[pl.BlockSpec((tm, k), lambda i, j: (i, 0)),
                  pl.BlockSpec((k, tn), lambda i, j: (0, j))],
        out_specs=pl.BlockSpec((tm, tn), lambda i, j: (i, j)),
        out_shape=jax.ShapeDtypeStruct((m, n), out_dtype),
        scratch_shapes=[pltpu.VMEM((tm, k), BF16)],
        compiler_params=_cparams(2),
        name="in_proj",
    )(x, w)


S5_TILE_CHUNKS = 32


def _s5_uproj_kernel(hv_ref, w_ref, u_ref, lhs_ref):
    m = S5_TILE_CHUNKS
    for t in range(S5_LC):
        lhs_ref[t * m:(t + 1) * m, :] = hv_ref[:, t * D_MODEL:(t + 1) * D_MODEL].astype(BF16)
    z = _dot(lhs_ref[...], w_ref[...])
    for t in range(S5_LC):
        for s in range(S5_SLABS):
            u_ref[s, :, t * 128:(t + 1) * 128] = z[t * m:(t + 1) * m, s * 128:(s + 1) * 128]


def _s5_uproj(h, w_u):
    t_rows = h.shape[0]
    n_chunks = t_rows // S5_LC
    m = S5_TILE_CHUNKS
    hv = h.reshape(n_chunks, S5_LC * D_MODEL)
    return pl.pallas_call(
        _s5_uproj_kernel,
        grid=(n_chunks // m,),
        in_specs=[pl.BlockSpec((m, S5_LC * D_MODEL), lambda i: (i, 0)),
                  pl.BlockSpec((D_MODEL, S5_WIDTH), lambda i: (0, 0))],
        out_specs=pl.BlockSpec((S5_SLABS, m, S5_LC * 128), lambda i: (0, i, 0)),
        out_shape=jax.ShapeDtypeStruct((S5_SLABS, n_chunks, S5_LC * 128), F32),
        scratch_shapes=[pltpu.VMEM((S5_LC * m, D_MODEL), BF16)],
        compiler_params=_cparams(1),
        name="s5_uproj",
    )(hv, w_u)


def _s5_core_kernel(u_ref, stack_ref, ms_ref, mo_ref, ak_ref, pre_ref, pim_ref, d_ref, z_ref,
                    sre_ref, sim_ref, xre_ref, xim_ref):
    n = u_ref.shape[1]
    half = 8 * S5_STATE
    u = u_ref[0]
    ub = u.astype(BF16)
    s = _dot(ub, ms_ref[0])
    sre_ref[...] = s[:, :half]
    sim_ref[...] = s[:, half:]

    ak = ak_ref[0]
    pre = pre_ref[0]
    pim = pim_ref[0]
    row = lax.broadcasted_iota(jnp.int32, (8, half), 0)

    def body(rb, carry):
        cre, cim = carry
        r0 = pl.multiple_of(rb * 8, 8)
        xr = sre_ref[pl.ds(r0, 8), :]
        xi = sim_ref[pl.ds(r0, 8), :]
        for idx, k in enumerate((1, 2, 4)):
            are = ak[2 * idx:2 * idx + 1, :]
            aim = ak[2 * idx + 1:2 * idx + 2, :]
            keep = row >= k
            shr = jnp.where(keep, pltpu.roll(xr, k, 0), 0.0)
            shi = jnp.where(keep, pltpu.roll(xi, k, 0), 0.0)
            xr, xi = xr + are * shr - aim * shi, xi + are * shi + aim * shr
        xr = xr + pre * cre - pim * cim
        xi = xi + pre * cim + pim * cre
        first = row == 0
        xre_ref[pl.ds(r0, 8), :] = jnp.where(first, cre, pltpu.roll(xr, 1, 0))
        xim_ref[pl.ds(r0, 8), :] = jnp.where(first, cim, pltpu.roll(xi, 1, 0))
        return xr[7:8, :], xi[7:8, :]

    zero = jnp.zeros((1, half), F32)
    lax.fori_loop(0, n // 8, body, (zero, zero))

    xprev = jnp.concatenate([xre_ref[...], xim_ref[...]], axis=1).astype(BF16)
    y_inter = _dot(xprev, mo_ref[0])
    d = d_ref[0]
    npair = S5_LC // 2
    for q in range(npair):
        yq = _dot(ub[:, :(q + 1) * 256], stack_ref[0, (npair - 1 - q) * 256:, :])
        sl = slice(q * 256, (q + 1) * 256)
        y = yq + y_inter[:, sl] + d[:, sl] * u[:, sl]
        z_ref[0, :, sl] = _gelu(y)


def _s5_core(u, prm, batch):
    n_tot = u.shape[1]
    n = n_tot // batch
    w = S5_LC * 128
    half = 8 * S5_STATE
    slab = lambda s, b: (s, 0, 0)
    return pl.pallas_call(
        _s5_core_kernel,
        grid=(S5_SLABS, batch),
        in_specs=[pl.BlockSpec((1, n, w), lambda s, b: (s, b, 0)),
                  pl.BlockSpec((1, w, 256), slab),
                  pl.BlockSpec((1, w, 2 * half), slab),
                  pl.BlockSpec((1, 2 * half, w), slab),
                  pl.BlockSpec((1, 8, half), slab),
                  pl.BlockSpec((1, 8, half), slab),
                  pl.BlockSpec((1, 8, half), slab),
                  pl.BlockSpec((1, 1, w), slab)],
        out_specs=pl.BlockSpec((1, n, w), lambda s, b: (s, b, 0)),
        out_shape=jax.ShapeDtypeStruct(u.shape, F32),
        scratch_shapes=[pltpu.VMEM((n, half), F32)] * 4,
        compiler_params=_cparams(2),
        name="s5_core",
    )(u, prm["stack"], prm["ms"], prm["mo"], prm["ak"], prm["pre"], prm["pim"], prm["d"])


def _s5_out_kernel(z_ref, wglu_ref, bglu_ref, wout_ref, p_ref, zf_ref):
    m = S5_TILE_CHUNKS
    for t in range(S5_LC):
        for s in range(S5_SLABS):
            zf_ref[t * m:(t + 1) * m, s * 128:(s + 1) * 128] = z_ref[s, :, t * 128:(t + 1) * 128]
    zf = zf_ref[...]
    gate = _sigmoid(_dot(zf.astype(BF16), wglu_ref[...]) + bglu_ref[...])
    p = _dot((zf * gate).astype(BF16), wout_ref[...])
    for t in range(S5_LC):
        p_ref[:, t * D_MODEL:(t + 1) * D_MODEL] = p[t * m:(t + 1) * m, :]


def _s5_out(z, w_glu, b_glu, w_out_top):
    n_chunks = z.shape[1]
    m = S5_TILE_CHUNKS
    p = pl.pallas_call(
        _s5_out_kernel,
        grid=(n_chunks // m,),
        in_specs=[pl.BlockSpec((S5_SLABS, m, S5_LC * 128), lambda i: (0, i, 0)),
                  pl.BlockSpec((S5_WIDTH, S5_WIDTH), lambda i: (0, 0)),
                  pl.BlockSpec((1, S5_WIDTH), lambda i: (0, 0)),
                  pl.BlockSpec((S5_WIDTH, D_MODEL), lambda i: (0, 0))],
        out_specs=pl.BlockSpec((m, S5_LC * D_MODEL), lambda i: (i, 0)),
        out_shape=jax.ShapeDtypeStruct((n_chunks, S5_LC * D_MODEL), F32),
        scratch_shapes=[pltpu.VMEM((S5_LC * m, S5_WIDTH), F32)],
        compiler_params=_cparams(1),
        name="s5_out",
    )(z, w_glu, b_glu, w_out_top)
    return p.reshape(n_chunks * S5_LC, D_MODEL)


def _s5_params(a_re, a_im, log_dt, b_re, b_im, c_re, c_im, d):
    g, p, c, lc = S5_GROUPS, S5_STATE, S5_GROUP, S5_LC
    a = lax.complex(a_re.astype(F32), a_im.astype(F32))
    adt = a * jnp.exp(log_dt.astype(F32))[:, None]
    zoh = (jnp.exp(adt) - 1.0) / a
    bz = lax.complex(b_re.astype(F32), b_im.astype(F32)) * zoh[:, :, None]
    cc = lax.complex(c_re.astype(F32), c_im.astype(F32))
    jj = jnp.arange(lc + 1, dtype=F32)
    apow = jnp.exp(adt[None] * jj[:, None, None])
    eye8 = jnp.eye(8, dtype=F32)

    kre = jnp.einsum("gop,jgp,gpc->jgco", cc, apow[:lc], bz, precision=HIGHEST).real
    kg = kre.reshape(lc, S5_SLABS, 8, c, c)
    kblk = jnp.einsum("jsgco,gh->jsgcho", kg, eye8).reshape(lc, S5_SLABS, 128, 128)
    kpad = jnp.concatenate([jnp.zeros_like(kblk[:1]), kblk], axis=0)
    npair = lc // 2
    dd = jnp.arange(npair)[:, None, None]
    lo_in = jnp.arange(2)[None, :, None]
    lo_out = jnp.arange(2)[None, None, :]
    lag = 2 * dd + lo_out - lo_in
    tiles = kpad[lag + 1]
    tiles = tiles.transpose(3, 0, 1, 4, 2, 5).reshape(S5_SLABS, npair, 256, 256)
    stack = tiles[:, ::-1].reshape(S5_SLABS, npair * 256, 256)

    wst = apow[lc - 1 - jnp.arange(lc)][:, :, :, None] * bz[None]
    wst = wst.reshape(lc, S5_SLABS, 8, p, c)
    ms_c = jnp.einsum("tsgpc,gh->stgchp", wst, lax.complex(eye8, jnp.zeros_like(eye8)))
    ms_c = ms_c.reshape(S5_SLABS, lc * 128, 8 * p)
    ms = jnp.concatenate([ms_c.real, ms_c.imag], axis=2)

    zc = cc[None] * apow[1:lc + 1][:, :, None, :]
    zc = zc.reshape(lc, S5_SLABS, 8, c, p)
    mo_c = jnp.einsum("tsgop,gh->sgptho", zc, lax.complex(eye8, jnp.zeros_like(eye8)))
    mo_c = mo_c.reshape(S5_SLABS, 8 * p, lc * 128)
    mo = jnp.concatenate([mo_c.real, -mo_c.imag], axis=1)

    def slab_layout(x):
        return x.reshape(x.shape[:-2] + (S5_SLABS, 8 * p))

    kk = jnp.array([1.0, 2.0, 4.0], F32)
    a_k = slab_layout(jnp.exp(adt[None] * (lc * kk)[:, None, None]))
    ak = jnp.stack([a_k[0].real, a_k[0].imag, a_k[1].real, a_k[1].imag, a_k[2].real, a_k[2].imag,
                    jnp.zeros_like(a_k[0].real), jnp.zeros_like(a_k[0].real)], axis=1)
    rr = jnp.arange(1, 9, dtype=F32)
    a_r = slab_layout(jnp.exp(adt[None] * (lc * rr)[:, None, None]))
    pre = a_r.real.transpose(1, 0, 2)
    pim = a_r.imag.transpose(1, 0, 2)
    dsk = jnp.tile(d.astype(F32).reshape(S5_SLABS, 1, 128), (1, 1, lc))
    return {"stack": stack.astype(BF16), "ms": ms.astype(BF16), "mo": mo.astype(BF16),
            "ak": ak, "pre": pre, "pim": pim, "d": dsk}


GLA_TILE = 256


def _gla_kernel(q_ref, k_ref, v_ref, r_ref, gl_ref, wgh_ref, wgl_ref, bg_ref, ng_ref, o_ref, st_ref):
    @pl.when(pl.program_id(1) == 0)
    def _():
        st_ref[...] = jnp.zeros_like(st_ref)

    rowi = lax.broadcasted_iota(jnp.int32, (CHUNK, CHUNK), 0)
    coli = lax.broadcasted_iota(jnp.int32, (CHUNK, CHUNK), 1)
    tri = rowi >= coli
    tri_b = jnp.where(tri, 1.0, 0.0).astype(BF16)
    wgh = wgh_ref[...]
    wgl = wgl_ref[...]
    bg = bg_ref[...]
    ng = ng_ref[...]
    scale = GLA_DK ** -0.5

    def chunk(c, carry):
        r0 = pl.multiple_of(c * CHUNK, CHUNK)
        rows = pl.ds(r0, CHUNK)
        glh, gll = _split(gl_ref[rows, :])
        pre = _dot(glh, wgh) + _dot(gll, wgh) + _dot(glh, wgl) + bg
        logf = (jnp.minimum(pre, 0.0) - jnp.log1p(jnp.exp(-jnp.abs(pre)))) * (1.0 / GLA_TAU)
        lh, ll = _split(logf)
        bcum = _dot(tri_b, lh) + _dot(tri_b, ll)
        blast = bcum[CHUNK - 1:CHUNK, :]
        kk = k_ref[rows, :]
        q_in = q_ref[rows, :] * scale * jnp.exp(bcum)
        k_in = kk * jnp.exp(-bcum)
        k_st = kk * jnp.exp(blast - bcum)
        dec = jnp.exp(blast)
        heads = range(GLA_HEADS)
        sks = [slice(h * GLA_DK, (h + 1) * GLA_DK) for h in heads]
        svs = [slice(h * GLA_DV, (h + 1) * GLA_DV) for h in heads]
        qh = [q_in[:, sks[h]].astype(BF16) for h in heads]
        vh = [v_ref[rows, svs[h]].astype(BF16) for h in heads]
        st_old = [st_ref[h] for h in heads]
        att = [_dotg(qh[h], k_in[:, sks[h]].astype(BF16), NT_DIMS) for h in heads]
        o_inter = [_dotg(qh[h], st_old[h].astype(BF16), NT_DIMS) for h in heads]
        kv = [_dotg(vh[h], k_st[:, sks[h]].astype(BF16), TN_DIMS) for h in heads]
        o_intra = [_dot(jnp.where(tri, att[h], 0.0).astype(BF16), vh[h]) for h in heads]
        for h in heads:
            o = o_intra[h] + o_inter[h]
            on = o * lax.rsqrt(jnp.mean(o * o, axis=1, keepdims=True) + NORM_EPS) * ng
            rr = r_ref[rows, svs[h]]
            o_ref[rows, svs[h]] = (on * (rr * _sigmoid(rr))).astype(o_ref.dtype)
            st_ref[h] = st_old[h] * dec[:, sks[h]] + kv[h]
        return carry

    lax.fori_loop(0, GLA_TILE // CHUNK, chunk, 0)


def _gla(proj, wgh, wgl, bg, ng, batch, lp):
    t_rows = proj.shape[0]
    nt = lp // GLA_TILE
    hk = GLA_HEADS * GLA_DK
    hv = GLA_HEADS * GLA_DV
    rowmap = lambda col: (lambda b, t: (b * nt + t, col))
    const = lambda b, t: (0, 0)
    return pl.pallas_call(
        _gla_kernel,
        grid=(batch, nt),
        in_specs=[pl.BlockSpec((GLA_TILE, hk), rowmap(0)),
                  pl.BlockSpec((GLA_TILE, hk), rowmap(1)),
                  pl.BlockSpec((GLA_TILE, hv), rowmap(1)),
                  pl.BlockSpec((GLA_TILE, hv), rowmap(2)),
                  pl.BlockSpec((GLA_TILE, 128), rowmap(24)),
                  pl.BlockSpec((128, hk), const),
                  pl.BlockSpec((128, hk), const),
                  pl.BlockSpec((1, hk), const),
                  pl.BlockSpec((1, GLA_DV), const)],
        out_specs=pl.BlockSpec((GLA_TILE, hv), rowmap(0)),
        out_shape=jax.ShapeDtypeStruct((t_rows, hv), BF16),
        scratch_shapes=[pltpu.VMEM((GLA_HEADS, GLA_DV, GLA_DK), F32)],
        compiler_params=_cparams(2),
        name="gla",
    )(proj, proj, proj, proj, proj, wgh, wgl, bg, ng)


GDN_TILE = 256
GDN_QKV = 3 * GDN_HEADS * GDN_DK


def _gdn_kernel(qkv_ref, z_ref, ba_ref, cw_ref, hp_ref, ng_ref, o_ref,
                ext_ref, act_ref, beta_ref, g_ref, s_ref):
    tile = GDN_TILE

    @pl.when(pl.program_id(1) == 0)
    def _():
        ext_ref[0:8, :] = jnp.zeros((8, GDN_QKV), F32)
        s_ref[...] = jnp.zeros_like(s_ref)

    @pl.when(pl.program_id(1) > 0)
    def _():
        ext_ref[0:8, :] = ext_ref[tile:tile + 8, :]

    ext_ref[8:tile + 8, :] = qkv_ref[...]
    qscale = GDN_DK ** -0.5
    for cb in range(GDN_QKV // 128):
        cols = slice(cb * 128, (cb + 1) * 128)
        w = cw_ref[:, cols]
        y = (w[3:4] * ext_ref[8:tile + 8, cols] + w[2:3] * ext_ref[7:tile + 7, cols]
             + w[1:2] * ext_ref[6:tile + 6, cols] + w[0:1] * ext_ref[5:tile + 5, cols])
        a = y * _sigmoid(y)
        if cb < 2 * GDN_HEADS:
            a = a * lax.rsqrt(jnp.sum(a * a, axis=1, keepdims=True) + NORM_EPS)
            if cb < GDN_HEADS:
                a = a * qscale
        act_ref[:, cols] = a

    ba = ba_ref[...]
    beta_ref[...] = _sigmoid(ba)
    g_ref[...] = hp_ref[0:1, :] * _softplus(ba + hp_ref[1:2, :])

    rowi = lax.broadcasted_iota(jnp.int32, (CHUNK, CHUNK), 0)
    coli = lax.broadcasted_iota(jnp.int32, (CHUNK, CHUNK), 1)
    tri = rowi >= coli
    strict = rowi > coli
    tri_b = jnp.where(tri, 1.0, 0.0).astype(BF16)
    upper_b = jnp.where(rowi <= coli, 1.0, 0.0).astype(BF16)
    eye = jnp.where(rowi == coli, 1.0, 0.0)
    level_masks = []
    for lvl in range(6):
        bi = rowi >> lvl
        bj = coli >> lvl
        level_masks.append(((bi & 1) == 1) & (bj == bi - 1))
    ng = ng_ref[...]

    def chunk(c, carry):
        r0 = pl.multiple_of(c * CHUNK, CHUNK)
        rows = pl.ds(r0, CHUNK)
        gh, gl = _split(g_ref[rows, :])
        gc = _dot(tri_b, gh) + _dot(tri_b, gl)
        gct = _dotg(gh, upper_b, TN_DIMS) + _dotg(gl, upper_b, TN_DIMS)
        eg = jnp.exp(gc)
        elast = jnp.exp(gc[CHUNK - 1:CHUNK, :] - gc)
        dlast = jnp.exp(gc[CHUNK - 1:CHUNK, :])
        beta = beta_ref[rows, :]
        heads = range(GDN_HEADS)
        hcol = lambda t, h: t[:, GDN_HEADS + h:GDN_HEADS + h + 1]
        qs = [act_ref[rows, h * GDN_DK:(h + 1) * GDN_DK] for h in heads]
        ks = [act_ref[rows, (GDN_HEADS + h) * GDN_DK:(GDN_HEADS + h + 1) * GDN_DK] for h in heads]
        vs = [act_ref[rows, (2 * GDN_HEADS + h) * GDN_DK:(2 * GDN_HEADS + h + 1) * GDN_DK] for h in heads]
        bcols = [beta[:, h:h + 1] for h in heads]
        kbs = [ks[h] * bcols[h] for h in heads]
        khb = [ks[h].astype(BF16) for h in heads]
        gams = [jnp.exp(jnp.where(tri, hcol(gc, h) - gct[GDN_HEADS + h:GDN_HEADS + h + 1, :], -1e30)) for h in heads]
        kk = [_dotg(kbs[h].astype(BF16), khb[h], NT_DIMS) for h in heads]
        qk = [_dotg(qs[h].astype(BF16), khb[h], NT_DIMS) for h in heads]
        mm = [jnp.where(strict, kk[h] * gams[h], 0.0) for h in heads]
        tinv = [eye - jnp.where(level_masks[0], mm[h], 0.0) for h in heads]
        for lvl in range(1, 6):
            tb = [tinv[h].astype(BF16) for h in heads]
            xs = [_dot(jnp.where(level_masks[lvl], mm[h], 0.0).astype(BF16), tb[h]).astype(BF16) for h in heads]
            tinv = [tinv[h] - _dot(tb[h], xs[h]) for h in heads]
        rhs = [jnp.concatenate([vs[h] * bcols[h], kbs[h] * hcol(eg, h)], axis=1).astype(BF16) for h in heads]
        sol = [_dot(tinv[h].astype(BF16), rhs[h]) for h in heads]
        s_old = [s_ref[h] for h in heads]
        sb = [s_old[h].astype(BF16) for h in heads]
        ws = [_dot(sol[h][:, GDN_DV:].astype(BF16), sb[h]) for h in heads]
        qsd = [_dot((qs[h] * hcol(eg, h)).astype(BF16), sb[h]) for h in heads]
        vnb = [(sol[h][:, :GDN_DV] - ws[h]).astype(BF16) for h in heads]
        av = [_dot(jnp.where(tri, qk[h] * gams[h], 0.0).astype(BF16), vnb[h]) for h in heads]
        kv = [_dotg((ks[h] * hcol(elast, h)).astype(BF16), vnb[h], TN_DIMS) for h in heads]
        for h in heads:
            o = qsd[h] + av[h]
            on = o * lax.rsqrt(jnp.mean(o * o, axis=1, keepdims=True) + NORM_EPS) * ng
            zz = z_ref[rows, h * GDN_DV:(h + 1) * GDN_DV]
            o_ref[rows, h * GDN_DV:(h + 1) * GDN_DV] = (on * (zz * _sigmoid(zz))).astype(o_ref.dtype)
            s_ref[h] = s_old[h] * hcol(dlast, h) + kv[h]
        return carry

    lax.fori_loop(0, tile // CHUNK, chunk, 0)


def _gdn(proj, cw, hp, ng, batch, lp):
    t_rows = proj.shape[0]
    nt = lp // GDN_TILE
    hv = GDN_HEADS * GDN_DV
    rowmap = lambda col: (lambda b, t: (b * nt + t, col))
    const = lambda b, t: (0, 0)
    return pl.pallas_call(
        _gdn_kernel,
        grid=(batch, nt),
        in_specs=[pl.BlockSpec((GDN_TILE, GDN_QKV), rowmap(0)),
                  pl.BlockSpec((GDN_TILE, hv), rowmap(3)),
                  pl.BlockSpec((GDN_TILE, 128), rowmap(48)),
                  pl.BlockSpec((8, GDN_QKV), const),
                  pl.BlockSpec((8, 128), const),
                  pl.BlockSpec((1, GDN_DV), const)],
        out_specs=pl.BlockSpec((GDN_TILE, hv), rowmap(0)),
        out_shape=jax.ShapeDtypeStruct((t_rows, hv), BF16),
        scratch_shapes=[pltpu.VMEM((GDN_TILE + 8, GDN_QKV), F32),
                        pltpu.VMEM((GDN_TILE, GDN_QKV), F32),
                        pltpu.VMEM((GDN_TILE, 128), F32),
                        pltpu.VMEM((GDN_TILE, 128), F32),
                        pltpu.VMEM((GDN_HEADS, GDN_DK, GDN_DV), F32)],
        compiler_params=_cparams(2),
        name="gdn",
    )(proj, proj, proj, cw, hp, ng)


LRU_TILE = 256


def _lru_kernel(x_ref, gate_ref, cw_ref, cb_ref, wax_ref, bax_ref, sp_ref, o_ref,
                ext_ref, a_ref, b_ref, h_ref):
    tile = LRU_TILE

    @pl.when(pl.program_id(1) == 0)
    def _():
        ext_ref[0:8, :] = jnp.zeros((8, LRU_WIDTH), F32)
        h_ref[...] = jnp.zeros_like(h_ref)

    @pl.when(pl.program_id(1) > 0)
    def _():
        ext_ref[0:8, :] = ext_ref[tile:tile + 8, :]

    ext_ref[8:tile + 8, :] = x_ref[...]
    for blk in range(LRU_BLOCKS):
        cols = slice(blk * LRU_BW, (blk + 1) * LRU_BW)
        w = cw_ref[:, cols]
        xc = (w[3:4] * ext_ref[8:tile + 8, cols] + w[2:3] * ext_ref[7:tile + 7, cols]
              + w[1:2] * ext_ref[6:tile + 6, cols] + w[0:1] * ext_ref[5:tile + 5, cols]
              + cb_ref[:, cols])
        pre = _dot(xc.astype(BF16), wax_ref[blk]) + bax_ref[blk]
        r = _sigmoid(pre[:, :LRU_BW])
        i = _sigmoid(pre[:, LRU_BW:])
        log_a = sp_ref[:, cols] * r
        a_ref[:, cols] = jnp.exp(log_a)
        th = jnp.tanh(log_a)
        b_ref[:, cols] = jnp.sqrt(-2.0 * th / (1.0 - th)) * (i * xc)

    row = lax.broadcasted_iota(jnp.int32, (8, LRU_WIDTH), 0)

    def body(rb, carry):
        r0 = pl.multiple_of(rb * 8, 8)
        aa = a_ref[pl.ds(r0, 8), :]
        bb = b_ref[pl.ds(r0, 8), :]
        for k in (1, 2, 4):
            keep = row >= k
            a_sh = jnp.where(keep, pltpu.roll(aa, k, 0), 1.0)
            b_sh = jnp.where(keep, pltpu.roll(bb, k, 0), 0.0)
            bb = aa * b_sh + bb
            aa = aa * a_sh
        hs = bb + aa * carry
        gt = gate_ref[pl.ds(r0, 8), :]
        o_ref[pl.ds(r0, 8), :] = (hs * _gelu(gt)).astype(o_ref.dtype)
        return hs[7:8, :]

    h_ref[...] = lax.fori_loop(0, tile // 8, body, h_ref[...])


def _lru(proj, cw, cb, wax, bax, sp, batch, lp):
    t_rows = proj.shape[0]
    nt = lp // LRU_TILE
    rowmap = lambda col: (lambda b, t: (b * nt + t, col))
    const = lambda b, t: (0, 0)
    return pl.pallas_call(
        _lru_kernel,
        grid=(batch, nt),
        in_specs=[pl.BlockSpec((LRU_TILE, LRU_WIDTH), rowmap(4)),
                  pl.BlockSpec((LRU_TILE, LRU_WIDTH), rowmap(5)),
                  pl.BlockSpec((8, LRU_WIDTH), const),
                  pl.BlockSpec((1, LRU_WIDTH), const),
                  pl.BlockSpec((LRU_BLOCKS, LRU_BW, 2 * LRU_BW), lambda b, t: (0, 0, 0)),
                  pl.BlockSpec((LRU_BLOCKS, 1, 2 * LRU_BW), lambda b, t: (0, 0, 0)),
                  pl.BlockSpec((1, LRU_WIDTH), const)],
        out_specs=pl.BlockSpec((LRU_TILE, LRU_WIDTH), rowmap(0)),
        out_shape=jax.ShapeDtypeStruct((t_rows, LRU_WIDTH), BF16),
        scratch_shapes=[pltpu.VMEM((LRU_TILE + 8, LRU_WIDTH), F32),
                        pltpu.VMEM((LRU_TILE, LRU_WIDTH), F32),
                        pltpu.VMEM((LRU_TILE, LRU_WIDTH), F32),
                        pltpu.VMEM((1, LRU_WIDTH), F32)],
        compiler_params=_cparams(2),
        name="lru",
    )(proj, proj, cw, cb, wax, bax, sp)


OUT_TILE = 256


def _out_ln_kernel(n_add, h_ref, *refs):
    adds = refs[:n_add]
    a1_ref, w1_ref, a2_ref, w2_ref, g_ref, b_ref, o_ref = refs[n_add:]
    acc = DN_ALPHA * h_ref[...]
    for r in adds:
        acc = acc + r[...]
    acc = acc + _dot(a1_ref[...], w1_ref[...])
    if a2_ref is not None:
        acc = acc + _dot(a2_ref[...], w2_ref[...])
    o_ref[...] = _layer_norm(acc, g_ref[...], b_ref[...])


def _out_ln_kernel_1(h_ref, p_ref, a1_ref, w1_ref, g_ref, b_ref, o_ref):
    _out_ln_kernel(1, h_ref, p_ref, a1_ref, w1_ref, None, None, g_ref, b_ref, o_ref)


def _out_ln_kernel_2(h_ref, a1_ref, w1_ref, a2_ref, w2_ref, g_ref, b_ref, o_ref):
    _out_ln_kernel(0, h_ref, a1_ref, w1_ref, a2_ref, w2_ref, g_ref, b_ref, o_ref)


def _out_ln(h, addend, pairs, g, b):
    t_rows = h.shape[0]
    tm = OUT_TILE
    row = lambda i: (i, 0)
    const = lambda i: (0, 0)
    ins = [h]
    specs = [pl.BlockSpec((tm, D_MODEL), row)]
    if addend is not None:
        ins.append(addend)
        specs.append(pl.BlockSpec((tm, D_MODEL), row))
    for a, w in pairs:
        ins += [a, w]
        specs += [pl.BlockSpec((tm, a.shape[1]), row), pl.BlockSpec(w.shape, const)]
    ins += [g, b]
    specs += [pl.BlockSpec((1, D_MODEL), const)] * 2
    body = _out_ln_kernel_1 if addend is not None else _out_ln_kernel_2
    return pl.pallas_call(
        body,
        grid=(t_rows // tm,),
        in_specs=specs,
        out_specs=pl.BlockSpec((tm, D_MODEL), row),
        out_shape=jax.ShapeDtypeStruct((t_rows, D_MODEL), F32),
        compiler_params=_cparams(1),
        name="out_ln",
    )(*ins)


ROUTER_TILE = 768
MOE_TILE = 512
ROW_W = D_MODEL + 128
BIG = 1e30


def _router_kernel(h_ref, wh_ref, wl_ref, b_ref, xr_ref, gid_ref):
    h = h_ref[...]
    hh, hl = _split(h)
    logits = _dot(hh, wh_ref[...]) + _dot(hl, wh_ref[...]) + _dot(hh, wl_ref[...]) + b_ref[...]
    lane = lax.broadcasted_iota(jnp.int32, logits.shape, 1).astype(F32)

    def first_argmax(vals, vmax):
        return jnp.min(jnp.where(vals == vmax, lane, 4096.0), axis=1, keepdims=True)

    is_g = lane < MOE_GROUPS
    lg = jnp.where(is_g, logits, -BIG)
    m = jnp.max(lg, axis=1, keepdims=True)
    gidx = first_argmax(lg, m)
    denom = jnp.sum(jnp.where(is_g, jnp.exp(logits - m), 0.0), axis=1, keepdims=True)
    p_top = 1.0 / denom
    base = MOE_GROUPS + MOE_EPG * gidx
    in_grp = (lane >= base) & (lane < base + MOE_EPG)
    le = jnp.where(in_grp, logits, -BIG)
    v1 = jnp.max(le, axis=1, keepdims=True)
    i1 = first_argmax(le, v1)
    le2 = jnp.where(lane == i1, -BIG, le)
    v2 = jnp.max(le2, axis=1, keepdims=True)
    i2 = first_argmax(le2, v2)
    e21 = jnp.exp(v2 - v1)
    w1 = p_top / (1.0 + e21)
    w2 = p_top * e21 / (1.0 + e21)
    comb = jnp.where(lane == i1 - base, w1, 0.0) + jnp.where(lane == i2 - base, w2, 0.0)
    xr_ref[:, :D_MODEL] = h
    xr_ref[:, D_MODEL:] = comb
    gid_ref[...] = jnp.broadcast_to(gidx, gid_ref.shape).astype(jnp.int32)


def _router(h, wh, wl, b):
    t_rows = h.shape[0]
    tm = ROUTER_TILE
    return pl.pallas_call(
        _router_kernel,
        grid=(t_rows // tm,),
        in_specs=[pl.BlockSpec((tm, D_MODEL), lambda i: (i, 0)),
                  pl.BlockSpec((D_MODEL, 128), lambda i: (0, 0)),
                  pl.BlockSpec((D_MODEL, 128), lambda i: (0, 0)),
                  pl.BlockSpec((1, 128), lambda i: (0, 0))],
        out_specs=[pl.BlockSpec((tm, ROW_W), lambda i: (i, 0)),
                   pl.BlockSpec((tm, 128), lambda i: (i, 0))],
        out_shape=[jax.ShapeDtypeStruct((t_rows, ROW_W), F32),
                   jax.ShapeDtypeStruct((t_rows, 128), jnp.int32)],
        compiler_params=_cparams(1),
        name="router",
    )(h, wh, wl, b)


DISPATCH_TILE = 256


def _dispatch_kernel(pos_ref, x_ref, init_ref, xs_ref, sem):
    del init_ref
    base = pl.program_id(0) * DISPATCH_TILE

    def start(r, carry):
        pltpu.make_async_copy(x_ref.at[pl.ds(r, 1)], xs_ref.at[pl.ds(pos_ref[base + r], 1)], sem).start()
        return carry

    lax.fori_loop(0, DISPATCH_TILE, start, 0)

    def wait(r, carry):
        pltpu.make_async_copy(x_ref.at[pl.ds(0, 1)], xs_ref.at[pl.ds(0, 1)], sem).wait()
        return carry

    lax.fori_loop(0, DISPATCH_TILE, wait, 0)


def _dispatch(pos, xr, n_sorted):
    t_rows = xr.shape[0]
    tm = DISPATCH_TILE
    return pl.pallas_call(
        _dispatch_kernel,
        grid_spec=pltpu.PrefetchScalarGridSpec(
            num_scalar_prefetch=1,
            grid=(t_rows // tm,),
            in_specs=[pl.BlockSpec((tm, ROW_W), lambda i, pos: (i, 0)),
                      pl.BlockSpec(memory_space=pl.ANY)],
            out_specs=pl.BlockSpec(memory_space=pl.ANY),
            scratch_shapes=[pltpu.SemaphoreType.DMA(())]),
        out_shape=jax.ShapeDtypeStruct((n_sorted, ROW_W), F32),
        input_output_aliases={2: 0},
        compiler_params=_cparams(1),
        name="moe_dispatch",
    )(pos, xr, jnp.zeros((n_sorted, ROW_W), F32))


def _moe_kernel(tg_ref, xs_ref, wg_ref, wu_ref, wd_ref, y_ref, xb_ref):
    i = pl.program_id(0)
    e = pl.program_id(1)

    @pl.when((tg_ref[i] < 0) & (e == 0))
    def _():
        y_ref[...] = jnp.zeros_like(y_ref)

    @pl.when(tg_ref[i] >= 0)
    def _():
        @pl.when(e == 0)
        def _():
            xb_ref[...] = xs_ref[:, :D_MODEL].astype(BF16)

        xb = xb_ref[...]
        hg = _dot(xb, wg_ref[0, 0])
        hu = _dot(xb, wu_ref[0, 0])
        comb = xs_ref[:, D_MODEL:]
        lane = lax.broadcasted_iota(jnp.int32, comb.shape, 1)
        ce = jnp.sum(jnp.where(lane == e, comb, 0.0), axis=1, keepdims=True)
        hh = (hg * _sigmoid(hg)) * hu * ce
        contrib = _dot(hh.astype(BF16), wd_ref[0, 0])

        @pl.when(e == 0)
        def _():
            y_ref[...] = contrib

        @pl.when(e > 0)
        def _():
            y_ref[...] += contrib


def _moe(tile_group, xs, wg, wu, wd):
    n_sorted = xs.shape[0]
    tm = MOE_TILE
    grp = lambda i, e, tg: (jnp.maximum(tg[i], 0), e, 0, 0)
    return pl.pallas_call(
        _moe_kernel,
        grid_spec=pltpu.PrefetchScalarGridSpec(
            num_scalar_prefetch=1,
            grid=(n_sorted // tm, MOE_EPG),
            in_specs=[pl.BlockSpec((tm, ROW_W), lambda i, e, tg: (i, 0)),
                      pl.BlockSpec((1, 1, D_MODEL, MOE_FF), grp),
                      pl.BlockSpec((1, 1, D_MODEL, MOE_FF), grp),
                      pl.BlockSpec((1, 1, MOE_FF, D_MODEL), grp)],
            out_specs=pl.BlockSpec((tm, D_MODEL), lambda i, e, tg: (i, 0)),
            scratch_shapes=[pltpu.VMEM((tm, D_MODEL), BF16)]),
        out_shape=jax.ShapeDtypeStruct((n_sorted, D_MODEL), F32),
        compiler_params=_cparams(2),
        name="moe_experts",
    )(tile_group, xs, wg, wu, wd)


COMBINE_TILE = 256


def _combine_kernel(pos_ref, h_ref, g_ref, b_ref, ys_ref, o_ref, buf_ref, sem):
    base = pl.program_id(0) * COMBINE_TILE

    def start(r, carry):
        pltpu.make_async_copy(ys_ref.at[pl.ds(pos_ref[base + r], 1)], buf_ref.at[pl.ds(r, 1)], sem).start()
        return carry

    lax.fori_loop(0, COMBINE_TILE, start, 0)

    def wait(r, carry):
        pltpu.make_async_copy(ys_ref.at[pl.ds(0, 1)], buf_ref.at[pl.ds(0, 1)], sem).wait()
        return carry

    lax.fori_loop(0, COMBINE_TILE, wait, 0)
    o_ref[...] = _layer_norm(DN_ALPHA * h_ref[...] + buf_ref[...], g_ref[...], b_ref[...])


def _combine(pos, h, ys, g, b):
    t_rows = h.shape[0]
    tm = COMBINE_TILE
    return pl.pallas_call(
        _combine_kernel,
        grid_spec=pltpu.PrefetchScalarGridSpec(
            num_scalar_prefetch=1,
            grid=(t_rows // tm,),
            in_specs=[pl.BlockSpec((tm, D_MODEL), lambda i, pos: (i, 0)),
                      pl.BlockSpec((1, D_MODEL), lambda i, pos: (0, 0)),
                      pl.BlockSpec((1, D_MODEL), lambda i, pos: (0, 0)),
                      pl.BlockSpec(memory_space=pl.ANY)],
            out_specs=pl.BlockSpec((tm, D_MODEL), lambda i, pos: (i, 0)),
            scratch_shapes=[pltpu.VMEM((tm, D_MODEL), F32), pltpu.SemaphoreType.DMA(())]),
        out_shape=jax.ShapeDtypeStruct((t_rows, D_MODEL), F32),
        compiler_params=_cparams(1),
        name="moe_combine",
    )(pos, h, g, b, ys)


def _hier_moe_ln(h, w_rg, b_rg, w_re, b_re, w_gate, w_up, w_down, ln_g, ln_b):
    t_rows = h.shape[0]
    wr = jnp.concatenate([w_rg.astype(F32), w_re.astype(F32).reshape(D_MODEL, MOE_GROUPS * MOE_EPG)], axis=1)
    wr = jnp.pad(wr, ((0, 0), (0, 128 - wr.shape[1])))
    wrh, wrl = _split(wr)
    br = jnp.concatenate([b_rg.astype(F32), b_re.astype(F32).reshape(-1)])
    br = jnp.pad(br, (0, 128 - br.shape[0])).reshape(1, 128)
    xr, gid = _router(h, wrh, wrl, br)

    gid = gid[:, 0]
    onehot = (gid[:, None] == jnp.arange(MOE_GROUPS)[None, :]).astype(jnp.int32)
    rank = jnp.cumsum(onehot, axis=0) - onehot
    counts = jnp.sum(onehot, axis=0)
    tiles = (counts + MOE_TILE - 1) // MOE_TILE
    tile_end = jnp.cumsum(tiles)
    tile_start = tile_end - tiles
    pos = jnp.sum(onehot * (rank + tile_start[None, :] * MOE_TILE), axis=1).astype(jnp.int32)
    n_tiles = t_rows // MOE_TILE + MOE_GROUPS
    tidx = jnp.arange(n_tiles)[:, None]
    in_g = (tidx >= tile_start[None, :]) & (tidx < tile_end[None, :])
    tile_group = jnp.where(jnp.any(in_g, axis=1), jnp.argmax(in_g, axis=1), -1).astype(jnp.int32)

    xs = _dispatch(pos, xr, n_tiles * MOE_TILE)
    ys = _moe(tile_group, xs, w_gate.astype(BF16), w_up.astype(BF16), w_down.astype(BF16))
    return _combine(pos, h, ys, ln_g.reshape(1, D_MODEL).astype(F32), ln_b.reshape(1, D_MODEL).astype(F32))


def _layer_ab(h, batch, lp, w_in, s5, w_glu, b_glu, gla_w_gate, gla_b_gate, gla_norm, w_out, ln_g, ln_b):
    hk = GLA_HEADS * GLA_DK
    hv = GLA_HEADS * GLA_DV
    o_q = S5_WIDTH
    o_g = o_q + 2 * hk + hv
    o_r = o_g + GLA_RANK
    w_in = w_in.astype(F32)
    w_u = w_in[:, :S5_WIDTH].astype(BF16)
    w_main = jnp.concatenate([w_in[:, o_q:o_g], w_in[:, o_r:o_r + hv],
                              jnp.pad(w_in[:, o_g:o_r], ((0, 0), (0, 128 - GLA_RANK)))], axis=1).astype(BF16)
    u = _s5_uproj(h, w_u)
    z = _s5_core(u, s5, batch)
    p_s5 = _s5_out(z, w_glu.astype(BF16), b_glu.reshape(1, -1).astype(F32), w_out[:S5_WIDTH].astype(BF16))
    proj = _matmul(h, w_main, tn=640)
    wg = jnp.pad(gla_w_gate.astype(F32), ((0, 128 - GLA_RANK), (0, 0)))
    wgh, wgl = _split(wg)
    y_gla = _gla(proj, wgh, wgl, gla_b_gate.reshape(1, -1).astype(F32), gla_norm.reshape(1, -1).astype(F32),
                 batch, lp)
    return _out_ln(h, p_s5, [(y_gla, w_out[S5_WIDTH:].astype(BF16))],
                   ln_g.reshape(1, -1).astype(F32), ln_b.reshape(1, -1).astype(F32))


def _layer_cd(h, batch, lp, w_in, conv_w, a_log, dt_bias, gdn_norm, lru_conv_w, lru_conv_b,
              lru_w_a, lru_b_a, lru_w_x, lru_b_x, lru_lambda, w_out, ln_g, ln_b):
    nq = GDN_QKV
    hv = GDN_HEADS * GDN_DV
    w_in = w_in.astype(F32)
    o_z = nq
    o_b = o_z + hv
    o_a = o_b + GDN_HEADS
    o_x = o_a + GDN_HEADS
    o_gt = o_x + LRU_WIDTH
    w_main = jnp.concatenate([w_in[:, :o_b], w_in[:, o_x:o_gt + LRU_WIDTH],
                              jnp.pad(w_in[:, o_b:o_x], ((0, 0), (0, 128 - 2 * GDN_HEADS)))], axis=1).astype(BF16)
    proj = _matmul(h, w_main, tn=896)

    cw = jnp.pad(conv_w.astype(F32), ((0, 8 - CONV_K), (0, 0)))
    hp = jnp.zeros((8, 128), F32)
    hp = hp.at[0, GDN_HEADS:2 * GDN_HEADS].set(-jnp.exp(a_log.astype(F32)))
    hp = hp.at[1, GDN_HEADS:2 * GDN_HEADS].set(dt_bias.astype(F32))
    y_gdn = _gdn(proj, cw, hp, gdn_norm.reshape(1, -1).astype(F32), batch, lp)

    lcw = jnp.pad(lru_conv_w.astype(F32), ((0, 8 - CONV_K), (0, 0)))
    wax = jnp.concatenate([lru_w_a.astype(F32), lru_w_x.astype(F32)], axis=2).astype(BF16)
    bax = jnp.concatenate([lru_b_a.astype(F32).reshape(LRU_BLOCKS, 1, LRU_BW),
                           lru_b_x.astype(F32).reshape(LRU_BLOCKS, 1, LRU_BW)], axis=2)
    sp = (-LRU_C * jax.nn.softplus(-lru_lambda.astype(F32))).reshape(1, LRU_WIDTH)
    y_lru = _lru(proj, lcw, lru_conv_b.reshape(1, -1).astype(F32), wax, bax, sp, batch, lp)

    return _out_ln(h, None, [(y_gdn, w_out[:hv].astype(BF16)), (y_lru, w_out[hv:].astype(BF16))],
                   ln_g.reshape(1, -1).astype(F32), ln_b.reshape(1, -1).astype(F32))


def kernel(x, meta_tokens, ab_w_in, ab_s5_a_re, ab_s5_a_im, ab_s5_log_dt, ab_s5_b_re, ab_s5_b_im,
           ab_s5_c_re, ab_s5_c_im, ab_s5_d, ab_s5_w_glu, ab_s5_b_glu, ab_gla_w_gate, ab_gla_b_gate,
           ab_gla_norm, ab_w_out, cd_w_in, cd_conv_w, cd_gdn_a_log, cd_gdn_dt_bias, cd_gdn_norm,
           cd_lru_conv_w, cd_lru_conv_b, cd_lru_w_a, cd_lru_b_a, cd_lru_w_x, cd_lru_b_x, cd_lru_lambda,
           cd_w_out, moe_w_router_g, moe_b_router_g, moe_w_router_e, moe_b_router_e, moe_w_gate,
           moe_w_up, moe_w_down, ln_mix_g, ln_mix_b, ln_ffn_g, ln_ffn_b):
    batch, seq, _ = x.shape
    length = seq + N_META
    lp = -(-length // ROW_ALIGN) * ROW_ALIGN
    meta = jnp.broadcast_to(meta_tokens.astype(x.dtype)[None], (batch, N_META, D_MODEL))
    h = jnp.concatenate([meta, x, jnp.zeros((batch, lp - length, D_MODEL), x.dtype)], axis=1)
    h = h.reshape(batch * lp, D_MODEL).astype(F32)

    for layer in range(DEPTH):
        j = layer // 2
        if layer % 2 == 0:
            s5 = _s5_params(ab_s5_a_re[j], ab_s5_a_im[j], ab_s5_log_dt[j], ab_s5_b_re[j], ab_s5_b_im[j],
                            ab_s5_c_re[j], ab_s5_c_im[j], ab_s5_d[j])
            h = _layer_ab(h, batch, lp, ab_w_in[j], s5, ab_s5_w_glu[j], ab_s5_b_glu[j], ab_gla_w_gate[j],
                          ab_gla_b_gate[j], ab_gla_norm[j], ab_w_out[j], ln_mix_g[layer], ln_mix_b[layer])
        else:
            h = _layer_cd(h, batch, lp, cd_w_in[j], cd_conv_w[j], cd_gdn_a_log[j], cd_gdn_dt_bias[j],
                          cd_gdn_norm[j], cd_lru_conv_w[j], cd_lru_conv_b[j], cd_lru_w_a[j], cd_lru_b_a[j],
                          cd_lru_w_x[j], cd_lru_b_x[j], cd_lru_lambda[j], cd_w_out[j],
                          ln_mix_g[layer], ln_mix_b[layer])
        h = _hier_moe_ln(h, moe_w_router_g[layer], moe_b_router_g[layer], moe_w_router_e[layer],
                         moe_b_router_e[layer], moe_w_gate[layer], moe_w_up[layer], moe_w_down[layer],
                         ln_ffn_g[layer], ln_ffn_b[layer])
    return h.reshape(batch, lp, D_MODEL)[:, N_META:length].astype(x.dtype)
```

```python
import functools
import math

import jax
import jax.numpy as jnp
from jax import lax
from jax.experimental import pallas as pl
from jax.experimental.pallas import tpu as pltpu

F32 = jnp.float32
BF16 = jnp.bfloat16
HIGHEST = lax.Precision.HIGHEST

D_MODEL = 2048
N_META = 16
CONV_K = 4
DEPTH = 2
DN_ALPHA = (2.0 * DEPTH) ** 0.25
LN_EPS = 1e-5
NORM_EPS = 1e-6

S5_WIDTH = 1024
S5_GROUP = 16
S5_GROUPS = 64
S5_STATE = 64
S5_LC = 16
S5_SLABS = 8

GLA_HEADS = 4
GLA_DK = 128
GLA_DV = 256
GLA_RANK = 16
GLA_TAU = 16.0

GDN_HEADS = 8
GDN_DK = 128
GDN_DV = 128

LRU_WIDTH = 1024
LRU_BLOCKS = 8
LRU_BW = 128
LRU_C = 8.0

MOE_GROUPS = 4
MOE_EPG = 8
MOE_FF = 256

CHUNK = 64
ROW_ALIGN = 768
VMEM_LIMIT = 56 * 1024 * 1024

NT_DIMS = (((1,), (1,)), ((), ()))
TN_DIMS = (((0,), (0,)), ((), ()))


def _cparams(n_axes):
    return pltpu.CompilerParams(dimension_semantics=("arbitrary",) * n_axes,
                                vmem_limit_bytes=VMEM_LIMIT)


def _dot(a, b):
    return jnp.dot(a, b, preferred_element_type=F32)


def _dotg(a, b, dims):
    return lax.dot_general(a, b, dims, preferred_element_type=F32)


def _split(a):
    hi = a.astype(BF16)
    lo = (a - hi.astype(F32)).astype(BF16)
    return hi, lo


def _sigmoid(x):
    return 1.0 / (1.0 + jnp.exp(-x))


def _softplus(x):
    return jnp.maximum(x, 0.0) + jnp.log1p(jnp.exp(-jnp.abs(x)))


def _gelu(x):
    return 0.5 * x * (1.0 + jnp.tanh(math.sqrt(2.0 / math.pi) * (x + 0.044715 * (x * x * x))))


def _layer_norm(x, g, b):
    mu = jnp.mean(x, axis=-1, keepdims=True)
    xc = x - mu
    var = jnp.mean(xc * xc, axis=-1, keepdims=True)
    return xc * lax.rsqrt(var + LN_EPS) * g + b


def _mm_kernel(x_ref, w_ref, o_ref, xb_ref):
    @pl.when(pl.program_id(1) == 0)
    def _():
        xb_ref[...] = x_ref[...].astype(BF16)

    o_ref[...] = _dot(xb_ref[...], w_ref[...]).astype(o_ref.dtype)


def _matmul(x, w, tn, tm=768, out_dtype=F32):
    m, k = x.shape
    n = w.shape[1]
    return pl.pallas_call(
        _mm_kernel,
        grid=(m // tm, n // tn),
        in_specs=[pl.BlockSpec((tm, k), lambda i, j: (i, 0)),
                  pl.BlockSpec((k, tn), lambda i, j: (0, j))],
        out_specs=pl.BlockSpec((tm, tn), lambda i, j: (i, j)),
        out_shape=jax.ShapeDtypeStruct((m, n), out_dtype),
        scratch_shapes=[pltpu.VMEM((tm, k), BF16)],
        compiler_params=_cparams(2),
        name="in_proj",
    )(x, w)


S5_TILE_CHUNKS = 32


def _s5_row_perm():
    m = S5_TILE_CHUNKS
    r = jnp.arange(S5_LC * m)
    nat = (r % m) * S5_LC + r // m
    return (nat[:, None] == jnp.arange(S5_LC * m)[None, :]).astype(BF16)


def _s5_uproj_kernel(h_ref, perm_ref, w_ref, u_ref):
    m = S5_TILE_CHUNKS
    lhs = _dot(perm_ref[...], h_ref[...].astype(BF16)).astype(BF16)
    z = _dot(lhs, w_ref[...])
    for t in range(S5_LC):
        for s in range(S5_SLABS):
            u_ref[s, :, t * 128:(t + 1) * 128] = z[t * m:(t + 1) * m, s * 128:(s + 1) * 128]


def _s5_uproj(h, w_u):
    t_rows = h.shape[0]
    n_chunks = t_rows // S5_LC
    m = S5_TILE_CHUNKS
    return pl.pallas_call(
        _s5_uproj_kernel,
        grid=(n_chunks // m,),
        in_specs=[pl.BlockSpec((m * S5_LC, D_MODEL), lambda i: (i, 0)),
                  pl.BlockSpec((m * S5_LC, m * S5_LC), lambda i: (0, 0)),
                  pl.BlockSpec((D_MODEL, S5_WIDTH), lambda i: (0, 0))],
        out_specs=pl.BlockSpec((S5_SLABS, m, S5_LC * 128), lambda i: (0, i, 0)),
        out_shape=jax.ShapeDtypeStruct((S5_SLABS, n_chunks, S5_LC * 128), F32),
        compiler_params=_cparams(1),
        name="s5_uproj",
    )(h, _s5_row_perm(), w_u)


def _s5_core_kernel(u_ref, stack_ref, ms_ref, mo_ref, ak_ref, pre_ref, pim_ref, d_ref, z_ref,
                    sre_ref, sim_ref, xre_ref, xim_ref):
    n = u_ref.shape[1]
    half = 8 * S5_STATE
    u = u_ref[0]
    ub = u.astype(BF16)
    s = _dot(ub, ms_ref[0])
    sre_ref[...] = s[:, :half]
    sim_ref[...] = s[:, half:]

    ak = ak_ref[0]
    pre = pre_ref[0]
    pim = pim_ref[0]
    row = lax.broadcasted_iota(jnp.int32, (8, half), 0)

    def body(rb, carry):
        cre, cim = carry
        r0 = pl.multiple_of(rb * 8, 8)
        xr = sre_ref[pl.ds(r0, 8), :]
        xi = sim_ref[pl.ds(r0, 8), :]
        for idx, k in enumerate((1, 2, 4)):
            are = ak[2 * idx:2 * idx + 1, :]
            aim = ak[2 * idx + 1:2 * idx + 2, :]
            keep = row >= k
            shr = jnp.where(keep, pltpu.roll(xr, k, 0), 0.0)
            shi = jnp.where(keep, pltpu.roll(xi, k, 0), 0.0)
            xr, xi = xr + are * shr - aim * shi, xi + are * shi + aim * shr
        xr = xr + pre * cre - pim * cim
        xi = xi + pre * cim + pim * cre
        first = row == 0
        xre_ref[pl.ds(r0, 8), :] = jnp.where(first, cre, pltpu.roll(xr, 1, 0))
        xim_ref[pl.ds(r0, 8), :] = jnp.where(first, cim, pltpu.roll(xi, 1, 0))
        return xr[7:8, :], xi[7:8, :]

    zero = jnp.zeros((1, half), F32)
    lax.fori_loop(0, n // 8, body, (zero, zero))

    xprev = jnp.concatenate([xre_ref[...], xim_ref[...]], axis=1).astype(BF16)
    y_inter = _dot(xprev, mo_ref[0])
    d = d_ref[0]
    npair = S5_LC // 2
    for q in range(npair):
        yq = _dot(ub[:, :(q + 1) * 256], stack_ref[0, (npair - 1 - q) * 256:, :])
        sl = slice(q * 256, (q + 1) * 256)
        y = yq + y_inter[:, sl] + d[:, sl] * u[:, sl]
        z_ref[0, :, sl] = _gelu(y)


def _s5_core(u, prm, batch):
    n_tot = u.shape[1]
    n = n_tot // batch
    w = S5_LC * 128
    half = 8 * S5_STATE
    slab = lambda s, b: (s, 0, 0)
    return pl.pallas_call(
        _s5_core_kernel,
        grid=(S5_SLABS, batch),
        in_specs=[pl.BlockSpec((1, n, w), lambda s, b: (s, b, 0)),
                  pl.BlockSpec((1, w, 256), slab),
                  pl.BlockSpec((1, w, 2 * half), slab),
                  pl.BlockSpec((1, 2 * half, w), slab),
                  pl.BlockSpec((1, 8, half), slab),
                  pl.BlockSpec((1, 8, half), slab),
                  pl.BlockSpec((1, 8, half), slab),
                  pl.BlockSpec((1, 1, w), slab)],
        out_specs=pl.BlockSpec((1, n, w), lambda s, b: (s, b, 0)),
        out_shape=jax.ShapeDtypeStruct(u.shape, F32),
        scratch_shapes=[pltpu.VMEM((n, half), F32)] * 4,
        compiler_params=_cparams(2),
        name="s5_core",
    )(u, prm["stack"], prm["ms"], prm["mo"], prm["ak"], prm["pre"], prm["pim"], prm["d"])


def _s5_out_kernel(z_ref, perm_ref, wglu_ref, bglu_ref, wout_ref, p_ref, zf_ref):
    m = S5_TILE_CHUNKS
    for t in range(S5_LC):
        for s in range(S5_SLABS):
            zf_ref[t * m:(t + 1) * m, s * 128:(s + 1) * 128] = z_ref[s, :, t * 128:(t + 1) * 128]
    zf = zf_ref[...]
    gate = _sigmoid(_dot(zf.astype(BF16), wglu_ref[...]) + bglu_ref[...])
    gated = (zf * gate).astype(BF16)
    nat = _dotg(perm_ref[...], gated, TN_DIMS).astype(BF16)
    p_ref[...] = _dot(nat, wout_ref[...])


def _s5_out(z, w_glu, b_glu, w_out_top):
    n_chunks = z.shape[1]
    m = S5_TILE_CHUNKS
    return pl.pallas_call(
        _s5_out_kernel,
        grid=(n_chunks // m,),
        in_specs=[pl.BlockSpec((S5_SLABS, m, S5_LC * 128), lambda i: (0, i, 0)),
                  pl.BlockSpec((m * S5_LC, m * S5_LC), lambda i: (0, 0)),
                  pl.BlockSpec((S5_WIDTH, S5_WIDTH), lambda i: (0, 0)),
                  pl.BlockSpec((1, S5_WIDTH), lambda i: (0, 0)),
                  pl.BlockSpec((S5_WIDTH, D_MODEL), lambda i: (0, 0))],
        out_specs=pl.BlockSpec((m * S5_LC, D_MODEL), lambda i: (i, 0)),
        out_shape=jax.ShapeDtypeStruct((n_chunks * S5_LC, D_MODEL), F32),
        scratch_shapes=[pltpu.VMEM((S5_LC * m, S5_WIDTH), F32)],
        compiler_params=_cparams(1),
        name="s5_out",
    )(z, _s5_row_perm(), w_glu, b_glu, w_out_top)


def _s5_params(a_re, a_im, log_dt, b_re, b_im, c_re, c_im, d):
    g, p, c, lc = S5_GROUPS, S5_STATE, S5_GROUP, S5_LC
    a = lax.complex(a_re.astype(F32), a_im.astype(F32))
    adt = a * jnp.exp(log_dt.astype(F32))[:, None]
    zoh = (jnp.exp(adt) - 1.0) / a
    bz = lax.complex(b_re.astype(F32), b_im.astype(F32)) * zoh[:, :, None]
    cc = lax.complex(c_re.astype(F32), c_im.astype(F32))
    jj = jnp.arange(lc + 1, dtype=F32)
    apow = jnp.exp(adt[None] * jj[:, None, None])
    eye8 = jnp.eye(8, dtype=F32)

    kre = jnp.einsum("gop,jgp,gpc->jgco", cc, apow[:lc], bz, precision=HIGHEST).real
    kg = kre.reshape(lc, S5_SLABS, 8, c, c)
    kblk = jnp.einsum("jsgco,gh->jsgcho", kg, eye8).reshape(lc, S5_SLABS, 128, 128)
    kpad = jnp.concatenate([jnp.zeros_like(kblk[:1]), kblk], axis=0)
    npair = lc // 2
    dd = jnp.arange(npair)[:, None, None]
    lo_in = jnp.arange(2)[None, :, None]
    lo_out = jnp.arange(2)[None, None, :]
    lag = 2 * dd + lo_out - lo_in
    tiles = kpad[lag + 1]
    tiles = tiles.transpose(3, 0, 1, 4, 2, 5).reshape(S5_SLABS, npair, 256, 256)
    stack = tiles[:, ::-1].reshape(S5_SLABS, npair * 256, 256)

    wst = apow[lc - 1 - jnp.arange(lc)][:, :, :, None] * bz[None]
    wst = wst.reshape(lc, S5_SLABS, 8, p, c)
    ms_c = jnp.einsum("tsgpc,gh->stgchp", wst, lax.complex(eye8, jnp.zeros_like(eye8)))
    ms_c = ms_c.reshape(S5_SLABS, lc * 128, 8 * p)
    ms = jnp.concatenate([ms_c.real, ms_c.imag], axis=2)

    zc = cc[None] * apow[1:lc + 1][:, :, None, :]
    zc = zc.reshape(lc, S5_SLABS, 8, c, p)
    mo_c = jnp.einsum("tsgop,gh->sgptho", zc, lax.complex(eye8, jnp.zeros_like(eye8)))
    mo_c = mo_c.reshape(S5_SLABS, 8 * p, lc * 128)
    mo = jnp.concatenate([mo_c.real, -mo_c.imag], axis=1)

    def slab_layout(x):
        return x.reshape(x.shape[:-2] + (S5_SLABS, 8 * p))

    kk = jnp.array([1.0, 2.0, 4.0], F32)
    a_k = slab_layout(jnp.exp(adt[None] * (lc * kk)[:, None, None]))
    ak = jnp.stack([a_k[0].real, a_k[0].imag, a_k[1].real, a_k[1].imag, a_k[2].real, a_k[2].imag,
                    jnp.zeros_like(a_k[0].real), jnp.zeros_like(a_k[0].real)], axis=1)
    rr = jnp.arange(1, 9, dtype=F32)
    a_r = slab_layout(jnp.exp(adt[None] * (lc * rr)[:, None, None]))
    pre = a_r.real.transpose(1, 0, 2)
    pim = a_r.imag.transpose(1, 0, 2)
    dsk = jnp.tile(d.astype(F32).reshape(S5_SLABS, 1, 128), (1, 1, lc))
    return {"stack": stack.astype(BF16), "ms": ms.astype(BF16), "mo": mo.astype(BF16),
            "ak": ak, "pre": pre, "pim": pim, "d": dsk}


GLA_TILE = 256


def _gla_kernel(q_ref, k_ref, v_ref, r_ref, gl_ref, wgh_ref, wgl_ref, bg_ref, ng_ref, o_ref, st_ref):
    @pl.when(pl.program_id(1) == 0)
    def _():
        st_ref[...] = jnp.zeros_like(st_ref)

    rowi = lax.broadcasted_iota(jnp.int32, (CHUNK, CHUNK), 0)
    coli = lax.broadcasted_iota(jnp.int32, (CHUNK, CHUNK), 1)
    tri = rowi >= coli
    tri_b = jnp.where(tri, 1.0, 0.0).astype(BF16)
    wgh = wgh_ref[...]
    wgl = wgl_ref[...]
    bg = bg_ref[...]
    ng = ng_ref[...]
    scale = GLA_DK ** -0.5

    def chunk(c, carry):
        r0 = pl.multiple_of(c * CHUNK, CHUNK)
        rows = pl.ds(r0, CHUNK)
        glh, gll = _split(gl_ref[rows, :])
        pre = _dot(glh, wgh) + _dot(gll, wgh) + _dot(glh, wgl) + bg
        logf = (jnp.minimum(pre, 0.0) - jnp.log1p(jnp.exp(-jnp.abs(pre)))) * (1.0 / GLA_TAU)
        lh, ll = _split(logf)
        bcum = _dot(tri_b, lh) + _dot(tri_b, ll)
        blast = bcum[CHUNK - 1:CHUNK, :]
        kk = k_ref[rows, :]
        q_in = q_ref[rows, :] * scale * jnp.exp(bcum)
        k_in = kk * jnp.exp(-bcum)
        k_st = kk * jnp.exp(blast - bcum)
        dec = jnp.exp(blast)
        heads = range(GLA_HEADS)
        sks = [slice(h * GLA_DK, (h + 1) * GLA_DK) for h in heads]
        svs = [slice(h * GLA_DV, (h + 1) * GLA_DV) for h in heads]
        qh = [q_in[:, sks[h]].astype(BF16) for h in heads]
        vh = [v_ref[rows, svs[h]].astype(BF16) for h in heads]
        st_old = [st_ref[h] for h in heads]
        att = [_dotg(qh[h], k_in[:, sks[h]].astype(BF16), NT_DIMS) for h in heads]
        o_inter = [_dotg(qh[h], st_old[h].astype(BF16), NT_DIMS) for h in heads]
        kv = [_dotg(vh[h], k_st[:, sks[h]].astype(BF16), TN_DIMS) for h in heads]
        o_intra = [_dot(jnp.where(tri, att[h], 0.0).astype(BF16), vh[h]) for h in heads]
        for h in heads:
            o = o_intra[h] + o_inter[h]
            on = o * lax.rsqrt(jnp.mean(o * o, axis=1, keepdims=True) + NORM_EPS) * ng
            rr = r_ref[rows, svs[h]]
            o_ref[rows, svs[h]] = (on * (rr * _sigmoid(rr))).astype(o_ref.dtype)
            st_ref[h] = st_old[h] * dec[:, sks[h]] + kv[h]
        return carry

    lax.fori_loop(0, GLA_TILE // CHUNK, chunk, 0)


def _gla(proj, wgh, wgl, bg, ng, batch, lp):
    t_rows = proj.shape[0]
    nt = lp // GLA_TILE
    hk = GLA_HEADS * GLA_DK
    hv = GLA_HEADS * GLA_DV
    rowmap = lambda col: (lambda b, t: (b * nt + t, col))
    const = lambda b, t: (0, 0)
    return pl.pallas_call(
        _gla_kernel,
        grid=(batch, nt),
        in_specs=[pl.BlockSpec((GLA_TILE, hk), rowmap(0)),
                  pl.BlockSpec((GLA_TILE, hk), rowmap(1)),
                  pl.BlockSpec((GLA_TILE, hv), rowmap(1)),
                  pl.BlockSpec((GLA_TILE, hv), rowmap(2)),
                  pl.BlockSpec((GLA_TILE, 128), rowmap(24)),
                  pl.BlockSpec((128, hk), const),
                  pl.BlockSpec((128, hk), const),
                  pl.BlockSpec((1, hk), const),
                  pl.BlockSpec((1, GLA_DV), const)],
        out_specs=pl.BlockSpec((GLA_TILE, hv), rowmap(0)),
        out_shape=jax.ShapeDtypeStruct((t_rows, hv), BF16),
        scratch_shapes=[pltpu.VMEM((GLA_HEADS, GLA_DV, GLA_DK), F32)],
        compiler_params=_cparams(2),
        name="gla",
    )(proj, proj, proj, proj, proj, wgh, wgl, bg, ng)


GDN_TILE = 256
GDN_QKV = 3 * GDN_HEADS * GDN_DK


def _gdn_kernel(qkv_ref, z_ref, ba_ref, cw_ref, hp_ref, ng_ref, o_ref,
                ext_ref, act_ref, beta_ref, g_ref, s_ref):
    tile = GDN_TILE

    @pl.when(pl.program_id(1) == 0)
    def _():
        ext_ref[0:8, :] = jnp.zeros((8, GDN_QKV), F32)
        s_ref[...] = jnp.zeros_like(s_ref)

    @pl.when(pl.program_id(1) > 0)
    def _():
        ext_ref[0:8, :] = ext_ref[tile:tile + 8, :]

    ext_ref[8:tile + 8, :] = qkv_ref[...]
    qscale = GDN_DK ** -0.5
    for cb in range(GDN_QKV // 128):
        cols = slice(cb * 128, (cb + 1) * 128)
        w = cw_ref[:, cols]
        y = (w[3:4] * ext_ref[8:tile + 8, cols] + w[2:3] * ext_ref[7:tile + 7, cols]
             + w[1:2] * ext_ref[6:tile + 6, cols] + w[0:1] * ext_ref[5:tile + 5, cols])
        a = y * _sigmoid(y)
        if cb < 2 * GDN_HEADS:
            a = a * lax.rsqrt(jnp.sum(a * a, axis=1, keepdims=True) + NORM_EPS)
            if cb < GDN_HEADS:
                a = a * qscale
        act_ref[:, cols] = a

    ba = ba_ref[...]
    beta_ref[...] = _sigmoid(ba)
    g_ref[...] = hp_ref[0:1, :] * _softplus(ba + hp_ref[1:2, :])

    rowi = lax.broadcasted_iota(jnp.int32, (CHUNK, CHUNK), 0)
    coli = lax.broadcasted_iota(jnp.int32, (CHUNK, CHUNK), 1)
    tri = rowi >= coli
    strict = rowi > coli
    tri_b = jnp.where(tri, 1.0, 0.0).astype(BF16)
    upper_b = jnp.where(rowi <= coli, 1.0, 0.0).astype(BF16)
    eye = jnp.where(rowi == coli, 1.0, 0.0)
    level_masks = []
    for lvl in range(6):
        bi = rowi >> lvl
        bj = coli >> lvl
        level_masks.append(((bi & 1) == 1) & (bj == bi - 1))
    ng = ng_ref[...]

    def chunk(c, carry):
        r0 = pl.multiple_of(c * CHUNK, CHUNK)
        rows = pl.ds(r0, CHUNK)
        gh, gl = _split(g_ref[rows, :])
        gc = _dot(tri_b, gh) + _dot(tri_b, gl)
        gct = _dotg(gh, upper_b, TN_DIMS) + _dotg(gl, upper_b, TN_DIMS)
        eg = jnp.exp(gc)
        elast = jnp.exp(gc[CHUNK - 1:CHUNK, :] - gc)
        dlast = jnp.exp(gc[CHUNK - 1:CHUNK, :])
        beta = beta_ref[rows, :]
        heads = range(GDN_HEADS)
        hcol = lambda t, h: t[:, GDN_HEADS + h:GDN_HEADS + h + 1]
        qs = [act_ref[rows, h * GDN_DK:(h + 1) * GDN_DK] for h in heads]
        ks = [act_ref[rows, (GDN_HEADS + h) * GDN_DK:(GDN_HEADS + h + 1) * GDN_DK] for h in heads]
        vs = [act_ref[rows, (2 * GDN_HEADS + h) * GDN_DK:(2 * GDN_HEADS + h + 1) * GDN_DK] for h in heads]
        bcols = [beta[:, h:h + 1] for h in heads]
        kbs = [ks[h] * bcols[h] for h in heads]
        khb = [ks[h].astype(BF16) for h in heads]
        gams = [jnp.exp(jnp.where(tri, hcol(gc, h) - gct[GDN_HEADS + h:GDN_HEADS + h + 1, :], -1e30)) for h in heads]
        kk = [_dotg(kbs[h].astype(BF16), khb[h], NT_DIMS) for h in heads]
        qk = [_dotg(qs[h].astype(BF16), khb[h], NT_DIMS) for h in heads]
        mm = [jnp.where(strict, kk[h] * gams[h], 0.0) for h in heads]
        tinv = [eye - jnp.where(level_masks[0], mm[h], 0.0) for h in heads]
        for lvl in range(1, 6):
            tb = [tinv[h].astype(BF16) for h in heads]
            xs = [_dot(jnp.where(level_masks[lvl], mm[h], 0.0).astype(BF16), tb[h]).astype(BF16) for h in heads]
            tinv = [tinv[h] - _dot(tb[h], xs[h]) for h in heads]
        rhs = [jnp.concatenate([vs[h] * bcols[h], kbs[h] * hcol(eg, h)], axis=1).astype(BF16) for h in heads]
        sol = [_dot(tinv[h].astype(BF16), rhs[h]) for h in heads]
        s_old = [s_ref[h] for h in heads]
        sb = [s_old[h].astype(BF16) for h in heads]
        ws = [_dot(sol[h][:, GDN_DV:].astype(BF16), sb[h]) for h in heads]
        qsd = [_dot((qs[h] * hcol(eg, h)).astype(BF16), sb[h]) for h in heads]
        vnb = [(sol[h][:, :GDN_DV] - ws[h]).astype(BF16) for h in heads]
        av = [_dot(jnp.where(tri, qk[h] * gams[h], 0.0).astype(BF16), vnb[h]) for h in heads]
        kv = [_dotg((ks[h] * hcol(elast, h)).astype(BF16), vnb[h], TN_DIMS) for h in heads]
        for h in heads:
            o = qsd[h] + av[h]
            on = o * lax.rsqrt(jnp.mean(o * o, axis=1, keepdims=True) + NORM_EPS) * ng
            zz = z_ref[rows, h * GDN_DV:(h + 1) * GDN_DV]
            o_ref[rows, h * GDN_DV:(h + 1) * GDN_DV] = (on * (zz * _sigmoid(zz))).astype(o_ref.dtype)
            s_ref[h] = s_old[h] * hcol(dlast, h) + kv[h]
        return carry

    lax.fori_loop(0, tile // CHUNK, chunk, 0)


def _gdn(proj, cw, hp, ng, batch, lp):
    t_rows = proj.shape[0]
    nt = lp // GDN_TILE
    hv = GDN_HEADS * GDN_DV
    rowmap = lambda col: (lambda b, t: (b * nt + t, col))
    const = lambda b, t: (0, 0)
    return pl.pallas_call(
        _gdn_kernel,
        grid=(batch, nt),
        in_specs=[pl.BlockSpec((GDN_TILE, GDN_QKV), rowmap(0)),
                  pl.BlockSpec((GDN_TILE, hv), rowmap(3)),
                  pl.BlockSpec((GDN_TILE, 128), rowmap(48)),
                  pl.BlockSpec((8, GDN_QKV), const),
                  pl.BlockSpec((8, 128), const),
                  pl.BlockSpec((1, GDN_DV), const)],
        out_specs=pl.BlockSpec((GDN_TILE, hv), rowmap(0)),
        out_shape=jax.ShapeDtypeStruct((t_rows, hv), BF16),
        scratch_shapes=[pltpu.VMEM((GDN_TILE + 8, GDN_QKV), F32),
                        pltpu.VMEM((GDN_TILE, GDN_QKV), F32),
                        pltpu.VMEM((GDN_TILE, 128), F32),
                        pltpu.VMEM((GDN_TILE, 128), F32),
                        pltpu.VMEM((GDN_HEADS, GDN_DK, GDN_DV), F32)],
        compiler_params=_cparams(2),
        name="gdn",
    )(proj, proj, proj, cw, hp, ng)


LRU_TILE = 256


def _lru_kernel(x_ref, gate_ref, cw_ref, cb_ref, wax_ref, bax_ref, sp_ref, o_ref,
                ext_ref, a_ref, b_ref, h_ref):
    tile = LRU_TILE

    @pl.when(pl.program_id(1) == 0)
    def _():
        ext_ref[0:8, :] = jnp.zeros((8, LRU_WIDTH), F32)
        h_ref[...] = jnp.zeros_like(h_ref)

    @pl.when(pl.program_id(1) > 0)
    def _():
        ext_ref[0:8, :] = ext_ref[tile:tile + 8, :]

    ext_ref[8:tile + 8, :] = x_ref[...]
    for blk in range(LRU_BLOCKS):
        cols = slice(blk * LRU_BW, (blk + 1) * LRU_BW)
        w = cw_ref[:, cols]
        xc = (w[3:4] * ext_ref[8:tile + 8, cols] + w[2:3] * ext_ref[7:tile + 7, cols]
              + w[1:2] * ext_ref[6:tile + 6, cols] + w[0:1] * ext_ref[5:tile + 5, cols]
              + cb_ref[:, cols])
        pre = _dot(xc.astype(BF16), wax_ref[blk]) + bax_ref[blk]
        r = _sigmoid(pre[:, :LRU_BW])
        i = _sigmoid(pre[:, LRU_BW:])
        log_a = sp_ref[:, cols] * r
        a_ref[:, cols] = jnp.exp(log_a)
        th = jnp.tanh(log_a)
        b_ref[:, cols] = jnp.sqrt(-2.0 * th / (1.0 - th)) * (i * xc)

    row = lax.broadcasted_iota(jnp.int32, (8, LRU_WIDTH), 0)

    def body(rb, carry):
        r0 = pl.multiple_of(rb * 8, 8)
        aa = a_ref[pl.ds(r0, 8), :]
        bb = b_ref[pl.ds(r0, 8), :]
        for k in (1, 2, 4):
            keep = row >= k
            a_sh = jnp.where(keep, pltpu.roll(aa, k, 0), 1.0)
            b_sh = jnp.where(keep, pltpu.roll(bb, k, 0), 0.0)
            bb = aa * b_sh + bb
            aa = aa * a_sh
        hs = bb + aa * carry
        gt = gate_ref[pl.ds(r0, 8), :]
        o_ref[pl.ds(r0, 8), :] = (hs * _gelu(gt)).astype(o_ref.dtype)
        return hs[7:8, :]

    h_ref[...] = lax.fori_loop(0, tile // 8, body, h_ref[...])


def _lru(proj, cw, cb, wax, bax, sp, batch, lp):
    t_rows = proj.shape[0]
    nt = lp // LRU_TILE
    rowmap = lambda col: (lambda b, t: (b * nt + t, col))
    const = lambda b, t: (0, 0)
    return pl.pallas_call(
        _lru_kernel,
        grid=(batch, nt),
        in_specs=[pl.BlockSpec((LRU_TILE, LRU_WIDTH), rowmap(4)),
                  pl.BlockSpec((LRU_TILE, LRU_WIDTH), rowmap(5)),
                  pl.BlockSpec((8, LRU_WIDTH), const),
                  pl.BlockSpec((1, LRU_WIDTH), const),
                  pl.BlockSpec((LRU_BLOCKS, LRU_BW, 2 * LRU_BW), lambda b, t: (0, 0, 0)),
                  pl.BlockSpec((LRU_BLOCKS, 1, 2 * LRU_BW), lambda b, t: (0, 0, 0)),
                  pl.BlockSpec((1, LRU_WIDTH), const)],
        out_specs=pl.BlockSpec((LRU_TILE, LRU_WIDTH), rowmap(0)),
        out_shape=jax.ShapeDtypeStruct((t_rows, LRU_WIDTH), BF16),
        scratch_shapes=[pltpu.VMEM((LRU_TILE + 8, LRU_WIDTH), F32),
                        pltpu.VMEM((LRU_TILE, LRU_WIDTH), F32),
                        pltpu.VMEM((LRU_TILE, LRU_WIDTH), F32),
                        pltpu.VMEM((1, LRU_WIDTH), F32)],
        compiler_params=_cparams(2),
        name="lru",
    )(proj, proj, cw, cb, wax, bax, sp)


OUT_TILE = 256


def _out_ln_kernel(n_add, h_ref, *refs):
    adds = refs[:n_add]
    a1_ref, w1_ref, a2_ref, w2_ref, g_ref, b_ref, o_ref = refs[n_add:]
    acc = DN_ALPHA * h_ref[...]
    for r in adds:
        acc = acc + r[...]
    acc = acc + _dot(a1_ref[...], w1_ref[...])
    if a2_ref is not None:
        acc = acc + _dot(a2_ref[...], w2_ref[...])
    o_ref[...] = _layer_norm(acc, g_ref[...], b_ref[...])


def _out_ln_kernel_1(h_ref, p_ref, a1_ref, w1_ref, g_ref, b_ref, o_ref):
    _out_ln_kernel(1, h_ref, p_ref, a1_ref, w1_ref, None, None, g_ref, b_ref, o_ref)


def _out_ln_kernel_2(h_ref, a1_ref, w1_ref, a2_ref, w2_ref, g_ref, b_ref, o_ref):
    _out_ln_kernel(0, h_ref, a1_ref, w1_ref, a2_ref, w2_ref, g_ref, b_ref, o_ref)


def _out_ln(h, addend, pairs, g, b):
    t_rows = h.shape[0]
    tm = OUT_TILE
    row = lambda i: (i, 0)
    const = lambda i: (0, 0)
    ins = [h]
    specs = [pl.BlockSpec((tm, D_MODEL), row)]
    if addend is not None:
        ins.append(addend)
        specs.append(pl.BlockSpec((tm, D_MODEL), row))
    for a, w in pairs:
        ins += [a, w]
        specs += [pl.BlockSpec((tm, a.shape[1]), row), pl.BlockSpec(w.shape, const)]
    ins += [g, b]
    specs += [pl.BlockSpec((1, D_MODEL), const)] * 2
    body = _out_ln_kernel_1 if addend is not None else _out_ln_kernel_2
    return pl.pallas_call(
        body,
        grid=(t_rows // tm,),
        in_specs=specs,
        out_specs=pl.BlockSpec((tm, D_MODEL), row),
        out_shape=jax.ShapeDtypeStruct((t_rows, D_MODEL), F32),
        compiler_params=_cparams(1),
        name="out_ln",
    )(*ins)


ROUTER_TILE = 768
MOE_TILE = 256
BIG = 1e30


def _router_kernel(h_ref, wh_ref, wl_ref, b_ref, r_ref):
    h = h_ref[...]
    hh, hl = _split(h)
    logits = _dot(hh, wh_ref[...]) + _dot(hl, wh_ref[...]) + _dot(hh, wl_ref[...]) + b_ref[...]
    lane = lax.broadcasted_iota(jnp.int32, logits.shape, 1).astype(F32)

    def first_argmax(vals, vmax):
        return jnp.min(jnp.where(vals == vmax, lane, 4096.0), axis=1, keepdims=True)

    is_g = lane < MOE_GROUPS
    lg = jnp.where(is_g, logits, -BIG)
    m = jnp.max(lg, axis=1, keepdims=True)
    gidx = first_argmax(lg, m)
    denom = jnp.sum(jnp.where(is_g, jnp.exp(logits - m), 0.0), axis=1, keepdims=True)
    p_top = 1.0 / denom
    base = MOE_GROUPS + MOE_EPG * gidx
    in_grp = (lane >= base) & (lane < base + MOE_EPG)
    le = jnp.where(in_grp, logits, -BIG)
    v1 = jnp.max(le, axis=1, keepdims=True)
    i1 = first_argmax(le, v1)
    le2 = jnp.where(lane == i1, -BIG, le)
    v2 = jnp.max(le2, axis=1, keepdims=True)
    i2 = first_argmax(le2, v2)
    e21 = jnp.exp(v2 - v1)
    w1 = p_top / (1.0 + e21)
    w2 = p_top * e21 / (1.0 + e21)
    r_ref[...] = (jnp.where(lane == 0.0, i1 - MOE_GROUPS, 0.0) + jnp.where(lane == 1.0, i2 - MOE_GROUPS, 0.0)
                  + jnp.where(lane == 2.0, w1, 0.0) + jnp.where(lane == 3.0, w2, 0.0))


def _router(h, wh, wl, b):
    t_rows = h.shape[0]
    tm = ROUTER_TILE
    return pl.pallas_call(
        _router_kernel,
        grid=(t_rows // tm,),
        in_specs=[pl.BlockSpec((tm, D_MODEL), lambda i: (i, 0)),
                  pl.BlockSpec((D_MODEL, 128), lambda i: (0, 0)),
                  pl.BlockSpec((D_MODEL, 128), lambda i: (0, 0)),
                  pl.BlockSpec((1, 128), lambda i: (0, 0))],
        out_specs=pl.BlockSpec((tm, 128), lambda i: (i, 0)),
        out_shape=jax.ShapeDtypeStruct((t_rows, 128), F32),
        compiler_params=_cparams(1),
        name="router",
    )(h, wh, wl, b)


def _expert_ffn(x, w_row, wg, wu, wd):
    xb = x.astype(BF16)
    hg = _dot(xb, wg)
    hu = _dot(xb, wu)
    hh = (hg * _sigmoid(hg)) * hu * w_row
    return _dot(hh.astype(BF16), wd)


def _moe_kernel(nv_ref, tok_ref, dst_ref, te_ref, h_hbm, w_ref, wg_ref, wu_ref, wd_ref, y_hbm,
                xbuf, ybuf, gsem, ssem):
    del te_ref
    tm = MOE_TILE
    i = pl.program_id(0)
    nv = nv_ref[0]
    slot = i % 2
    other = 1 - slot

    def gather_copy(tile, row, buf_slot):
        src = h_hbm.at[pl.ds(tok_ref[tile * tm + row], 1)]
        return pltpu.make_async_copy(src, xbuf.at[buf_slot, pl.ds(row, 1)], gsem.at[buf_slot])

    def scatter_copy(tile, row, buf_slot):
        dst = y_hbm.at[pl.ds(dst_ref[tile * tm + row], 1)]
        return pltpu.make_async_copy(ybuf.at[buf_slot, pl.ds(row, 1)], dst, ssem.at[buf_slot])

    def wait_gather(buf_slot):
        pltpu.make_async_copy(h_hbm.at[pl.ds(0, tm)], xbuf.at[buf_slot], gsem.at[buf_slot]).wait()

    def wait_scatter(buf_slot):
        pltpu.make_async_copy(ybuf.at[buf_slot], y_hbm.at[pl.ds(0, tm)], ssem.at[buf_slot]).wait()

    def compute(buf_slot):
        w_row = jnp.concatenate([w_ref[...], w_ref[...]], axis=1)
        ybuf[buf_slot] = _expert_ffn(xbuf[buf_slot], w_row, wg_ref[0], wu_ref[0], wd_ref[0])

    @pl.when(i == 0)
    def _():
        def body(r, c):
            gather_copy(0, r, 0).start()
            return c
        lax.fori_loop(0, tm, body, 0)
        ybuf[1] = jnp.zeros((tm, D_MODEL), F32)
        fill = pltpu.make_async_copy(ybuf.at[1], y_hbm.at[pl.ds(y_hbm.shape[0] - tm, tm)], ssem.at[1])
        fill.start()
        fill.wait()
        wait_gather(0)
        for r in range(tm):
            gather_copy(1, r, 1).start()
        compute(0)

    @pl.when((i > 0) & (i < nv))
    def _():
        wait_gather(slot)

        @pl.when(i >= 2)
        def _():
            wait_scatter(slot)

        for r in range(tm):
            gather_copy(i + 1, r, other).start()
        for r in range(tm):
            scatter_copy(i - 1, r, other).start()
        compute(slot)

    @pl.when(i == nv)
    def _():
        wait_gather(slot)

        @pl.when(i >= 2)
        def _():
            wait_scatter(slot)

        def body(r, c):
            scatter_copy(i - 1, r, other).start()
            return c
        lax.fori_loop(0, tm, body, 0)
        wait_scatter(other)


def _moe(nv, tok, dst, te, h, w_sorted, wg, wu, wd, n_out_rows):
    n_tiles = tok.shape[0] // MOE_TILE
    tm = MOE_TILE
    ex = lambda i, nv, tok, dst, te: (te[i], 0, 0)
    return pl.pallas_call(
        _moe_kernel,
        grid_spec=pltpu.PrefetchScalarGridSpec(
            num_scalar_prefetch=4,
            grid=(n_tiles,),
            in_specs=[pl.BlockSpec(memory_space=pl.ANY),
                      pl.BlockSpec((tm, 128), lambda i, nv, tok, dst, te: (i, 0)),
                      pl.BlockSpec((1, D_MODEL, MOE_FF), ex),
                      pl.BlockSpec((1, D_MODEL, MOE_FF), ex),
                      pl.BlockSpec((1, MOE_FF, D_MODEL), ex)],
            out_specs=pl.BlockSpec(memory_space=pl.ANY),
            scratch_shapes=[pltpu.VMEM((2, tm, D_MODEL), F32),
                            pltpu.VMEM((2, tm, D_MODEL), F32),
                            pltpu.SemaphoreType.DMA((2,)),
                            pltpu.SemaphoreType.DMA((2,))]),
        out_shape=jax.ShapeDtypeStruct((n_out_rows, D_MODEL), F32),
        compiler_params=_cparams(1),
        name="moe_experts",
    )(nv, tok, dst, te, h, w_sorted, wg, wu, wd)


MOE_LN_TILE = 256


def _moe_ln_kernel(h_ref, y0_ref, y1_ref, g_ref, b_ref, o_ref):
    o_ref[...] = _layer_norm(DN_ALPHA * h_ref[...] + (y0_ref[...] + y1_ref[...]), g_ref[...], b_ref[...])


def _moe_ln(h, y2, g, b):
    t_rows = h.shape[0]
    tm = MOE_LN_TILE
    nb = t_rows // tm
    const = lambda i: (0, 0)
    return pl.pallas_call(
        _moe_ln_kernel,
        grid=(nb,),
        in_specs=[pl.BlockSpec((tm, D_MODEL), lambda i: (i, 0)),
                  pl.BlockSpec((tm, D_MODEL), lambda i: (i, 0)),
                  pl.BlockSpec((tm, D_MODEL), lambda i: (nb + i, 0)),
                  pl.BlockSpec((1, D_MODEL), const),
                  pl.BlockSpec((1, D_MODEL), const)],
        out_specs=pl.BlockSpec((tm, D_MODEL), lambda i: (i, 0)),
        out_shape=jax.ShapeDtypeStruct((t_rows, D_MODEL), F32),
        compiler_params=_cparams(1),
        name="moe_ln",
    )(h, y2, y2, g, b)


def _hier_moe_ln(h, w_rg, b_rg, w_re, b_re, w_gate, w_up, w_down, ln_g, ln_b):
    t_rows = h.shape[0]
    n_exp = MOE_GROUPS * MOE_EPG
    wr = jnp.concatenate([w_rg.astype(F32), w_re.astype(F32).reshape(D_MODEL, n_exp)], axis=1)
    wr = jnp.pad(wr, ((0, 0), (0, 128 - wr.shape[1])))
    wrh, wrl = _split(wr)
    br = jnp.concatenate([b_rg.astype(F32), b_re.astype(F32).reshape(-1)])
    br = jnp.pad(br, (0, 128 - br.shape[0])).reshape(1, 128)
    rinfo = _router(h, wrh, wrl, br)

    n_asg = 2 * t_rows
    tm = MOE_TILE
    e_a = rinfo[:, 0:2].astype(jnp.int32).reshape(n_asg)
    w_a = rinfo[:, 2:4].reshape(n_asg)
    asg = jnp.arange(n_asg, dtype=jnp.int32)
    onehot = (e_a[:, None] == jnp.arange(n_exp, dtype=jnp.int32)[None, :]).astype(jnp.int32)
    rank = jnp.cumsum(onehot, axis=0) - onehot
    counts = jnp.sum(onehot, axis=0)
    tiles = (counts + tm - 1) // tm
    tile_end = jnp.cumsum(tiles)
    tile_start = tile_end - tiles
    pos = jnp.sum(onehot * (rank + tile_start[None, :] * tm), axis=1)
    n_tiles = n_asg // tm + n_exp + 1
    n_sorted = n_tiles * tm
    nv = tile_end[-1:].astype(jnp.int32)
    tok = jnp.zeros((n_sorted,), jnp.int32).at[pos].set(asg >> 1)
    dump = n_asg + jnp.arange(n_sorted, dtype=jnp.int32) % tm
    dst = dump.at[pos].set((asg & 1) * t_rows + (asg >> 1))
    w_sorted = jnp.zeros((n_sorted,), F32).at[pos].set(w_a)
    w_sorted = jnp.broadcast_to(w_sorted[:, None], (n_sorted, 128))
    te = jnp.minimum(jnp.sum(jnp.arange(n_tiles)[:, None] >= tile_end[None, :], axis=1), n_exp - 1).astype(jnp.int32)

    y2 = _moe(nv, tok, dst, te, h, w_sorted,
              w_gate.astype(BF16).reshape(n_exp, D_MODEL, MOE_FF),
              w_up.astype(BF16).reshape(n_exp, D_MODEL, MOE_FF),
              w_down.astype(BF16).reshape(n_exp, MOE_FF, D_MODEL), n_asg + tm)
    return _moe_ln(h, y2, ln_g.reshape(1, D_MODEL).astype(F32), ln_b.reshape(1, D_MODEL).astype(F32))


def _layer_ab(h, batch, lp, w_in, s5, w_glu, b_glu, gla_w_gate, gla_b_gate, gla_norm, w_out, ln_g, ln_b):
    hk = GLA_HEADS * GLA_DK
    hv = GLA_HEADS * GLA_DV
    o_q = S5_WIDTH
    o_g = o_q + 2 * hk + hv
    o_r = o_g + GLA_RANK
    w_in = w_in.astype(F32)
    w_u = w_in[:, :S5_WIDTH].astype(BF16)
    w_main = jnp.concatenate([w_in[:, o_q:o_g], w_in[:, o_r:o_r + hv],
                              jnp.pad(w_in[:, o_g:o_r], ((0, 0), (0, 128 - GLA_RANK)))], axis=1).astype(BF16)
    u = _s5_uproj(h, w_u)
    z = _s5_core(u, s5, batch)
    p_s5 = _s5_out(z, w_glu.astype(BF16), b_glu.reshape(1, -1).astype(F32), w_out[:S5_WIDTH].astype(BF16))
    proj = _matmul(h, w_main, tn=640)
    wg = jnp.pad(gla_w_gate.astype(F32), ((0, 128 - GLA_RANK), (0, 0)))
    wgh, wgl = _split(wg)
    y_gla = _gla(proj, wgh, wgl, gla_b_gate.reshape(1, -1).astype(F32), gla_norm.reshape(1, -1).astype(F32),
                 batch, lp)
    return _out_ln(h, p_s5, [(y_gla, w_out[S5_WIDTH:].astype(BF16))],
                   ln_g.reshape(1, -1).astype(F32), ln_b.reshape(1, -1).astype(F32))


def _layer_cd(h, batch, lp, w_in, conv_w, a_log, dt_bias, gdn_norm, lru_conv_w, lru_conv_b,
              lru_w_a, lru_b_a, lru_w_x, lru_b_x, lru_lambda, w_out, ln_g, ln_b):
    nq = GDN_QKV
    hv = GDN_HEADS * GDN_DV
    w_in = w_in.astype(F32)
    o_z = nq
    o_b = o_z + hv
    o_a = o_b + GDN_HEADS
    o_x = o_a + GDN_HEADS
    o_gt = o_x + LRU_WIDTH
    w_main = jnp.concatenate([w_in[:, :o_b], w_in[:, o_x:o_gt + LRU_WIDTH],
                              jnp.pad(w_in[:, o_b:o_x], ((0, 0), (0, 128 - 2 * GDN_HEADS)))], axis=1).astype(BF16)
    proj = _matmul(h, w_main, tn=896)

    cw = jnp.pad(conv_w.astype(F32), ((0, 8 - CONV_K), (0, 0)))
    hp = jnp.zeros((8, 128), F32)
    hp = hp.at[0, GDN_HEADS:2 * GDN_HEADS].set(-jnp.exp(a_log.astype(F32)))
    hp = hp.at[1, GDN_HEADS:2 * GDN_HEADS].set(dt_bias.astype(F32))
    y_gdn = _gdn(proj, cw, hp, gdn_norm.reshape(1, -1).astype(F32), batch, lp)

    lcw = jnp.pad(lru_conv_w.astype(F32), ((0, 8 - CONV_K), (0, 0)))
    wax = jnp.concatenate([lru_w_a.astype(F32), lru_w_x.astype(F32)], axis=2).astype(BF16)
    bax = jnp.concatenate([lru_b_a.astype(F32).reshape(LRU_BLOCKS, 1, LRU_BW),
                           lru_b_x.astype(F32).reshape(LRU_BLOCKS, 1, LRU_BW)], axis=2)
    sp = (-LRU_C * jax.nn.softplus(-lru_lambda.astype(F32))).reshape(1, LRU_WIDTH)
    y_lru = _lru(proj, lcw, lru_conv_b.reshape(1, -1).astype(F32), wax, bax, sp, batch, lp)

    return _out_ln(h, None, [(y_gdn, w_out[:hv].astype(BF16)), (y_lru, w_out[hv:].astype(BF16))],
                   ln_g.reshape(1, -1).astype(F32), ln_b.reshape(1, -1).astype(F32))


def kernel(x, meta_tokens, ab_w_in, ab_s5_a_re, ab_s5_a_im, ab_s5_log_dt, ab_s5_b_re, ab_s5_b_im,
           ab_s5_c_re, ab_s5_c_im, ab_s5_d, ab_s5_w_glu, ab_s5_b_glu, ab_gla_w_gate, ab_gla_b_gate,
           ab_gla_norm, ab_w_out, cd_w_in, cd_conv_w, cd_gdn_a_log, cd_gdn_dt_bias, cd_gdn_norm,
           cd_lru_conv_w, cd_lru_conv_b, cd_lru_w_a, cd_lru_b_a, cd_lru_w_x, cd_lru_b_x, cd_lru_lambda,
           cd_w_out, moe_w_router_g, moe_b_router_g, moe_w_router_e, moe_b_router_e, moe_w_gate,
           moe_w_up, moe_w_down, ln_mix_g, ln_mix_b, ln_ffn_g, ln_ffn_b):
    batch, seq, _ = x.shape
    length = seq + N_META
    lp = -(-length // ROW_ALIGN) * ROW_ALIGN
    meta = jnp.broadcast_to(meta_tokens.astype(x.dtype)[None], (batch, N_META, D_MODEL))
    h = jnp.concatenate([meta, x, jnp.zeros((batch, lp - length, D_MODEL), x.dtype)], axis=1)
    h = h.reshape(batch * lp, D_MODEL).astype(F32)

    for layer in range(DEPTH):
        j = layer // 2
        if layer % 2 == 0:
            s5 = _s5_params(ab_s5_a_re[j], ab_s5_a_im[j], ab_s5_log_dt[j], ab_s5_b_re[j], ab_s5_b_im[j],
                            ab_s5_c_re[j], ab_s5_c_im[j], ab_s5_d[j])
            h = _layer_ab(h, batch, lp, ab_w_in[j], s5, ab_s5_w_glu[j], ab_s5_b_glu[j], ab_gla_w_gate[j],
                          ab_gla_b_gate[j], ab_gla_norm[j], ab_w_out[j], ln_mix_g[layer], ln_mix_b[layer])
        else:
            h = _layer_cd(h, batch, lp, cd_w_in[j], cd_conv_w[j], cd_gdn_a_log[j], cd_gdn_dt_bias[j],
                          cd_gdn_norm[j], cd_lru_conv_w[j], cd_lru_conv_b[j], cd_lru_w_a[j], cd_lru_b_a[j],
                          cd_lru_w_x[j], cd_lru_b_x[j], cd_lru_lambda[j], cd_w_out[j],
                          ln_mix_g[layer], ln_mix_b[layer])
        h = _hier_moe_ln(h, moe_w_router_g[layer], moe_b_router_g[layer], moe_w_router_e[layer],
                         moe_b_router_e[layer], moe_w_gate[layer], moe_w_up[layer], moe_w_down[layer],
                         ln_ffn_g[layer], ln_ffn_b[layer])
    return h.reshape(batch, lp, D_MODEL)[:, N_META:length].astype(x.dtype)
```

```python
import functools
import math

import jax
import jax.numpy as jnp
from jax import lax
from jax.experimental import pallas as pl
from jax.experimental.pallas import tpu as pltpu

F32 = jnp.float32
BF16 = jnp.bfloat16
HIGHEST = lax.Precision.HIGHEST

D_MODEL = 2048
N_META = 16
CONV_K = 4
DEPTH = 2
DN_ALPHA = (2.0 * DEPTH) ** 0.25
LN_EPS = 1e-5
NORM_EPS = 1e-6

S5_WIDTH = 1024
S5_GROUP = 16
S5_GROUPS = 64
S5_STATE = 64
S5_LC = 16
S5_SLABS = 8

GLA_HEADS = 4
GLA_DK = 128
GLA_DV = 256
GLA_RANK = 16
GLA_TAU = 16.0

GDN_HEADS = 8
GDN_DK = 128
GDN_DV = 128

LRU_WIDTH = 1024
LRU_BLOCKS = 8
LRU_BW = 128
LRU_C = 8.0

MOE_GROUPS = 4
MOE_EPG = 8
MOE_FF = 256

LANE_BLOCKS = D_MODEL // 128
CHUNK = 64
ROW_ALIGN = 768
VMEM_LIMIT = 56 * 1024 * 1024

NT_DIMS = (((1,), (1,)), ((), ()))
TN_DIMS = (((0,), (0,)), ((), ()))


def _cparams(n_axes):
    return pltpu.CompilerParams(dimension_semantics=("arbitrary",) * n_axes,
                                vmem_limit_bytes=VMEM_LIMIT)


def _dot(a, b):
    return jnp.dot(a, b, preferred_element_type=F32)


def _dotg(a, b, dims):
    return lax.dot_general(a, b, dims, preferred_element_type=F32)


def _split(a):
    hi = a.astype(BF16)
    lo = (a - hi.astype(F32)).astype(BF16)
    return hi, lo


def _sigmoid(x):
    return 1.0 / (1.0 + jnp.exp(-x))


def _softplus(x):
    return jnp.maximum(x, 0.0) + jnp.log1p(jnp.exp(-jnp.abs(x)))


def _gelu(x):
    return 0.5 * x * (1.0 + jnp.tanh(math.sqrt(2.0 / math.pi) * (x + 0.044715 * (x * x * x))))


def _layer_norm(x, g, b):
    mu = jnp.mean(x, axis=-1, keepdims=True)
    xc = x - mu
    var = jnp.mean(xc * xc, axis=-1, keepdims=True)
    return xc * lax.rsqrt(var + LN_EPS) * g + b


def _mm_kernel(x_ref, w_ref, o_ref, xb_ref):
    @pl.when(pl.program_id(1) == 0)
    def _():
        xb_ref[...] = x_ref[...].astype(BF16)

    o_ref[...] = _dot(xb_ref[...], w_ref[...]).astype(o_ref.dtype)


def _matmul(x, w, tn, tm=768, out_dtype=F32):
    m, k = x.shape
    n = w.shape[1]
    return pl.pallas_call(
        _mm_kernel,
        grid=(m // tm, n // tn),
        in_specs=[pl.BlockSpec((tm, k), lambda i, j: (i, 0)),
                  pl.BlockSpec((k, tn), lambda i, j: (0, j))],
        out_specs=pl.BlockSpec((tm, tn), lambda i, j: (i, j)),
        out_shape=jax.ShapeDtypeStruct((m, n), out_dtype),
        scratch_shapes=[pltpu.VMEM((tm, k), BF16)],
        compiler_params=_cparams(2),
        name="in_proj",
    )(x, w)


S5_TILE_CHUNKS = 32


def _s5_row_perm():
    m = S5_TILE_CHUNKS
    r = jnp.arange(S5_LC * m)
    nat = (r % m) * S5_LC + r // m
    return (nat[:, None] == jnp.arange(S5_LC * m)[None, :]).astype(BF16)


def _s5_uproj_kernel(h_ref, perm_ref, w_ref, u_ref):
    m = S5_TILE_CHUNKS
    lhs = _dot(perm_ref[...], h_ref[...].astype(BF16)).astype(BF16)
    z = _dot(lhs, w_ref[...])
    for t in range(S5_LC):
        for s in range(S5_SLABS):
            u_ref[s, :, t * 128:(t + 1) * 128] = z[t * m:(t + 1) * m, s * 128:(s + 1) * 128]


def _s5_uproj(h, w_u):
    t_rows = h.shape[0]
    n_chunks = t_rows // S5_LC
    m = S5_TILE_CHUNKS
    return pl.pallas_call(
        _s5_uproj_kernel,
        grid=(n_chunks // m,),
        in_specs=[pl.BlockSpec((m * S5_LC, D_MODEL), lambda i: (i, 0)),
                  pl.BlockSpec((m * S5_LC, m * S5_LC), lambda i: (0, 0)),
                  pl.BlockSpec((D_MODEL, S5_WIDTH), lambda i: (0, 0))],
        out_specs=pl.BlockSpec((S5_SLABS, m, S5_LC * 128), lambda i: (0, i, 0)),
        out_shape=jax.ShapeDtypeStruct((S5_SLABS, n_chunks, S5_LC * 128), F32),
        compiler_params=_cparams(1),
        name="s5_uproj",
    )(h, _s5_row_perm(), w_u)


def _s5_core_kernel(u_ref, stack_ref, ms_ref, mo_ref, ak_ref, pre_ref, pim_ref, d_ref, z_ref,
                    sre_ref, sim_ref, xre_ref, xim_ref):
    n = u_ref.shape[1]
    half = 8 * S5_STATE
    u = u_ref[0]
    ub = u.astype(BF16)
    s = _dot(ub, ms_ref[0])
    sre_ref[...] = s[:, :half]
    sim_ref[...] = s[:, half:]

    ak = ak_ref[0]
    pre = pre_ref[0]
    pim = pim_ref[0]
    row = lax.broadcasted_iota(jnp.int32, (8, half), 0)

    def body(rb, carry):
        cre, cim = carry
        r0 = pl.multiple_of(rb * 8, 8)
        xr = sre_ref[pl.ds(r0, 8), :]
        xi = sim_ref[pl.ds(r0, 8), :]
        for idx, k in enumerate((1, 2, 4)):
            are = ak[2 * idx:2 * idx + 1, :]
            aim = ak[2 * idx + 1:2 * idx + 2, :]
            keep = row >= k
            shr = jnp.where(keep, pltpu.roll(xr, k, 0), 0.0)
            shi = jnp.where(keep, pltpu.roll(xi, k, 0), 0.0)
            xr, xi = xr + are * shr - aim * shi, xi + are * shi + aim * shr
        xr = xr + pre * cre - pim * cim
        xi = xi + pre * cim + pim * cre
        first = row == 0
        xre_ref[pl.ds(r0, 8), :] = jnp.where(first, cre, pltpu.roll(xr, 1, 0))
        xim_ref[pl.ds(r0, 8), :] = jnp.where(first, cim, pltpu.roll(xi, 1, 0))
        return xr[7:8, :], xi[7:8, :]

    zero = jnp.zeros((1, half), F32)
    lax.fori_loop(0, n // 8, body, (zero, zero))

    xprev = jnp.concatenate([xre_ref[...], xim_ref[...]], axis=1).astype(BF16)
    y_inter = _dot(xprev, mo_ref[0])
    d = d_ref[0]
    npair = S5_LC // 2
    for q in range(npair):
        yq = _dot(ub[:, :(q + 1) * 256], stack_ref[0, (npair - 1 - q) * 256:, :])
        sl = slice(q * 256, (q + 1) * 256)
        y = yq + y_inter[:, sl] + d[:, sl] * u[:, sl]
        z_ref[0, :, sl] = _gelu(y)


def _s5_core(u, prm, batch):
    n_tot = u.shape[1]
    n = n_tot // batch
    w = S5_LC * 128
    half = 8 * S5_STATE
    slab = lambda s, b: (s, 0, 0)
    return pl.pallas_call(
        _s5_core_kernel,
        grid=(S5_SLABS, batch),
        in_specs=[pl.BlockSpec((1, n, w), lambda s, b: (s, b, 0)),
                  pl.BlockSpec((1, w, 256), slab),
                  pl.BlockSpec((1, w, 2 * half), slab),
                  pl.BlockSpec((1, 2 * half, w), slab),
                  pl.BlockSpec((1, 8, half), slab),
                  pl.BlockSpec((1, 8, half), slab),
                  pl.BlockSpec((1, 8, half), slab),
                  pl.BlockSpec((1, 1, w), slab)],
        out_specs=pl.BlockSpec((1, n, w), lambda s, b: (s, b, 0)),
        out_shape=jax.ShapeDtypeStruct(u.shape, F32),
        scratch_shapes=[pltpu.VMEM((n, half), F32)] * 4,
        compiler_params=_cparams(2),
        name="s5_core",
    )(u, prm["stack"], prm["ms"], prm["mo"], prm["ak"], prm["pre"], prm["pim"], prm["d"])


def _s5_out_kernel(z_ref, perm_ref, wglu_ref, bglu_ref, wout_ref, p_ref, zf_ref):
    m = S5_TILE_CHUNKS
    for t in range(S5_LC):
        for s in range(S5_SLABS):
            zf_ref[t * m:(t + 1) * m, s * 128:(s + 1) * 128] = z_ref[s, :, t * 128:(t + 1) * 128]
    zf = zf_ref[...]
    gate = _sigmoid(_dot(zf.astype(BF16), wglu_ref[...]) + bglu_ref[...])
    gated = (zf * gate).astype(BF16)
    nat = _dotg(perm_ref[...], gated, TN_DIMS).astype(BF16)
    p_ref[...] = _dot(nat, wout_ref[...])


def _s5_out(z, w_glu, b_glu, w_out_top):
    n_chunks = z.shape[1]
    m = S5_TILE_CHUNKS
    return pl.pallas_call(
        _s5_out_kernel,
        grid=(n_chunks // m,),
        in_specs=[pl.BlockSpec((S5_SLABS, m, S5_LC * 128), lambda i: (0, i, 0)),
                  pl.BlockSpec((m * S5_LC, m * S5_LC), lambda i: (0, 0)),
                  pl.BlockSpec((S5_WIDTH, S5_WIDTH), lambda i: (0, 0)),
                  pl.BlockSpec((1, S5_WIDTH), lambda i: (0, 0)),
                  pl.BlockSpec((S5_WIDTH, D_MODEL), lambda i: (0, 0))],
        out_specs=pl.BlockSpec((m * S5_LC, D_MODEL), lambda i: (i, 0)),
        out_shape=jax.ShapeDtypeStruct((n_chunks * S5_LC, D_MODEL), F32),
        scratch_shapes=[pltpu.VMEM((S5_LC * m, S5_WIDTH), F32)],
        compiler_params=_cparams(1),
        name="s5_out",
    )(z, _s5_row_perm(), w_glu, b_glu, w_out_top)


def _s5_params(a_re, a_im, log_dt, b_re, b_im, c_re, c_im, d):
    g, p, c, lc = S5_GROUPS, S5_STATE, S5_GROUP, S5_LC
    a = lax.complex(a_re.astype(F32), a_im.astype(F32))
    adt = a * jnp.exp(log_dt.astype(F32))[:, None]
    zoh = (jnp.exp(adt) - 1.0) / a
    bz = lax.complex(b_re.astype(F32), b_im.astype(F32)) * zoh[:, :, None]
    cc = lax.complex(c_re.astype(F32), c_im.astype(F32))
    jj = jnp.arange(lc + 1, dtype=F32)
    apow = jnp.exp(adt[None] * jj[:, None, None])
    eye8 = jnp.eye(8, dtype=F32)

    kre = jnp.einsum("gop,jgp,gpc->jgco", cc, apow[:lc], bz, precision=HIGHEST).real
    kg = kre.reshape(lc, S5_SLABS, 8, c, c)
    kblk = jnp.einsum("jsgco,gh->jsgcho", kg, eye8).reshape(lc, S5_SLABS, 128, 128)
    kpad = jnp.concatenate([jnp.zeros_like(kblk[:1]), kblk], axis=0)
    npair = lc // 2
    dd = jnp.arange(npair)[:, None, None]
    lo_in = jnp.arange(2)[None, :, None]
    lo_out = jnp.arange(2)[None, None, :]
    lag = 2 * dd + lo_out - lo_in
    tiles = kpad[lag + 1]
    tiles = tiles.transpose(3, 0, 1, 4, 2, 5).reshape(S5_SLABS, npair, 256, 256)
    stack = tiles[:, ::-1].reshape(S5_SLABS, npair * 256, 256)

    wst = apow[lc - 1 - jnp.arange(lc)][:, :, :, None] * bz[None]
    wst = wst.reshape(lc, S5_SLABS, 8, p, c)
    ms_c = jnp.einsum("tsgpc,gh->stgchp", wst, lax.complex(eye8, jnp.zeros_like(eye8)))
    ms_c = ms_c.reshape(S5_SLABS, lc * 128, 8 * p)
    ms = jnp.concatenate([ms_c.real, ms_c.imag], axis=2)

    zc = cc[None] * apow[1:lc + 1][:, :, None, :]
    zc = zc.reshape(lc, S5_SLABS, 8, c, p)
    mo_c = jnp.einsum("tsgop,gh->sgptho", zc, lax.complex(eye8, jnp.zeros_like(eye8)))
    mo_c = mo_c.reshape(S5_SLABS, 8 * p, lc * 128)
    mo = jnp.concatenate([mo_c.real, -mo_c.imag], axis=1)

    def slab_layout(x):
        return x.reshape(x.shape[:-2] + (S5_SLABS, 8 * p))

    kk = jnp.array([1.0, 2.0, 4.0], F32)
    a_k = slab_layout(jnp.exp(adt[None] * (lc * kk)[:, None, None]))
    ak = jnp.stack([a_k[0].real, a_k[0].imag, a_k[1].real, a_k[1].imag, a_k[2].real, a_k[2].imag,
                    jnp.zeros_like(a_k[0].real), jnp.zeros_like(a_k[0].real)], axis=1)
    rr = jnp.arange(1, 9, dtype=F32)
    a_r = slab_layout(jnp.exp(adt[None] * (lc * rr)[:, None, None]))
    pre = a_r.real.transpose(1, 0, 2)
    pim = a_r.imag.transpose(1, 0, 2)
    dsk = jnp.tile(d.astype(F32).reshape(S5_SLABS, 1, 128), (1, 1, lc))
    return {"stack": stack.astype(BF16), "ms": ms.astype(BF16), "mo": mo.astype(BF16),
            "ak": ak, "pre": pre, "pim": pim, "d": dsk}


GLA_TILE = 256


def _gla_kernel(q_ref, k_ref, v_ref, r_ref, gl_ref, wgh_ref, wgl_ref, bg_ref, ng_ref, o_ref, st_ref):
    @pl.when(pl.program_id(1) == 0)
    def _():
        st_ref[...] = jnp.zeros_like(st_ref)

    rowi = lax.broadcasted_iota(jnp.int32, (CHUNK, CHUNK), 0)
    coli = lax.broadcasted_iota(jnp.int32, (CHUNK, CHUNK), 1)
    tri = rowi >= coli
    tri_b = jnp.where(tri, 1.0, 0.0).astype(BF16)
    wgh = wgh_ref[...]
    wgl = wgl_ref[...]
    bg = bg_ref[...]
    ng = ng_ref[...]
    scale = GLA_DK ** -0.5

    def chunk(c, carry):
        r0 = pl.multiple_of(c * CHUNK, CHUNK)
        rows = pl.ds(r0, CHUNK)
        glh, gll = _split(gl_ref[rows, :])
        pre = _dot(glh, wgh) + _dot(gll, wgh) + _dot(glh, wgl) + bg
        logf = (jnp.minimum(pre, 0.0) - jnp.log1p(jnp.exp(-jnp.abs(pre)))) * (1.0 / GLA_TAU)
        lh, ll = _split(logf)
        bcum = _dot(tri_b, lh) + _dot(tri_b, ll)
        blast = bcum[CHUNK - 1:CHUNK, :]
        kk = k_ref[rows, :]
        q_in = q_ref[rows, :] * scale * jnp.exp(bcum)
        k_in = kk * jnp.exp(-bcum)
        k_st = kk * jnp.exp(blast - bcum)
        dec = jnp.exp(blast)
        heads = range(GLA_HEADS)
        sks = [slice(h * GLA_DK, (h + 1) * GLA_DK) for h in heads]
        svs = [slice(h * GLA_DV, (h + 1) * GLA_DV) for h in heads]
        qh = [q_in[:, sks[h]].astype(BF16) for h in heads]
        vh = [v_ref[rows, svs[h]].astype(BF16) for h in heads]
        st_old = [st_ref[h] for h in heads]
        att = [_dotg(qh[h], k_in[:, sks[h]].astype(BF16), NT_DIMS) for h in heads]
        o_inter = [_dotg(qh[h], st_old[h].astype(BF16), NT_DIMS) for h in heads]
        kv = [_dotg(vh[h], k_st[:, sks[h]].astype(BF16), TN_DIMS) for h in heads]
        o_intra = [_dot(jnp.where(tri, att[h], 0.0).astype(BF16), vh[h]) for h in heads]
        for h in heads:
            o = o_intra[h] + o_inter[h]
            on = o * lax.rsqrt(jnp.mean(o * o, axis=1, keepdims=True) + NORM_EPS) * ng
            rr = r_ref[rows, svs[h]]
            o_ref[rows, svs[h]] = (on * (rr * _sigmoid(rr))).astype(o_ref.dtype)
            st_ref[h] = st_old[h] * dec[:, sks[h]] + kv[h]
        return carry

    lax.fori_loop(0, GLA_TILE // CHUNK, chunk, 0)


def _gla(proj, wgh, wgl, bg, ng, batch, lp):
    t_rows = proj.shape[0]
    nt = lp // GLA_TILE
    hk = GLA_HEADS * GLA_DK
    hv = GLA_HEADS * GLA_DV
    rowmap = lambda col: (lambda b, t: (b * nt + t, col))
    const = lambda b, t: (0, 0)
    return pl.pallas_call(
        _gla_kernel,
        grid=(batch, nt),
        in_specs=[pl.BlockSpec((GLA_TILE, hk), rowmap(0)),
                  pl.BlockSpec((GLA_TILE, hk), rowmap(1)),
                  pl.BlockSpec((GLA_TILE, hv), rowmap(1)),
                  pl.BlockSpec((GLA_TILE, hv), rowmap(2)),
                  pl.BlockSpec((GLA_TILE, 128), rowmap(24)),
                  pl.BlockSpec((128, hk), const),
                  pl.BlockSpec((128, hk), const),
                  pl.BlockSpec((1, hk), const),
                  pl.BlockSpec((1, GLA_DV), const)],
        out_specs=pl.BlockSpec((GLA_TILE, hv), rowmap(0)),
        out_shape=jax.ShapeDtypeStruct((t_rows, hv), BF16),
        scratch_shapes=[pltpu.VMEM((GLA_HEADS, GLA_DV, GLA_DK), F32)],
        compiler_params=_cparams(2),
        name="gla",
    )(proj, proj, proj, proj, proj, wgh, wgl, bg, ng)


GDN_TILE = 256
GDN_QKV = 3 * GDN_HEADS * GDN_DK


def _gdn_kernel(qkv_ref, z_ref, ba_ref, cw_ref, hp_ref, ng_ref, o_ref,
                ext_ref, act_ref, beta_ref, g_ref, s_ref):
    tile = GDN_TILE

    @pl.when(pl.program_id(1) == 0)
    def _():
        ext_ref[0:8, :] = jnp.zeros((8, GDN_QKV), F32)
        s_ref[...] = jnp.zeros_like(s_ref)

    @pl.when(pl.program_id(1) > 0)
    def _():
        ext_ref[0:8, :] = ext_ref[tile:tile + 8, :]

    ext_ref[8:tile + 8, :] = qkv_ref[...]
    qscale = GDN_DK ** -0.5
    for cb in range(GDN_QKV // 128):
        cols = slice(cb * 128, (cb + 1) * 128)
        w = cw_ref[:, cols]
        y = (w[3:4] * ext_ref[8:tile + 8, cols] + w[2:3] * ext_ref[7:tile + 7, cols]
             + w[1:2] * ext_ref[6:tile + 6, cols] + w[0:1] * ext_ref[5:tile + 5, cols])
        a = y * _sigmoid(y)
        if cb < 2 * GDN_HEADS:
            a = a * lax.rsqrt(jnp.sum(a * a, axis=1, keepdims=True) + NORM_EPS)
            if cb < GDN_HEADS:
                a = a * qscale
        act_ref[:, cols] = a

    ba = ba_ref[...]
    beta_ref[...] = _sigmoid(ba)
    g_ref[...] = hp_ref[0:1, :] * _softplus(ba + hp_ref[1:2, :])

    rowi = lax.broadcasted_iota(jnp.int32, (CHUNK, CHUNK), 0)
    coli = lax.broadcasted_iota(jnp.int32, (CHUNK, CHUNK), 1)
    tri = rowi >= coli
    strict = rowi > coli
    tri_b = jnp.where(tri, 1.0, 0.0).astype(BF16)
    upper_b = jnp.where(rowi <= coli, 1.0, 0.0).astype(BF16)
    eye = jnp.where(rowi == coli, 1.0, 0.0)
    level_masks = []
    for lvl in range(6):
        bi = rowi >> lvl
        bj = coli >> lvl
        level_masks.append(((bi & 1) == 1) & (bj == bi - 1))
    ng = ng_ref[...]

    def chunk(c, carry):
        r0 = pl.multiple_of(c * CHUNK, CHUNK)
        rows = pl.ds(r0, CHUNK)
        gh, gl = _split(g_ref[rows, :])
        gc = _dot(tri_b, gh) + _dot(tri_b, gl)
        gct = _dotg(gh, upper_b, TN_DIMS) + _dotg(gl, upper_b, TN_DIMS)
        eg = jnp.exp(gc)
        elast = jnp.exp(gc[CHUNK - 1:CHUNK, :] - gc)
        dlast = jnp.exp(gc[CHUNK - 1:CHUNK, :])
        beta = beta_ref[rows, :]
        heads = range(GDN_HEADS)
        hcol = lambda t, h: t[:, GDN_HEADS + h:GDN_HEADS + h + 1]
        qs = [act_ref[rows, h * GDN_DK:(h + 1) * GDN_DK] for h in heads]
        ks = [act_ref[rows, (GDN_HEADS + h) * GDN_DK:(GDN_HEADS + h + 1) * GDN_DK] for h in heads]
        vs = [act_ref[rows, (2 * GDN_HEADS + h) * GDN_DK:(2 * GDN_HEADS + h + 1) * GDN_DK] for h in heads]
        bcols = [beta[:, h:h + 1] for h in heads]
        kbs = [ks[h] * bcols[h] for h in heads]
        khb = [ks[h].astype(BF16) for h in heads]
        gams = [jnp.exp(jnp.where(tri, hcol(gc, h) - gct[GDN_HEADS + h:GDN_HEADS + h + 1, :], -1e30)) for h in heads]
        kk = [_dotg(kbs[h].astype(BF16), khb[h], NT_DIMS) for h in heads]
        qk = [_dotg(qs[h].astype(BF16), khb[h], NT_DIMS) for h in heads]
        mm = [jnp.where(strict, kk[h] * gams[h], 0.0) for h in heads]
        tinv = [eye - jnp.where(level_masks[0], mm[h], 0.0) for h in heads]
        for lvl in range(1, 6):
            tb = [tinv[h].astype(BF16) for h in heads]
            xs = [_dot(jnp.where(level_masks[lvl], mm[h], 0.0).astype(BF16), tb[h]).astype(BF16) for h in heads]
            tinv = [tinv[h] - _dot(tb[h], xs[h]) for h in heads]
        rhs = [jnp.concatenate([vs[h] * bcols[h], kbs[h] * hcol(eg, h)], axis=1).astype(BF16) for h in heads]
        sol = [_dot(tinv[h].astype(BF16), rhs[h]) for h in heads]
        s_old = [s_ref[h] for h in heads]
        sb = [s_old[h].astype(BF16) for h in heads]
        ws = [_dot(sol[h][:, GDN_DV:].astype(BF16), sb[h]) for h in heads]
        qsd = [_dot((qs[h] * hcol(eg, h)).astype(BF16), sb[h]) for h in heads]
        vnb = [(sol[h][:, :GDN_DV] - ws[h]).astype(BF16) for h in heads]
        av = [_dot(jnp.where(tri, qk[h] * gams[h], 0.0).astype(BF16), vnb[h]) for h in heads]
        kv = [_dotg((ks[h] * hcol(elast, h)).astype(BF16), vnb[h], TN_DIMS) for h in heads]
        for h in heads:
            o = qsd[h] + av[h]
            on = o * lax.rsqrt(jnp.mean(o * o, axis=1, keepdims=True) + NORM_EPS) * ng
            zz = z_ref[rows, h * GDN_DV:(h + 1) * GDN_DV]
            o_ref[rows, h * GDN_DV:(h + 1) * GDN_DV] = (on * (zz * _sigmoid(zz))).astype(o_ref.dtype)
            s_ref[h] = s_old[h] * hcol(dlast, h) + kv[h]
        return carry

    lax.fori_loop(0, tile // CHUNK, chunk, 0)


def _gdn(proj, cw, hp, ng, batch, lp):
    t_rows = proj.shape[0]
    nt = lp // GDN_TILE
    hv = GDN_HEADS * GDN_DV
    rowmap = lambda col: (lambda b, t: (b * nt + t, col))
    const = lambda b, t: (0, 0)
    return pl.pallas_call(
        _gdn_kernel,
        grid=(batch, nt),
        in_specs=[pl.BlockSpec((GDN_TILE, GDN_QKV), rowmap(0)),
                  pl.BlockSpec((GDN_TILE, hv), rowmap(3)),
                  pl.BlockSpec((GDN_TILE, 128), rowmap(48)),
                  pl.BlockSpec((8, GDN_QKV), const),
                  pl.BlockSpec((8, 128), const),
                  pl.BlockSpec((1, GDN_DV), const)],
        out_specs=pl.BlockSpec((GDN_TILE, hv), rowmap(0)),
        out_shape=jax.ShapeDtypeStruct((t_rows, hv), BF16),
        scratch_shapes=[pltpu.VMEM((GDN_TILE + 8, GDN_QKV), F32),
                        pltpu.VMEM((GDN_TILE, GDN_QKV), F32),
                        pltpu.VMEM((GDN_TILE, 128), F32),
                        pltpu.VMEM((GDN_TILE, 128), F32),
                        pltpu.VMEM((GDN_HEADS, GDN_DK, GDN_DV), F32)],
        compiler_params=_cparams(2),
        name="gdn",
    )(proj, proj, proj, cw, hp, ng)


LRU_TILE = 256


def _lru_kernel(x_ref, gate_ref, cw_ref, cb_ref, wax_ref, bax_ref, sp_ref, o_ref,
                ext_ref, a_ref, b_ref, h_ref):
    tile = LRU_TILE

    @pl.when(pl.program_id(1) == 0)
    def _():
        ext_ref[0:8, :] = jnp.zeros((8, LRU_WIDTH), F32)
        h_ref[...] = jnp.zeros_like(h_ref)

    @pl.when(pl.program_id(1) > 0)
    def _():
        ext_ref[0:8, :] = ext_ref[tile:tile + 8, :]

    ext_ref[8:tile + 8, :] = x_ref[...]
    for blk in range(LRU_BLOCKS):
        cols = slice(blk * LRU_BW, (blk + 1) * LRU_BW)
        w = cw_ref[:, cols]
        xc = (w[3:4] * ext_ref[8:tile + 8, cols] + w[2:3] * ext_ref[7:tile + 7, cols]
              + w[1:2] * ext_ref[6:tile + 6, cols] + w[0:1] * ext_ref[5:tile + 5, cols]
              + cb_ref[:, cols])
        pre = _dot(xc.astype(BF16), wax_ref[blk]) + bax_ref[blk]
        r = _sigmoid(pre[:, :LRU_BW])
        i = _sigmoid(pre[:, LRU_BW:])
        log_a = sp_ref[:, cols] * r
        a_ref[:, cols] = jnp.exp(log_a)
        th = jnp.tanh(log_a)
        b_ref[:, cols] = jnp.sqrt(-2.0 * th / (1.0 - th)) * (i * xc)

    row = lax.broadcasted_iota(jnp.int32, (8, LRU_WIDTH), 0)

    def body(rb, carry):
        r0 = pl.multiple_of(rb * 8, 8)
        aa = a_ref[pl.ds(r0, 8), :]
        bb = b_ref[pl.ds(r0, 8), :]
        for k in (1, 2, 4):
            keep = row >= k
            a_sh = jnp.where(keep, pltpu.roll(aa, k, 0), 1.0)
            b_sh = jnp.where(keep, pltpu.roll(bb, k, 0), 0.0)
            bb = aa * b_sh + bb
            aa = aa * a_sh
        hs = bb + aa * carry
        gt = gate_ref[pl.ds(r0, 8), :]
        o_ref[pl.ds(r0, 8), :] = (hs * _gelu(gt)).astype(o_ref.dtype)
        return hs[7:8, :]

    h_ref[...] = lax.fori_loop(0, tile // 8, body, h_ref[...])


def _lru(proj, cw, cb, wax, bax, sp, batch, lp):
    t_rows = proj.shape[0]
    nt = lp // LRU_TILE
    rowmap = lambda col: (lambda b, t: (b * nt + t, col))
    const = lambda b, t: (0, 0)
    return pl.pallas_call(
        _lru_kernel,
        grid=(batch, nt),
        in_specs=[pl.BlockSpec((LRU_TILE, LRU_WIDTH), rowmap(4)),
                  pl.BlockSpec((LRU_TILE, LRU_WIDTH), rowmap(5)),
                  pl.BlockSpec((8, LRU_WIDTH), const),
                  pl.BlockSpec((1, LRU_WIDTH), const),
                  pl.BlockSpec((LRU_BLOCKS, LRU_BW, 2 * LRU_BW), lambda b, t: (0, 0, 0)),
                  pl.BlockSpec((LRU_BLOCKS, 1, 2 * LRU_BW), lambda b, t: (0, 0, 0)),
                  pl.BlockSpec((1, LRU_WIDTH), const)],
        out_specs=pl.BlockSpec((LRU_TILE, LRU_WIDTH), rowmap(0)),
        out_shape=jax.ShapeDtypeStruct((t_rows, LRU_WIDTH), BF16),
        scratch_shapes=[pltpu.VMEM((LRU_TILE + 8, LRU_WIDTH), F32),
                        pltpu.VMEM((LRU_TILE, LRU_WIDTH), F32),
                        pltpu.VMEM((LRU_TILE, LRU_WIDTH), F32),
                        pltpu.VMEM((1, LRU_WIDTH), F32)],
        compiler_params=_cparams(2),
        name="lru",
    )(proj, proj, cw, cb, wax, bax, sp)


OUT_TILE = 256


def _out_ln_kernel(n_add, h_ref, *refs):
    adds = refs[:n_add]
    (a1_ref, w1_ref, a2_ref, w2_ref, g_ref, b_ref, wrh_ref, wrl_ref, br_ref,
     o_ref, r_ref, o3_ref) = refs[n_add:]
    acc = DN_ALPHA * h_ref[...]
    for r in adds:
        acc = acc + r[...]
    acc = acc + _dot(a1_ref[...], w1_ref[...])
    if a2_ref is not None:
        acc = acc + _dot(a2_ref[...], w2_ref[...])
    hn = _layer_norm(acc, g_ref[...], b_ref[...])
    o_ref[...] = hn
    tm = hn.shape[0]
    for c in range(LANE_BLOCKS):
        o3_ref[pl.ds(c, tm, stride=LANE_BLOCKS), :] = hn[:, c * 128:(c + 1) * 128]
    r_ref[...] = _route(hn, wrh_ref[...], wrl_ref[...], br_ref[...])


def _out_ln_kernel_1(h_ref, p_ref, a1_ref, w1_ref, *rest):
    _out_ln_kernel(1, h_ref, p_ref, a1_ref, w1_ref, None, None, *rest)


def _out_ln_kernel_2(h_ref, a1_ref, w1_ref, a2_ref, w2_ref, *rest):
    _out_ln_kernel(0, h_ref, a1_ref, w1_ref, a2_ref, w2_ref, *rest)


def _out_ln(h, addend, pairs, g, b, router):
    t_rows = h.shape[0]
    tm = OUT_TILE
    row = lambda i: (i, 0)
    const = lambda i: (0, 0)
    ins = [h]
    specs = [pl.BlockSpec((tm, D_MODEL), row)]
    if addend is not None:
        ins.append(addend)
        specs.append(pl.BlockSpec((tm, D_MODEL), row))
    for a, w in pairs:
        ins += [a, w]
        specs += [pl.BlockSpec((tm, a.shape[1]), row), pl.BlockSpec(w.shape, const)]
    ins += [g, b]
    specs += [pl.BlockSpec((1, D_MODEL), const)] * 2
    ins += list(router)
    specs += [pl.BlockSpec((D_MODEL, 128), const), pl.BlockSpec((D_MODEL, 128), const),
              pl.BlockSpec((1, 128), const)]
    body = _out_ln_kernel_1 if addend is not None else _out_ln_kernel_2
    return pl.pallas_call(
        body,
        grid=(t_rows // tm,),
        in_specs=specs,
        out_specs=[pl.BlockSpec((tm, D_MODEL), row), pl.BlockSpec((tm, 128), row),
                   pl.BlockSpec((tm * LANE_BLOCKS, 128), row)],
        out_shape=[jax.ShapeDtypeStruct((t_rows, D_MODEL), F32),
                   jax.ShapeDtypeStruct((t_rows, 128), F32),
                   jax.ShapeDtypeStruct((t_rows * LANE_BLOCKS, 128), F32)],
        compiler_params=_cparams(1),
        name="out_ln",
    )(*ins)


ROUTER_TILE = 768
MOE_TILE = 256
BIG = 1e30


def _route(h, wh, wl, b):
    hh, hl = _split(h)
    logits = _dot(hh, wh) + _dot(hl, wh) + _dot(hh, wl) + b
    lane = lax.broadcasted_iota(jnp.int32, logits.shape, 1).astype(F32)

    def first_argmax(vals, vmax):
        return jnp.min(jnp.where(vals == vmax, lane, 4096.0), axis=1, keepdims=True)

    is_g = lane < MOE_GROUPS
    lg = jnp.where(is_g, logits, -BIG)
    m = jnp.max(lg, axis=1, keepdims=True)
    gidx = first_argmax(lg, m)
    denom = jnp.sum(jnp.where(is_g, jnp.exp(logits - m), 0.0), axis=1, keepdims=True)
    p_top = 1.0 / denom
    base = MOE_GROUPS + MOE_EPG * gidx
    in_grp = (lane >= base) & (lane < base + MOE_EPG)
    le = jnp.where(in_grp, logits, -BIG)
    v1 = jnp.max(le, axis=1, keepdims=True)
    i1 = first_argmax(le, v1)
    le2 = jnp.where(lane == i1, -BIG, le)
    v2 = jnp.max(le2, axis=1, keepdims=True)
    i2 = first_argmax(le2, v2)
    e21 = jnp.exp(v2 - v1)
    w1 = p_top / (1.0 + e21)
    w2 = p_top * e21 / (1.0 + e21)
    return (jnp.where(lane == 0.0, i1 - MOE_GROUPS, 0.0) + jnp.where(lane == 1.0, i2 - MOE_GROUPS, 0.0)
            + jnp.where(lane == 2.0, w1, 0.0) + jnp.where(lane == 3.0, w2, 0.0))


def _router_weights(w_rg, b_rg, w_re, b_re):
    n_exp = MOE_GROUPS * MOE_EPG
    wr = jnp.concatenate([w_rg.astype(F32), w_re.astype(F32).reshape(D_MODEL, n_exp)], axis=1)
    wr = jnp.pad(wr, ((0, 0), (0, 128 - wr.shape[1])))
    wrh, wrl = _split(wr)
    br = jnp.concatenate([b_rg.astype(F32), b_re.astype(F32).reshape(-1)])
    br = jnp.pad(br, (0, 128 - br.shape[0])).reshape(1, 128)
    return wrh, wrl, br


def _moe_kernel(nv_ref, tok_ref, dst_ref, te_ref, h_hbm, wg_ref, wu_ref, wd_ref, y_hbm,
                xbuf, ybuf, xmat, wgb, wub, wdb, gsem, ssem):
    tm = MOE_TILE
    i = pl.program_id(0)
    nv = nv_ref[0]
    slot = i % 2
    other = 1 - slot

    lb = LANE_BLOCKS

    def buf_rows(row):
        r0 = row * lb
        return pl.ds(r0 if isinstance(row, int) else pl.multiple_of(r0, lb), lb)

    def gather_copy(tile, row, buf_slot):
        src = h_hbm.at[pl.ds(pl.multiple_of(tok_ref[tile * tm + row] * lb, lb), lb)]
        return pltpu.make_async_copy(src, xbuf.at[buf_slot, buf_rows(row)], gsem.at[buf_slot])

    def scatter_copy(tile, row, buf_slot):
        dst = y_hbm.at[pl.ds(pl.multiple_of(dst_ref[tile * tm + row] * lb, lb), lb)]
        return pltpu.make_async_copy(ybuf.at[buf_slot, buf_rows(row)], dst, ssem.at[buf_slot])

    def wait_gather(buf_slot):
        pltpu.make_async_copy(h_hbm.at[pl.ds(0, tm * lb)], xbuf.at[buf_slot], gsem.at[buf_slot]).wait()

    def wait_scatter(buf_slot):
        pltpu.make_async_copy(ybuf.at[buf_slot], y_hbm.at[pl.ds(0, tm * lb)], ssem.at[buf_slot]).wait()

    @pl.when((i == 0) | (te_ref[i] != te_ref[jnp.maximum(i - 1, 0)]))
    def _():
        wgb[...] = wg_ref[0].astype(BF16)
        wub[...] = wu_ref[0].astype(BF16)
        wdb[...] = wd_ref[0].astype(BF16)

    def compute(buf_slot):
        for c in range(lb):
            xmat[:, c * 128:(c + 1) * 128] = xbuf[buf_slot, pl.ds(c, tm, stride=lb), :]
        xb = xmat[...].astype(BF16)
        hg = _dot(xb, wgb[...])
        hu = _dot(xb, wub[...])
        hh = (hg * _sigmoid(hg)) * hu
        y = _dot(hh.astype(BF16), wdb[...])
        for c in range(lb):
            ybuf[buf_slot, pl.ds(c, tm, stride=lb), :] = y[:, c * 128:(c + 1) * 128]

    @pl.when(i == 0)
    def _():
        def body(r, c):
            gather_copy(0, r, 0).start()
            return c
        lax.fori_loop(0, tm, body, 0)
        ybuf[1] = jnp.zeros((tm * lb, 128), F32)
        fill = pltpu.make_async_copy(ybuf.at[1], y_hbm.at[pl.ds(y_hbm.shape[0] - tm * lb, tm * lb)], ssem.at[1])
        fill.start()
        fill.wait()
        wait_gather(0)
        for r in range(tm):
            gather_copy(1, r, 1).start()
        compute(0)

    @pl.when((i > 0) & (i < nv))
    def _():
        wait_gather(slot)

        @pl.when(i >= 2)
        def _():
            wait_scatter(slot)

        for r in range(tm):
            gather_copy(i + 1, r, other).start()
        for r in range(tm):
            scatter_copy(i - 1, r, other).start()
        compute(slot)

    @pl.when(i == nv)
    def _():
        wait_gather(slot)

        @pl.when(i >= 2)
        def _():
            wait_scatter(slot)

        def body(r, c):
            scatter_copy(i - 1, r, other).start()
            return c
        lax.fori_loop(0, tm, body, 0)
        wait_scatter(other)


def _moe(nv, tok, dst, te, h, wg, wu, wd, n_out_rows):
    n_tiles = tok.shape[0] // MOE_TILE
    tm = MOE_TILE
    ex = lambda i, nv, tok, dst, te: (te[i], 0, 0)
    return pl.pallas_call(
        _moe_kernel,
        grid_spec=pltpu.PrefetchScalarGridSpec(
            num_scalar_prefetch=4,
            grid=(n_tiles,),
            in_specs=[pl.BlockSpec(memory_space=pl.ANY),
                      pl.BlockSpec((1, D_MODEL, MOE_FF), ex),
                      pl.BlockSpec((1, D_MODEL, MOE_FF), ex),
                      pl.BlockSpec((1, MOE_FF, D_MODEL), ex)],
            out_specs=pl.BlockSpec(memory_space=pl.ANY),
            scratch_shapes=[pltpu.VMEM((2, tm * LANE_BLOCKS, 128), F32),
                            pltpu.VMEM((2, tm * LANE_BLOCKS, 128), F32),
                            pltpu.VMEM((tm, D_MODEL), F32),
                            pltpu.VMEM((D_MODEL, MOE_FF), BF16),
                            pltpu.VMEM((D_MODEL, MOE_FF), BF16),
                            pltpu.VMEM((MOE_FF, D_MODEL), BF16),
                            pltpu.SemaphoreType.DMA((2,)),
                            pltpu.SemaphoreType.DMA((2,))]),
        out_shape=jax.ShapeDtypeStruct((n_out_rows * LANE_BLOCKS, 128), F32),
        compiler_params=_cparams(1),
        name="moe_experts",
    )(nv, tok, dst, te, h, wg, wu, wd)


MOE_LN_TILE = 256


def _moe_ln_kernel(h_ref, y0_ref, y1_ref, r_ref, g_ref, b_ref, o_ref, acc_ref):
    r = r_ref[...]
    w0 = r[:, 2:3]
    w1 = r[:, 3:4]
    tm = h_ref.shape[0]
    for c in range(LANE_BLOCKS):
        cols = slice(c * 128, (c + 1) * 128)
        rows = pl.ds(c, tm, stride=LANE_BLOCKS)
        acc_ref[:, cols] = DN_ALPHA * h_ref[:, cols] + (w0 * y0_ref[rows, :] + w1 * y1_ref[rows, :])
    o_ref[...] = _layer_norm(acc_ref[...], g_ref[...], b_ref[...])


def _moe_ln(h, y2, rinfo, g, b):
    t_rows = h.shape[0]
    tm = MOE_LN_TILE
    nb = t_rows // tm
    const = lambda i: (0, 0)
    return pl.pallas_call(
        _moe_ln_kernel,
        grid=(nb,),
        in_specs=[pl.BlockSpec((tm, D_MODEL), lambda i: (i, 0)),
                  pl.BlockSpec((tm * LANE_BLOCKS, 128), lambda i: (i, 0)),
                  pl.BlockSpec((tm * LANE_BLOCKS, 128), lambda i: (nb + i, 0)),
                  pl.BlockSpec((tm, 128), lambda i: (i, 0)),
                  pl.BlockSpec((1, D_MODEL), const),
                  pl.BlockSpec((1, D_MODEL), const)],
        out_specs=pl.BlockSpec((tm, D_MODEL), lambda i: (i, 0)),
        out_shape=jax.ShapeDtypeStruct((t_rows, D_MODEL), F32),
        scratch_shapes=[pltpu.VMEM((tm, D_MODEL), F32)],
        compiler_params=_cparams(1),
        name="moe_ln",
    )(h, y2, y2, rinfo, g, b)


def _hier_moe_ln(h, rinfo, h3, w_gate, w_up, w_down, ln_g, ln_b):
    t_rows = h.shape[0]
    n_exp = MOE_GROUPS * MOE_EPG

    n_asg = 2 * t_rows
    tm = MOE_TILE
    e_a = rinfo[:, 0:2].astype(jnp.int32).reshape(n_asg)
    asg = jnp.arange(n_asg, dtype=jnp.int32)
    onehot = (e_a[:, None] == jnp.arange(n_exp, dtype=jnp.int32)[None, :]).astype(jnp.int32)
    rank = jnp.cumsum(onehot, axis=0) - onehot
    counts = jnp.sum(onehot, axis=0)
    tiles = (counts + tm - 1) // tm
    tile_end = jnp.cumsum(tiles)
    tile_start = tile_end - tiles
    pos = jnp.sum(onehot * (rank + tile_start[None, :] * tm), axis=1)
    n_tiles = n_asg // tm + n_exp + 1
    n_sorted = n_tiles * tm
    nv = tile_end[-1:].astype(jnp.int32)
    slot_asg = jnp.full((n_sorted,), -1, jnp.int32).at[pos].set(asg)
    is_pad = slot_asg < 0
    tok = jnp.where(is_pad, 0, slot_asg >> 1)
    dump = n_asg + jnp.arange(n_sorted, dtype=jnp.int32) % tm
    dst = jnp.where(is_pad, dump, (slot_asg & 1) * t_rows + (slot_asg >> 1))
    te = jnp.minimum(jnp.sum(jnp.arange(n_tiles)[:, None] >= tile_end[None, :], axis=1), n_exp - 1).astype(jnp.int32)

    y2 = _moe(nv, tok, dst, te, h3,
              w_gate.astype(F32).reshape(n_exp, D_MODEL, MOE_FF),
              w_up.astype(F32).reshape(n_exp, D_MODEL, MOE_FF),
              w_down.astype(F32).reshape(n_exp, MOE_FF, D_MODEL), n_asg + tm)
    return _moe_ln(h, y2, rinfo, ln_g.reshape(1, D_MODEL).astype(F32), ln_b.reshape(1, D_MODEL).astype(F32))


def _layer_ab(h, batch, lp, w_in, s5, w_glu, b_glu, gla_w_gate, gla_b_gate, gla_norm, w_out, ln_g, ln_b,
              router):
    hk = GLA_HEADS * GLA_DK
    hv = GLA_HEADS * GLA_DV
    o_q = S5_WIDTH
    o_g = o_q + 2 * hk + hv
    o_r = o_g + GLA_RANK
    w_in = w_in.astype(F32)
    w_u = w_in[:, :S5_WIDTH].astype(BF16)
    w_main = jnp.concatenate([w_in[:, o_q:o_g], w_in[:, o_r:o_r + hv],
                              jnp.pad(w_in[:, o_g:o_r], ((0, 0), (0, 128 - GLA_RANK)))], axis=1).astype(BF16)
    u = _s5_uproj(h, w_u)
    z = _s5_core(u, s5, batch)
    p_s5 = _s5_out(z, w_glu.astype(BF16), b_glu.reshape(1, -1).astype(F32), w_out[:S5_WIDTH].astype(BF16))
    proj = _matmul(h, w_main, tn=640)
    wg = jnp.pad(gla_w_gate.astype(F32), ((0, 128 - GLA_RANK), (0, 0)))
    wgh, wgl = _split(wg)
    y_gla = _gla(proj, wgh, wgl, gla_b_gate.reshape(1, -1).astype(F32), gla_norm.reshape(1, -1).astype(F32),
                 batch, lp)
    return _out_ln(h, p_s5, [(y_gla, w_out[S5_WIDTH:].astype(BF16))],
                   ln_g.reshape(1, -1).astype(F32), ln_b.reshape(1, -1).astype(F32), router)


def _layer_cd(h, batch, lp, w_in, conv_w, a_log, dt_bias, gdn_norm, lru_conv_w, lru_conv_b,
              lru_w_a, lru_b_a, lru_w_x, lru_b_x, lru_lambda, w_out, ln_g, ln_b, router):
    nq = GDN_QKV
    hv = GDN_HEADS * GDN_DV
    w_in = w_in.astype(F32)
    o_z = nq
    o_b = o_z + hv
    o_a = o_b + GDN_HEADS
    o_x = o_a + GDN_HEADS
    o_gt = o_x + LRU_WIDTH
    w_main = jnp.concatenate([w_in[:, :o_b], w_in[:, o_x:o_gt + LRU_WIDTH],
                              jnp.pad(w_in[:, o_b:o_x], ((0, 0), (0, 128 - 2 * GDN_HEADS)))], axis=1).astype(BF16)
    proj = _matmul(h, w_main, tn=896)

    cw = jnp.pad(conv_w.astype(F32), ((0, 8 - CONV_K), (0, 0)))
    hp = jnp.zeros((8, 128), F32)
    hp = hp.at[0, GDN_HEADS:2 * GDN_HEADS].set(-jnp.exp(a_log.astype(F32)))
    hp = hp.at[1, GDN_HEADS:2 * GDN_HEADS].set(dt_bias.astype(F32))
    y_gdn = _gdn(proj, cw, hp, gdn_norm.reshape(1, -1).astype(F32), batch, lp)

    lcw = jnp.pad(lru_conv_w.astype(F32), ((0, 8 - CONV_K), (0, 0)))
    wax = jnp.concatenate([lru_w_a.astype(F32), lru_w_x.astype(F32)], axis=2).astype(BF16)
    bax = jnp.concatenate([lru_b_a.astype(F32).reshape(LRU_BLOCKS, 1, LRU_BW),
                           lru_b_x.astype(F32).reshape(LRU_BLOCKS, 1, LRU_BW)], axis=2)
    sp = (-LRU_C * jax.nn.softplus(-lru_lambda.astype(F32))).reshape(1, LRU_WIDTH)
    y_lru = _lru(proj, lcw, lru_conv_b.reshape(1, -1).astype(F32), wax, bax, sp, batch, lp)

    return _out_ln(h, None, [(y_gdn, w_out[:hv].astype(BF16)), (y_lru, w_out[hv:].astype(BF16))],
                   ln_g.reshape(1, -1).astype(F32), ln_b.reshape(1, -1).astype(F32), router)


def kernel(x, meta_tokens, ab_w_in, ab_s5_a_re, ab_s5_a_im, ab_s5_log_dt, ab_s5_b_re, ab_s5_b_im,
           ab_s5_c_re, ab_s5_c_im, ab_s5_d, ab_s5_w_glu, ab_s5_b_glu, ab_gla_w_gate, ab_gla_b_gate,
           ab_gla_norm, ab_w_out, cd_w_in, cd_conv_w, cd_gdn_a_log, cd_gdn_dt_bias, cd_gdn_norm,
           cd_lru_conv_w, cd_lru_conv_b, cd_lru_w_a, cd_lru_b_a, cd_lru_w_x, cd_lru_b_x, cd_lru_lambda,
           cd_w_out, moe_w_router_g, moe_b_router_g, moe_w_router_e, moe_b_router_e, moe_w_gate,
           moe_w_up, moe_w_down, ln_mix_g, ln_mix_b, ln_ffn_g, ln_ffn_b):
    batch, seq, _ = x.shape
    length = seq + N_META
    lp = -(-length // ROW_ALIGN) * ROW_ALIGN
    meta = jnp.broadcast_to(meta_tokens.astype(x.dtype)[None], (batch, N_META, D_MODEL))
    h = jnp.concatenate([meta, x, jnp.zeros((batch, lp - length, D_MODEL), x.dtype)], axis=1)
    h = h.reshape(batch * lp, D_MODEL).astype(F32)

    for layer in range(DEPTH):
        j = layer // 2
        router = _router_weights(moe_w_router_g[layer], moe_b_router_g[layer], moe_w_router_e[layer],
                                 moe_b_router_e[layer])
        if layer % 2 == 0:
            s5 = _s5_params(ab_s5_a_re[j], ab_s5_a_im[j], ab_s5_log_dt[j], ab_s5_b_re[j], ab_s5_b_im[j],
                            ab_s5_c_re[j], ab_s5_c_im[j], ab_s5_d[j])
            h, rinfo, h3 = _layer_ab(h, batch, lp, ab_w_in[j], s5, ab_s5_w_glu[j], ab_s5_b_glu[j],
                                 ab_gla_w_gate[j], ab_gla_b_gate[j], ab_gla_norm[j], ab_w_out[j],
                                 ln_mix_g[layer], ln_mix_b[layer], router)
        else:
            h, rinfo, h3 = _layer_cd(h, batch, lp, cd_w_in[j], cd_conv_w[j], cd_gdn_a_log[j], cd_gdn_dt_bias[j],
                                 cd_gdn_norm[j], cd_lru_conv_w[j], cd_lru_conv_b[j], cd_lru_w_a[j],
                                 cd_lru_b_a[j], cd_lru_w_x[j], cd_lru_b_x[j], cd_lru_lambda[j], cd_w_out[j],
                                 ln_mix_g[layer], ln_mix_b[layer], router)
        h = _hier_moe_ln(h, rinfo, h3, moe_w_gate[layer], moe_w_up[layer], moe_w_down[layer],
                         ln_ffn_g[layer], ln_ffn_b[layer])
    return h.reshape(batch, lp, D_MODEL)[:, N_META:length].astype(x.dtype)
```

```python
import functools
import math

import jax
import jax.numpy as jnp
from jax import lax
from jax.experimental import pallas as pl
from jax.experimental.pallas import tpu as pltpu

F32 = jnp.float32
BF16 = jnp.bfloat16
HIGHEST = lax.Precision.HIGHEST

D_MODEL = 2048
N_META = 16
CONV_K = 4
DEPTH = 2
DN_ALPHA = (2.0 * DEPTH) ** 0.25
LN_EPS = 1e-5
NORM_EPS = 1e-6

S5_WIDTH = 1024
S5_GROUP = 16
S5_GROUPS = 64
S5_STATE = 64
S5_LC = 16
S5_SLABS = 8

GLA_HEADS = 4
GLA_DK = 128
GLA_DV = 256
GLA_RANK = 16
GLA_TAU = 16.0

GDN_HEADS = 8
GDN_DK = 128
GDN_DV = 128

LRU_WIDTH = 1024
LRU_BLOCKS = 8
LRU_BW = 128
LRU_C = 8.0

MOE_GROUPS = 4
MOE_EPG = 8
MOE_FF = 256

LANE_BLOCKS = D_MODEL // 128
CHUNK = 64
ROW_ALIGN = 768
VMEM_LIMIT = 56 * 1024 * 1024

NT_DIMS = (((1,), (1,)), ((), ()))
TN_DIMS = (((0,), (0,)), ((), ()))


def _cparams(n_axes):
    return pltpu.CompilerParams(dimension_semantics=("arbitrary",) * n_axes,
                                vmem_limit_bytes=VMEM_LIMIT)


def _dot(a, b):
    return jnp.dot(a, b, preferred_element_type=F32)


def _dotg(a, b, dims):
    return lax.dot_general(a, b, dims, preferred_element_type=F32)


def _split(a):
    hi = a.astype(BF16)
    lo = (a - hi.astype(F32)).astype(BF16)
    return hi, lo


def _sigmoid(x):
    return 1.0 / (1.0 + jnp.exp(-x))


def _softplus(x):
    return jnp.maximum(x, 0.0) + jnp.log1p(jnp.exp(-jnp.abs(x)))


def _gelu(x):
    return 0.5 * x * (1.0 + jnp.tanh(math.sqrt(2.0 / math.pi) * (x + 0.044715 * (x * x * x))))


def _layer_norm(x, g, b):
    mu = jnp.mean(x, axis=-1, keepdims=True)
    xc = x - mu
    var = jnp.mean(xc * xc, axis=-1, keepdims=True)
    return xc * lax.rsqrt(var + LN_EPS) * g + b


PROJ_TILE = 512
PROJ_COLS = 1024


def _in_proj_kernel(x_ref, w_ref, wg_ref, o_ref, g_ref):
    xb = x_ref[...].astype(BF16)
    for j in range(w_ref.shape[1] // PROJ_COLS):
        cols = slice(j * PROJ_COLS, (j + 1) * PROJ_COLS)
        o_ref[:, cols] = _dot(xb, w_ref[:, cols]).astype(o_ref.dtype)
    g_ref[...] = _dot(xb, wg_ref[...])


def _in_proj(x, w, w_gates):
    m, k = x.shape
    n = w.shape[1]
    tm = PROJ_TILE
    const = lambda i: (0, 0)
    return pl.pallas_call(
        _in_proj_kernel,
        grid=(m // tm,),
        in_specs=[pl.BlockSpec((tm, k), lambda i: (i, 0)),
                  pl.BlockSpec((k, n), const, pipeline_mode=pl.Buffered(1)),
                  pl.BlockSpec((k, 128), const, pipeline_mode=pl.Buffered(1))],
        out_specs=[pl.BlockSpec((tm, n), lambda i: (i, 0)),
                   pl.BlockSpec((tm, 128), lambda i: (i, 0))],
        out_shape=[jax.ShapeDtypeStruct((m, n), BF16),
                   jax.ShapeDtypeStruct((m, 128), F32)],
        compiler_params=_cparams(1),
        name="in_proj",
    )(x, w, w_gates)


S5_TILE_CHUNKS = 32


def _s5_row_perm():
    m = S5_TILE_CHUNKS
    r = jnp.arange(S5_LC * m)
    nat = (r % m) * S5_LC + r // m
    return (nat[:, None] == jnp.arange(S5_LC * m)[None, :]).astype(BF16)


def _s5_uproj_kernel(h_ref, perm_ref, w_ref, u_ref):
    m = S5_TILE_CHUNKS
    lhs = _dot(perm_ref[...], h_ref[...].astype(BF16)).astype(BF16)
    z = _dot(lhs, w_ref[...])
    for t in range(S5_LC):
        for s in range(S5_SLABS):
            u_ref[s, :, t * 128:(t + 1) * 128] = z[t * m:(t + 1) * m, s * 128:(s + 1) * 128]


def _s5_uproj(h, w_u):
    t_rows = h.shape[0]
    n_chunks = t_rows // S5_LC
    m = S5_TILE_CHUNKS
    return pl.pallas_call(
        _s5_uproj_kernel,
        grid=(n_chunks // m,),
        in_specs=[pl.BlockSpec((m * S5_LC, D_MODEL), lambda i: (i, 0)),
                  pl.BlockSpec((m * S5_LC, m * S5_LC), lambda i: (0, 0)),
                  pl.BlockSpec((D_MODEL, S5_WIDTH), lambda i: (0, 0))],
        out_specs=pl.BlockSpec((S5_SLABS, m, S5_LC * 128), lambda i: (0, i, 0)),
        out_shape=jax.ShapeDtypeStruct((S5_SLABS, n_chunks, S5_LC * 128), F32),
        compiler_params=_cparams(1),
        name="s5_uproj",
    )(h, _s5_row_perm(), w_u)


def _s5_core_kernel(u_ref, stack_ref, ms_ref, mo_ref, ak_ref, pre_ref, pim_ref, d_ref, z_ref,
                    sre_ref, sim_ref, xre_ref, xim_ref):
    n = u_ref.shape[1]
    half = 8 * S5_STATE
    u = u_ref[0]
    ub = u.astype(BF16)
    s = _dot(ub, ms_ref[0])
    sre_ref[...] = s[:, :half]
    sim_ref[...] = s[:, half:]

    ak = ak_ref[0]
    pre = pre_ref[0]
    pim = pim_ref[0]
    row = lax.broadcasted_iota(jnp.int32, (8, half), 0)

    def body(rb, carry):
        cre, cim = carry
        r0 = pl.multiple_of(rb * 8, 8)
        xr = sre_ref[pl.ds(r0, 8), :]
        xi = sim_ref[pl.ds(r0, 8), :]
        for idx, k in enumerate((1, 2, 4)):
            are = ak[2 * idx:2 * idx + 1, :]
            aim = ak[2 * idx + 1:2 * idx + 2, :]
            keep = row >= k
            shr = jnp.where(keep, pltpu.roll(xr, k, 0), 0.0)
            shi = jnp.where(keep, pltpu.roll(xi, k, 0), 0.0)
            xr, xi = xr + are * shr - aim * shi, xi + are * shi + aim * shr
        xr = xr + pre * cre - pim * cim
        xi = xi + pre * cim + pim * cre
        first = row == 0
        xre_ref[pl.ds(r0, 8), :] = jnp.where(first, cre, pltpu.roll(xr, 1, 0))
        xim_ref[pl.ds(r0, 8), :] = jnp.where(first, cim, pltpu.roll(xi, 1, 0))
        return xr[7:8, :], xi[7:8, :]

    zero = jnp.zeros((1, half), F32)
    lax.fori_loop(0, n // 8, body, (zero, zero))

    xprev = jnp.concatenate([xre_ref[...], xim_ref[...]], axis=1).astype(BF16)
    y_inter = _dot(xprev, mo_ref[0])
    d = d_ref[0]
    npair = S5_LC // 2
    for q in range(npair):
        yq = _dot(ub[:, :(q + 1) * 256], stack_ref[0, (npair - 1 - q) * 256:, :])
        sl = slice(q * 256, (q + 1) * 256)
        y = yq + y_inter[:, sl] + d[:, sl] * u[:, sl]
        z_ref[0, :, sl] = _gelu(y)


def _s5_core(u, prm, batch):
    n_tot = u.shape[1]
    n = n_tot // batch
    w = S5_LC * 128
    half = 8 * S5_STATE
    slab = lambda s, b: (s, 0, 0)
    return pl.pallas_call(
        _s5_core_kernel,
        grid=(S5_SLABS, batch),
        in_specs=[pl.BlockSpec((1, n, w), lambda s, b: (s, b, 0)),
                  pl.BlockSpec((1, w, 256), slab),
                  pl.BlockSpec((1, w, 2 * half), slab),
                  pl.BlockSpec((1, 2 * half, w), slab),
                  pl.BlockSpec((1, 8, half), slab),
                  pl.BlockSpec((1, 8, half), slab),
                  pl.BlockSpec((1, 8, half), slab),
                  pl.BlockSpec((1, 1, w), slab)],
        out_specs=pl.BlockSpec((1, n, w), lambda s, b: (s, b, 0)),
        out_shape=jax.ShapeDtypeStruct(u.shape, F32),
        scratch_shapes=[pltpu.VMEM((n, half), F32)] * 4,
        compiler_params=_cparams(2),
        name="s5_core",
    )(u, prm["stack"], prm["ms"], prm["mo"], prm["ak"], prm["pre"], prm["pim"], prm["d"])


def _s5_out_kernel(z_ref, perm_ref, wglu_ref, bglu_ref, wout_ref, p_ref, zf_ref):
    m = S5_TILE_CHUNKS
    for t in range(S5_LC):
        for s in range(S5_SLABS):
            zf_ref[t * m:(t + 1) * m, s * 128:(s + 1) * 128] = z_ref[s, :, t * 128:(t + 1) * 128]
    zf = zf_ref[...]
    gate = _sigmoid(_dot(zf.astype(BF16), wglu_ref[...]) + bglu_ref[...])
    gated = (zf * gate).astype(BF16)
    nat = _dotg(perm_ref[...], gated, TN_DIMS).astype(BF16)
    p_ref[...] = _dot(nat, wout_ref[...])


def _s5_out(z, w_glu, b_glu, w_out_top):
    n_chunks = z.shape[1]
    m = S5_TILE_CHUNKS
    return pl.pallas_call(
        _s5_out_kernel,
        grid=(n_chunks // m,),
        in_specs=[pl.BlockSpec((S5_SLABS, m, S5_LC * 128), lambda i: (0, i, 0)),
                  pl.BlockSpec((m * S5_LC, m * S5_LC), lambda i: (0, 0)),
                  pl.BlockSpec((S5_WIDTH, S5_WIDTH), lambda i: (0, 0)),
                  pl.BlockSpec((1, S5_WIDTH), lambda i: (0, 0)),
                  pl.BlockSpec((S5_WIDTH, D_MODEL), lambda i: (0, 0))],
        out_specs=pl.BlockSpec((m * S5_LC, D_MODEL), lambda i: (i, 0)),
        out_shape=jax.ShapeDtypeStruct((n_chunks * S5_LC, D_MODEL), F32),
        scratch_shapes=[pltpu.VMEM((S5_LC * m, S5_WIDTH), F32)],
        compiler_params=_cparams(1),
        name="s5_out",
    )(z, _s5_row_perm(), w_glu, b_glu, w_out_top)


def _s5_params(a_re, a_im, log_dt, b_re, b_im, c_re, c_im, d):
    g, p, c, lc = S5_GROUPS, S5_STATE, S5_GROUP, S5_LC
    a = lax.complex(a_re.astype(F32), a_im.astype(F32))
    adt = a * jnp.exp(log_dt.astype(F32))[:, None]
    zoh = (jnp.exp(adt) - 1.0) / a
    bz = lax.complex(b_re.astype(F32), b_im.astype(F32)) * zoh[:, :, None]
    cc = lax.complex(c_re.astype(F32), c_im.astype(F32))
    jj = jnp.arange(lc + 1, dtype=F32)
    apow = jnp.exp(adt[None] * jj[:, None, None])
    eye8 = jnp.eye(8, dtype=F32)

    kre = jnp.einsum("gop,jgp,gpc->jgco", cc, apow[:lc], bz, precision=HIGHEST).real
    kg = kre.reshape(lc, S5_SLABS, 8, c, c)
    kblk = jnp.einsum("jsgco,gh->jsgcho", kg, eye8).reshape(lc, S5_SLABS, 128, 128)
    kpad = jnp.concatenate([jnp.zeros_like(kblk[:1]), kblk], axis=0)
    npair = lc // 2
    dd = jnp.arange(npair)[:, None, None]
    lo_in = jnp.arange(2)[None, :, None]
    lo_out = jnp.arange(2)[None, None, :]
    lag = 2 * dd + lo_out - lo_in
    tiles = kpad[lag + 1]
    tiles = tiles.transpose(3, 0, 1, 4, 2, 5).reshape(S5_SLABS, npair, 256, 256)
    stack = tiles[:, ::-1].reshape(S5_SLABS, npair * 256, 256)

    wst = apow[lc - 1 - jnp.arange(lc)][:, :, :, None] * bz[None]
    wst = wst.reshape(lc, S5_SLABS, 8, p, c)
    ms_c = jnp.einsum("tsgpc,gh->stgchp", wst, lax.complex(eye8, jnp.zeros_like(eye8)))
    ms_c = ms_c.reshape(S5_SLABS, lc * 128, 8 * p)
    ms = jnp.concatenate([ms_c.real, ms_c.imag], axis=2)

    zc = cc[None] * apow[1:lc + 1][:, :, None, :]
    zc = zc.reshape(lc, S5_SLABS, 8, c, p)
    mo_c = jnp.einsum("tsgop,gh->sgptho", zc, lax.complex(eye8, jnp.zeros_like(eye8)))
    mo_c = mo_c.reshape(S5_SLABS, 8 * p, lc * 128)
    mo = jnp.concatenate([mo_c.real, -mo_c.imag], axis=1)

    def slab_layout(x):
        return x.reshape(x.shape[:-2] + (S5_SLABS, 8 * p))

    kk = jnp.array([1.0, 2.0, 4.0], F32)
    a_k = slab_layout(jnp.exp(adt[None] * (lc * kk)[:, None, None]))
    ak = jnp.stack([a_k[0].real, a_k[0].imag, a_k[1].real, a_k[1].imag, a_k[2].real, a_k[2].imag,
                    jnp.zeros_like(a_k[0].real), jnp.zeros_like(a_k[0].real)], axis=1)
    rr = jnp.arange(1, 9, dtype=F32)
    a_r = slab_layout(jnp.exp(adt[None] * (lc * rr)[:, None, None]))
    pre = a_r.real.transpose(1, 0, 2)
    pim = a_r.imag.transpose(1, 0, 2)
    dsk = jnp.tile(d.astype(F32).reshape(S5_SLABS, 1, 128), (1, 1, lc))
    return {"stack": stack.astype(BF16), "ms": ms.astype(BF16), "mo": mo.astype(BF16),
            "ak": ak, "pre": pre, "pim": pim, "d": dsk}


GLA_TILE = 256


def _gla_kernel(q_ref, k_ref, v_ref, r_ref, gl_ref, wgh_ref, wgl_ref, bg_ref, ng_ref, o_ref, st_ref):
    @pl.when(pl.program_id(1) == 0)
    def _():
        st_ref[...] = jnp.zeros_like(st_ref)

    rowi = lax.broadcasted_iota(jnp.int32, (CHUNK, CHUNK), 0)
    coli = lax.broadcasted_iota(jnp.int32, (CHUNK, CHUNK), 1)
    tri = rowi >= coli
    tri_b = jnp.where(tri, 1.0, 0.0).astype(BF16)
    wgh = wgh_ref[...]
    wgl = wgl_ref[...]
    bg = bg_ref[...]
    ng = ng_ref[...]
    scale = GLA_DK ** -0.5

    def chunk(c, carry):
        r0 = pl.multiple_of(c * CHUNK, CHUNK)
        rows = pl.ds(r0, CHUNK)
        glh, gll = _split(gl_ref[rows, :])
        pre = _dot(glh, wgh) + _dot(gll, wgh) + _dot(glh, wgl) + bg
        logf = (jnp.minimum(pre, 0.0) - jnp.log1p(jnp.exp(-jnp.abs(pre)))) * (1.0 / GLA_TAU)
        lh, ll = _split(logf)
        bcum = _dot(tri_b, lh) + _dot(tri_b, ll)
        blast = bcum[CHUNK - 1:CHUNK, :]
        kk = k_ref[rows, :].astype(F32)
        q_in = q_ref[rows, :].astype(F32) * scale * jnp.exp(bcum)
        k_in = kk * jnp.exp(-bcum)
        k_st = kk * jnp.exp(blast - bcum)
        dec = jnp.exp(blast)
        heads = range(GLA_HEADS)
        sks = [slice(h * GLA_DK, (h + 1) * GLA_DK) for h in heads]
        svs = [slice(h * GLA_DV, (h + 1) * GLA_DV) for h in heads]
        qh = [q_in[:, sks[h]].astype(BF16) for h in heads]
        vh = [v_ref[rows, svs[h]].astype(BF16) for h in heads]
        st_old = [st_ref[h] for h in heads]
        att = [_dotg(qh[h], k_in[:, sks[h]].astype(BF16), NT_DIMS) for h in heads]
        o_inter = [_dotg(qh[h], st_old[h].astype(BF16), NT_DIMS) for h in heads]
        kv = [_dotg(vh[h], k_st[:, sks[h]].astype(BF16), TN_DIMS) for h in heads]
        o_intra = [_dot(jnp.where(tri, att[h], 0.0).astype(BF16), vh[h]) for h in heads]
        for h in heads:
            o = o_intra[h] + o_inter[h]
            on = o * lax.rsqrt(jnp.mean(o * o, axis=1, keepdims=True) + NORM_EPS) * ng
            rr = r_ref[rows, svs[h]].astype(F32)
            o_ref[rows, svs[h]] = (on * (rr * _sigmoid(rr))).astype(o_ref.dtype)
            st_ref[h] = st_old[h] * dec[:, sks[h]] + kv[h]
        return carry

    lax.fori_loop(0, GLA_TILE // CHUNK, chunk, 0)


def _gla(proj, g_low, wgh, wgl, bg, ng, batch, lp):
    t_rows = proj.shape[0]
    nt = lp // GLA_TILE
    hk = GLA_HEADS * GLA_DK
    hv = GLA_HEADS * GLA_DV
    rowmap = lambda col: (lambda b, t: (b * nt + t, col))
    const = lambda b, t: (0, 0)
    return pl.pallas_call(
        _gla_kernel,
        grid=(batch, nt),
        in_specs=[pl.BlockSpec((GLA_TILE, hk), rowmap(0)),
                  pl.BlockSpec((GLA_TILE, hk), rowmap(1)),
                  pl.BlockSpec((GLA_TILE, hv), rowmap(1)),
                  pl.BlockSpec((GLA_TILE, hv), rowmap(2)),
                  pl.BlockSpec((GLA_TILE, 128), rowmap(0)),
                  pl.BlockSpec((128, hk), const),
                  pl.BlockSpec((128, hk), const),
                  pl.BlockSpec((1, hk), const),
                  pl.BlockSpec((1, GLA_DV), const)],
        out_specs=pl.BlockSpec((GLA_TILE, hv), rowmap(0)),
        out_shape=jax.ShapeDtypeStruct((t_rows, hv), BF16),
        scratch_shapes=[pltpu.VMEM((GLA_HEADS, GLA_DV, GLA_DK), F32)],
        compiler_params=_cparams(2),
        name="gla",
    )(proj, proj, proj, proj, g_low, wgh, wgl, bg, ng)


GDN_TILE = 256
GDN_GROUP = 4
GDN_QKV = 3 * GDN_HEADS * GDN_DK


def _gdn_kernel(qkv_ref, z_ref, ba_ref, cw_ref, hp_ref, ng_ref, o_ref,
                ext_ref, act_ref, beta_ref, g_ref, s_ref):
    tile = GDN_TILE

    @pl.when(pl.program_id(1) == 0)
    def _():
        ext_ref[0:8, :] = jnp.zeros((8, GDN_QKV), F32)
        s_ref[...] = jnp.zeros_like(s_ref)

    @pl.when(pl.program_id(1) > 0)
    def _():
        ext_ref[0:8, :] = ext_ref[tile:tile + 8, :]

    ext_ref[8:tile + 8, :] = qkv_ref[...].astype(F32)
    qscale = GDN_DK ** -0.5
    for cb in range(GDN_QKV // 128):
        cols = slice(cb * 128, (cb + 1) * 128)
        w = cw_ref[:, cols]
        y = (w[3:4] * ext_ref[8:tile + 8, cols] + w[2:3] * ext_ref[7:tile + 7, cols]
             + w[1:2] * ext_ref[6:tile + 6, cols] + w[0:1] * ext_ref[5:tile + 5, cols])
        a = y * _sigmoid(y)
        if cb < 2 * GDN_HEADS:
            a = a * lax.rsqrt(jnp.sum(a * a, axis=1, keepdims=True) + NORM_EPS)
            if cb < GDN_HEADS:
                a = a * qscale
        act_ref[:, cols] = a

    ba = ba_ref[...]
    beta_ref[...] = _sigmoid(ba)
    g_ref[...] = hp_ref[0:1, :] * _softplus(ba + hp_ref[1:2, :])

    rowi = lax.broadcasted_iota(jnp.int32, (CHUNK, CHUNK), 0)
    coli = lax.broadcasted_iota(jnp.int32, (CHUNK, CHUNK), 1)
    tri = rowi >= coli
    strict = rowi > coli
    tri_b = jnp.where(tri, 1.0, 0.0).astype(BF16)
    upper_b = jnp.where(rowi <= coli, 1.0, 0.0).astype(BF16)
    eye = jnp.where(rowi == coli, 1.0, 0.0)
    level_masks = []
    for lvl in range(6):
        bi = rowi >> lvl
        bj = coli >> lvl
        level_masks.append(((bi & 1) == 1) & (bj == bi - 1))
    ng = ng_ref[...]

    heads = range(GDN_HEADS)
    grp = range(GDN_GROUP)
    items = [(j, h) for j in grp for h in heads]
    hcol = lambda t, h: t[:, GDN_HEADS + h:GDN_HEADS + h + 1]

    def group(gi, carry):
        base = gi * (GDN_GROUP * CHUNK)
        rows = [pl.ds(pl.multiple_of(base + j * CHUNK, CHUNK), CHUNK) for j in grp]
        gsp = [_split(g_ref[rows[j], :]) for j in grp]
        gc = [_dot(tri_b, gsp[j][0]) + _dot(tri_b, gsp[j][1]) for j in grp]
        gct = [_dotg(gsp[j][0], upper_b, TN_DIMS) + _dotg(gsp[j][1], upper_b, TN_DIMS) for j in grp]
        eg = [jnp.exp(gc[j]) for j in grp]
        elast = [jnp.exp(gc[j][CHUNK - 1:CHUNK, :] - gc[j]) for j in grp]
        dlast = [jnp.exp(gc[j][CHUNK - 1:CHUNK, :]) for j in grp]
        beta = [beta_ref[rows[j], :] for j in grp]
        qs = {(j, h): act_ref[rows[j], h * GDN_DK:(h + 1) * GDN_DK] for j, h in items}
        ks = {(j, h): act_ref[rows[j], (GDN_HEADS + h) * GDN_DK:(GDN_HEADS + h + 1) * GDN_DK] for j, h in items}
        vs = {(j, h): act_ref[rows[j], (2 * GDN_HEADS + h) * GDN_DK:(2 * GDN_HEADS + h + 1) * GDN_DK]
              for j, h in items}
        bcols = {(j, h): beta[j][:, h:h + 1] for j, h in items}
        kbs = {it: ks[it] * bcols[it] for it in items}
        khb = {it: ks[it].astype(BF16) for it in items}
        gams = {(j, h): jnp.exp(jnp.where(tri, hcol(gc[j], h) - gct[j][GDN_HEADS + h:GDN_HEADS + h + 1, :], -1e30))
                for j, h in items}
        kk = {it: _dotg(kbs[it].astype(BF16), khb[it], NT_DIMS) for it in items}
        qk = {it: _dotg(qs[it].astype(BF16), khb[it], NT_DIMS) for it in items}
        mm = {it: jnp.where(strict, kk[it] * gams[it], 0.0) for it in items}
        tinv = {it: eye - jnp.where(level_masks[0], mm[it], 0.0) for it in items}
        for lvl in range(1, 6):
            tb = {it: tinv[it].astype(BF16) for it in items}
            xs = {it: _dot(jnp.where(level_masks[lvl], mm[it], 0.0).astype(BF16), tb[it]).astype(BF16)
                  for it in items}
            tinv = {it: tinv[it] - _dot(tb[it], xs[it]) for it in items}
        rhs = {(j, h): jnp.concatenate([vs[j, h] * bcols[j, h], kbs[j, h] * hcol(eg[j], h)], axis=1).astype(BF16)
               for j, h in items}
        sol = {it: _dot(tinv[it].astype(BF16), rhs[it]) for it in items}
        aqk = {it: jnp.where(tri, qk[it] * gams[it], 0.0).astype(BF16) for it in items}
        qdec = {(j, h): (qs[j, h] * hcol(eg[j], h)).astype(BF16) for j, h in items}
        kst = {(j, h): (ks[j, h] * hcol(elast[j], h)).astype(BF16) for j, h in items}
        for j in grp:
            s_old = [s_ref[h] for h in heads]
            sb = [s_old[h].astype(BF16) for h in heads]
            ws = [_dot(sol[j, h][:, GDN_DV:].astype(BF16), sb[h]) for h in heads]
            qsd = [_dot(qdec[j, h], sb[h]) for h in heads]
            vnb = [(sol[j, h][:, :GDN_DV] - ws[h]).astype(BF16) for h in heads]
            av = [_dot(aqk[j, h], vnb[h]) for h in heads]
            kv = [_dotg(kst[j, h], vnb[h], TN_DIMS) for h in heads]
            for h in heads:
                o = qsd[h] + av[h]
                on = o * lax.rsqrt(jnp.mean(o * o, axis=1, keepdims=True) + NORM_EPS) * ng
                zz = z_ref[rows[j], h * GDN_DV:(h + 1) * GDN_DV].astype(F32)
                o_ref[rows[j], h * GDN_DV:(h + 1) * GDN_DV] = (on * (zz * _sigmoid(zz))).astype(o_ref.dtype)
                s_ref[h] = s_old[h] * hcol(dlast[j], h) + kv[h]
        return carry

    lax.fori_loop(0, tile // (GDN_GROUP * CHUNK), group, 0)


def _gdn(proj, ba, cw, hp, ng, batch, lp):
    t_rows = proj.shape[0]
    nt = lp // GDN_TILE
    hv = GDN_HEADS * GDN_DV
    rowmap = lambda col: (lambda b, t: (b * nt + t, col))
    const = lambda b, t: (0, 0)
    return pl.pallas_call(
        _gdn_kernel,
        grid=(batch, nt),
        in_specs=[pl.BlockSpec((GDN_TILE, GDN_QKV), rowmap(0)),
                  pl.BlockSpec((GDN_TILE, hv), rowmap(3)),
                  pl.BlockSpec((GDN_TILE, 128), rowmap(0)),
                  pl.BlockSpec((8, GDN_QKV), const),
                  pl.BlockSpec((8, 128), const),
                  pl.BlockSpec((1, GDN_DV), const)],
        out_specs=pl.BlockSpec((GDN_TILE, hv), rowmap(0)),
        out_shape=jax.ShapeDtypeStruct((t_rows, hv), BF16),
        scratch_shapes=[pltpu.VMEM((GDN_TILE + 8, GDN_QKV), F32),
                        pltpu.VMEM((GDN_TILE, GDN_QKV), F32),
                        pltpu.VMEM((GDN_TILE, 128), F32),
                        pltpu.VMEM((GDN_TILE, 128), F32),
                        pltpu.VMEM((GDN_HEADS, GDN_DK, GDN_DV), F32)],
        compiler_params=_cparams(2),
        name="gdn",
    )(proj, proj, ba, cw, hp, ng)


LRU_TILE = 256


def _lru_kernel(x_ref, gate_ref, cw_ref, cb_ref, wax_ref, bax_ref, sp_ref, o_ref,
                ext_ref, a_ref, b_ref, h_ref):
    tile = LRU_TILE

    @pl.when(pl.program_id(1) == 0)
    def _():
        ext_ref[0:8, :] = jnp.zeros((8, LRU_WIDTH), F32)
        h_ref[...] = jnp.zeros_like(h_ref)

    @pl.when(pl.program_id(1) > 0)
    def _():
        ext_ref[0:8, :] = ext_ref[tile:tile + 8, :]

    ext_ref[8:tile + 8, :] = x_ref[...].astype(F32)
    for blk in range(LRU_BLOCKS):
        cols = slice(blk * LRU_BW, (blk + 1) * LRU_BW)
        w = cw_ref[:, cols]
        xc = (w[3:4] * ext_ref[8:tile + 8, cols] + w[2:3] * ext_ref[7:tile + 7, cols]
              + w[1:2] * ext_ref[6:tile + 6, cols] + w[0:1] * ext_ref[5:tile + 5, cols]
              + cb_ref[:, cols])
        pre = _dot(xc.astype(BF16), wax_ref[blk]) + bax_ref[blk]
        r = _sigmoid(pre[:, :LRU_BW])
        i = _sigmoid(pre[:, LRU_BW:])
        log_a = sp_ref[:, cols] * r
        a_ref[:, cols] = jnp.exp(log_a)
        th = jnp.tanh(log_a)
        b_ref[:, cols] = jnp.sqrt(-2.0 * th / (1.0 - th)) * (i * xc)

    row = lax.broadcasted_iota(jnp.int32, (8, LRU_WIDTH), 0)

    def body(rb, carry):
        r0 = pl.multiple_of(rb * 8, 8)
        aa = a_ref[pl.ds(r0, 8), :]
        bb = b_ref[pl.ds(r0, 8), :]
        for k in (1, 2, 4):
            keep = row >= k
            a_sh = jnp.where(keep, pltpu.roll(aa, k, 0), 1.0)
            b_sh = jnp.where(keep, pltpu.roll(bb, k, 0), 0.0)
            bb = aa * b_sh + bb
            aa = aa * a_sh
        hs = bb + aa * carry
        gt = gate_ref[pl.ds(r0, 8), :].astype(F32)
        o_ref[pl.ds(r0, 8), :] = (hs * _gelu(gt)).astype(o_ref.dtype)
        return hs[7:8, :]

    h_ref[...] = lax.fori_loop(0, tile // 8, body, h_ref[...])


def _lru(proj, cw, cb, wax, bax, sp, batch, lp):
    t_rows = proj.shape[0]
    nt = lp // LRU_TILE
    rowmap = lambda col: (lambda b, t: (b * nt + t, col))
    const = lambda b, t: (0, 0)
    return pl.pallas_call(
        _lru_kernel,
        grid=(batch, nt),
        in_specs=[pl.BlockSpec((LRU_TILE, LRU_WIDTH), rowmap(4)),
                  pl.BlockSpec((LRU_TILE, LRU_WIDTH), rowmap(5)),
                  pl.BlockSpec((8, LRU_WIDTH), const),
                  pl.BlockSpec((1, LRU_WIDTH), const),
                  pl.BlockSpec((LRU_BLOCKS, LRU_BW, 2 * LRU_BW), lambda b, t: (0, 0, 0)),
                  pl.BlockSpec((LRU_BLOCKS, 1, 2 * LRU_BW), lambda b, t: (0, 0, 0)),
                  pl.BlockSpec((1, LRU_WIDTH), const)],
        out_specs=pl.BlockSpec((LRU_TILE, LRU_WIDTH), rowmap(0)),
        out_shape=jax.ShapeDtypeStruct((t_rows, LRU_WIDTH), BF16),
        scratch_shapes=[pltpu.VMEM((LRU_TILE + 8, LRU_WIDTH), F32),
                        pltpu.VMEM((LRU_TILE, LRU_WIDTH), F32),
                        pltpu.VMEM((LRU_TILE, LRU_WIDTH), F32),
                        pltpu.VMEM((1, LRU_WIDTH), F32)],
        compiler_params=_cparams(2),
        name="lru",
    )(proj, proj, cw, cb, wax, bax, sp)


OUT_TILE = 256


def _out_ln_kernel(n_add, h_ref, *refs):
    adds = refs[:n_add]
    (a1_ref, w1_ref, a2_ref, w2_ref, g_ref, b_ref, wrh_ref, wrl_ref, br_ref,
     o_ref, r_ref, o3_ref, cnt_ref, run_ref) = refs[n_add:]

    @pl.when(pl.program_id(0) == 0)
    def _():
        run_ref[...] = jnp.zeros_like(run_ref)

    acc = DN_ALPHA * h_ref[...]
    for r in adds:
        acc = acc + r[...]
    acc = acc + _dot(a1_ref[...], w1_ref[...])
    if a2_ref is not None:
        acc = acc + _dot(a2_ref[...], w2_ref[...])
    hn = _layer_norm(acc, g_ref[...], b_ref[...])
    o_ref[...] = hn
    tm = hn.shape[0]
    for c in range(LANE_BLOCKS):
        o3_ref[pl.ds(c, tm, stride=LANE_BLOCKS), :] = hn[:, c * 128:(c + 1) * 128]
    r = _route(hn, wrh_ref[...], wrl_ref[...], br_ref[...])
    lane = lax.broadcasted_iota(jnp.int32, r.shape, 1).astype(F32)
    oh1 = lane == r[:, 0:1]
    oh2 = lane == r[:, 1:2]
    picked = jnp.where(oh1 | oh2, 1.0, 0.0)
    ri = lax.broadcasted_iota(jnp.int32, (tm, tm), 0)
    ci = lax.broadcasted_iota(jnp.int32, (tm, tm), 1)
    before = _dot(jnp.where(ri > ci, 1.0, 0.0).astype(BF16), picked.astype(BF16)) + run_ref[0:1, :]
    rank1 = jnp.sum(jnp.where(oh1, before, 0.0), axis=1, keepdims=True)
    rank2 = jnp.sum(jnp.where(oh2, before, 0.0), axis=1, keepdims=True)
    r_ref[...] = r + jnp.where(lane == 4.0, rank1, 0.0) + jnp.where(lane == 5.0, rank2, 0.0)
    total = run_ref[0:1, :] + jnp.sum(picked, axis=0, keepdims=True)
    run_ref[...] = jnp.broadcast_to(total, run_ref.shape)
    cnt_ref[...] = jnp.broadcast_to(total, cnt_ref.shape)


def _out_ln_kernel_1(h_ref, p_ref, a1_ref, w1_ref, *rest):
    _out_ln_kernel(1, h_ref, p_ref, a1_ref, w1_ref, None, None, *rest)


def _out_ln_kernel_2(h_ref, a1_ref, w1_ref, a2_ref, w2_ref, *rest):
    _out_ln_kernel(0, h_ref, a1_ref, w1_ref, a2_ref, w2_ref, *rest)


def _out_ln(h, addend, pairs, g, b, router):
    t_rows = h.shape[0]
    tm = OUT_TILE
    row = lambda i: (i, 0)
    const = lambda i: (0, 0)
    ins = [h]
    specs = [pl.BlockSpec((tm, D_MODEL), row)]
    if addend is not None:
        ins.append(addend)
        specs.append(pl.BlockSpec((tm, D_MODEL), row))
    for a, w in pairs:
        ins += [a, w]
        specs += [pl.BlockSpec((tm, a.shape[1]), row), pl.BlockSpec(w.shape, const)]
    ins += [g, b]
    specs += [pl.BlockSpec((1, D_MODEL), const)] * 2
    ins += list(router)
    specs += [pl.BlockSpec((D_MODEL, 128), const), pl.BlockSpec((D_MODEL, 128), const),
              pl.BlockSpec((1, 128), const)]
    body = _out_ln_kernel_1 if addend is not None else _out_ln_kernel_2
    return pl.pallas_call(
        body,
        grid=(t_rows // tm,),
        in_specs=specs,
        out_specs=[pl.BlockSpec((tm, D_MODEL), row), pl.BlockSpec((tm, 128), row),
                   pl.BlockSpec((tm * LANE_BLOCKS, 128), row),
                   pl.BlockSpec((8, 128), const)],
        out_shape=[jax.ShapeDtypeStruct((t_rows, D_MODEL), F32),
                   jax.ShapeDtypeStruct((t_rows, 128), F32),
                   jax.ShapeDtypeStruct((t_rows * LANE_BLOCKS, 128), F32),
                   jax.ShapeDtypeStruct((8, 128), F32)],
        scratch_shapes=[pltpu.VMEM((8, 128), F32)],
        compiler_params=_cparams(1),
        name="out_ln",
    )(*ins)


ROUTER_TILE = 768
MOE_TILE = 256
BIG = 1e30


def _route(h, wh, wl, b):
    hh, hl = _split(h)
    logits = _dot(hh, wh) + _dot(hl, wh) + _dot(hh, wl) + b
    lane = lax.broadcasted_iota(jnp.int32, logits.shape, 1).astype(F32)

    def first_argmax(vals, vmax):
        return jnp.min(jnp.where(vals == vmax, lane, 4096.0), axis=1, keepdims=True)

    is_g = lane < MOE_GROUPS
    lg = jnp.where(is_g, logits, -BIG)
    m = jnp.max(lg, axis=1, keepdims=True)
    gidx = first_argmax(lg, m)
    denom = jnp.sum(jnp.where(is_g, jnp.exp(logits - m), 0.0), axis=1, keepdims=True)
    p_top = 1.0 / denom
    base = MOE_GROUPS + MOE_EPG * gidx
    in_grp = (lane >= base) & (lane < base + MOE_EPG)
    le = jnp.where(in_grp, logits, -BIG)
    v1 = jnp.max(le, axis=1, keepdims=True)
    i1 = first_argmax(le, v1)
    le2 = jnp.where(lane == i1, -BIG, le)
    v2 = jnp.max(le2, axis=1, keepdims=True)
    i2 = first_argmax(le2, v2)
    e21 = jnp.exp(v2 - v1)
    w1 = p_top / (1.0 + e21)
    w2 = p_top * e21 / (1.0 + e21)
    return (jnp.where(lane == 0.0, i1 - MOE_GROUPS, 0.0) + jnp.where(lane == 1.0, i2 - MOE_GROUPS, 0.0)
            + jnp.where(lane == 2.0, w1, 0.0) + jnp.where(lane == 3.0, w2, 0.0))


def _router_weights(w_rg, b_rg, w_re, b_re):
    n_exp = MOE_GROUPS * MOE_EPG
    wr = jnp.concatenate([w_rg.astype(F32), w_re.astype(F32).reshape(D_MODEL, n_exp)], axis=1)
    wr = jnp.pad(wr, ((0, 0), (0, 128 - wr.shape[1])))
    wrh, wrl = _split(wr)
    br = jnp.concatenate([b_rg.astype(F32), b_re.astype(F32).reshape(-1)])
    br = jnp.pad(br, (0, 128 - br.shape[0])).reshape(1, 128)
    return wrh, wrl, br


def _moe_kernel(nv_ref, tok_ref, dst_ref, te_ref, h_hbm, wg_ref, wu_ref, wd_ref, y_hbm,
                xbuf, ybuf, xmat, wgb, wub, wdb, gsem, ssem):
    tm = MOE_TILE
    i = pl.program_id(0)
    nv = nv_ref[0]
    slot = i % 2
    other = 1 - slot

    lb = LANE_BLOCKS

    def buf_rows(row):
        r0 = row * lb
        return pl.ds(r0 if isinstance(row, int) else pl.multiple_of(r0, lb), lb)

    def gather_copy(tile, row, buf_slot):
        src = h_hbm.at[pl.ds(pl.multiple_of(tok_ref[tile * tm + row] * lb, lb), lb)]
        return pltpu.make_async_copy(src, xbuf.at[buf_slot, buf_rows(row)], gsem.at[buf_slot])

    def scatter_copy(tile, row, buf_slot):
        dst = y_hbm.at[pl.ds(pl.multiple_of(dst_ref[tile * tm + row] * lb, lb), lb)]
        return pltpu.make_async_copy(ybuf.at[buf_slot, buf_rows(row)], dst, ssem.at[buf_slot])

    def wait_gather(buf_slot):
        pltpu.make_async_copy(h_hbm.at[pl.ds(0, tm * lb)], xbuf.at[buf_slot], gsem.at[buf_slot]).wait()

    def wait_scatter(buf_slot):
        pltpu.make_async_copy(ybuf.at[buf_slot], y_hbm.at[pl.ds(0, tm * lb)], ssem.at[buf_slot]).wait()

    @pl.when((i == 0) | (te_ref[i] != te_ref[jnp.maximum(i - 1, 0)]))
    def _():
        wgb[...] = wg_ref[0].astype(BF16)
        wub[...] = wu_ref[0].astype(BF16)
        wdb[...] = wd_ref[0].astype(BF16)

    def compute(buf_slot):
        for c in range(lb):
            xmat[:, c * 128:(c + 1) * 128] = xbuf[buf_slot, pl.ds(c, tm, stride=lb), :]
        xb = xmat[...].astype(BF16)
        hg = _dot(xb, wgb[...])
        hu = _dot(xb, wub[...])
        hh = (hg * _sigmoid(hg)) * hu
        y = _dot(hh.astype(BF16), wdb[...])
        for c in range(lb):
            ybuf[buf_slot, pl.ds(c, tm, stride=lb), :] = y[:, c * 128:(c + 1) * 128]

    @pl.when(i == 0)
    def _():
        def body(r, c):
            gather_copy(0, r, 0).start()
            return c
        lax.fori_loop(0, tm, body, 0)
        ybuf[1] = jnp.zeros((tm * lb, 128), F32)
        fill = pltpu.make_async_copy(ybuf.at[1], y_hbm.at[pl.ds(y_hbm.shape[0] - tm * lb, tm * lb)], ssem.at[1])
        fill.start()
        fill.wait()
        wait_gather(0)
        for r in range(tm):
            gather_copy(1, r, 1).start(priority=r % 2)
        compute(0)

    @pl.when((i > 0) & (i < nv))
    def _():
        wait_gather(slot)

        @pl.when(i >= 2)
        def _():
            wait_scatter(slot)

        for r in range(tm):
            gather_copy(i + 1, r, other).start(priority=r % 2)
        for r in range(tm):
            scatter_copy(i - 1, r, other).start(priority=r % 2)
        compute(slot)

    @pl.when(i == nv)
    def _():
        wait_gather(slot)

        @pl.when(i >= 2)
        def _():
            wait_scatter(slot)

        def body(r, c):
            scatter_copy(i - 1, r, other).start()
            return c
        lax.fori_loop(0, tm, body, 0)
        wait_scatter(other)


def _moe(nv, tok, dst, te, h, wg, wu, wd, n_out_rows):
    n_tiles = tok.shape[0] // MOE_TILE
    tm = MOE_TILE
    ex = lambda i, nv, tok, dst, te: (te[i], 0, 0)
    return pl.pallas_call(
        _moe_kernel,
        grid_spec=pltpu.PrefetchScalarGridSpec(
            num_scalar_prefetch=4,
            grid=(n_tiles,),
            in_specs=[pl.BlockSpec(memory_space=pl.ANY),
                      pl.BlockSpec((1, D_MODEL, MOE_FF), ex),
                      pl.BlockSpec((1, D_MODEL, MOE_FF), ex),
                      pl.BlockSpec((1, MOE_FF, D_MODEL), ex)],
            out_specs=pl.BlockSpec(memory_space=pl.ANY),
            scratch_shapes=[pltpu.VMEM((2, tm * LANE_BLOCKS, 128), F32),
                            pltpu.VMEM((2, tm * LANE_BLOCKS, 128), F32),
                            pltpu.VMEM((tm, D_MODEL), F32),
                            pltpu.VMEM((D_MODEL, MOE_FF), BF16),
                            pltpu.VMEM((D_MODEL, MOE_FF), BF16),
                            pltpu.VMEM((MOE_FF, D_MODEL), BF16),
                            pltpu.SemaphoreType.DMA((2,)),
                            pltpu.SemaphoreType.DMA((2,))]),
        out_shape=jax.ShapeDtypeStruct((n_out_rows * LANE_BLOCKS, 128), F32),
        compiler_params=_cparams(1),
        name="moe_experts",
    )(nv, tok, dst, te, h, wg, wu, wd)


MOE_LN_TILE = 256


def _moe_ln_kernel(h_ref, y0_ref, y1_ref, r_ref, g_ref, b_ref, o_ref, acc_ref):
    r = r_ref[...]
    w0 = r[:, 2:3]
    w1 = r[:, 3:4]
    tm = h_ref.shape[0]
    for c in range(LANE_BLOCKS):
        cols = slice(c * 128, (c + 1) * 128)
        rows = pl.ds(c, tm, stride=LANE_BLOCKS)
        acc_ref[:, cols] = DN_ALPHA * h_ref[:, cols] + (w0 * y0_ref[rows, :] + w1 * y1_ref[rows, :])
    o_ref[...] = _layer_norm(acc_ref[...], g_ref[...], b_ref[...])


def _moe_ln(h, y2, rinfo, g, b):
    t_rows = h.shape[0]
    tm = MOE_LN_TILE
    nb = t_rows // tm
    const = lambda i: (0, 0)
    return pl.pallas_call(
        _moe_ln_kernel,
        grid=(nb,),
        in_specs=[pl.BlockSpec((tm, D_MODEL), lambda i: (i, 0)),
                  pl.BlockSpec((tm * LANE_BLOCKS, 128), lambda i: (i, 0)),
                  pl.BlockSpec((tm * LANE_BLOCKS, 128), lambda i: (nb + i, 0)),
                  pl.BlockSpec((tm, 128), lambda i: (i, 0)),
                  pl.BlockSpec((1, D_MODEL), const),
                  pl.BlockSpec((1, D_MODEL), const)],
        out_specs=pl.BlockSpec((tm, D_MODEL), lambda i: (i, 0)),
        out_shape=jax.ShapeDtypeStruct((t_rows, D_MODEL), F32),
        scratch_shapes=[pltpu.VMEM((tm, D_MODEL), F32)],
        compiler_params=_cparams(1),
        name="moe_ln",
    )(h, y2, y2, rinfo, g, b)


def _hier_moe_ln(h, rinfo, h3, counts, w_gate, w_up, w_down, ln_g, ln_b):
    t_rows = h.shape[0]
    n_exp = MOE_GROUPS * MOE_EPG

    n_asg = 2 * t_rows
    tm = MOE_TILE
    e_a = rinfo[:, 0:2].astype(jnp.int32).reshape(n_asg)
    rank_a = rinfo[:, 4:6].astype(jnp.int32).reshape(n_asg)
    asg = jnp.arange(n_asg, dtype=jnp.int32)
    counts = counts[0, :n_exp].astype(jnp.int32)
    tiles = (counts + tm - 1) // tm
    tile_end = jnp.cumsum(tiles)
    tile_start = tile_end - tiles
    pos = jnp.take(tile_start * tm, e_a) + rank_a
    n_tiles = n_asg // tm + n_exp + 1
    n_sorted = n_tiles * tm
    nv = tile_end[-1:].astype(jnp.int32)
    slot_asg = jnp.full((n_sorted,), -1, jnp.int32).at[pos].set(asg)
    is_pad = slot_asg < 0
    tok = jnp.where(is_pad, 0, slot_asg >> 1)
    dump = n_asg + jnp.arange(n_sorted, dtype=jnp.int32) % tm
    dst = jnp.where(is_pad, dump, (slot_asg & 1) * t_rows + (slot_asg >> 1))
    te = jnp.minimum(jnp.sum(jnp.arange(n_tiles)[:, None] >= tile_end[None, :], axis=1), n_exp - 1).astype(jnp.int32)

    y2 = _moe(nv, tok, dst, te, h3,
              w_gate.astype(F32).reshape(n_exp, D_MODEL, MOE_FF),
              w_up.astype(F32).reshape(n_exp, D_MODEL, MOE_FF),
              w_down.astype(F32).reshape(n_exp, MOE_FF, D_MODEL), n_asg + tm)
    return _moe_ln(h, y2, rinfo, ln_g.reshape(1, D_MODEL).astype(F32), ln_b.reshape(1, D_MODEL).astype(F32))


def _layer_ab(h, batch, lp, w_in, s5, w_glu, b_glu, gla_w_gate, gla_b_gate, gla_norm, w_out, ln_g, ln_b,
              router):
    hk = GLA_HEADS * GLA_DK
    hv = GLA_HEADS * GLA_DV
    o_q = S5_WIDTH
    o_g = o_q + 2 * hk + hv
    o_r = o_g + GLA_RANK
    w_in = w_in.astype(F32)
    w_u = w_in[:, :S5_WIDTH].astype(BF16)
    w_main = jnp.concatenate([w_in[:, o_q:o_g], w_in[:, o_r:o_r + hv]], axis=1).astype(BF16)
    w_low = jnp.pad(w_in[:, o_g:o_r], ((0, 0), (0, 128 - GLA_RANK))).astype(BF16)
    u = _s5_uproj(h, w_u)
    z = _s5_core(u, s5, batch)
    p_s5 = _s5_out(z, w_glu.astype(BF16), b_glu.reshape(1, -1).astype(F32), w_out[:S5_WIDTH].astype(BF16))
    proj, g_low = _in_proj(h, w_main, w_low)
    wg = jnp.pad(gla_w_gate.astype(F32), ((0, 128 - GLA_RANK), (0, 0)))
    wgh, wgl = _split(wg)
    y_gla = _gla(proj, g_low, wgh, wgl, gla_b_gate.reshape(1, -1).astype(F32),
                 gla_norm.reshape(1, -1).astype(F32), batch, lp)
    return _out_ln(h, p_s5, [(y_gla, w_out[S5_WIDTH:].astype(BF16))],
                   ln_g.reshape(1, -1).astype(F32), ln_b.reshape(1, -1).astype(F32), router)


def _layer_cd(h, batch, lp, w_in, conv_w, a_log, dt_bias, gdn_norm, lru_conv_w, lru_conv_b,
              lru_w_a, lru_b_a, lru_w_x, lru_b_x, lru_lambda, w_out, ln_g, ln_b, router):
    nq = GDN_QKV
    hv = GDN_HEADS * GDN_DV
    w_in = w_in.astype(F32)
    o_z = nq
    o_b = o_z + hv
    o_a = o_b + GDN_HEADS
    o_x = o_a + GDN_HEADS
    o_gt = o_x + LRU_WIDTH
    w_main = jnp.concatenate([w_in[:, :o_b], w_in[:, o_x:o_gt + LRU_WIDTH]], axis=1).astype(BF16)
    w_ba = jnp.pad(w_in[:, o_b:o_x], ((0, 0), (0, 128 - 2 * GDN_HEADS))).astype(BF16)
    proj, ba = _in_proj(h, w_main, w_ba)

    cw = jnp.pad(conv_w.astype(F32), ((0, 8 - CONV_K), (0, 0)))
    hp = jnp.zeros((8, 128), F32)
    hp = hp.at[0, GDN_HEADS:2 * GDN_HEADS].set(-jnp.exp(a_log.astype(F32)))
    hp = hp.at[1, GDN_HEADS:2 * GDN_HEADS].set(dt_bias.astype(F32))
    y_gdn = _gdn(proj, ba, cw, hp, gdn_norm.reshape(1, -1).astype(F32), batch, lp)

    lcw = jnp.pad(lru_conv_w.astype(F32), ((0, 8 - CONV_K), (0, 0)))
    wax = jnp.concatenate([lru_w_a.astype(F32), lru_w_x.astype(F32)], axis=2).astype(BF16)
    bax = jnp.concatenate([lru_b_a.astype(F32).reshape(LRU_BLOCKS, 1, LRU_BW),
                           lru_b_x.astype(F32).reshape(LRU_BLOCKS, 1, LRU_BW)], axis=2)
    sp = (-LRU_C * jax.nn.softplus(-lru_lambda.astype(F32))).reshape(1, LRU_WIDTH)
    y_lru = _lru(proj, lcw, lru_conv_b.reshape(1, -1).astype(F32), wax, bax, sp, batch, lp)

    return _out_ln(h, None, [(y_gdn, w_out[:hv].astype(BF16)), (y_lru, w_out[hv:].astype(BF16))],
                   ln_g.reshape(1, -1).astype(F32), ln_b.reshape(1, -1).astype(F32), router)


def kernel(x, meta_tokens, ab_w_in, ab_s5_a_re, ab_s5_a_im, ab_s5_log_dt, ab_s5_b_re, ab_s5_b_im,
           ab_s5_c_re, ab_s5_c_im, ab_s5_d, ab_s5_w_glu, ab_s5_b_glu, ab_gla_w_gate, ab_gla_b_gate,
           ab_gla_norm, ab_w_out, cd_w_in, cd_conv_w, cd_gdn_a_log, cd_gdn_dt_bias, cd_gdn_norm,
           cd_lru_conv_w, cd_lru_conv_b, cd_lru_w_a, cd_lru_b_a, cd_lru_w_x, cd_lru_b_x, cd_lru_lambda,
           cd_w_out, moe_w_router_g, moe_b_router_g, moe_w_router_e, moe_b_router_e, moe_w_gate,
           moe_w_up, moe_w_down, ln_mix_g, ln_mix_b, ln_ffn_g, ln_ffn_b):
    batch, seq, _ = x.shape
    length = seq + N_META
    lp = -(-length // ROW_ALIGN) * ROW_ALIGN
    meta = jnp.broadcast_to(meta_tokens.astype(x.dtype)[None], (batch, N_META, D_MODEL))
    h = jnp.concatenate([meta, x, jnp.zeros((batch, lp - length, D_MODEL), x.dtype)], axis=1)
    h = h.reshape(batch * lp, D_MODEL).astype(F32)

    for layer in range(DEPTH):
        j = layer // 2
        router = _router_weights(moe_w_router_g[layer], moe_b_router_g[layer], moe_w_router_e[layer],
                                 moe_b_router_e[layer])
        if layer % 2 == 0:
            s5 = _s5_params(ab_s5_a_re[j], ab_s5_a_im[j], ab_s5_log_dt[j], ab_s5_b_re[j], ab_s5_b_im[j],
                            ab_s5_c_re[j], ab_s5_c_im[j], ab_s5_d[j])
            h, rinfo, h3, counts = _layer_ab(h, batch, lp, ab_w_in[j], s5, ab_s5_w_glu[j], ab_s5_b_glu[j],
                                 ab_gla_w_gate[j], ab_gla_b_gate[j], ab_gla_norm[j], ab_w_out[j],
                                 ln_mix_g[layer], ln_mix_b[layer], router)
        else:
            h, rinfo, h3, counts = _layer_cd(h, batch, lp, cd_w_in[j], cd_conv_w[j], cd_gdn_a_log[j], cd_gdn_dt_bias[j],
                                 cd_gdn_norm[j], cd_lru_conv_w[j], cd_lru_conv_b[j], cd_lru_w_a[j],
                                 cd_lru_b_a[j], cd_lru_w_x[j], cd_lru_b_x[j], cd_lru_lambda[j], cd_w_out[j],
                                 ln_mix_g[layer], ln_mix_b[layer], router)
        h = _hier_moe_ln(h, rinfo, h3, counts, moe_w_gate[layer], moe_w_up[layer], moe_w_down[layer],
                         ln_ffn_g[layer], ln_ffn_b[layer])
    return h.reshape(batch, lp, D_MODEL)[:, N_META:length].astype(x.dtype)
```

```python
import functools
import math

import jax
import jax.numpy as jnp
from jax import lax
from jax.experimental import pallas as pl
from jax.experimental.pallas import tpu as pltpu

F32 = jnp.float32
BF16 = jnp.bfloat16
HIGHEST = lax.Precision.HIGHEST

D_MODEL = 2048
N_META = 16
CONV_K = 4
DEPTH = 2
DN_ALPHA = (2.0 * DEPTH) ** 0.25
LN_EPS = 1e-5
NORM_EPS = 1e-6

S5_WIDTH = 1024
S5_GROUP = 16
S5_GROUPS = 64
S5_STATE = 64
S5_LC = 16
S5_SLABS = 8

GLA_HEADS = 4
GLA_DK = 128
GLA_DV = 256
GLA_RANK = 16
GLA_TAU = 16.0

GDN_HEADS = 8
GDN_DK = 128
GDN_DV = 128

LRU_WIDTH = 1024
LRU_BLOCKS = 8
LRU_BW = 128
LRU_C = 8.0

MOE_GROUPS = 4
MOE_EPG = 8
MOE_FF = 256

LANE_BLOCKS = D_MODEL // 128
CHUNK = 64
ROW_ALIGN = 768
VMEM_LIMIT = 56 * 1024 * 1024

NT_DIMS = (((1,), (1,)), ((), ()))
TN_DIMS = (((0,), (0,)), ((), ()))


def _cparams(n_axes):
    return pltpu.CompilerParams(dimension_semantics=("arbitrary",) * n_axes,
                                vmem_limit_bytes=VMEM_LIMIT)


def _dot(a, b):
    return jnp.dot(a, b, preferred_element_type=F32)


def _dotg(a, b, dims):
    return lax.dot_general(a, b, dims, preferred_element_type=F32)


def _split(a):
    hi = a.astype(BF16)
    lo = (a - hi.astype(F32)).astype(BF16)
    return hi, lo


def _sigmoid(x):
    return 1.0 / (1.0 + jnp.exp(-x))


def _softplus(x):
    return jnp.maximum(x, 0.0) + jnp.log1p(jnp.exp(-jnp.abs(x)))


def _gelu(x):
    return 0.5 * x * (1.0 + jnp.tanh(math.sqrt(2.0 / math.pi) * (x + 0.044715 * (x * x * x))))


def _layer_norm(x, g, b):
    mu = jnp.mean(x, axis=-1, keepdims=True)
    xc = x - mu
    var = jnp.mean(xc * xc, axis=-1, keepdims=True)
    return xc * lax.rsqrt(var + LN_EPS) * g + b


PROJ_TILE = 512
PROJ_COLS = 1024


def _in_proj_kernel(x_ref, w_ref, wg_ref, o_ref, g_ref):
    xb = x_ref[...].astype(BF16)
    for j in range(w_ref.shape[1] // PROJ_COLS):
        cols = slice(j * PROJ_COLS, (j + 1) * PROJ_COLS)
        o_ref[:, cols] = _dot(xb, w_ref[:, cols]).astype(o_ref.dtype)
    g_ref[...] = _dot(xb, wg_ref[...])


def _in_proj(x, w, w_gates):
    m, k = x.shape
    n = w.shape[1]
    tm = PROJ_TILE
    const = lambda i: (0, 0)
    return pl.pallas_call(
        _in_proj_kernel,
        grid=(m // tm,),
        in_specs=[pl.BlockSpec((tm, k), lambda i: (i, 0)),
                  pl.BlockSpec((k, n), const, pipeline_mode=pl.Buffered(1)),
                  pl.BlockSpec((k, 128), const, pipeline_mode=pl.Buffered(1))],
        out_specs=[pl.BlockSpec((tm, n), lambda i: (i, 0)),
                   pl.BlockSpec((tm, 128), lambda i: (i, 0))],
        out_shape=[jax.ShapeDtypeStruct((m, n), BF16),
                   jax.ShapeDtypeStruct((m, 128), F32)],
        compiler_params=_cparams(1),
        name="in_proj",
    )(x, w, w_gates)


S5_TILE_CHUNKS = 32


def _s5_row_perm():
    m = S5_TILE_CHUNKS
    r = jnp.arange(S5_LC * m)
    nat = (r % m) * S5_LC + r // m
    return (nat[:, None] == jnp.arange(S5_LC * m)[None, :]).astype(BF16)


def _s5_uproj_kernel(h_ref, perm_ref, w_ref, u_ref):
    m = S5_TILE_CHUNKS
    lhs = _dot(perm_ref[...], h_ref[...].astype(BF16)).astype(BF16)
    z = _dot(lhs, w_ref[...])
    for t in range(S5_LC):
        for s in range(S5_SLABS):
            u_ref[s, :, t * 128:(t + 1) * 128] = z[t * m:(t + 1) * m, s * 128:(s + 1) * 128]


def _s5_uproj(h, w_u):
    t_rows = h.shape[0]
    n_chunks = t_rows // S5_LC
    m = S5_TILE_CHUNKS
    return pl.pallas_call(
        _s5_uproj_kernel,
        grid=(n_chunks // m,),
        in_specs=[pl.BlockSpec((m * S5_LC, D_MODEL), lambda i: (i, 0)),
                  pl.BlockSpec((m * S5_LC, m * S5_LC), lambda i: (0, 0)),
                  pl.BlockSpec((D_MODEL, S5_WIDTH), lambda i: (0, 0))],
        out_specs=pl.BlockSpec((S5_SLABS, m, S5_LC * 128), lambda i: (0, i, 0)),
        out_shape=jax.ShapeDtypeStruct((S5_SLABS, n_chunks, S5_LC * 128), F32),
        compiler_params=_cparams(1),
        name="s5_uproj",
    )(h, _s5_row_perm(), w_u)


def _s5_core_kernel(u_ref, stack_ref, ms_ref, mot_ref, ak_ref, pre_ref, pim_ref, d_ref, z_ref,
                    sre_ref, sim_ref, xre_ref, xim_ref):
    n = u_ref.shape[1]
    half = 8 * S5_STATE
    u = u_ref[0]
    ub = u.astype(BF16)
    s = _dot(ub, ms_ref[0])
    sre_ref[...] = s[:, :half]
    sim_ref[...] = s[:, half:]

    ak = ak_ref[0]
    pre = pre_ref[0]
    pim = pim_ref[0]
    row = lax.broadcasted_iota(jnp.int32, (8, half), 0)

    def body(rb, carry):
        cre, cim = carry
        r0 = pl.multiple_of(rb * 8, 8)
        xr = sre_ref[pl.ds(r0, 8), :]
        xi = sim_ref[pl.ds(r0, 8), :]
        for idx, k in enumerate((1, 2, 4)):
            are = ak[2 * idx:2 * idx + 1, :]
            aim = ak[2 * idx + 1:2 * idx + 2, :]
            keep = row >= k
            shr = jnp.where(keep, pltpu.roll(xr, k, 0), 0.0)
            shi = jnp.where(keep, pltpu.roll(xi, k, 0), 0.0)
            xr, xi = xr + are * shr - aim * shi, xi + are * shi + aim * shr
        xr = xr + pre * cre - pim * cim
        xi = xi + pre * cim + pim * cre
        first = row == 0
        xre_ref[pl.ds(r0, 8), :] = jnp.where(first, cre, pltpu.roll(xr, 1, 0))
        xim_ref[pl.ds(r0, 8), :] = jnp.where(first, cim, pltpu.roll(xi, 1, 0))
        return xr[7:8, :], xi[7:8, :]

    zero = jnp.zeros((1, half), F32)
    lax.fori_loop(0, n // 8, body, (zero, zero))

    xprev = jnp.concatenate([xre_ref[...], xim_ref[...]], axis=1).astype(BF16)
    y_inter = _dotg(xprev, mot_ref[0], NT_DIMS)
    d = d_ref[0]
    npair = S5_LC // 2
    for q in range(npair):
        yq = _dot(ub[:, :(q + 1) * 256], stack_ref[0, (npair - 1 - q) * 256:, :])
        sl = slice(q * 256, (q + 1) * 256)
        y = yq + y_inter[:, sl] + d[:, sl] * u[:, sl]
        z_ref[0, :, sl] = _gelu(y)


def _s5_core(u, prm, batch):
    n_tot = u.shape[1]
    n = n_tot // batch
    w = S5_LC * 128
    half = 8 * S5_STATE
    slab = lambda s, b: (s, 0, 0)
    return pl.pallas_call(
        _s5_core_kernel,
        grid=(S5_SLABS, batch),
        in_specs=[pl.BlockSpec((1, n, w), lambda s, b: (s, b, 0)),
                  pl.BlockSpec((1, w, 256), slab),
                  pl.BlockSpec((1, w, 2 * half), slab),
                  pl.BlockSpec((1, w, 2 * half), slab),
                  pl.BlockSpec((1, 8, half), slab),
                  pl.BlockSpec((1, 8, half), slab),
                  pl.BlockSpec((1, 8, half), slab),
                  pl.BlockSpec((1, 1, w), slab)],
        out_specs=pl.BlockSpec((1, n, w), lambda s, b: (s, b, 0)),
        out_shape=jax.ShapeDtypeStruct(u.shape, F32),
        scratch_shapes=[pltpu.VMEM((n, half), F32)] * 4,
        compiler_params=_cparams(2),
        name="s5_core",
    )(u, prm["stack"], prm["ms"], prm["mot"], prm["ak"], prm["pre"], prm["pim"], prm["d"])


def _s5_out_kernel(z_ref, perm_ref, wglu_ref, bglu_ref, wout_ref, p_ref, zf_ref):
    m = S5_TILE_CHUNKS
    for t in range(S5_LC):
        for s in range(S5_SLABS):
            zf_ref[t * m:(t + 1) * m, s * 128:(s + 1) * 128] = z_ref[s, :, t * 128:(t + 1) * 128]
    zf = zf_ref[...]
    gate = _sigmoid(_dot(zf.astype(BF16), wglu_ref[...]) + bglu_ref[...])
    gated = (zf * gate).astype(BF16)
    nat = _dotg(perm_ref[...], gated, TN_DIMS).astype(BF16)
    p_ref[...] = _dot(nat, wout_ref[...])


def _s5_out(z, w_glu, b_glu, w_out_top):
    n_chunks = z.shape[1]
    m = S5_TILE_CHUNKS
    return pl.pallas_call(
        _s5_out_kernel,
        grid=(n_chunks // m,),
        in_specs=[pl.BlockSpec((S5_SLABS, m, S5_LC * 128), lambda i: (0, i, 0)),
                  pl.BlockSpec((m * S5_LC, m * S5_LC), lambda i: (0, 0)),
                  pl.BlockSpec((S5_WIDTH, S5_WIDTH), lambda i: (0, 0)),
                  pl.BlockSpec((1, S5_WIDTH), lambda i: (0, 0)),
                  pl.BlockSpec((S5_WIDTH, D_MODEL), lambda i: (0, 0))],
        out_specs=pl.BlockSpec((m * S5_LC, D_MODEL), lambda i: (i, 0)),
        out_shape=jax.ShapeDtypeStruct((n_chunks * S5_LC, D_MODEL), F32),
        scratch_shapes=[pltpu.VMEM((S5_LC * m, S5_WIDTH), F32)],
        compiler_params=_cparams(1),
        name="s5_out",
    )(z, _s5_row_perm(), w_glu, b_glu, w_out_top)


def _s5_params(a_re, a_im, log_dt, b_re, b_im, c_re, c_im, d):
    g, p, c, lc = S5_GROUPS, S5_STATE, S5_GROUP, S5_LC
    a = lax.complex(a_re.astype(F32), a_im.astype(F32))
    adt = a * jnp.exp(log_dt.astype(F32))[:, None]
    zoh = (jnp.exp(adt) - 1.0) / a
    bz = lax.complex(b_re.astype(F32), b_im.astype(F32)) * zoh[:, :, None]
    cc = lax.complex(c_re.astype(F32), c_im.astype(F32))
    jj = jnp.arange(lc + 1, dtype=F32)
    apow = jnp.exp(adt[None] * jj[:, None, None])

    def expand(compact, width):
        rows = compact.shape[-2]
        row_grp = (jnp.arange(rows) % 128) // S5_GROUP
        lane_grp = jnp.arange(8 * width) // width
        tiled = jnp.tile(compact, (1,) * (compact.ndim - 1) + (8,))
        return jnp.where(row_grp[:, None] == lane_grp[None, :], tiled, 0.0)

    kre = jnp.einsum("gop,jgp,gpc->jgco", cc, apow[:lc], bz, precision=HIGHEST).real
    kblk = expand(kre.reshape(lc, S5_SLABS, 128, c), c)
    kpad = jnp.concatenate([jnp.zeros_like(kblk[:1]), kblk], axis=0)
    npair = lc // 2
    dd = jnp.arange(npair)[:, None, None]
    lo_in = jnp.arange(2)[None, :, None]
    lo_out = jnp.arange(2)[None, None, :]
    lag = 2 * dd + lo_out - lo_in
    tiles = kpad[lag + 1]
    tiles = tiles.transpose(3, 0, 1, 4, 2, 5).reshape(S5_SLABS, npair, 256, 256)
    stack = tiles[:, ::-1].reshape(S5_SLABS, npair * 256, 256)

    wst = apow[lc - 1 - jnp.arange(lc)][:, :, :, None] * bz[None]
    wst = wst.reshape(lc, S5_SLABS, 8, p, c).transpose(1, 0, 2, 4, 3).reshape(S5_SLABS, lc * 128, p)
    ms = jnp.concatenate([expand(wst.real, p), expand(wst.imag, p)], axis=2)

    zc = cc[None] * apow[1:lc + 1][:, :, None, :]
    zc = zc.reshape(lc, S5_SLABS, 8 * c, p).transpose(1, 0, 2, 3).reshape(S5_SLABS, lc * 128, p)
    mot = jnp.concatenate([expand(zc.real, p), expand(-zc.imag, p)], axis=2)

    def slab_layout(x):
        return x.reshape(x.shape[:-2] + (S5_SLABS, 8 * p))

    kk = jnp.array([1.0, 2.0, 4.0], F32)
    a_k = slab_layout(jnp.exp(adt[None] * (lc * kk)[:, None, None]))
    ak = jnp.stack([a_k[0].real, a_k[0].imag, a_k[1].real, a_k[1].imag, a_k[2].real, a_k[2].imag,
                    jnp.zeros_like(a_k[0].real), jnp.zeros_like(a_k[0].real)], axis=1)
    rr = jnp.arange(1, 9, dtype=F32)
    a_r = slab_layout(jnp.exp(adt[None] * (lc * rr)[:, None, None]))
    pre = a_r.real.transpose(1, 0, 2)
    pim = a_r.imag.transpose(1, 0, 2)
    dsk = jnp.tile(d.astype(F32).reshape(S5_SLABS, 1, 128), (1, 1, lc))
    return {"stack": stack.astype(BF16), "ms": ms.astype(BF16), "mot": mot.astype(BF16),
            "ak": ak, "pre": pre, "pim": pim, "d": dsk}


GLA_TILE = 256


def _gla_kernel(q_ref, k_ref, v_ref, r_ref, gl_ref, wgh_ref, wgl_ref, bg_ref, ng_ref, o_ref, st_ref):
    @pl.when(pl.program_id(1) == 0)
    def _():
        st_ref[...] = jnp.zeros_like(st_ref)

    rowi = lax.broadcasted_iota(jnp.int32, (CHUNK, CHUNK), 0)
    coli = lax.broadcasted_iota(jnp.int32, (CHUNK, CHUNK), 1)
    tri = rowi >= coli
    wgh = wgh_ref[...]
    wgl = wgl_ref[...]
    bg = bg_ref[...]
    ng = ng_ref[...]
    scale = GLA_DK ** -0.5

    nch = GLA_TILE // CHUNK
    ti = lax.broadcasted_iota(jnp.int32, (GLA_TILE, GLA_TILE), 0)
    tj = lax.broadcasted_iota(jnp.int32, (GLA_TILE, GLA_TILE), 1)
    tri_t = jnp.where((ti >= tj) & ((ti // CHUNK) == (tj // CHUNK)), 1.0, 0.0).astype(BF16)
    glh, gll = _split(gl_ref[...])
    pre = _dot(glh, wgh) + _dot(gll, wgh) + _dot(glh, wgl) + bg
    logf = (jnp.minimum(pre, 0.0) - jnp.log1p(jnp.exp(-jnp.abs(pre)))) * (1.0 / GLA_TAU)
    lh, ll = _split(logf)
    bcum = _dot(tri_t, lh) + _dot(tri_t, ll)
    ebc = jnp.exp(bcum)
    kk = k_ref[...].astype(F32)
    q_in = (q_ref[...].astype(F32) * scale * ebc).astype(BF16)
    k_in = (kk * jnp.exp(-bcum)).astype(BF16)

    heads = range(GLA_HEADS)
    chunks = range(nch)
    items = [(c, h) for c in chunks for h in heads]
    rws = [slice(c * CHUNK, (c + 1) * CHUNK) for c in chunks]
    sks = [slice(h * GLA_DK, (h + 1) * GLA_DK) for h in heads]
    svs = [slice(h * GLA_DV, (h + 1) * GLA_DV) for h in heads]
    blast = [bcum[(c + 1) * CHUNK - 1:(c + 1) * CHUNK, :] for c in chunks]
    k_st = [(kk[rws[c], :] * jnp.exp(blast[c] - bcum[rws[c], :])).astype(BF16) for c in chunks]
    dec = [jnp.exp(blast[c]) for c in chunks]
    vh = {(c, h): v_ref[rws[c], svs[h]].astype(BF16) for c, h in items}
    att = {(c, h): _dotg(q_in[rws[c], sks[h]], k_in[rws[c], sks[h]], NT_DIMS) for c, h in items}
    kv = {(c, h): _dotg(vh[c, h], k_st[c][:, sks[h]], TN_DIMS) for c, h in items}
    o_intra = {it: _dot(jnp.where(tri, att[it], 0.0).astype(BF16), vh[it]) for it in items}
    st = [st_ref[h] for h in heads]
    for c in chunks:
        o_inter = [_dotg(q_in[rws[c], sks[h]], st[h].astype(BF16), NT_DIMS) for h in heads]
        for h in heads:
            o = o_intra[c, h] + o_inter[h]
            on = o * lax.rsqrt(jnp.mean(o * o, axis=1, keepdims=True) + NORM_EPS) * ng
            rr = r_ref[rws[c], svs[h]].astype(F32)
            o_ref[rws[c], svs[h]] = (on * (rr * _sigmoid(rr))).astype(o_ref.dtype)
        st = [st[h] * dec[c][:, sks[h]] + kv[c, h] for h in heads]
    for h in heads:
        st_ref[h] = st[h]


def _gla(proj, g_low, wgh, wgl, bg, ng, batch, lp):
    t_rows = proj.shape[0]
    nt = lp // GLA_TILE
    hk = GLA_HEADS * GLA_DK
    hv = GLA_HEADS * GLA_DV
    rowmap = lambda col: (lambda b, t: (b * nt + t, col))
    const = lambda b, t: (0, 0)
    return pl.pallas_call(
        _gla_kernel,
        grid=(batch, nt),
        in_specs=[pl.BlockSpec((GLA_TILE, hk), rowmap(0)),
                  pl.BlockSpec((GLA_TILE, hk), rowmap(1)),
                  pl.BlockSpec((GLA_TILE, hv), rowmap(1)),
                  pl.BlockSpec((GLA_TILE, hv), rowmap(2)),
                  pl.BlockSpec((GLA_TILE, 128), rowmap(0)),
                  pl.BlockSpec((128, hk), const),
                  pl.BlockSpec((128, hk), const),
                  pl.BlockSpec((1, hk), const),
                  pl.BlockSpec((1, GLA_DV), const)],
        out_specs=pl.BlockSpec((GLA_TILE, hv), rowmap(0)),
        out_shape=jax.ShapeDtypeStruct((t_rows, hv), BF16),
        scratch_shapes=[pltpu.VMEM((GLA_HEADS, GLA_DV, GLA_DK), F32)],
        compiler_params=_cparams(2),
        name="gla",
    )(proj, proj, proj, proj, g_low, wgh, wgl, bg, ng)


GDN_TILE = 256
GDN_GROUP = 4
GDN_QKV = 3 * GDN_HEADS * GDN_DK


def _gdn_kernel(qkv_ref, z_ref, ba_ref, cw_ref, hp_ref, ng_ref, o_ref,
                ext_ref, act_ref, beta_ref, g_ref, s_ref):
    tile = GDN_TILE

    @pl.when(pl.program_id(1) == 0)
    def _():
        ext_ref[0:8, :] = jnp.zeros((8, GDN_QKV), F32)
        s_ref[...] = jnp.zeros_like(s_ref)

    @pl.when(pl.program_id(1) > 0)
    def _():
        ext_ref[0:8, :] = ext_ref[tile:tile + 8, :]

    ext_ref[8:tile + 8, :] = qkv_ref[...].astype(F32)
    qscale = GDN_DK ** -0.5
    for cb in range(GDN_QKV // 128):
        cols = slice(cb * 128, (cb + 1) * 128)
        w = cw_ref[:, cols]
        y = (w[3:4] * ext_ref[8:tile + 8, cols] + w[2:3] * ext_ref[7:tile + 7, cols]
             + w[1:2] * ext_ref[6:tile + 6, cols] + w[0:1] * ext_ref[5:tile + 5, cols])
        a = y * _sigmoid(y)
        if cb < 2 * GDN_HEADS:
            a = a * lax.rsqrt(jnp.sum(a * a, axis=1, keepdims=True) + NORM_EPS)
            if cb < GDN_HEADS:
                a = a * qscale
        act_ref[:, cols] = a

    ba = ba_ref[...]
    beta_ref[...] = _sigmoid(ba)
    g_ref[...] = hp_ref[0:1, :] * _softplus(ba + hp_ref[1:2, :])

    rowi = lax.broadcasted_iota(jnp.int32, (CHUNK, CHUNK), 0)
    coli = lax.broadcasted_iota(jnp.int32, (CHUNK, CHUNK), 1)
    tri = rowi >= coli
    strict = rowi > coli
    tri_b = jnp.where(tri, 1.0, 0.0).astype(BF16)
    upper_b = jnp.where(rowi <= coli, 1.0, 0.0).astype(BF16)
    eye = jnp.where(rowi == coli, 1.0, 0.0)
    level_masks = []
    for lvl in range(6):
        bi = rowi >> lvl
        bj = coli >> lvl
        level_masks.append(((bi & 1) == 1) & (bj == bi - 1))
    ng = ng_ref[...]

    heads = range(GDN_HEADS)
    grp = range(GDN_GROUP)
    items = [(j, h) for j in grp for h in heads]
    hcol = lambda t, h: t[:, GDN_HEADS + h:GDN_HEADS + h + 1]

    def group(gi, carry):
        base = gi * (GDN_GROUP * CHUNK)
        rows = [pl.ds(pl.multiple_of(base + j * CHUNK, CHUNK), CHUNK) for j in grp]
        gsp = [_split(g_ref[rows[j], :]) for j in grp]
        gc = [_dot(tri_b, gsp[j][0]) + _dot(tri_b, gsp[j][1]) for j in grp]
        gct = [_dotg(gsp[j][0], upper_b, TN_DIMS) + _dotg(gsp[j][1], upper_b, TN_DIMS) for j in grp]
        eg = [jnp.exp(gc[j]) for j in grp]
        elast = [jnp.exp(gc[j][CHUNK - 1:CHUNK, :] - gc[j]) for j in grp]
        dlast = [jnp.exp(gc[j][CHUNK - 1:CHUNK, :]) for j in grp]
        beta = [beta_ref[rows[j], :] for j in grp]
        qs = {(j, h): act_ref[rows[j], h * GDN_DK:(h + 1) * GDN_DK] for j, h in items}
        ks = {(j, h): act_ref[rows[j], (GDN_HEADS + h) * GDN_DK:(GDN_HEADS + h + 1) * GDN_DK] for j, h in items}
        vs = {(j, h): act_ref[rows[j], (2 * GDN_HEADS + h) * GDN_DK:(2 * GDN_HEADS + h + 1) * GDN_DK]
              for j, h in items}
        bcols = {(j, h): beta[j][:, h:h + 1] for j, h in items}
        kbs = {it: ks[it] * bcols[it] for it in items}
        khb = {it: ks[it].astype(BF16) for it in items}
        gams = {(j, h): jnp.exp(jnp.where(tri, hcol(gc[j], h) - gct[j][GDN_HEADS + h:GDN_HEADS + h + 1, :], -1e30))
                for j, h in items}
        kk = {it: _dotg(kbs[it].astype(BF16), khb[it], NT_DIMS) for it in items}
        qk = {it: _dotg(qs[it].astype(BF16), khb[it], NT_DIMS) for it in items}
        mm = {it: jnp.where(strict, kk[it] * gams[it], 0.0) for it in items}
        tinv = {it: eye - jnp.where(level_masks[0], mm[it], 0.0) for it in items}
        for lvl in range(1, 6):
            tb = {it: tinv[it].astype(BF16) for it in items}
            xs = {it: _dot(jnp.where(level_masks[lvl], mm[it], 0.0).astype(BF16), tb[it]).astype(BF16)
                  for it in items}
            tinv = {it: tinv[it] - _dot(tb[it], xs[it]) for it in items}
        rhs = {(j, h): jnp.concatenate([vs[j, h] * bcols[j, h], kbs[j, h] * hcol(eg[j], h)], axis=1).astype(BF16)
               for j, h in items}
        sol = {it: _dot(tinv[it].astype(BF16), rhs[it]) for it in items}
        aqk = {it: jnp.where(tri, qk[it] * gams[it], 0.0).astype(BF16) for it in items}
        qdec = {(j, h): (qs[j, h] * hcol(eg[j], h)).astype(BF16) for j, h in items}
        kst = {(j, h): (ks[j, h] * hcol(elast[j], h)).astype(BF16) for j, h in items}
        for j in grp:
            s_old = [s_ref[h] for h in heads]
            sb = [s_old[h].astype(BF16) for h in heads]
            ws = [_dot(sol[j, h][:, GDN_DV:].astype(BF16), sb[h]) for h in heads]
            qsd = [_dot(qdec[j, h], sb[h]) for h in heads]
            vnb = [(sol[j, h][:, :GDN_DV] - ws[h]).astype(BF16) for h in heads]
            av = [_dot(aqk[j, h], vnb[h]) for h in heads]
            kv = [_dotg(kst[j, h], vnb[h], TN_DIMS) for h in heads]
            for h in heads:
                o = qsd[h] + av[h]
                on = o * lax.rsqrt(jnp.mean(o * o, axis=1, keepdims=True) + NORM_EPS) * ng
                zz = z_ref[rows[j], h * GDN_DV:(h + 1) * GDN_DV].astype(F32)
                o_ref[rows[j], h * GDN_DV:(h + 1) * GDN_DV] = (on * (zz * _sigmoid(zz))).astype(o_ref.dtype)
                s_ref[h] = s_old[h] * hcol(dlast[j], h) + kv[h]
        return carry

    lax.fori_loop(0, tile // (GDN_GROUP * CHUNK), group, 0)


def _gdn(proj, ba, cw, hp, ng, batch, lp):
    t_rows = proj.shape[0]
    nt = lp // GDN_TILE
    hv = GDN_HEADS * GDN_DV
    rowmap = lambda col: (lambda b, t: (b * nt + t, col))
    const = lambda b, t: (0, 0)
    return pl.pallas_call(
        _gdn_kernel,
        grid=(batch, nt),
        in_specs=[pl.BlockSpec((GDN_TILE, GDN_QKV), rowmap(0)),
                  pl.BlockSpec((GDN_TILE, hv), rowmap(3)),
                  pl.BlockSpec((GDN_TILE, 128), rowmap(0)),
                  pl.BlockSpec((8, GDN_QKV), const),
                  pl.BlockSpec((8, 128), const),
                  pl.BlockSpec((1, GDN_DV), const)],
        out_specs=pl.BlockSpec((GDN_TILE, hv), rowmap(0)),
        out_shape=jax.ShapeDtypeStruct((t_rows, hv), BF16),
        scratch_shapes=[pltpu.VMEM((GDN_TILE + 8, GDN_QKV), F32),
                        pltpu.VMEM((GDN_TILE, GDN_QKV), F32),
                        pltpu.VMEM((GDN_TILE, 128), F32),
                        pltpu.VMEM((GDN_TILE, 128), F32),
                        pltpu.VMEM((GDN_HEADS, GDN_DK, GDN_DV), F32)],
        compiler_params=_cparams(2),
        name="gdn",
    )(proj, proj, ba, cw, hp, ng)


LRU_TILE = 256


def _lru_kernel(x_ref, gate_ref, cw_ref, cb_ref, wax_ref, bax_ref, sp_ref, o_ref,
                ext_ref, a_ref, b_ref, h_ref):
    tile = LRU_TILE

    @pl.when(pl.program_id(1) == 0)
    def _():
        ext_ref[0:8, :] = jnp.zeros((8, LRU_WIDTH), F32)
        h_ref[...] = jnp.zeros_like(h_ref)

    @pl.when(pl.program_id(1) > 0)
    def _():
        ext_ref[0:8, :] = ext_ref[tile:tile + 8, :]

    ext_ref[8:tile + 8, :] = x_ref[...].astype(F32)
    for blk in range(LRU_BLOCKS):
        cols = slice(blk * LRU_BW, (blk + 1) * LRU_BW)
        w = cw_ref[:, cols]
        xc = (w[3:4] * ext_ref[8:tile + 8, cols] + w[2:3] * ext_ref[7:tile + 7, cols]
              + w[1:2] * ext_ref[6:tile + 6, cols] + w[0:1] * ext_ref[5:tile + 5, cols]
              + cb_ref[:, cols])
        pre = _dot(xc.astype(BF16), wax_ref[blk]) + bax_ref[blk]
        r = _sigmoid(pre[:, :LRU_BW])
        i = _sigmoid(pre[:, LRU_BW:])
        log_a = sp_ref[:, cols] * r
        a_ref[:, cols] = jnp.exp(log_a)
        th = jnp.tanh(log_a)
        b_ref[:, cols] = jnp.sqrt(-2.0 * th / (1.0 - th)) * (i * xc)

    row = lax.broadcasted_iota(jnp.int32, (8, LRU_WIDTH), 0)

    def body(rb, carry):
        r0 = pl.multiple_of(rb * 8, 8)
        aa = a_ref[pl.ds(r0, 8), :]
        bb = b_ref[pl.ds(r0, 8), :]
        for k in (1, 2, 4):
            keep = row >= k
            a_sh = jnp.where(keep, pltpu.roll(aa, k, 0), 1.0)
            b_sh = jnp.where(keep, pltpu.roll(bb, k, 0), 0.0)
            bb = aa * b_sh + bb
            aa = aa * a_sh
        hs = bb + aa * carry
        gt = gate_ref[pl.ds(r0, 8), :].astype(F32)
        o_ref[pl.ds(r0, 8), :] = (hs * _gelu(gt)).astype(o_ref.dtype)
        return hs[7:8, :]

    h_ref[...] = lax.fori_loop(0, tile // 8, body, h_ref[...])


def _lru(proj, cw, cb, wax, bax, sp, batch, lp):
    t_rows = proj.shape[0]
    nt = lp // LRU_TILE
    rowmap = lambda col: (lambda b, t: (b * nt + t, col))
    const = lambda b, t: (0, 0)
    return pl.pallas_call(
        _lru_kernel,
        grid=(batch, nt),
        in_specs=[pl.BlockSpec((LRU_TILE, LRU_WIDTH), rowmap(4)),
                  pl.BlockSpec((LRU_TILE, LRU_WIDTH), rowmap(5)),
                  pl.BlockSpec((8, LRU_WIDTH), const),
                  pl.BlockSpec((1, LRU_WIDTH), const),
                  pl.BlockSpec((LRU_BLOCKS, LRU_BW, 2 * LRU_BW), lambda b, t: (0, 0, 0)),
                  pl.BlockSpec((LRU_BLOCKS, 1, 2 * LRU_BW), lambda b, t: (0, 0, 0)),
                  pl.BlockSpec((1, LRU_WIDTH), const)],
        out_specs=pl.BlockSpec((LRU_TILE, LRU_WIDTH), rowmap(0)),
        out_shape=jax.ShapeDtypeStruct((t_rows, LRU_WIDTH), BF16),
        scratch_shapes=[pltpu.VMEM((LRU_TILE + 8, LRU_WIDTH), F32),
                        pltpu.VMEM((LRU_TILE, LRU_WIDTH), F32),
                        pltpu.VMEM((LRU_TILE, LRU_WIDTH), F32),
                        pltpu.VMEM((1, LRU_WIDTH), F32)],
        compiler_params=_cparams(2),
        name="lru",
    )(proj, proj, cw, cb, wax, bax, sp)


OUT_TILE = 256


def _out_ln_kernel(n_add, h_ref, *refs):
    adds = refs[:n_add]
    (a1_ref, w1_ref, a2_ref, w2_ref, g_ref, b_ref, wrh_ref, wrl_ref, br_ref,
     o_ref, r_ref, o3_ref, cnt_ref, run_ref) = refs[n_add:]

    @pl.when(pl.program_id(0) == 0)
    def _():
        run_ref[...] = jnp.zeros_like(run_ref)

    acc = DN_ALPHA * h_ref[...]
    for r in adds:
        acc = acc + r[...]
    acc = acc + _dot(a1_ref[...], w1_ref[...])
    if a2_ref is not None:
        acc = acc + _dot(a2_ref[...], w2_ref[...])
    hn = _layer_norm(acc, g_ref[...], b_ref[...])
    o_ref[...] = hn
    tm = hn.shape[0]
    for c in range(LANE_BLOCKS):
        o3_ref[pl.ds(c, tm, stride=LANE_BLOCKS), :] = hn[:, c * 128:(c + 1) * 128]
    r = _route(hn, wrh_ref[...], wrl_ref[...], br_ref[...])
    lane = lax.broadcasted_iota(jnp.int32, r.shape, 1).astype(F32)
    oh1 = lane == r[:, 0:1]
    oh2 = lane == r[:, 1:2]
    picked = jnp.where(oh1 | oh2, 1.0, 0.0)
    ri = lax.broadcasted_iota(jnp.int32, (tm, tm), 0)
    ci = lax.broadcasted_iota(jnp.int32, (tm, tm), 1)
    before = _dot(jnp.where(ri > ci, 1.0, 0.0).astype(BF16), picked.astype(BF16)) + run_ref[0:1, :]
    rank1 = jnp.sum(jnp.where(oh1, before, 0.0), axis=1, keepdims=True)
    rank2 = jnp.sum(jnp.where(oh2, before, 0.0), axis=1, keepdims=True)
    r_ref[...] = r + jnp.where(lane == 4.0, rank1, 0.0) + jnp.where(lane == 5.0, rank2, 0.0)
    total = run_ref[0:1, :] + jnp.sum(picked, axis=0, keepdims=True)
    run_ref[...] = jnp.broadcast_to(total, run_ref.shape)
    cnt_ref[...] = jnp.broadcast_to(total, cnt_ref.shape)


def _out_ln_kernel_1(h_ref, p_ref, a1_ref, w1_ref, *rest):
    _out_ln_kernel(1, h_ref, p_ref, a1_ref, w1_ref, None, None, *rest)


def _out_ln_kernel_2(h_ref, a1_ref, w1_ref, a2_ref, w2_ref, *rest):
    _out_ln_kernel(0, h_ref, a1_ref, w1_ref, a2_ref, w2_ref, *rest)


def _out_ln(h, addend, pairs, g, b, router):
    t_rows = h.shape[0]
    tm = OUT_TILE
    row = lambda i: (i, 0)
    const = lambda i: (0, 0)
    ins = [h]
    specs = [pl.BlockSpec((tm, D_MODEL), row)]
    if addend is not None:
        ins.append(addend)
        specs.append(pl.BlockSpec((tm, D_MODEL), row))
    for a, w in pairs:
        ins += [a, w]
        specs += [pl.BlockSpec((tm, a.shape[1]), row), pl.BlockSpec(w.shape, const)]
    ins += [g, b]
    specs += [pl.BlockSpec((1, D_MODEL), const)] * 2
    ins += list(router)
    specs += [pl.BlockSpec((D_MODEL, 128), const), pl.BlockSpec((D_MODEL, 128), const),
              pl.BlockSpec((1, 128), const)]
    body = _out_ln_kernel_1 if addend is not None else _out_ln_kernel_2
    return pl.pallas_call(
        body,
        grid=(t_rows // tm,),
        in_specs=specs,
        out_specs=[pl.BlockSpec((tm, D_MODEL), row), pl.BlockSpec((tm, 128), row),
                   pl.BlockSpec((tm * LANE_BLOCKS, 128), row),
                   pl.BlockSpec((8, 128), const)],
        out_shape=[jax.ShapeDtypeStruct((t_rows, D_MODEL), F32),
                   jax.ShapeDtypeStruct((t_rows, 128), F32),
                   jax.ShapeDtypeStruct((t_rows * LANE_BLOCKS, 128), F32),
                   jax.ShapeDtypeStruct((8, 128), F32)],
        scratch_shapes=[pltpu.VMEM((8, 128), F32)],
        compiler_params=_cparams(1),
        name="out_ln",
    )(*ins)


ROUTER_TILE = 768
MOE_TILE = 256
BIG = 1e30


def _route(h, wh, wl, b):
    hh, hl = _split(h)
    logits = _dot(hh, wh) + _dot(hl, wh) + _dot(hh, wl) + b
    lane = lax.broadcasted_iota(jnp.int32, logits.shape, 1).astype(F32)

    def first_argmax(vals, vmax):
        return jnp.min(jnp.where(vals == vmax, lane, 4096.0), axis=1, keepdims=True)

    is_g = lane < MOE_GROUPS
    lg = jnp.where(is_g, logits, -BIG)
    m = jnp.max(lg, axis=1, keepdims=True)
    gidx = first_argmax(lg, m)
    denom = jnp.sum(jnp.where(is_g, jnp.exp(logits - m), 0.0), axis=1, keepdims=True)
    p_top = 1.0 / denom
    base = MOE_GROUPS + MOE_EPG * gidx
    in_grp = (lane >= base) & (lane < base + MOE_EPG)
    le = jnp.where(in_grp, logits, -BIG)
    v1 = jnp.max(le, axis=1, keepdims=True)
    i1 = first_argmax(le, v1)
    le2 = jnp.where(lane == i1, -BIG, le)
    v2 = jnp.max(le2, axis=1, keepdims=True)
    i2 = first_argmax(le2, v2)
    e21 = jnp.exp(v2 - v1)
    w1 = p_top / (1.0 + e21)
    w2 = p_top * e21 / (1.0 + e21)
    return (jnp.where(lane == 0.0, i1 - MOE_GROUPS, 0.0) + jnp.where(lane == 1.0, i2 - MOE_GROUPS, 0.0)
            + jnp.where(lane == 2.0, w1, 0.0) + jnp.where(lane == 3.0, w2, 0.0))


def _router_weights(w_rg, b_rg, w_re, b_re):
    n_exp = MOE_GROUPS * MOE_EPG
    wr = jnp.concatenate([w_rg.astype(F32), w_re.astype(F32).reshape(D_MODEL, n_exp)], axis=1)
    wr = jnp.pad(wr, ((0, 0), (0, 128 - wr.shape[1])))
    wrh, wrl = _split(wr)
    br = jnp.concatenate([b_rg.astype(F32), b_re.astype(F32).reshape(-1)])
    br = jnp.pad(br, (0, 128 - br.shape[0])).reshape(1, 128)
    return wrh, wrl, br


def _moe_kernel(nv_ref, tok_ref, dst_ref, te_ref, h_hbm, wg_ref, wu_ref, wd_ref, y_hbm,
                xbuf, ybuf, xmat, wgb, wub, wdb, gsem, ssem):
    tm = MOE_TILE
    i = pl.program_id(0)
    nv = nv_ref[0]
    slot = i % 2
    other = 1 - slot

    lb = LANE_BLOCKS

    def buf_rows(row):
        r0 = row * lb
        return pl.ds(r0 if isinstance(row, int) else pl.multiple_of(r0, lb), lb)

    def gather_copy(tile, row, buf_slot):
        src = h_hbm.at[pl.ds(pl.multiple_of(tok_ref[tile * tm + row] * lb, lb), lb)]
        return pltpu.make_async_copy(src, xbuf.at[buf_slot, buf_rows(row)], gsem.at[buf_slot])

    def scatter_copy(tile, row, buf_slot):
        dst = y_hbm.at[pl.ds(pl.multiple_of(dst_ref[tile * tm + row] * lb, lb), lb)]
        return pltpu.make_async_copy(ybuf.at[buf_slot, buf_rows(row)], dst, ssem.at[buf_slot])

    def wait_gather(buf_slot):
        pltpu.make_async_copy(h_hbm.at[pl.ds(0, tm * lb)], xbuf.at[buf_slot], gsem.at[buf_slot]).wait()

    def wait_scatter(buf_slot):
        pltpu.make_async_copy(ybuf.at[buf_slot], y_hbm.at[pl.ds(0, tm * lb)], ssem.at[buf_slot]).wait()

    @pl.when((i == 0) | (te_ref[i] != te_ref[jnp.maximum(i - 1, 0)]))
    def _():
        wgb[...] = wg_ref[0].astype(BF16)
        wub[...] = wu_ref[0].astype(BF16)
        wdb[...] = wd_ref[0].astype(BF16)

    def compute(buf_slot):
        for c in range(lb):
            xmat[:, c * 128:(c + 1) * 128] = xbuf[buf_slot, pl.ds(c, tm, stride=lb), :]
        xb = xmat[...].astype(BF16)
        hg = _dot(xb, wgb[...])
        hu = _dot(xb, wub[...])
        hh = (hg * _sigmoid(hg)) * hu
        y = _dot(hh.astype(BF16), wdb[...])
        for c in range(lb):
            ybuf[buf_slot, pl.ds(c, tm, stride=lb), :] = y[:, c * 128:(c + 1) * 128]

    @pl.when(i == 0)
    def _():
        def body(r, c):
            gather_copy(0, r, 0).start()
            return c
        lax.fori_loop(0, tm, body, 0)
        ybuf[1] = jnp.zeros((tm * lb, 128), F32)
        fill = pltpu.make_async_copy(ybuf.at[1], y_hbm.at[pl.ds(y_hbm.shape[0] - tm * lb, tm * lb)], ssem.at[1])
        fill.start()
        fill.wait()
        wait_gather(0)
        for r in range(tm):
            gather_copy(1, r, 1).start(priority=r % 2)
        compute(0)

    @pl.when((i > 0) & (i < nv))
    def _():
        wait_gather(slot)

        @pl.when(i >= 2)
        def _():
            wait_scatter(slot)

        for r in range(tm):
            gather_copy(i + 1, r, other).start(priority=r % 2)
        for r in range(tm):
            scatter_copy(i - 1, r, other).start(priority=r % 2)
        compute(slot)

    @pl.when(i == nv)
    def _():
        wait_gather(slot)

        @pl.when(i >= 2)
        def _():
            wait_scatter(slot)

        def body(r, c):
            scatter_copy(i - 1, r, other).start()
            return c
        lax.fori_loop(0, tm, body, 0)
        wait_scatter(other)


def _moe(nv, tok, dst, te, h, wg, wu, wd, n_out_rows):
    n_tiles = tok.shape[0] // MOE_TILE
    tm = MOE_TILE
    ex = lambda i, nv, tok, dst, te: (te[i], 0, 0)
    return pl.pallas_call(
        _moe_kernel,
        grid_spec=pltpu.PrefetchScalarGridSpec(
            num_scalar_prefetch=4,
            grid=(n_tiles,),
            in_specs=[pl.BlockSpec(memory_space=pl.ANY),
                      pl.BlockSpec((1, D_MODEL, MOE_FF), ex),
                      pl.BlockSpec((1, D_MODEL, MOE_FF), ex),
                      pl.BlockSpec((1, MOE_FF, D_MODEL), ex)],
            out_specs=pl.BlockSpec(memory_space=pl.ANY),
            scratch_shapes=[pltpu.VMEM((2, tm * LANE_BLOCKS, 128), F32),
                            pltpu.VMEM((2, tm * LANE_BLOCKS, 128), F32),
                            pltpu.VMEM((tm, D_MODEL), F32),
                            pltpu.VMEM((D_MODEL, MOE_FF), BF16),
                            pltpu.VMEM((D_MODEL, MOE_FF), BF16),
                            pltpu.VMEM((MOE_FF, D_MODEL), BF16),
                            pltpu.SemaphoreType.DMA((2,)),
                            pltpu.SemaphoreType.DMA((2,))]),
        out_shape=jax.ShapeDtypeStruct((n_out_rows * LANE_BLOCKS, 128), F32),
        compiler_params=_cparams(1),
        name="moe_experts",
    )(nv, tok, dst, te, h, wg, wu, wd)


MOE_LN_TILE = 256


def _moe_ln_kernel(h_ref, y0_ref, y1_ref, r_ref, g_ref, b_ref, o_ref, acc_ref):
    r = r_ref[...]
    w0 = r[:, 2:3]
    w1 = r[:, 3:4]
    tm = h_ref.shape[0]
    for c in range(LANE_BLOCKS):
        cols = slice(c * 128, (c + 1) * 128)
        rows = pl.ds(c, tm, stride=LANE_BLOCKS)
        acc_ref[:, cols] = DN_ALPHA * h_ref[:, cols] + (w0 * y0_ref[rows, :] + w1 * y1_ref[rows, :])
    o_ref[...] = _layer_norm(acc_ref[...], g_ref[...], b_ref[...])


def _moe_ln(h, y2, rinfo, g, b):
    t_rows = h.shape[0]
    tm = MOE_LN_TILE
    nb = t_rows // tm
    const = lambda i: (0, 0)
    return pl.pallas_call(
        _moe_ln_kernel,
        grid=(nb,),
        in_specs=[pl.BlockSpec((tm, D_MODEL), lambda i: (i, 0)),
                  pl.BlockSpec((tm * LANE_BLOCKS, 128), lambda i: (i, 0)),
                  pl.BlockSpec((tm * LANE_BLOCKS, 128), lambda i: (nb + i, 0)),
                  pl.BlockSpec((tm, 128), lambda i: (i, 0)),
                  pl.BlockSpec((1, D_MODEL), const),
                  pl.BlockSpec((1, D_MODEL), const)],
        out_specs=pl.BlockSpec((tm, D_MODEL), lambda i: (i, 0)),
        out_shape=jax.ShapeDtypeStruct((t_rows, D_MODEL), F32),
        scratch_shapes=[pltpu.VMEM((tm, D_MODEL), F32)],
        compiler_params=_cparams(1),
        name="moe_ln",
    )(h, y2, y2, rinfo, g, b)


def _hier_moe_ln(h, rinfo, h3, counts, layer, w_gate, w_up, w_down, ln_g, ln_b):
    t_rows = h.shape[0]
    n_exp = MOE_GROUPS * MOE_EPG

    n_asg = 2 * t_rows
    tm = MOE_TILE
    e_a = rinfo[:, 0:2].astype(jnp.int32).reshape(n_asg)
    rank_a = rinfo[:, 4:6].astype(jnp.int32).reshape(n_asg)
    asg = jnp.arange(n_asg, dtype=jnp.int32)
    counts = counts[0, :n_exp].astype(jnp.int32)
    tiles = (counts + tm - 1) // tm
    tile_end = jnp.cumsum(tiles)
    tile_start = tile_end - tiles
    pos = jnp.take(tile_start * tm, e_a) + rank_a
    n_tiles = n_asg // tm + n_exp + 1
    n_sorted = n_tiles * tm
    nv = tile_end[-1:].astype(jnp.int32)
    slot_asg = jnp.full((n_sorted,), -1, jnp.int32).at[pos].set(asg)
    is_pad = slot_asg < 0
    tok = jnp.where(is_pad, 0, slot_asg >> 1)
    dump = n_asg + jnp.arange(n_sorted, dtype=jnp.int32) % tm
    dst = jnp.where(is_pad, dump, (slot_asg & 1) * t_rows + (slot_asg >> 1))
    te = jnp.minimum(jnp.sum(jnp.arange(n_tiles)[:, None] >= tile_end[None, :], axis=1), n_exp - 1)
    te = (te + layer * n_exp).astype(jnp.int32)

    y2 = _moe(nv, tok, dst, te, h3,
              w_gate.astype(F32).reshape(-1, D_MODEL, MOE_FF),
              w_up.astype(F32).reshape(-1, D_MODEL, MOE_FF),
              w_down.astype(F32).reshape(-1, MOE_FF, D_MODEL), n_asg + tm)
    return _moe_ln(h, y2, rinfo, ln_g.reshape(1, D_MODEL).astype(F32), ln_b.reshape(1, D_MODEL).astype(F32))


def _layer_ab(h, batch, lp, w_in, s5, w_glu, b_glu, gla_w_gate, gla_b_gate, gla_norm, w_out, ln_g, ln_b,
              router):
    hk = GLA_HEADS * GLA_DK
    hv = GLA_HEADS * GLA_DV
    o_q = S5_WIDTH
    o_g = o_q + 2 * hk + hv
    o_r = o_g + GLA_RANK
    w_in = w_in.astype(F32)
    w_u = w_in[:, :S5_WIDTH].astype(BF16)
    w_main = jnp.concatenate([w_in[:, o_q:o_g], w_in[:, o_r:o_r + hv]], axis=1).astype(BF16)
    w_low = jnp.pad(w_in[:, o_g:o_r], ((0, 0), (0, 128 - GLA_RANK))).astype(BF16)
    u = _s5_uproj(h, w_u)
    z = _s5_core(u, s5, batch)
    p_s5 = _s5_out(z, w_glu.astype(BF16), b_glu.reshape(1, -1).astype(F32), w_out[:S5_WIDTH].astype(BF16))
    proj, g_low = _in_proj(h, w_main, w_low)
    wg = jnp.pad(gla_w_gate.astype(F32), ((0, 128 - GLA_RANK), (0, 0)))
    wgh, wgl = _split(wg)
    y_gla = _gla(proj, g_low, wgh, wgl, gla_b_gate.reshape(1, -1).astype(F32),
                 gla_norm.reshape(1, -1).astype(F32), batch, lp)
    return _out_ln(h, p_s5, [(y_gla, w_out[S5_WIDTH:].astype(BF16))],
                   ln_g.reshape(1, -1).astype(F32), ln_b.reshape(1, -1).astype(F32), router)


def _layer_cd(h, batch, lp, w_in, conv_w, a_log, dt_bias, gdn_norm, lru_conv_w, lru_conv_b,
              lru_w_a, lru_b_a, lru_w_x, lru_b_x, lru_lambda, w_out, ln_g, ln_b, router):
    nq = GDN_QKV
    hv = GDN_HEADS * GDN_DV
    w_in = w_in.astype(F32)
    o_z = nq
    o_b = o_z + hv
    o_a = o_b + GDN_HEADS
    o_x = o_a + GDN_HEADS
    o_gt = o_x + LRU_WIDTH
    w_main = jnp.concatenate([w_in[:, :o_b], w_in[:, o_x:o_gt + LRU_WIDTH]], axis=1).astype(BF16)
    w_ba = jnp.pad(w_in[:, o_b:o_x], ((0, 0), (0, 128 - 2 * GDN_HEADS))).astype(BF16)
    proj, ba = _in_proj(h, w_main, w_ba)

    cw = jnp.pad(conv_w.astype(F32), ((0, 8 - CONV_K), (0, 0)))
    hp = jnp.zeros((8, 128), F32)
    hp = hp.at[0, GDN_HEADS:2 * GDN_HEADS].set(-jnp.exp(a_log.astype(F32)))
    hp = hp.at[1, GDN_HEADS:2 * GDN_HEADS].set(dt_bias.astype(F32))
    y_gdn = _gdn(proj, ba, cw, hp, gdn_norm.reshape(1, -1).astype(F32), batch, lp)

    lcw = jnp.pad(lru_conv_w.astype(F32), ((0, 8 - CONV_K), (0, 0)))
    wax = jnp.concatenate([lru_w_a.astype(F32), lru_w_x.astype(F32)], axis=2).astype(BF16)
    bax = jnp.concatenate([lru_b_a.astype(F32).reshape(LRU_BLOCKS, 1, LRU_BW),
                           lru_b_x.astype(F32).reshape(LRU_BLOCKS, 1, LRU_BW)], axis=2)
    sp = (-LRU_C * jax.nn.softplus(-lru_lambda.astype(F32))).reshape(1, LRU_WIDTH)
    y_lru = _lru(proj, lcw, lru_conv_b.reshape(1, -1).astype(F32), wax, bax, sp, batch, lp)

    return _out_ln(h, None, [(y_gdn, w_out[:hv].astype(BF16)), (y_lru, w_out[hv:].astype(BF16))],
                   ln_g.reshape(1, -1).astype(F32), ln_b.reshape(1, -1).astype(F32), router)


def kernel(x, meta_tokens, ab_w_in, ab_s5_a_re, ab_s5_a_im, ab_s5_log_dt, ab_s5_b_re, ab_s5_b_im,
           ab_s5_c_re, ab_s5_c_im, ab_s5_d, ab_s5_w_glu, ab_s5_b_glu, ab_gla_w_gate, ab_gla_b_gate,
           ab_gla_norm, ab_w_out, cd_w_in, cd_conv_w, cd_gdn_a_log, cd_gdn_dt_bias, cd_gdn_norm,
           cd_lru_conv_w, cd_lru_conv_b, cd_lru_w_a, cd_lru_b_a, cd_lru_w_x, cd_lru_b_x, cd_lru_lambda,
           cd_w_out, moe_w_router_g, moe_b_router_g, moe_w_router_e, moe_b_router_e, moe_w_gate,
           moe_w_up, moe_w_down, ln_mix_g, ln_mix_b, ln_ffn_g, ln_ffn_b):
    batch, seq, _ = x.shape
    length = seq + N_META
    lp = -(-length // ROW_ALIGN) * ROW_ALIGN
    meta = jnp.broadcast_to(meta_tokens.astype(x.dtype)[None], (batch, N_META, D_MODEL))
    h = jnp.concatenate([meta, x, jnp.zeros((batch, lp - length, D_MODEL), x.dtype)], axis=1)
    h = h.reshape(batch * lp, D_MODEL).astype(F32)

    for layer in range(DEPTH):
        j = layer // 2
        router = _router_weights(moe_w_router_g[layer], moe_b_router_g[layer], moe_w_router_e[layer],
                                 moe_b_router_e[layer])
        if layer % 2 == 0:
            s5 = _s5_params(ab_s5_a_re[j], ab_s5_a_im[j], ab_s5_log_dt[j], ab_s5_b_re[j], ab_s5_b_im[j],
                            ab_s5_c_re[j], ab_s5_c_im[j], ab_s5_d[j])
            h, rinfo, h3, counts = _layer_ab(h, batch, lp, ab_w_in[j], s5, ab_s5_w_glu[j], ab_s5_b_glu[j],
                                 ab_gla_w_gate[j], ab_gla_b_gate[j], ab_gla_norm[j], ab_w_out[j],
                                 ln_mix_g[layer], ln_mix_b[layer], router)
        else:
            h, rinfo, h3, counts = _layer_cd(h, batch, lp, cd_w_in[j], cd_conv_w[j], cd_gdn_a_log[j], cd_gdn_dt_bias[j],
                                 cd_gdn_norm[j], cd_lru_conv_w[j], cd_lru_conv_b[j], cd_lru_w_a[j],
                                 cd_lru_b_a[j], cd_lru_w_x[j], cd_lru_b_x[j], cd_lru_lambda[j], cd_w_out[j],
                                 ln_mix_g[layer], ln_mix_b[layer], router)
        h = _hier_moe_ln(h, rinfo, h3, counts, layer, moe_w_gate, moe_w_up, moe_w_down,
                         ln_ffn_g[layer], ln_ffn_b[layer])
    return h.reshape(batch, lp, D_MODEL)[:, N_META:length].astype(x.dtype)
```

```python
import functools
import math

import jax
import jax.numpy as jnp
from jax import lax
from jax.experimental import pallas as pl
from jax.experimental.pallas import tpu as pltpu

F32 = jnp.float32
BF16 = jnp.bfloat16
HIGHEST = lax.Precision.HIGHEST

D_MODEL = 2048
N_META = 16
CONV_K = 4
DEPTH = 2
DN_ALPHA = (2.0 * DEPTH) ** 0.25
LN_EPS = 1e-5
NORM_EPS = 1e-6

S5_WIDTH = 1024
S5_GROUP = 16
S5_GROUPS = 64
S5_STATE = 64
S5_LC = 16
S5_SLABS = 8

GLA_HEADS = 4
GLA_DK = 128
GLA_DV = 256
GLA_RANK = 16
GLA_TAU = 16.0

GDN_HEADS = 8
GDN_DK = 128
GDN_DV = 128

LRU_WIDTH = 1024
LRU_BLOCKS = 8
LRU_BW = 128
LRU_C = 8.0

MOE_GROUPS = 4
MOE_EPG = 8
MOE_FF = 256

LANE_BLOCKS = D_MODEL // 128
CHUNK = 64
ROW_ALIGN = 768
VMEM_LIMIT = 56 * 1024 * 1024

NT_DIMS = (((1,), (1,)), ((), ()))
TN_DIMS = (((0,), (0,)), ((), ()))


def _cparams(n_axes):
    return pltpu.CompilerParams(dimension_semantics=("arbitrary",) * n_axes,
                                vmem_limit_bytes=VMEM_LIMIT)


def _dot(a, b):
    return jnp.dot(a, b, preferred_element_type=F32)


def _dotg(a, b, dims):
    return lax.dot_general(a, b, dims, preferred_element_type=F32)


def _split(a):
    hi = a.astype(BF16)
    lo = (a - hi.astype(F32)).astype(BF16)
    return hi, lo


def _sigmoid(x):
    return 1.0 / (1.0 + jnp.exp(-x))


def _softplus(x):
    return jnp.maximum(x, 0.0) + jnp.log1p(jnp.exp(-jnp.abs(x)))


def _gelu(x):
    return 0.5 * x * (1.0 + jnp.tanh(math.sqrt(2.0 / math.pi) * (x + 0.044715 * (x * x * x))))


def _layer_norm(x, g, b):
    mu = jnp.mean(x, axis=-1, keepdims=True)
    xc = x - mu
    var = jnp.mean(xc * xc, axis=-1, keepdims=True)
    return xc * lax.rsqrt(var + LN_EPS) * g + b


PROJ_TILE = 512
PROJ_COLS = 1024


def _in_proj_kernel(x_ref, w_ref, wg_ref, o_ref, g_ref):
    xb = x_ref[...].astype(BF16)
    for j in range(w_ref.shape[1] // PROJ_COLS):
        cols = slice(j * PROJ_COLS, (j + 1) * PROJ_COLS)
        o_ref[:, cols] = _dot(xb, w_ref[:, cols]).astype(o_ref.dtype)
    g_ref[...] = _dot(xb, wg_ref[...])


def _in_proj(x, w, w_gates):
    m, k = x.shape
    n = w.shape[1]
    tm = PROJ_TILE
    const = lambda i: (0, 0)
    return pl.pallas_call(
        _in_proj_kernel,
        grid=(m // tm,),
        in_specs=[pl.BlockSpec((tm, k), lambda i: (i, 0)),
                  pl.BlockSpec((k, n), const, pipeline_mode=pl.Buffered(1)),
                  pl.BlockSpec((k, 128), const, pipeline_mode=pl.Buffered(1))],
        out_specs=[pl.BlockSpec((tm, n), lambda i: (i, 0)),
                   pl.BlockSpec((tm, 128), lambda i: (i, 0))],
        out_shape=[jax.ShapeDtypeStruct((m, n), BF16),
                   jax.ShapeDtypeStruct((m, 128), F32)],
        compiler_params=_cparams(1),
        name="in_proj",
    )(x, w, w_gates)


S5_TILE_CHUNKS = 32


def _s5_row_perm():
    m = S5_TILE_CHUNKS
    r = jnp.arange(S5_LC * m)
    nat = (r % m) * S5_LC + r // m
    return (nat[:, None] == jnp.arange(S5_LC * m)[None, :]).astype(BF16)


def _s5_uproj_kernel(h_ref, perm_ref, w_ref, u_ref):
    m = S5_TILE_CHUNKS
    lhs = _dot(perm_ref[...], h_ref[...].astype(BF16)).astype(BF16)
    z = _dot(lhs, w_ref[...])
    for t in range(S5_LC):
        for s in range(S5_SLABS):
            u_ref[s, :, t * 128:(t + 1) * 128] = z[t * m:(t + 1) * m, s * 128:(s + 1) * 128]


def _s5_uproj(h, w_u):
    t_rows = h.shape[0]
    n_chunks = t_rows // S5_LC
    m = S5_TILE_CHUNKS
    return pl.pallas_call(
        _s5_uproj_kernel,
        grid=(n_chunks // m,),
        in_specs=[pl.BlockSpec((m * S5_LC, D_MODEL), lambda i: (i, 0)),
                  pl.BlockSpec((m * S5_LC, m * S5_LC), lambda i: (0, 0)),
                  pl.BlockSpec((D_MODEL, S5_WIDTH), lambda i: (0, 0))],
        out_specs=pl.BlockSpec((S5_SLABS, m, S5_LC * 128), lambda i: (0, i, 0)),
        out_shape=jax.ShapeDtypeStruct((S5_SLABS, n_chunks, S5_LC * 128), F32),
        compiler_params=_cparams(1),
        name="s5_uproj",
    )(h, _s5_row_perm(), w_u)


def _s5_core_kernel(u_ref, stack_ref, ms_ref, mot_ref, ak_ref, pre_ref, pim_ref, d_ref, z_ref,
                    sre_ref, sim_ref, xre_ref, xim_ref):
    n = u_ref.shape[1]
    half = 8 * S5_STATE
    u = u_ref[0]
    ub = u.astype(BF16)
    s = _dot(ub, ms_ref[0])
    sre_ref[...] = s[:, :half]
    sim_ref[...] = s[:, half:]

    ak = ak_ref[0]
    pre = pre_ref[0]
    pim = pim_ref[0]
    row = lax.broadcasted_iota(jnp.int32, (8, half), 0)

    def body(rb, carry):
        cre, cim = carry
        r0 = pl.multiple_of(rb * 8, 8)
        xr = sre_ref[pl.ds(r0, 8), :]
        xi = sim_ref[pl.ds(r0, 8), :]
        for idx, k in enumerate((1, 2, 4)):
            are = ak[2 * idx:2 * idx + 1, :]
            aim = ak[2 * idx + 1:2 * idx + 2, :]
            keep = row >= k
            shr = jnp.where(keep, pltpu.roll(xr, k, 0), 0.0)
            shi = jnp.where(keep, pltpu.roll(xi, k, 0), 0.0)
            xr, xi = xr + are * shr - aim * shi, xi + are * shi + aim * shr
        xr = xr + pre * cre - pim * cim
        xi = xi + pre * cim + pim * cre
        first = row == 0
        xre_ref[pl.ds(r0, 8), :] = jnp.where(first, cre, pltpu.roll(xr, 1, 0))
        xim_ref[pl.ds(r0, 8), :] = jnp.where(first, cim, pltpu.roll(xi, 1, 0))
        return xr[7:8, :], xi[7:8, :]

    zero = jnp.zeros((1, half), F32)
    lax.fori_loop(0, n // 8, body, (zero, zero))

    xprev = jnp.concatenate([xre_ref[...], xim_ref[...]], axis=1).astype(BF16)
    y_inter = _dotg(xprev, mot_ref[0], NT_DIMS)
    d = d_ref[0]
    npair = S5_LC // 2
    for q in range(npair):
        yq = _dot(ub[:, :(q + 1) * 256], stack_ref[0, (npair - 1 - q) * 256:, :])
        sl = slice(q * 256, (q + 1) * 256)
        y = yq + y_inter[:, sl] + d[:, sl] * u[:, sl]
        z_ref[0, :, sl] = _gelu(y)


def _s5_core(u, prm, batch):
    n_tot = u.shape[1]
    n = n_tot // batch
    w = S5_LC * 128
    half = 8 * S5_STATE
    slab = lambda s, b: (s, 0, 0)
    return pl.pallas_call(
        _s5_core_kernel,
        grid=(S5_SLABS, batch),
        in_specs=[pl.BlockSpec((1, n, w), lambda s, b: (s, b, 0)),
                  pl.BlockSpec((1, w, 256), slab),
                  pl.BlockSpec((1, w, 2 * half), slab),
                  pl.BlockSpec((1, w, 2 * half), slab),
                  pl.BlockSpec((1, 8, half), slab),
                  pl.BlockSpec((1, 8, half), slab),
                  pl.BlockSpec((1, 8, half), slab),
                  pl.BlockSpec((1, 1, w), slab)],
        out_specs=pl.BlockSpec((1, n, w), lambda s, b: (s, b, 0)),
        out_shape=jax.ShapeDtypeStruct(u.shape, F32),
        scratch_shapes=[pltpu.VMEM((n, half), F32)] * 4,
        compiler_params=_cparams(2),
        name="s5_core",
    )(u, prm["stack"], prm["ms"], prm["mot"], prm["ak"], prm["pre"], prm["pim"], prm["d"])


def _s5_out_kernel(z_ref, perm_ref, wglu_ref, bglu_ref, wout_ref, p_ref, zf_ref):
    m = S5_TILE_CHUNKS
    for t in range(S5_LC):
        for s in range(S5_SLABS):
            zf_ref[t * m:(t + 1) * m, s * 128:(s + 1) * 128] = z_ref[s, :, t * 128:(t + 1) * 128]
    zf = zf_ref[...]
    gate = _sigmoid(_dot(zf.astype(BF16), wglu_ref[...]) + bglu_ref[...])
    gated = (zf * gate).astype(BF16)
    nat = _dotg(perm_ref[...], gated, TN_DIMS).astype(BF16)
    p_ref[...] = _dot(nat, wout_ref[...])


def _s5_out(z, w_glu, b_glu, w_out_top):
    n_chunks = z.shape[1]
    m = S5_TILE_CHUNKS
    return pl.pallas_call(
        _s5_out_kernel,
        grid=(n_chunks // m,),
        in_specs=[pl.BlockSpec((S5_SLABS, m, S5_LC * 128), lambda i: (0, i, 0)),
                  pl.BlockSpec((m * S5_LC, m * S5_LC), lambda i: (0, 0)),
                  pl.BlockSpec((S5_WIDTH, S5_WIDTH), lambda i: (0, 0)),
                  pl.BlockSpec((1, S5_WIDTH), lambda i: (0, 0)),
                  pl.BlockSpec((S5_WIDTH, D_MODEL), lambda i: (0, 0))],
        out_specs=pl.BlockSpec((m * S5_LC, D_MODEL), lambda i: (i, 0)),
        out_shape=jax.ShapeDtypeStruct((n_chunks * S5_LC, D_MODEL), F32),
        scratch_shapes=[pltpu.VMEM((S5_LC * m, S5_WIDTH), F32)],
        compiler_params=_cparams(1),
        name="s5_out",
    )(z, _s5_row_perm(), w_glu, b_glu, w_out_top)


def _s5_params(a_re, a_im, log_dt, b_re, b_im, c_re, c_im, d):
    g, p, c, lc = S5_GROUPS, S5_STATE, S5_GROUP, S5_LC
    a = lax.complex(a_re.astype(F32), a_im.astype(F32))
    adt = a * jnp.exp(log_dt.astype(F32))[:, None]
    zoh = (jnp.exp(adt) - 1.0) / a
    bz = lax.complex(b_re.astype(F32), b_im.astype(F32)) * zoh[:, :, None]
    cc = lax.complex(c_re.astype(F32), c_im.astype(F32))
    jj = jnp.arange(lc + 1, dtype=F32)
    apow = jnp.exp(adt[None] * jj[:, None, None])

    def expand(compact, width):
        rows = compact.shape[-2]
        row_grp = (jnp.arange(rows) % 128) // S5_GROUP
        lane_grp = jnp.arange(8 * width) // width
        tiled = jnp.tile(compact, (1,) * (compact.ndim - 1) + (8,))
        return jnp.where(row_grp[:, None] == lane_grp[None, :], tiled, 0.0)

    kre = jnp.einsum("gop,jgp,gpc->jgco", cc, apow[:lc], bz, precision=HIGHEST).real
    kblk = expand(kre.reshape(lc, S5_SLABS, 128, c), c)
    kpad = jnp.concatenate([jnp.zeros_like(kblk[:1]), kblk], axis=0)
    npair = lc // 2
    dd = jnp.arange(npair)[:, None, None]
    lo_in = jnp.arange(2)[None, :, None]
    lo_out = jnp.arange(2)[None, None, :]
    lag = 2 * dd + lo_out - lo_in
    tiles = kpad[lag + 1]
    tiles = tiles.transpose(3, 0, 1, 4, 2, 5).reshape(S5_SLABS, npair, 256, 256)
    stack = tiles[:, ::-1].reshape(S5_SLABS, npair * 256, 256)

    wst = apow[lc - 1 - jnp.arange(lc)][:, :, :, None] * bz[None]
    wst = wst.reshape(lc, S5_SLABS, 8, p, c).transpose(1, 0, 2, 4, 3).reshape(S5_SLABS, lc * 128, p)
    ms = jnp.concatenate([expand(wst.real, p), expand(wst.imag, p)], axis=2)

    zc = cc[None] * apow[1:lc + 1][:, :, None, :]
    zc = zc.reshape(lc, S5_SLABS, 8 * c, p).transpose(1, 0, 2, 3).reshape(S5_SLABS, lc * 128, p)
    mot = jnp.concatenate([expand(zc.real, p), expand(-zc.imag, p)], axis=2)

    def slab_layout(x):
        return x.reshape(x.shape[:-2] + (S5_SLABS, 8 * p))

    kk = jnp.array([1.0, 2.0, 4.0], F32)
    a_k = slab_layout(jnp.exp(adt[None] * (lc * kk)[:, None, None]))
    ak = jnp.stack([a_k[0].real, a_k[0].imag, a_k[1].real, a_k[1].imag, a_k[2].real, a_k[2].imag,
                    jnp.zeros_like(a_k[0].real), jnp.zeros_like(a_k[0].real)], axis=1)
    rr = jnp.arange(1, 9, dtype=F32)
    a_r = slab_layout(jnp.exp(adt[None] * (lc * rr)[:, None, None]))
    pre = a_r.real.transpose(1, 0, 2)
    pim = a_r.imag.transpose(1, 0, 2)
    dsk = jnp.tile(d.astype(F32).reshape(S5_SLABS, 1, 128), (1, 1, lc))
    return {"stack": stack.astype(BF16), "ms": ms.astype(BF16), "mot": mot.astype(BF16),
            "ak": ak, "pre": pre, "pim": pim, "d": dsk}


GLA_TILE = 256


def _gla_kernel(q_ref, k_ref, v_ref, r_ref, gl_ref, wgh_ref, wgl_ref, bg_ref, ng_ref, o_ref, st_ref):
    @pl.when(pl.program_id(1) == 0)
    def _():
        st_ref[...] = jnp.zeros_like(st_ref)

    rowi = lax.broadcasted_iota(jnp.int32, (CHUNK, CHUNK), 0)
    coli = lax.broadcasted_iota(jnp.int32, (CHUNK, CHUNK), 1)
    tri = rowi >= coli
    wgh = wgh_ref[...]
    wgl = wgl_ref[...]
    bg = bg_ref[...]
    ng = ng_ref[...]
    scale = GLA_DK ** -0.5

    nch = GLA_TILE // CHUNK
    ti = lax.broadcasted_iota(jnp.int32, (GLA_TILE, GLA_TILE), 0)
    tj = lax.broadcasted_iota(jnp.int32, (GLA_TILE, GLA_TILE), 1)
    tri_t = jnp.where((ti >= tj) & ((ti // CHUNK) == (tj // CHUNK)), 1.0, 0.0).astype(BF16)
    glh, gll = _split(gl_ref[...])
    pre = _dot(glh, wgh) + _dot(gll, wgh) + _dot(glh, wgl) + bg
    logf = (jnp.minimum(pre, 0.0) - jnp.log1p(jnp.exp(-jnp.abs(pre)))) * (1.0 / GLA_TAU)
    lh, ll = _split(logf)
    bcum = _dot(tri_t, lh) + _dot(tri_t, ll)
    ebc = jnp.exp(bcum)
    kk = k_ref[...].astype(F32)
    q_in = (q_ref[...].astype(F32) * scale * ebc).astype(BF16)
    k_in = (kk * jnp.exp(-bcum)).astype(BF16)

    heads = range(GLA_HEADS)
    chunks = range(nch)
    items = [(c, h) for c in chunks for h in heads]
    rws = [slice(c * CHUNK, (c + 1) * CHUNK) for c in chunks]
    sks = [slice(h * GLA_DK, (h + 1) * GLA_DK) for h in heads]
    svs = [slice(h * GLA_DV, (h + 1) * GLA_DV) for h in heads]
    blast = [bcum[(c + 1) * CHUNK - 1:(c + 1) * CHUNK, :] for c in chunks]
    k_st = [(kk[rws[c], :] * jnp.exp(blast[c] - bcum[rws[c], :])).astype(BF16) for c in chunks]
    dec = [jnp.exp(blast[c]) for c in chunks]
    vh = {(c, h): v_ref[rws[c], svs[h]].astype(BF16) for c, h in items}
    att = {(c, h): _dotg(q_in[rws[c], sks[h]], k_in[rws[c], sks[h]], NT_DIMS) for c, h in items}
    kv = {(c, h): _dotg(vh[c, h], k_st[c][:, sks[h]], TN_DIMS) for c, h in items}
    o_intra = {it: _dot(jnp.where(tri, att[it], 0.0).astype(BF16), vh[it]) for it in items}
    st = [st_ref[h] for h in heads]
    for c in chunks:
        o_inter = [_dotg(q_in[rws[c], sks[h]], st[h].astype(BF16), NT_DIMS) for h in heads]
        for h in heads:
            o = o_intra[c, h] + o_inter[h]
            on = o * lax.rsqrt(jnp.mean(o * o, axis=1, keepdims=True) + NORM_EPS) * ng
            rr = r_ref[rws[c], svs[h]].astype(F32)
            o_ref[rws[c], svs[h]] = (on * (rr * _sigmoid(rr))).astype(o_ref.dtype)
        st = [st[h] * dec[c][:, sks[h]] + kv[c, h] for h in heads]
    for h in heads:
        st_ref[h] = st[h]


def _gla(proj, g_low, wgh, wgl, bg, ng, batch, lp):
    t_rows = proj.shape[0]
    nt = lp // GLA_TILE
    hk = GLA_HEADS * GLA_DK
    hv = GLA_HEADS * GLA_DV
    rowmap = lambda col: (lambda b, t: (b * nt + t, col))
    const = lambda b, t: (0, 0)
    return pl.pallas_call(
        _gla_kernel,
        grid=(batch, nt),
        in_specs=[pl.BlockSpec((GLA_TILE, hk), rowmap(0)),
                  pl.BlockSpec((GLA_TILE, hk), rowmap(1)),
                  pl.BlockSpec((GLA_TILE, hv), rowmap(1)),
                  pl.BlockSpec((GLA_TILE, hv), rowmap(2)),
                  pl.BlockSpec((GLA_TILE, 128), rowmap(0)),
                  pl.BlockSpec((128, hk), const),
                  pl.BlockSpec((128, hk), const),
                  pl.BlockSpec((1, hk), const),
                  pl.BlockSpec((1, GLA_DV), const)],
        out_specs=pl.BlockSpec((GLA_TILE, hv), rowmap(0)),
        out_shape=jax.ShapeDtypeStruct((t_rows, hv), BF16),
        scratch_shapes=[pltpu.VMEM((GLA_HEADS, GLA_DV, GLA_DK), F32)],
        compiler_params=_cparams(2),
        name="gla",
    )(proj, proj, proj, proj, g_low, wgh, wgl, bg, ng)


GDN_TILE = 256
GDN_GROUP = 4
GDN_QKV = 3 * GDN_HEADS * GDN_DK


def _gdn_kernel(qkv_ref, z_ref, ba_ref, cw_ref, hp_ref, ng_ref, o_ref,
                ext_ref, act_ref, beta_ref, g_ref, s_ref):
    tile = GDN_TILE

    @pl.when(pl.program_id(1) == 0)
    def _():
        ext_ref[0:8, :] = jnp.zeros((8, GDN_QKV), F32)
        s_ref[...] = jnp.zeros_like(s_ref)

    @pl.when(pl.program_id(1) > 0)
    def _():
        ext_ref[0:8, :] = ext_ref[tile:tile + 8, :]

    ext_ref[8:tile + 8, :] = qkv_ref[...].astype(F32)
    qscale = GDN_DK ** -0.5
    for cb in range(GDN_QKV // 128):
        cols = slice(cb * 128, (cb + 1) * 128)
        w = cw_ref[:, cols]
        y = (w[3:4] * ext_ref[8:tile + 8, cols] + w[2:3] * ext_ref[7:tile + 7, cols]
             + w[1:2] * ext_ref[6:tile + 6, cols] + w[0:1] * ext_ref[5:tile + 5, cols])
        a = y * _sigmoid(y)
        if cb < 2 * GDN_HEADS:
            a = a * lax.rsqrt(jnp.sum(a * a, axis=1, keepdims=True) + NORM_EPS)
            if cb < GDN_HEADS:
                a = a * qscale
        act_ref[:, cols] = a

    ba = ba_ref[...]
    beta_ref[...] = _sigmoid(ba)
    g_ref[...] = hp_ref[0:1, :] * _softplus(ba + hp_ref[1:2, :])

    rowi = lax.broadcasted_iota(jnp.int32, (CHUNK, CHUNK), 0)
    coli = lax.broadcasted_iota(jnp.int32, (CHUNK, CHUNK), 1)
    tri = rowi >= coli
    strict = rowi > coli
    tri_b = jnp.where(tri, 1.0, 0.0).astype(BF16)
    upper_b = jnp.where(rowi <= coli, 1.0, 0.0).astype(BF16)
    eye = jnp.where(rowi == coli, 1.0, 0.0)
    level_masks = []
    for lvl in range(6):
        bi = rowi >> lvl
        bj = coli >> lvl
        level_masks.append(((bi & 1) == 1) & (bj == bi - 1))
    ng = ng_ref[...]

    heads = range(GDN_HEADS)
    grp = range(GDN_GROUP)
    items = [(j, h) for j in grp for h in heads]
    hcol = lambda t, h: t[:, GDN_HEADS + h:GDN_HEADS + h + 1]

    def group(gi, carry):
        base = gi * (GDN_GROUP * CHUNK)
        rows = [pl.ds(pl.multiple_of(base + j * CHUNK, CHUNK), CHUNK) for j in grp]
        gsp = [_split(g_ref[rows[j], :]) for j in grp]
        gc = [_dot(tri_b, gsp[j][0]) + _dot(tri_b, gsp[j][1]) for j in grp]
        gct = [_dotg(gsp[j][0], upper_b, TN_DIMS) + _dotg(gsp[j][1], upper_b, TN_DIMS) for j in grp]
        eg = [jnp.exp(gc[j]) for j in grp]
        elast = [jnp.exp(gc[j][CHUNK - 1:CHUNK, :] - gc[j]) for j in grp]
        dlast = [jnp.exp(gc[j][CHUNK - 1:CHUNK, :]) for j in grp]
        beta = [beta_ref[rows[j], :] for j in grp]
        qs = {(j, h): act_ref[rows[j], h * GDN_DK:(h + 1) * GDN_DK] for j, h in items}
        ks = {(j, h): act_ref[rows[j], (GDN_HEADS + h) * GDN_DK:(GDN_HEADS + h + 1) * GDN_DK] for j, h in items}
        vs = {(j, h): act_ref[rows[j], (2 * GDN_HEADS + h) * GDN_DK:(2 * GDN_HEADS + h + 1) * GDN_DK]
              for j, h in items}
        bcols = {(j, h): beta[j][:, h:h + 1] for j, h in items}
        kbs = {it: ks[it] * bcols[it] for it in items}
        khb = {it: ks[it].astype(BF16) for it in items}
        gams = {(j, h): jnp.exp(jnp.where(tri, hcol(gc[j], h) - gct[j][GDN_HEADS + h:GDN_HEADS + h + 1, :], -1e30))
                for j, h in items}
        kk = {it: _dotg(kbs[it].astype(BF16), khb[it], NT_DIMS) for it in items}
        qk = {it: _dotg(qs[it].astype(BF16), khb[it], NT_DIMS) for it in items}
        mm = {it: jnp.where(strict, kk[it] * gams[it], 0.0) for it in items}
        tinv = {it: eye - jnp.where(level_masks[0], mm[it], 0.0) for it in items}
        for lvl in range(1, 6):
            tb = {it: tinv[it].astype(BF16) for it in items}
            xs = {it: _dot(jnp.where(level_masks[lvl], mm[it], 0.0).astype(BF16), tb[it]).astype(BF16)
                  for it in items}
            tinv = {it: tinv[it] - _dot(tb[it], xs[it]) for it in items}
        rhs = {(j, h): jnp.concatenate([vs[j, h] * bcols[j, h], kbs[j, h] * hcol(eg[j], h)], axis=1).astype(BF16)
               for j, h in items}
        sol = {it: _dot(tinv[it].astype(BF16), rhs[it]) for it in items}
        aqk = {it: jnp.where(tri, qk[it] * gams[it], 0.0).astype(BF16) for it in items}
        qdec = {(j, h): (qs[j, h] * hcol(eg[j], h)).astype(BF16) for j, h in items}
        kst = {(j, h): (ks[j, h] * hcol(elast[j], h)).astype(BF16) for j, h in items}
        for j in grp:
            s_old = [s_ref[h] for h in heads]
            sb = [s_old[h].astype(BF16) for h in heads]
            ws = [_dot(sol[j, h][:, GDN_DV:].astype(BF16), sb[h]) for h in heads]
            qsd = [_dot(qdec[j, h], sb[h]) for h in heads]
            vnb = [(sol[j, h][:, :GDN_DV] - ws[h]).astype(BF16) for h in heads]
            av = [_dot(aqk[j, h], vnb[h]) for h in heads]
            kv = [_dotg(kst[j, h], vnb[h], TN_DIMS) for h in heads]
            for h in heads:
                o = qsd[h] + av[h]
                on = o * lax.rsqrt(jnp.mean(o * o, axis=1, keepdims=True) + NORM_EPS) * ng
                zz = z_ref[rows[j], h * GDN_DV:(h + 1) * GDN_DV].astype(F32)
                o_ref[rows[j], h * GDN_DV:(h + 1) * GDN_DV] = (on * (zz * _sigmoid(zz))).astype(o_ref.dtype)
                s_ref[h] = s_old[h] * hcol(dlast[j], h) + kv[h]
        return carry

    lax.fori_loop(0, tile // (GDN_GROUP * CHUNK), group, 0)


def _gdn(proj, ba, cw, hp, ng, batch, lp):
    t_rows = proj.shape[0]
    nt = lp // GDN_TILE
    hv = GDN_HEADS * GDN_DV
    rowmap = lambda col: (lambda b, t: (b * nt + t, col))
    const = lambda b, t: (0, 0)
    return pl.pallas_call(
        _gdn_kernel,
        grid=(batch, nt),
        in_specs=[pl.BlockSpec((GDN_TILE, GDN_QKV), rowmap(0)),
                  pl.BlockSpec((GDN_TILE, hv), rowmap(3)),
                  pl.BlockSpec((GDN_TILE, 128), rowmap(0)),
                  pl.BlockSpec((8, GDN_QKV), const),
                  pl.BlockSpec((8, 128), const),
                  pl.BlockSpec((1, GDN_DV), const)],
        out_specs=pl.BlockSpec((GDN_TILE, hv), rowmap(0)),
        out_shape=jax.ShapeDtypeStruct((t_rows, hv), BF16),
        scratch_shapes=[pltpu.VMEM((GDN_TILE + 8, GDN_QKV), F32),
                        pltpu.VMEM((GDN_TILE, GDN_QKV), F32),
                        pltpu.VMEM((GDN_TILE, 128), F32),
                        pltpu.VMEM((GDN_TILE, 128), F32),
                        pltpu.VMEM((GDN_HEADS, GDN_DK, GDN_DV), F32)],
        compiler_params=_cparams(2),
        name="gdn",
    )(proj, proj, ba, cw, hp, ng)


LRU_TILE = 256


def _lru_kernel(x_ref, gate_ref, cw_ref, cb_ref, wax_ref, bax_ref, sp_ref, o_ref,
                ext_ref, a_ref, b_ref, h_ref):
    tile = LRU_TILE

    @pl.when(pl.program_id(1) == 0)
    def _():
        ext_ref[0:8, :] = jnp.zeros((8, LRU_WIDTH), F32)
        h_ref[...] = jnp.zeros_like(h_ref)

    @pl.when(pl.program_id(1) > 0)
    def _():
        ext_ref[0:8, :] = ext_ref[tile:tile + 8, :]

    ext_ref[8:tile + 8, :] = x_ref[...].astype(F32)
    for blk in range(LRU_BLOCKS):
        cols = slice(blk * LRU_BW, (blk + 1) * LRU_BW)
        w = cw_ref[:, cols]
        xc = (w[3:4] * ext_ref[8:tile + 8, cols] + w[2:3] * ext_ref[7:tile + 7, cols]
              + w[1:2] * ext_ref[6:tile + 6, cols] + w[0:1] * ext_ref[5:tile + 5, cols]
              + cb_ref[:, cols])
        pre = _dot(xc.astype(BF16), wax_ref[blk]) + bax_ref[blk]
        r = _sigmoid(pre[:, :LRU_BW])
        i = _sigmoid(pre[:, LRU_BW:])
        log_a = sp_ref[:, cols] * r
        a_ref[:, cols] = jnp.exp(log_a)
        th = jnp.tanh(log_a)
        b_ref[:, cols] = jnp.sqrt(-2.0 * th / (1.0 - th)) * (i * xc)

    row = lax.broadcasted_iota(jnp.int32, (8, LRU_WIDTH), 0)

    def body(rb, carry):
        r0 = pl.multiple_of(rb * 8, 8)
        aa = a_ref[pl.ds(r0, 8), :]
        bb = b_ref[pl.ds(r0, 8), :]
        for k in (1, 2, 4):
            keep = row >= k
            a_sh = jnp.where(keep, pltpu.roll(aa, k, 0), 1.0)
            b_sh = jnp.where(keep, pltpu.roll(bb, k, 0), 0.0)
            bb = aa * b_sh + bb
            aa = aa * a_sh
        hs = bb + aa * carry
        gt = gate_ref[pl.ds(r0, 8), :].astype(F32)
        o_ref[pl.ds(r0, 8), :] = (hs * _gelu(gt)).astype(o_ref.dtype)
        return hs[7:8, :]

    h_ref[...] = lax.fori_loop(0, tile // 8, body, h_ref[...])


def _lru(proj, cw, cb, wax, bax, sp, batch, lp):
    t_rows = proj.shape[0]
    nt = lp // LRU_TILE
    rowmap = lambda col: (lambda b, t: (b * nt + t, col))
    const = lambda b, t: (0, 0)
    return pl.pallas_call(
        _lru_kernel,
        grid=(batch, nt),
        in_specs=[pl.BlockSpec((LRU_TILE, LRU_WIDTH), rowmap(4)),
                  pl.BlockSpec((LRU_TILE, LRU_WIDTH), rowmap(5)),
                  pl.BlockSpec((8, LRU_WIDTH), const),
                  pl.BlockSpec((1, LRU_WIDTH), const),
                  pl.BlockSpec((LRU_BLOCKS, LRU_BW, 2 * LRU_BW), lambda b, t: (0, 0, 0)),
                  pl.BlockSpec((LRU_BLOCKS, 1, 2 * LRU_BW), lambda b, t: (0, 0, 0)),
                  pl.BlockSpec((1, LRU_WIDTH), const)],
        out_specs=pl.BlockSpec((LRU_TILE, LRU_WIDTH), rowmap(0)),
        out_shape=jax.ShapeDtypeStruct((t_rows, LRU_WIDTH), BF16),
        scratch_shapes=[pltpu.VMEM((LRU_TILE + 8, LRU_WIDTH), F32),
                        pltpu.VMEM((LRU_TILE, LRU_WIDTH), F32),
                        pltpu.VMEM((LRU_TILE, LRU_WIDTH), F32),
                        pltpu.VMEM((1, LRU_WIDTH), F32)],
        compiler_params=_cparams(2),
        name="lru",
    )(proj, proj, cw, cb, wax, bax, sp)


OUT_TILE = 256


def _out_ln_kernel(n_add, h_ref, *refs):
    adds = refs[:n_add]
    (a1_ref, w1_ref, a2_ref, w2_ref, g_ref, b_ref, wrh_ref, wrl_ref, br_ref,
     o_ref, r_ref, o3_ref, cnt_ref, run_ref) = refs[n_add:]

    @pl.when(pl.program_id(0) == 0)
    def _():
        run_ref[...] = jnp.zeros_like(run_ref)

    acc = DN_ALPHA * h_ref[...]
    for r in adds:
        acc = acc + r[...]
    acc = acc + _dot(a1_ref[...], w1_ref[...])
    if a2_ref is not None:
        acc = acc + _dot(a2_ref[...], w2_ref[...])
    hn = _layer_norm(acc, g_ref[...], b_ref[...])
    o_ref[...] = hn
    tm = hn.shape[0]
    for c in range(LANE_BLOCKS):
        o3_ref[pl.ds(c, tm, stride=LANE_BLOCKS), :] = hn[:, c * 128:(c + 1) * 128]
    r = _route(hn, wrh_ref[...], wrl_ref[...], br_ref[...])
    lane = lax.broadcasted_iota(jnp.int32, r.shape, 1).astype(F32)
    oh1 = lane == r[:, 0:1]
    oh2 = lane == r[:, 1:2]
    picked = jnp.where(oh1 | oh2, 1.0, 0.0)
    ri = lax.broadcasted_iota(jnp.int32, (tm, tm), 0)
    ci = lax.broadcasted_iota(jnp.int32, (tm, tm), 1)
    before = _dot(jnp.where(ri > ci, 1.0, 0.0).astype(BF16), picked.astype(BF16)) + run_ref[0:1, :]
    rank1 = jnp.sum(jnp.where(oh1, before, 0.0), axis=1, keepdims=True)
    rank2 = jnp.sum(jnp.where(oh2, before, 0.0), axis=1, keepdims=True)
    r_ref[...] = r + jnp.where(lane == 4.0, rank1, 0.0) + jnp.where(lane == 5.0, rank2, 0.0)
    total = run_ref[0:1, :] + jnp.sum(picked, axis=0, keepdims=True)
    run_ref[...] = jnp.broadcast_to(total, run_ref.shape)
    cnt_ref[...] = jnp.broadcast_to(total, cnt_ref.shape)


def _out_ln_kernel_1(h_ref, p_ref, a1_ref, w1_ref, *rest):
    _out_ln_kernel(1, h_ref, p_ref, a1_ref, w1_ref, None, None, *rest)


def _out_ln_kernel_2(h_ref, a1_ref, w1_ref, a2_ref, w2_ref, *rest):
    _out_ln_kernel(0, h_ref, a1_ref, w1_ref, a2_ref, w2_ref, *rest)


def _out_ln(h, addend, pairs, g, b, router):
    t_rows = h.shape[0]
    tm = OUT_TILE
    row = lambda i: (i, 0)
    const = lambda i: (0, 0)
    ins = [h]
    specs = [pl.BlockSpec((tm, D_MODEL), row)]
    if addend is not None:
        ins.append(addend)
        specs.append(pl.BlockSpec((tm, D_MODEL), row))
    for a, w in pairs:
        ins += [a, w]
        specs += [pl.BlockSpec((tm, a.shape[1]), row), pl.BlockSpec(w.shape, const)]
    ins += [g, b]
    specs += [pl.BlockSpec((1, D_MODEL), const)] * 2
    ins += list(router)
    specs += [pl.BlockSpec((D_MODEL, 128), const), pl.BlockSpec((D_MODEL, 128), const),
              pl.BlockSpec((1, 128), const)]
    body = _out_ln_kernel_1 if addend is not None else _out_ln_kernel_2
    return pl.pallas_call(
        body,
        grid=(t_rows // tm,),
        in_specs=specs,
        out_specs=[pl.BlockSpec((tm, D_MODEL), row), pl.BlockSpec((tm, 128), row),
                   pl.BlockSpec((tm * LANE_BLOCKS, 128), row),
                   pl.BlockSpec((8, 128), row)],
        out_shape=[jax.ShapeDtypeStruct((t_rows, D_MODEL), F32),
                   jax.ShapeDtypeStruct((t_rows, 128), F32),
                   jax.ShapeDtypeStruct((t_rows * LANE_BLOCKS, 128), F32),
                   jax.ShapeDtypeStruct((t_rows // tm * 8, 128), F32)],
        scratch_shapes=[pltpu.VMEM((8, 128), F32)],
        compiler_params=_cparams(1),
        name="out_ln",
    )(*ins)


ROUTER_TILE = 768
MOE_TILE = 256
BIG = 1e30


def _route(h, wh, wl, b):
    hh, hl = _split(h)
    logits = _dot(hh, wh) + _dot(hl, wh) + _dot(hh, wl) + b
    lane = lax.broadcasted_iota(jnp.int32, logits.shape, 1).astype(F32)

    def first_argmax(vals, vmax):
        return jnp.min(jnp.where(vals == vmax, lane, 4096.0), axis=1, keepdims=True)

    is_g = lane < MOE_GROUPS
    lg = jnp.where(is_g, logits, -BIG)
    m = jnp.max(lg, axis=1, keepdims=True)
    gidx = first_argmax(lg, m)
    denom = jnp.sum(jnp.where(is_g, jnp.exp(logits - m), 0.0), axis=1, keepdims=True)
    p_top = 1.0 / denom
    base = MOE_GROUPS + MOE_EPG * gidx
    in_grp = (lane >= base) & (lane < base + MOE_EPG)
    le = jnp.where(in_grp, logits, -BIG)
    v1 = jnp.max(le, axis=1, keepdims=True)
    i1 = first_argmax(le, v1)
    le2 = jnp.where(lane == i1, -BIG, le)
    v2 = jnp.max(le2, axis=1, keepdims=True)
    i2 = first_argmax(le2, v2)
    e21 = jnp.exp(v2 - v1)
    w1 = p_top / (1.0 + e21)
    w2 = p_top * e21 / (1.0 + e21)
    return (jnp.where(lane == 0.0, i1 - MOE_GROUPS, 0.0) + jnp.where(lane == 1.0, i2 - MOE_GROUPS, 0.0)
            + jnp.where(lane == 2.0, w1, 0.0) + jnp.where(lane == 3.0, w2, 0.0))


def _router_weights(w_rg, b_rg, w_re, b_re):
    n_exp = MOE_GROUPS * MOE_EPG
    wr = jnp.concatenate([w_rg.astype(F32), w_re.astype(F32).reshape(D_MODEL, n_exp)], axis=1)
    wr = jnp.pad(wr, ((0, 0), (0, 128 - wr.shape[1])))
    wrh, wrl = _split(wr)
    br = jnp.concatenate([b_rg.astype(F32), b_re.astype(F32).reshape(-1)])
    br = jnp.pad(br, (0, 128 - br.shape[0])).reshape(1, 128)
    return wrh, wrl, br


def _moe_kernel(nv_ref, tok_ref, te_ref, h_hbm, wg_ref, wu_ref, wd_ref, y_ref,
                xbuf, xmat, wgb, wub, wdb, gsem):
    tm = MOE_TILE
    i = pl.program_id(0)
    nv = nv_ref[0]
    slot = i % 2
    other = 1 - slot

    lb = LANE_BLOCKS

    def buf_rows(row):
        r0 = row * lb
        return pl.ds(r0 if isinstance(row, int) else pl.multiple_of(r0, lb), lb)

    def gather_copy(tile, row, buf_slot):
        src = h_hbm.at[pl.ds(pl.multiple_of(tok_ref[tile * tm + row] * lb, lb), lb)]
        return pltpu.make_async_copy(src, xbuf.at[buf_slot, buf_rows(row)], gsem.at[buf_slot])

    def wait_gather(buf_slot):
        pltpu.make_async_copy(h_hbm.at[pl.ds(0, tm * lb)], xbuf.at[buf_slot], gsem.at[buf_slot]).wait()

    @pl.when((i == 0) | (te_ref[i] != te_ref[jnp.maximum(i - 1, 0)]))
    def _():
        wgb[...] = wg_ref[0].astype(BF16)
        wub[...] = wu_ref[0].astype(BF16)
        wdb[...] = wd_ref[0].astype(BF16)

    def compute(buf_slot):
        for c in range(lb):
            xmat[:, c * 128:(c + 1) * 128] = xbuf[buf_slot, pl.ds(c, tm, stride=lb), :]
        xb = xmat[...].astype(BF16)
        hg = _dot(xb, wgb[...])
        hu = _dot(xb, wub[...])
        hh = (hg * _sigmoid(hg)) * hu
        y = _dot(hh.astype(BF16), wdb[...])
        for c in range(lb):
            y_ref[pl.ds(c, tm, stride=lb), :] = y[:, c * 128:(c + 1) * 128]

    @pl.when(i == 0)
    def _():
        def body(r, c):
            gather_copy(0, r, 0).start()
            return c
        lax.fori_loop(0, tm, body, 0)
        wait_gather(0)
        for r in range(tm):
            gather_copy(1, r, 1).start(priority=r % 2)
        compute(0)

    @pl.when((i > 0) & (i < nv))
    def _():
        wait_gather(slot)
        for r in range(tm):
            gather_copy(i + 1, r, other).start(priority=r % 2)
        compute(slot)

    @pl.when(i == nv)
    def _():
        wait_gather(slot)
        compute(slot)

    @pl.when(i > nv)
    def _():
        y_ref[...] = jnp.zeros_like(y_ref)


def _moe(nv, tok, te, h, wg, wu, wd):
    n_tiles = tok.shape[0] // MOE_TILE
    tm = MOE_TILE
    ex = lambda i, nv, tok, te: (te[i], 0, 0)
    return pl.pallas_call(
        _moe_kernel,
        grid_spec=pltpu.PrefetchScalarGridSpec(
            num_scalar_prefetch=3,
            grid=(n_tiles,),
            in_specs=[pl.BlockSpec(memory_space=pl.ANY),
                      pl.BlockSpec((1, D_MODEL, MOE_FF), ex),
                      pl.BlockSpec((1, D_MODEL, MOE_FF), ex),
                      pl.BlockSpec((1, MOE_FF, D_MODEL), ex)],
            out_specs=pl.BlockSpec((tm * LANE_BLOCKS, 128), lambda i, nv, tok, te: (i, 0)),
            scratch_shapes=[pltpu.VMEM((2, tm * LANE_BLOCKS, 128), F32),
                            pltpu.VMEM((tm, D_MODEL), F32),
                            pltpu.VMEM((D_MODEL, MOE_FF), BF16),
                            pltpu.VMEM((D_MODEL, MOE_FF), BF16),
                            pltpu.VMEM((MOE_FF, D_MODEL), BF16),
                            pltpu.SemaphoreType.DMA((2,))]),
        out_shape=jax.ShapeDtypeStruct((n_tiles * tm * LANE_BLOCKS, 128), F32),
        compiler_params=_cparams(1),
        name="moe_experts",
    )(nv, tok, te, h, wg, wu, wd)


COMBINE_RUN = 8
COMBINE_SLOTS = 768
COMBINE_PIECES = 2 * OUT_TILE // COMBINE_RUN + MOE_GROUPS * MOE_EPG


def _moe_combine_kernel(np_ref, psrc_ref, pdst_ref, h_ref, r_ref, off_ref, g_ref, b_ref, ys_hbm, o_ref,
                        buf, ymat, sem):
    i = pl.program_id(0)
    lb = LANE_BLOCKS
    run = COMBINE_RUN * lb
    tm = h_ref.shape[0]

    @pl.when(i == 0)
    def _():
        buf[...] = jnp.zeros_like(buf)

    def piece(k):
        src = pl.multiple_of(psrc_ref[i * COMBINE_PIECES + k] * lb, lb)
        dst = pl.multiple_of(pdst_ref[i * COMBINE_PIECES + k] * lb, lb)
        return pltpu.make_async_copy(ys_hbm.at[pl.ds(src, run)], buf.at[pl.ds(dst, run)], sem)

    def start(k, c):
        piece(k).start()
        return c

    lax.fori_loop(0, np_ref[i], start, 0)

    r = r_ref[...]
    lane = lax.broadcasted_iota(jnp.int32, r.shape, 1).astype(F32)
    off = off_ref[0:1, :]
    s1 = jnp.sum(jnp.where(lane == r[:, 0:1], off, 0.0), axis=1, keepdims=True) + r[:, 4:5]
    s2 = jnp.sum(jnp.where(lane == r[:, 1:2], off, 0.0), axis=1, keepdims=True) + r[:, 5:6]
    slot = lax.broadcasted_iota(jnp.int32, (tm, COMBINE_SLOTS), 1).astype(F32)
    p1 = jnp.where(slot == s1, 1.0, 0.0).astype(BF16)
    p2 = jnp.where(slot == s2, 1.0, 0.0).astype(BF16)

    def wait(k, c):
        piece(k).wait()
        return c

    lax.fori_loop(0, np_ref[i], wait, 0)
    for c in range(lb):
        ymat[:, c * 128:(c + 1) * 128] = buf[pl.ds(c, COMBINE_SLOTS, stride=lb), :].astype(BF16)
    yb = ymat[...]
    ffn = r[:, 2:3] * _dot(p1, yb) + r[:, 3:4] * _dot(p2, yb)
    o_ref[...] = _layer_norm(DN_ALPHA * h_ref[...] + ffn, g_ref[...], b_ref[...])


def _moe_combine(n_pieces, piece_src, piece_dst, h, rinfo, off, g, b, ys):
    t_rows = h.shape[0]
    tm = OUT_TILE
    const = lambda i, *_: (0, 0)
    row = lambda i, *_: (i, 0)
    return pl.pallas_call(
        _moe_combine_kernel,
        grid_spec=pltpu.PrefetchScalarGridSpec(
            num_scalar_prefetch=3,
            grid=(t_rows // tm,),
            in_specs=[pl.BlockSpec((tm, D_MODEL), row),
                      pl.BlockSpec((tm, 128), row),
                      pl.BlockSpec((8, 128), row),
                      pl.BlockSpec((1, D_MODEL), const),
                      pl.BlockSpec((1, D_MODEL), const),
                      pl.BlockSpec(memory_space=pl.ANY)],
            out_specs=pl.BlockSpec((tm, D_MODEL), row),
            scratch_shapes=[pltpu.VMEM((COMBINE_SLOTS * LANE_BLOCKS, 128), F32),
                            pltpu.VMEM((COMBINE_SLOTS, D_MODEL), BF16),
                            pltpu.SemaphoreType.DMA(())]),
        out_shape=jax.ShapeDtypeStruct((t_rows, D_MODEL), F32),
        compiler_params=_cparams(1),
        name="moe_combine",
    )(n_pieces, piece_src, piece_dst, h, rinfo, off, g, b, ys)


def _hier_moe_ln(h, rinfo, h3, counts, layer, w_gate, w_up, w_down, ln_g, ln_b):
    t_rows = h.shape[0]
    n_exp = MOE_GROUPS * MOE_EPG

    n_asg = 2 * t_rows
    tm = MOE_TILE
    e_a = rinfo[:, 0:2].astype(jnp.int32).reshape(n_asg)
    rank_a = rinfo[:, 4:6].astype(jnp.int32).reshape(n_asg)
    asg = jnp.arange(n_asg, dtype=jnp.int32)
    after = counts[::8, :n_exp].astype(jnp.int32)
    before = jnp.concatenate([jnp.zeros_like(after[:1]), after[:-1]], axis=0)
    total = after[-1]
    tiles = (total + tm - 1) // tm
    tile_end = jnp.cumsum(tiles)
    tile_start = tile_end - tiles
    pos = jnp.take(tile_start * tm, e_a) + rank_a
    n_tiles = n_asg // tm + n_exp + 1
    n_sorted = n_tiles * tm
    nv = tile_end[-1:].astype(jnp.int32)
    slot_asg = jnp.full((n_sorted,), -1, jnp.int32).at[pos].set(asg)
    tok = jnp.where(slot_asg < 0, 0, slot_asg >> 1)
    te = jnp.minimum(jnp.sum(jnp.arange(n_tiles)[:, None] >= tile_end[None, :], axis=1), n_exp - 1)
    te = (te + layer * n_exp).astype(jnp.int32)

    ys = _moe(nv, tok, te, h3,
              w_gate.astype(F32).reshape(-1, D_MODEL, MOE_FF),
              w_up.astype(F32).reshape(-1, D_MODEL, MOE_FF),
              w_down.astype(F32).reshape(-1, MOE_FF, D_MODEL))

    run = COMBINE_RUN
    pieces = (after - before + run - 1) // run
    pc_end = jnp.cumsum(pieces, axis=1)
    pc_start = pc_end - pieces
    kk = jnp.arange(COMBINE_PIECES, dtype=jnp.int32)
    ek = jnp.minimum(jnp.sum(kk[None, :, None] >= pc_end[:, None, :], axis=2), n_exp - 1)
    m = kk[None, :] - jnp.take_along_axis(pc_start, ek, axis=1)
    piece_src = jnp.take(tile_start * tm, ek) + jnp.take_along_axis(before, ek, axis=1) + m * run
    piece_src = jnp.minimum(piece_src, n_sorted - run)
    piece_dst = jnp.minimum((jnp.take_along_axis(pc_start, ek, axis=1) + m) * run, COMBINE_SLOTS - run)
    off = jnp.pad((pc_start * run - before).astype(F32), ((0, 0), (0, 128 - n_exp)))
    off = jnp.repeat(off, 8, axis=0)
    return _moe_combine(pc_end[:, -1].astype(jnp.int32), piece_src.reshape(-1).astype(jnp.int32),
                        piece_dst.reshape(-1).astype(jnp.int32), h, rinfo, off,
                        ln_g.reshape(1, D_MODEL).astype(F32), ln_b.reshape(1, D_MODEL).astype(F32), ys)


def _layer_ab(h, batch, lp, w_in, s5, w_glu, b_glu, gla_w_gate, gla_b_gate, gla_norm, w_out, ln_g, ln_b,
              router):
    hk = GLA_HEADS * GLA_DK
    hv = GLA_HEADS * GLA_DV
    o_q = S5_WIDTH
    o_g = o_q + 2 * hk + hv
    o_r = o_g + GLA_RANK
    w_in = w_in.astype(F32)
    w_u = w_in[:, :S5_WIDTH].astype(BF16)
    w_main = jnp.concatenate([w_in[:, o_q:o_g], w_in[:, o_r:o_r + hv]], axis=1).astype(BF16)
    w_low = jnp.pad(w_in[:, o_g:o_r], ((0, 0), (0, 128 - GLA_RANK))).astype(BF16)
    u = _s5_uproj(h, w_u)
    z = _s5_core(u, s5, batch)
    p_s5 = _s5_out(z, w_glu.astype(BF16), b_glu.reshape(1, -1).astype(F32), w_out[:S5_WIDTH].astype(BF16))
    proj, g_low = _in_proj(h, w_main, w_low)
    wg = jnp.pad(gla_w_gate.astype(F32), ((0, 128 - GLA_RANK), (0, 0)))
    wgh, wgl = _split(wg)
    y_gla = _gla(proj, g_low, wgh, wgl, gla_b_gate.reshape(1, -1).astype(F32),
                 gla_norm.reshape(1, -1).astype(F32), batch, lp)
    return _out_ln(h, p_s5, [(y_gla, w_out[S5_WIDTH:].astype(BF16))],
                   ln_g.reshape(1, -1).astype(F32), ln_b.reshape(1, -1).astype(F32), router)


def _layer_cd(h, batch, lp, w_in, conv_w, a_log, dt_bias, gdn_norm, lru_conv_w, lru_conv_b,
              lru_w_a, lru_b_a, lru_w_x, lru_b_x, lru_lambda, w_out, ln_g, ln_b, router):
    nq = GDN_QKV
    hv = GDN_HEADS * GDN_DV
    w_in = w_in.astype(F32)
    o_z = nq
    o_b = o_z + hv
    o_a = o_b + GDN_HEADS
    o_x = o_a + GDN_HEADS
    o_gt = o_x + LRU_WIDTH
    w_main = jnp.concatenate([w_in[:, :o_b], w_in[:, o_x:o_gt + LRU_WIDTH]], axis=1).astype(BF16)
    w_ba = jnp.pad(w_in[:, o_b:o_x], ((0, 0), (0, 128 - 2 * GDN_HEADS))).astype(BF16)
    proj, ba = _in_proj(h, w_main, w_ba)

    cw = jnp.pad(conv_w.astype(F32), ((0, 8 - CONV_K), (0, 0)))
    hp = jnp.zeros((8, 128), F32)
    hp = hp.at[0, GDN_HEADS:2 * GDN_HEADS].set(-jnp.exp(a_log.astype(F32)))
    hp = hp.at[1, GDN_HEADS:2 * GDN_HEADS].set(dt_bias.astype(F32))
    y_gdn = _gdn(proj, ba, cw, hp, gdn_norm.reshape(1, -1).astype(F32), batch, lp)

    lcw = jnp.pad(lru_conv_w.astype(F32), ((0, 8 - CONV_K), (0, 0)))
    wax = jnp.concatenate([lru_w_a.astype(F32), lru_w_x.astype(F32)], axis=2).astype(BF16)
    bax = jnp.concatenate([lru_b_a.astype(F32).reshape(LRU_BLOCKS, 1, LRU_BW),
                           lru_b_x.astype(F32).reshape(LRU_BLOCKS, 1, LRU_BW)], axis=2)
    sp = (-LRU_C * jax.nn.softplus(-lru_lambda.astype(F32))).reshape(1, LRU_WIDTH)
    y_lru = _lru(proj, lcw, lru_conv_b.reshape(1, -1).astype(F32), wax, bax, sp, batch, lp)

    return _out_ln(h, None, [(y_gdn, w_out[:hv].astype(BF16)), (y_lru, w_out[hv:].astype(BF16))],
                   ln_g.reshape(1, -1).astype(F32), ln_b.reshape(1, -1).astype(F32), router)


def kernel(x, meta_tokens, ab_w_in, ab_s5_a_re, ab_s5_a_im, ab_s5_log_dt, ab_s5_b_re, ab_s5_b_im,
           ab_s5_c_re, ab_s5_c_im, ab_s5_d, ab_s5_w_glu, ab_s5_b_glu, ab_gla_w_gate, ab_gla_b_gate,
           ab_gla_norm, ab_w_out, cd_w_in, cd_conv_w, cd_gdn_a_log, cd_gdn_dt_bias, cd_gdn_norm,
           cd_lru_conv_w, cd_lru_conv_b, cd_lru_w_a, cd_lru_b_a, cd_lru_w_x, cd_lru_b_x, cd_lru_lambda,
           cd_w_out, moe_w_router_g, moe_b_router_g, moe_w_router_e, moe_b_router_e, moe_w_gate,
           moe_w_up, moe_w_down, ln_mix_g, ln_mix_b, ln_ffn_g, ln_ffn_b):
    batch, seq, _ = x.shape
    length = seq + N_META
    lp = -(-length // ROW_ALIGN) * ROW_ALIGN
    meta = jnp.broadcast_to(meta_tokens.astype(x.dtype)[None], (batch, N_META, D_MODEL))
    h = jnp.concatenate([meta, x, jnp.zeros((batch, lp - length, D_MODEL), x.dtype)], axis=1)
    h = h.reshape(batch * lp, D_MODEL).astype(F32)

    for layer in range(DEPTH):
        j = layer // 2
        router = _router_weights(moe_w_router_g[layer], moe_b_router_g[layer], moe_w_router_e[layer],
                                 moe_b_router_e[layer])
        if layer % 2 == 0:
            s5 = _s5_params(ab_s5_a_re[j], ab_s5_a_im[j], ab_s5_log_dt[j], ab_s5_b_re[j], ab_s5_b_im[j],
                            ab_s5_c_re[j], ab_s5_c_im[j], ab_s5_d[j])
            h, rinfo, h3, counts = _layer_ab(h, batch, lp, ab_w_in[j], s5, ab_s5_w_glu[j], ab_s5_b_glu[j],
                                 ab_gla_w_gate[j], ab_gla_b_gate[j], ab_gla_norm[j], ab_w_out[j],
                                 ln_mix_g[layer], ln_mix_b[layer], router)
        else:
            h, rinfo, h3, counts = _layer_cd(h, batch, lp, cd_w_in[j], cd_conv_w[j], cd_gdn_a_log[j], cd_gdn_dt_bias[j],
                                 cd_gdn_norm[j], cd_lru_conv_w[j], cd_lru_conv_b[j], cd_lru_w_a[j],
                                 cd_lru_b_a[j], cd_lru_w_x[j], cd_lru_b_x[j], cd_lru_lambda[j], cd_w_out[j],
                                 ln_mix_g[layer], ln_mix_b[layer], router)
        h = _hier_moe_ln(h, rinfo, h3, counts, layer, moe_w_gate, moe_w_up, moe_w_down,
                         ln_ffn_g[layer], ln_ffn_b[layer])
    return h.reshape(batch, lp, D_MODEL)[:, N_META:length].astype(x.dtype)
```

```python
import functools
import math

import jax
import jax.numpy as jnp
from jax import lax
from jax.experimental import pallas as pl
from jax.experimental.pallas import tpu as pltpu

F32 = jnp.float32
BF16 = jnp.bfloat16
HIGHEST = lax.Precision.HIGHEST

D_MODEL = 2048
N_META = 16
CONV_K = 4
DEPTH = 2
DN_ALPHA = (2.0 * DEPTH) ** 0.25
LN_EPS = 1e-5
NORM_EPS = 1e-6

S5_WIDTH = 1024
S5_GROUP = 16
S5_GROUPS = 64
S5_STATE = 64
S5_LC = 16
S5_SLABS = 8

GLA_HEADS = 4
GLA_DK = 128
GLA_DV = 256
GLA_RANK = 16
GLA_TAU = 16.0

GDN_HEADS = 8
GDN_DK = 128
GDN_DV = 128

LRU_WIDTH = 1024
LRU_BLOCKS = 8
LRU_BW = 128
LRU_C = 8.0

MOE_GROUPS = 4
MOE_EPG = 8
MOE_FF = 256

LANE_BLOCKS = D_MODEL // 128
CHUNK = 64
ROW_ALIGN = 768
VMEM_LIMIT = 56 * 1024 * 1024

NT_DIMS = (((1,), (1,)), ((), ()))
TN_DIMS = (((0,), (0,)), ((), ()))


def _cparams(n_axes):
    return pltpu.CompilerParams(dimension_semantics=("arbitrary",) * n_axes,
                                vmem_limit_bytes=VMEM_LIMIT)


def _dot(a, b):
    return jnp.dot(a, b, preferred_element_type=F32)


def _dotg(a, b, dims):
    return lax.dot_general(a, b, dims, preferred_element_type=F32)


def _split(a):
    hi = a.astype(BF16)
    lo = (a - hi.astype(F32)).astype(BF16)
    return hi, lo


def _sigmoid(x):
    return 1.0 / (1.0 + jnp.exp(-x))


def _softplus(x):
    return jnp.maximum(x, 0.0) + jnp.log1p(jnp.exp(-jnp.abs(x)))


def _gelu(x):
    return 0.5 * x * (1.0 + jnp.tanh(math.sqrt(2.0 / math.pi) * (x + 0.044715 * (x * x * x))))


def _layer_norm(x, g, b):
    mu = jnp.mean(x, axis=-1, keepdims=True)
    xc = x - mu
    var = jnp.mean(xc * xc, axis=-1, keepdims=True)
    return xc * lax.rsqrt(var + LN_EPS) * g + b


PROJ_TILE = 512
PROJ_COLS = 1024


def _in_proj_kernel(x_ref, w_ref, wg_ref, o_ref, g_ref):
    xb = x_ref[...].astype(BF16)
    for j in range(w_ref.shape[1] // PROJ_COLS):
        cols = slice(j * PROJ_COLS, (j + 1) * PROJ_COLS)
        o_ref[:, cols] = _dot(xb, w_ref[:, cols]).astype(o_ref.dtype)
    g_ref[...] = _dot(xb, wg_ref[...])


def _in_proj(x, w, w_gates):
    m, k = x.shape
    n = w.shape[1]
    tm = PROJ_TILE
    const = lambda i: (0, 0)
    return pl.pallas_call(
        _in_proj_kernel,
        grid=(m // tm,),
        in_specs=[pl.BlockSpec((tm, k), lambda i: (i, 0)),
                  pl.BlockSpec((k, n), const, pipeline_mode=pl.Buffered(1)),
                  pl.BlockSpec((k, 128), const, pipeline_mode=pl.Buffered(1))],
        out_specs=[pl.BlockSpec((tm, n), lambda i: (i, 0)),
                   pl.BlockSpec((tm, 128), lambda i: (i, 0))],
        out_shape=[jax.ShapeDtypeStruct((m, n), BF16),
                   jax.ShapeDtypeStruct((m, 128), F32)],
        compiler_params=_cparams(1),
        name="in_proj",
    )(x, w, w_gates)


S5_TILE_CHUNKS = 32


def _s5_row_perm():
    m = S5_TILE_CHUNKS
    r = jnp.arange(S5_LC * m)
    nat = (r % m) * S5_LC + r // m
    return (nat[:, None] == jnp.arange(S5_LC * m)[None, :]).astype(BF16)


def _s5_uproj_kernel(h_ref, perm_ref, w_ref, u_ref):
    m = S5_TILE_CHUNKS
    lhs = _dot(perm_ref[...], h_ref[...].astype(BF16)).astype(BF16)
    z = _dot(lhs, w_ref[...])
    for t in range(S5_LC):
        for s in range(S5_SLABS):
            u_ref[s, :, t * 128:(t + 1) * 128] = z[t * m:(t + 1) * m, s * 128:(s + 1) * 128]


def _s5_uproj(h, w_u):
    t_rows = h.shape[0]
    n_chunks = t_rows // S5_LC
    m = S5_TILE_CHUNKS
    return pl.pallas_call(
        _s5_uproj_kernel,
        grid=(n_chunks // m,),
        in_specs=[pl.BlockSpec((m * S5_LC, D_MODEL), lambda i: (i, 0)),
                  pl.BlockSpec((m * S5_LC, m * S5_LC), lambda i: (0, 0)),
                  pl.BlockSpec((D_MODEL, S5_WIDTH), lambda i: (0, 0))],
        out_specs=pl.BlockSpec((S5_SLABS, m, S5_LC * 128), lambda i: (0, i, 0)),
        out_shape=jax.ShapeDtypeStruct((S5_SLABS, n_chunks, S5_LC * 128), F32),
        compiler_params=_cparams(1),
        name="s5_uproj",
    )(h, _s5_row_perm(), w_u)


def _s5_core_kernel(u_ref, stack_ref, ms_ref, mot_ref, ak_ref, pre_ref, pim_ref, d_ref, z_ref,
                    sre_ref, sim_ref, xre_ref, xim_ref):
    n = u_ref.shape[1]
    half = 8 * S5_STATE
    u = u_ref[0]
    ub = u.astype(BF16)
    s = _dot(ub, ms_ref[0])
    sre_ref[...] = s[:, :half]
    sim_ref[...] = s[:, half:]

    ak = ak_ref[0]
    pre = pre_ref[0]
    pim = pim_ref[0]
    row = lax.broadcasted_iota(jnp.int32, (8, half), 0)

    def body(rb, carry):
        cre, cim = carry
        r0 = pl.multiple_of(rb * 8, 8)
        xr = sre_ref[pl.ds(r0, 8), :]
        xi = sim_ref[pl.ds(r0, 8), :]
        for idx, k in enumerate((1, 2, 4)):
            are = ak[2 * idx:2 * idx + 1, :]
            aim = ak[2 * idx + 1:2 * idx + 2, :]
            keep = row >= k
            shr = jnp.where(keep, pltpu.roll(xr, k, 0), 0.0)
            shi = jnp.where(keep, pltpu.roll(xi, k, 0), 0.0)
            xr, xi = xr + are * shr - aim * shi, xi + are * shi + aim * shr
        xr = xr + pre * cre - pim * cim
        xi = xi + pre * cim + pim * cre
        first = row == 0
        xre_ref[pl.ds(r0, 8), :] = jnp.where(first, cre, pltpu.roll(xr, 1, 0))
        xim_ref[pl.ds(r0, 8), :] = jnp.where(first, cim, pltpu.roll(xi, 1, 0))
        return xr[7:8, :], xi[7:8, :]

    zero = jnp.zeros((1, half), F32)
    lax.fori_loop(0, n // 8, body, (zero, zero))

    xprev = jnp.concatenate([xre_ref[...], xim_ref[...]], axis=1).astype(BF16)
    y_inter = _dotg(xprev, mot_ref[0], NT_DIMS)
    d = d_ref[0]
    npair = S5_LC // 2
    for q in range(npair):
        yq = _dot(ub[:, :(q + 1) * 256], stack_ref[0, (npair - 1 - q) * 256:, :])
        sl = slice(q * 256, (q + 1) * 256)
        y = yq + y_inter[:, sl] + d[:, sl] * u[:, sl]
        z_ref[0, :, sl] = _gelu(y)


def _s5_core(u, prm, batch):
    n_tot = u.shape[1]
    n = n_tot // batch
    w = S5_LC * 128
    half = 8 * S5_STATE
    slab = lambda s, b: (s, 0, 0)
    return pl.pallas_call(
        _s5_core_kernel,
        grid=(S5_SLABS, batch),
        in_specs=[pl.BlockSpec((1, n, w), lambda s, b: (s, b, 0)),
                  pl.BlockSpec((1, w, 256), slab),
                  pl.BlockSpec((1, w, 2 * half), slab),
                  pl.BlockSpec((1, w, 2 * half), slab),
                  pl.BlockSpec((1, 8, half), slab),
                  pl.BlockSpec((1, 8, half), slab),
                  pl.BlockSpec((1, 8, half), slab),
                  pl.BlockSpec((1, 1, w), slab)],
        out_specs=pl.BlockSpec((1, n, w), lambda s, b: (s, b, 0)),
        out_shape=jax.ShapeDtypeStruct(u.shape, F32),
        scratch_shapes=[pltpu.VMEM((n, half), F32)] * 4,
        compiler_params=_cparams(2),
        name="s5_core",
    )(u, prm["stack"], prm["ms"], prm["mot"], prm["ak"], prm["pre"], prm["pim"], prm["d"])


def _s5_out_kernel(z_ref, perm_ref, wglu_ref, bglu_ref, wout_ref, p_ref, zf_ref):
    m = S5_TILE_CHUNKS
    for t in range(S5_LC):
        for s in range(S5_SLABS):
            zf_ref[t * m:(t + 1) * m, s * 128:(s + 1) * 128] = z_ref[s, :, t * 128:(t + 1) * 128]
    zf = zf_ref[...]
    gate = _sigmoid(_dot(zf.astype(BF16), wglu_ref[...]) + bglu_ref[...])
    gated = (zf * gate).astype(BF16)
    nat = _dotg(perm_ref[...], gated, TN_DIMS).astype(BF16)
    p_ref[...] = _dot(nat, wout_ref[...])


def _s5_out(z, w_glu, b_glu, w_out_top):
    n_chunks = z.shape[1]
    m = S5_TILE_CHUNKS
    return pl.pallas_call(
        _s5_out_kernel,
        grid=(n_chunks // m,),
        in_specs=[pl.BlockSpec((S5_SLABS, m, S5_LC * 128), lambda i: (0, i, 0)),
                  pl.BlockSpec((m * S5_LC, m * S5_LC), lambda i: (0, 0)),
                  pl.BlockSpec((S5_WIDTH, S5_WIDTH), lambda i: (0, 0)),
                  pl.BlockSpec((1, S5_WIDTH), lambda i: (0, 0)),
                  pl.BlockSpec((S5_WIDTH, D_MODEL), lambda i: (0, 0))],
        out_specs=pl.BlockSpec((m * S5_LC, D_MODEL), lambda i: (i, 0)),
        out_shape=jax.ShapeDtypeStruct((n_chunks * S5_LC, D_MODEL), F32),
        scratch_shapes=[pltpu.VMEM((S5_LC * m, S5_WIDTH), F32)],
        compiler_params=_cparams(1),
        name="s5_out",
    )(z, _s5_row_perm(), w_glu, b_glu, w_out_top)


def _s5_params(a_re, a_im, log_dt, b_re, b_im, c_re, c_im, d):
    g, p, c, lc = S5_GROUPS, S5_STATE, S5_GROUP, S5_LC
    a = lax.complex(a_re.astype(F32), a_im.astype(F32))
    adt = a * jnp.exp(log_dt.astype(F32))[:, None]
    zoh = (jnp.exp(adt) - 1.0) / a
    bz = lax.complex(b_re.astype(F32), b_im.astype(F32)) * zoh[:, :, None]
    cc = lax.complex(c_re.astype(F32), c_im.astype(F32))
    jj = jnp.arange(lc + 1, dtype=F32)
    apow = jnp.exp(adt[None] * jj[:, None, None])

    def expand(compact, width):
        rows = compact.shape[-2]
        row_grp = (jnp.arange(rows) % 128) // S5_GROUP
        lane_grp = jnp.arange(8 * width) // width
        tiled = jnp.tile(compact, (1,) * (compact.ndim - 1) + (8,))
        return jnp.where(row_grp[:, None] == lane_grp[None, :], tiled, 0.0)

    kre = jnp.einsum("gop,jgp,gpc->jgco", cc, apow[:lc], bz, precision=HIGHEST).real
    kblk = expand(kre.reshape(lc, S5_SLABS, 128, c), c)
    kpad = jnp.concatenate([jnp.zeros_like(kblk[:1]), kblk], axis=0)
    npair = lc // 2
    dd = jnp.arange(npair)[:, None, None]
    lo_in = jnp.arange(2)[None, :, None]
    lo_out = jnp.arange(2)[None, None, :]
    lag = 2 * dd + lo_out - lo_in
    tiles = kpad[lag + 1]
    tiles = tiles.transpose(3, 0, 1, 4, 2, 5).reshape(S5_SLABS, npair, 256, 256)
    stack = tiles[:, ::-1].reshape(S5_SLABS, npair * 256, 256)

    wst = apow[lc - 1 - jnp.arange(lc)][:, :, :, None] * bz[None]
    wst = wst.reshape(lc, S5_SLABS, 8, p, c).transpose(1, 0, 2, 4, 3).reshape(S5_SLABS, lc * 128, p)
    ms = jnp.concatenate([expand(wst.real, p), expand(wst.imag, p)], axis=2)

    zc = cc[None] * apow[1:lc + 1][:, :, None, :]
    zc = zc.reshape(lc, S5_SLABS, 8 * c, p).transpose(1, 0, 2, 3).reshape(S5_SLABS, lc * 128, p)
    mot = jnp.concatenate([expand(zc.real, p), expand(-zc.imag, p)], axis=2)

    def slab_layout(x):
        return x.reshape(x.shape[:-2] + (S5_SLABS, 8 * p))

    kk = jnp.array([1.0, 2.0, 4.0], F32)
    a_k = slab_layout(jnp.exp(adt[None] * (lc * kk)[:, None, None]))
    ak = jnp.stack([a_k[0].real, a_k[0].imag, a_k[1].real, a_k[1].imag, a_k[2].real, a_k[2].imag,
                    jnp.zeros_like(a_k[0].real), jnp.zeros_like(a_k[0].real)], axis=1)
    rr = jnp.arange(1, 9, dtype=F32)
    a_r = slab_layout(jnp.exp(adt[None] * (lc * rr)[:, None, None]))
    pre = a_r.real.transpose(1, 0, 2)
    pim = a_r.imag.transpose(1, 0, 2)
    dsk = jnp.tile(d.astype(F32).reshape(S5_SLABS, 1, 128), (1, 1, lc))
    return {"stack": stack.astype(BF16), "ms": ms.astype(BF16), "mot": mot.astype(BF16),
            "ak": ak, "pre": pre, "pim": pim, "d": dsk}


GLA_TILE = 256


def _gla_kernel(q_ref, k_ref, v_ref, r_ref, gl_ref, wgh_ref, wgl_ref, bg_ref, ng_ref, o_ref, st_ref):
    @pl.when(pl.program_id(1) == 0)
    def _():
        st_ref[...] = jnp.zeros_like(st_ref)

    rowi = lax.broadcasted_iota(jnp.int32, (CHUNK, CHUNK), 0)
    coli = lax.broadcasted_iota(jnp.int32, (CHUNK, CHUNK), 1)
    tri = rowi >= coli
    wgh = wgh_ref[...]
    wgl = wgl_ref[...]
    bg = bg_ref[...]
    ng = ng_ref[...]
    scale = GLA_DK ** -0.5

    nch = GLA_TILE // CHUNK
    ti = lax.broadcasted_iota(jnp.int32, (GLA_TILE, GLA_TILE), 0)
    tj = lax.broadcasted_iota(jnp.int32, (GLA_TILE, GLA_TILE), 1)
    tri_t = jnp.where((ti >= tj) & ((ti // CHUNK) == (tj // CHUNK)), 1.0, 0.0).astype(BF16)
    glh, gll = _split(gl_ref[...])
    pre = _dot(glh, wgh) + _dot(gll, wgh) + _dot(glh, wgl) + bg
    logf = (jnp.minimum(pre, 0.0) - jnp.log1p(jnp.exp(-jnp.abs(pre)))) * (1.0 / GLA_TAU)
    lh, ll = _split(logf)
    bcum = _dot(tri_t, lh) + _dot(tri_t, ll)
    ebc = jnp.exp(bcum)
    kk = k_ref[...].astype(F32)
    q_in = (q_ref[...].astype(F32) * scale * ebc).astype(BF16)
    k_in = (kk * jnp.exp(-bcum)).astype(BF16)

    heads = range(GLA_HEADS)
    chunks = range(nch)
    items = [(c, h) for c in chunks for h in heads]
    rws = [slice(c * CHUNK, (c + 1) * CHUNK) for c in chunks]
    sks = [slice(h * GLA_DK, (h + 1) * GLA_DK) for h in heads]
    svs = [slice(h * GLA_DV, (h + 1) * GLA_DV) for h in heads]
    blast = [bcum[(c + 1) * CHUNK - 1:(c + 1) * CHUNK, :] for c in chunks]
    k_st = [(kk[rws[c], :] * jnp.exp(blast[c] - bcum[rws[c], :])).astype(BF16) for c in chunks]
    dec = [jnp.exp(blast[c]) for c in chunks]
    vh = {(c, h): v_ref[rws[c], svs[h]].astype(BF16) for c, h in items}
    att = {(c, h): _dotg(q_in[rws[c], sks[h]], k_in[rws[c], sks[h]], NT_DIMS) for c, h in items}
    kv = {(c, h): _dotg(vh[c, h], k_st[c][:, sks[h]], TN_DIMS) for c, h in items}
    o_intra = {it: _dot(jnp.where(tri, att[it], 0.0).astype(BF16), vh[it]) for it in items}
    st = [st_ref[h] for h in heads]
    for c in chunks:
        o_inter = [_dotg(q_in[rws[c], sks[h]], st[h].astype(BF16), NT_DIMS) for h in heads]
        for h in heads:
            o = o_intra[c, h] + o_inter[h]
            on = o * lax.rsqrt(jnp.mean(o * o, axis=1, keepdims=True) + NORM_EPS) * ng
            rr = r_ref[rws[c], svs[h]].astype(F32)
            o_ref[rws[c], svs[h]] = (on * (rr * _sigmoid(rr))).astype(o_ref.dtype)
        st = [st[h] * dec[c][:, sks[h]] + kv[c, h] for h in heads]
    for h in heads:
        st_ref[h] = st[h]


def _gla(proj, g_low, wgh, wgl, bg, ng, batch, lp):
    t_rows = proj.shape[0]
    nt = lp // GLA_TILE
    hk = GLA_HEADS * GLA_DK
    hv = GLA_HEADS * GLA_DV
    rowmap = lambda col: (lambda b, t: (b * nt + t, col))
    const = lambda b, t: (0, 0)
    return pl.pallas_call(
        _gla_kernel,
        grid=(batch, nt),
        in_specs=[pl.BlockSpec((GLA_TILE, hk), rowmap(0)),
                  pl.BlockSpec((GLA_TILE, hk), rowmap(1)),
                  pl.BlockSpec((GLA_TILE, hv), rowmap(1)),
                  pl.BlockSpec((GLA_TILE, hv), rowmap(2)),
                  pl.BlockSpec((GLA_TILE, 128), rowmap(0)),
                  pl.BlockSpec((128, hk), const),
                  pl.BlockSpec((128, hk), const),
                  pl.BlockSpec((1, hk), const),
                  pl.BlockSpec((1, GLA_DV), const)],
        out_specs=pl.BlockSpec((GLA_TILE, hv), rowmap(0)),
        out_shape=jax.ShapeDtypeStruct((t_rows, hv), BF16),
        scratch_shapes=[pltpu.VMEM((GLA_HEADS, GLA_DV, GLA_DK), F32)],
        compiler_params=_cparams(2),
        name="gla",
    )(proj, proj, proj, proj, g_low, wgh, wgl, bg, ng)


GDN_TILE = 256
GDN_GROUP = 4
GDN_QKV = 3 * GDN_HEADS * GDN_DK


def _gdn_kernel(qkv_ref, z_ref, ba_ref, cw_ref, hp_ref, ng_ref, o_ref,
                ext_ref, act_ref, beta_ref, g_ref, s_ref):
    tile = GDN_TILE

    @pl.when(pl.program_id(1) == 0)
    def _():
        ext_ref[0:8, :] = jnp.zeros((8, GDN_QKV), F32)
        s_ref[...] = jnp.zeros_like(s_ref)

    @pl.when(pl.program_id(1) > 0)
    def _():
        ext_ref[0:8, :] = ext_ref[tile:tile + 8, :]

    ext_ref[8:tile + 8, :] = qkv_ref[...].astype(F32)
    qscale = GDN_DK ** -0.5
    for cb in range(GDN_QKV // 128):
        cols = slice(cb * 128, (cb + 1) * 128)
        w = cw_ref[:, cols]
        y = (w[3:4] * ext_ref[8:tile + 8, cols] + w[2:3] * ext_ref[7:tile + 7, cols]
             + w[1:2] * ext_ref[6:tile + 6, cols] + w[0:1] * ext_ref[5:tile + 5, cols])
        a = y * _sigmoid(y)
        if cb < 2 * GDN_HEADS:
            a = a * lax.rsqrt(jnp.sum(a * a, axis=1, keepdims=True) + NORM_EPS)
            if cb < GDN_HEADS:
                a = a * qscale
        act_ref[:, cols] = a

    ba = ba_ref[...]
    beta_ref[...] = _sigmoid(ba)
    g_ref[...] = hp_ref[0:1, :] * _softplus(ba + hp_ref[1:2, :])

    rowi = lax.broadcasted_iota(jnp.int32, (CHUNK, CHUNK), 0)
    coli = lax.broadcasted_iota(jnp.int32, (CHUNK, CHUNK), 1)
    tri = rowi >= coli
    strict = rowi > coli
    tri_b = jnp.where(tri, 1.0, 0.0).astype(BF16)
    upper_b = jnp.where(rowi <= coli, 1.0, 0.0).astype(BF16)
    eye = jnp.where(rowi == coli, 1.0, 0.0)
    level_masks = []
    for lvl in range(6):
        bi = rowi >> lvl
        bj = coli >> lvl
        level_masks.append(((bi & 1) == 1) & (bj == bi - 1))
    ng = ng_ref[...]

    heads = range(GDN_HEADS)
    grp = range(GDN_GROUP)
    items = [(j, h) for j in grp for h in heads]
    hcol = lambda t, h: t[:, GDN_HEADS + h:GDN_HEADS + h + 1]

    def group(gi, carry):
        base = gi * (GDN_GROUP * CHUNK)
        rows = [pl.ds(pl.multiple_of(base + j * CHUNK, CHUNK), CHUNK) for j in grp]
        gsp = [_split(g_ref[rows[j], :]) for j in grp]
        gc = [_dot(tri_b, gsp[j][0]) + _dot(tri_b, gsp[j][1]) for j in grp]
        gct = [_dotg(gsp[j][0], upper_b, TN_DIMS) + _dotg(gsp[j][1], upper_b, TN_DIMS) for j in grp]
        eg = [jnp.exp(gc[j]) for j in grp]
        elast = [jnp.exp(gc[j][CHUNK - 1:CHUNK, :] - gc[j]) for j in grp]
        dlast = [jnp.exp(gc[j][CHUNK - 1:CHUNK, :]) for j in grp]
        beta = [beta_ref[rows[j], :] for j in grp]
        qs = {(j, h): act_ref[rows[j], h * GDN_DK:(h + 1) * GDN_DK] for j, h in items}
        ks = {(j, h): act_ref[rows[j], (GDN_HEADS + h) * GDN_DK:(GDN_HEADS + h + 1) * GDN_DK] for j, h in items}
        vs = {(j, h): act_ref[rows[j], (2 * GDN_HEADS + h) * GDN_DK:(2 * GDN_HEADS + h + 1) * GDN_DK]
              for j, h in items}
        bcols = {(j, h): beta[j][:, h:h + 1] for j, h in items}
        kbs = {it: ks[it] * bcols[it] for it in items}
        khb = {it: ks[it].astype(BF16) for it in items}
        gams = {(j, h): jnp.exp(jnp.where(tri, hcol(gc[j], h) - gct[j][GDN_HEADS + h:GDN_HEADS + h + 1, :], -1e30))
                for j, h in items}
        kk = {it: _dotg(kbs[it].astype(BF16), khb[it], NT_DIMS) for it in items}
        qk = {it: _dotg(qs[it].astype(BF16), khb[it], NT_DIMS) for it in items}
        mm = {it: jnp.where(strict, kk[it] * gams[it], 0.0) for it in items}
        tinv = {it: eye - jnp.where(level_masks[0], mm[it], 0.0) for it in items}
        for lvl in range(1, 6):
            tb = {it: tinv[it].astype(BF16) for it in items}
            xs = {it: _dot(jnp.where(level_masks[lvl], mm[it], 0.0).astype(BF16), tb[it]).astype(BF16)
                  for it in items}
            tinv = {it: tinv[it] - _dot(tb[it], xs[it]) for it in items}
        rhs = {(j, h): jnp.concatenate([vs[j, h] * bcols[j, h], kbs[j, h] * hcol(eg[j], h)], axis=1).astype(BF16)
               for j, h in items}
        sol = {it: _dot(tinv[it].astype(BF16), rhs[it]) for it in items}
        aqk = {it: jnp.where(tri, qk[it] * gams[it], 0.0).astype(BF16) for it in items}
        qdec = {(j, h): (qs[j, h] * hcol(eg[j], h)).astype(BF16) for j, h in items}
        kst = {(j, h): (ks[j, h] * hcol(elast[j], h)).astype(BF16) for j, h in items}
        for j in grp:
            s_old = [s_ref[h] for h in heads]
            sb = [s_old[h].astype(BF16) for h in heads]
            ws = [_dot(sol[j, h][:, GDN_DV:].astype(BF16), sb[h]) for h in heads]
            qsd = [_dot(qdec[j, h], sb[h]) for h in heads]
            vnb = [(sol[j, h][:, :GDN_DV] - ws[h]).astype(BF16) for h in heads]
            av = [_dot(aqk[j, h], vnb[h]) for h in heads]
            kv = [_dotg(kst[j, h], vnb[h], TN_DIMS) for h in heads]
            for h in heads:
                o = qsd[h] + av[h]
                on = o * lax.rsqrt(jnp.mean(o * o, axis=1, keepdims=True) + NORM_EPS) * ng
                zz = z_ref[rows[j], h * GDN_DV:(h + 1) * GDN_DV].astype(F32)
                o_ref[rows[j], h * GDN_DV:(h + 1) * GDN_DV] = (on * (zz * _sigmoid(zz))).astype(o_ref.dtype)
                s_ref[h] = s_old[h] * hcol(dlast[j], h) + kv[h]
        return carry

    lax.fori_loop(0, tile // (GDN_GROUP * CHUNK), group, 0)


def _gdn(proj, ba, cw, hp, ng, batch, lp):
    t_rows = proj.shape[0]
    nt = lp // GDN_TILE
    hv = GDN_HEADS * GDN_DV
    rowmap = lambda col: (lambda b, t: (b * nt + t, col))
    const = lambda b, t: (0, 0)
    return pl.pallas_call(
        _gdn_kernel,
        grid=(batch, nt),
        in_specs=[pl.BlockSpec((GDN_TILE, GDN_QKV), rowmap(0)),
                  pl.BlockSpec((GDN_TILE, hv), rowmap(3)),
                  pl.BlockSpec((GDN_TILE, 128), rowmap(0)),
                  pl.BlockSpec((8, GDN_QKV), const),
                  pl.BlockSpec((8, 128), const),
                  pl.BlockSpec((1, GDN_DV), const)],
        out_specs=pl.BlockSpec((GDN_TILE, hv), rowmap(0)),
        out_shape=jax.ShapeDtypeStruct((t_rows, hv), BF16),
        scratch_shapes=[pltpu.VMEM((GDN_TILE + 8, GDN_QKV), F32),
                        pltpu.VMEM((GDN_TILE, GDN_QKV), F32),
                        pltpu.VMEM((GDN_TILE, 128), F32),
                        pltpu.VMEM((GDN_TILE, 128), F32),
                        pltpu.VMEM((GDN_HEADS, GDN_DK, GDN_DV), F32)],
        compiler_params=_cparams(2),
        name="gdn",
    )(proj, proj, ba, cw, hp, ng)


LRU_TILE = 256


def _lru_kernel(x_ref, gate_ref, cw_ref, cb_ref, wax_ref, bax_ref, sp_ref, o_ref,
                ext_ref, a_ref, b_ref, h_ref):
    tile = LRU_TILE

    @pl.when(pl.program_id(1) == 0)
    def _():
        ext_ref[0:8, :] = jnp.zeros((8, LRU_WIDTH), F32)
        h_ref[...] = jnp.zeros_like(h_ref)

    @pl.when(pl.program_id(1) > 0)
    def _():
        ext_ref[0:8, :] = ext_ref[tile:tile + 8, :]

    ext_ref[8:tile + 8, :] = x_ref[...].astype(F32)
    for blk in range(LRU_BLOCKS):
        cols = slice(blk * LRU_BW, (blk + 1) * LRU_BW)
        w = cw_ref[:, cols]
        xc = (w[3:4] * ext_ref[8:tile + 8, cols] + w[2:3] * ext_ref[7:tile + 7, cols]
              + w[1:2] * ext_ref[6:tile + 6, cols] + w[0:1] * ext_ref[5:tile + 5, cols]
              + cb_ref[:, cols])
        pre = _dot(xc.astype(BF16), wax_ref[blk]) + bax_ref[blk]
        r = _sigmoid(pre[:, :LRU_BW])
        i = _sigmoid(pre[:, LRU_BW:])
        log_a = sp_ref[:, cols] * r
        a_ref[:, cols] = jnp.exp(log_a)
        th = jnp.tanh(log_a)
        b_ref[:, cols] = jnp.sqrt(-2.0 * th / (1.0 - th)) * (i * xc)

    row = lax.broadcasted_iota(jnp.int32, (8, LRU_WIDTH), 0)

    def body(rb, carry):
        r0 = pl.multiple_of(rb * 8, 8)
        aa = a_ref[pl.ds(r0, 8), :]
        bb = b_ref[pl.ds(r0, 8), :]
        for k in (1, 2, 4):
            keep = row >= k
            a_sh = jnp.where(keep, pltpu.roll(aa, k, 0), 1.0)
            b_sh = jnp.where(keep, pltpu.roll(bb, k, 0), 0.0)
            bb = aa * b_sh + bb
            aa = aa * a_sh
        hs = bb + aa * carry
        gt = gate_ref[pl.ds(r0, 8), :].astype(F32)
        o_ref[pl.ds(r0, 8), :] = (hs * _gelu(gt)).astype(o_ref.dtype)
        return hs[7:8, :]

    h_ref[...] = lax.fori_loop(0, tile // 8, body, h_ref[...])


def _lru(proj, cw, cb, wax, bax, sp, batch, lp):
    t_rows = proj.shape[0]
    nt = lp // LRU_TILE
    rowmap = lambda col: (lambda b, t: (b * nt + t, col))
    const = lambda b, t: (0, 0)
    return pl.pallas_call(
        _lru_kernel,
        grid=(batch, nt),
        in_specs=[pl.BlockSpec((LRU_TILE, LRU_WIDTH), rowmap(4)),
                  pl.BlockSpec((LRU_TILE, LRU_WIDTH), rowmap(5)),
                  pl.BlockSpec((8, LRU_WIDTH), const),
                  pl.BlockSpec((1, LRU_WIDTH), const),
                  pl.BlockSpec((LRU_BLOCKS, LRU_BW, 2 * LRU_BW), lambda b, t: (0, 0, 0)),
                  pl.BlockSpec((LRU_BLOCKS, 1, 2 * LRU_BW), lambda b, t: (0, 0, 0)),
                  pl.BlockSpec((1, LRU_WIDTH), const)],
        out_specs=pl.BlockSpec((LRU_TILE, LRU_WIDTH), rowmap(0)),
        out_shape=jax.ShapeDtypeStruct((t_rows, LRU_WIDTH), BF16),
        scratch_shapes=[pltpu.VMEM((LRU_TILE + 8, LRU_WIDTH), F32),
                        pltpu.VMEM((LRU_TILE, LRU_WIDTH), F32),
                        pltpu.VMEM((LRU_TILE, LRU_WIDTH), F32),
                        pltpu.VMEM((1, LRU_WIDTH), F32)],
        compiler_params=_cparams(2),
        name="lru",
    )(proj, proj, cw, cb, wax, bax, sp)


OUT_TILE = 256


def _out_ln_kernel(n_add, h_ref, *refs):
    adds = refs[:n_add]
    (a1_ref, w1_ref, a2_ref, w2_ref, g_ref, b_ref, wrh_ref, wrl_ref, br_ref,
     o_ref, r_ref, o3_ref, cnt_ref, run_ref) = refs[n_add:]

    @pl.when(pl.program_id(0) == 0)
    def _():
        run_ref[...] = jnp.zeros_like(run_ref)

    acc = DN_ALPHA * h_ref[...]
    for r in adds:
        acc = acc + r[...]
    acc = acc + _dot(a1_ref[...], w1_ref[...])
    if a2_ref is not None:
        acc = acc + _dot(a2_ref[...], w2_ref[...])
    hn = _layer_norm(acc, g_ref[...], b_ref[...])
    o_ref[...] = hn
    tm = hn.shape[0]
    for c in range(LANE_BLOCKS):
        o3_ref[pl.ds(c, tm, stride=LANE_BLOCKS), :] = hn[:, c * 128:(c + 1) * 128]
    r = _route(hn, wrh_ref[...], wrl_ref[...], br_ref[...])
    lane = lax.broadcasted_iota(jnp.int32, r.shape, 1).astype(F32)
    oh1 = lane == r[:, 0:1]
    oh2 = lane == r[:, 1:2]
    picked = jnp.where(oh1 | oh2, 1.0, 0.0)
    ri = lax.broadcasted_iota(jnp.int32, (tm, tm), 0)
    ci = lax.broadcasted_iota(jnp.int32, (tm, tm), 1)
    before = _dot(jnp.where(ri > ci, 1.0, 0.0).astype(BF16), picked.astype(BF16)) + run_ref[0:1, :]
    rank1 = jnp.sum(jnp.where(oh1, before, 0.0), axis=1, keepdims=True)
    rank2 = jnp.sum(jnp.where(oh2, before, 0.0), axis=1, keepdims=True)
    r_ref[...] = r + jnp.where(lane == 4.0, rank1, 0.0) + jnp.where(lane == 5.0, rank2, 0.0)
    total = run_ref[0:1, :] + jnp.sum(picked, axis=0, keepdims=True)
    run_ref[...] = jnp.broadcast_to(total, run_ref.shape)
    cnt_ref[...] = jnp.broadcast_to(total, cnt_ref.shape)


def _out_ln_kernel_1(h_ref, p_ref, a1_ref, w1_ref, *rest):
    _out_ln_kernel(1, h_ref, p_ref, a1_ref, w1_ref, None, None, *rest)


def _out_ln_kernel_2(h_ref, a1_ref, w1_ref, a2_ref, w2_ref, *rest):
    _out_ln_kernel(0, h_ref, a1_ref, w1_ref, a2_ref, w2_ref, *rest)


def _out_ln(h, addend, pairs, g, b, router):
    t_rows = h.shape[0]
    tm = OUT_TILE
    row = lambda i: (i, 0)
    const = lambda i: (0, 0)
    ins = [h]
    specs = [pl.BlockSpec((tm, D_MODEL), row)]
    if addend is not None:
        ins.append(addend)
        specs.append(pl.BlockSpec((tm, D_MODEL), row))
    for a, w in pairs:
        ins += [a, w]
        specs += [pl.BlockSpec((tm, a.shape[1]), row), pl.BlockSpec(w.shape, const)]
    ins += [g, b]
    specs += [pl.BlockSpec((1, D_MODEL), const)] * 2
    ins += list(router)
    specs += [pl.BlockSpec((D_MODEL, 128), const), pl.BlockSpec((D_MODEL, 128), const),
              pl.BlockSpec((1, 128), const)]
    body = _out_ln_kernel_1 if addend is not None else _out_ln_kernel_2
    return pl.pallas_call(
        body,
        grid=(t_rows // tm,),
        in_specs=specs,
        out_specs=[pl.BlockSpec((tm, D_MODEL), row), pl.BlockSpec((tm, 128), row),
                   pl.BlockSpec((tm * LANE_BLOCKS, 128), row),
                   pl.BlockSpec((8, 128), const)],
        out_shape=[jax.ShapeDtypeStruct((t_rows, D_MODEL), F32),
                   jax.ShapeDtypeStruct((t_rows, 128), F32),
                   jax.ShapeDtypeStruct((t_rows * LANE_BLOCKS, 128), F32),
                   jax.ShapeDtypeStruct((8, 128), F32)],
        scratch_shapes=[pltpu.VMEM((8, 128), F32)],
        compiler_params=_cparams(1),
        name="out_ln",
    )(*ins)


ROUTER_TILE = 768
MOE_TILE = 256
BIG = 1e30


def _route(h, wh, wl, b):
    hh, hl = _split(h)
    logits = _dot(hh, wh) + _dot(hl, wh) + _dot(hh, wl) + b
    lane = lax.broadcasted_iota(jnp.int32, logits.shape, 1).astype(F32)

    def first_argmax(vals, vmax):
        return jnp.min(jnp.where(vals == vmax, lane, 4096.0), axis=1, keepdims=True)

    is_g = lane < MOE_GROUPS
    lg = jnp.where(is_g, logits, -BIG)
    m = jnp.max(lg, axis=1, keepdims=True)
    gidx = first_argmax(lg, m)
    denom = jnp.sum(jnp.where(is_g, jnp.exp(logits - m), 0.0), axis=1, keepdims=True)
    p_top = 1.0 / denom
    base = MOE_GROUPS + MOE_EPG * gidx
    in_grp = (lane >= base) & (lane < base + MOE_EPG)
    le = jnp.where(in_grp, logits, -BIG)
    v1 = jnp.max(le, axis=1, keepdims=True)
    i1 = first_argmax(le, v1)
    le2 = jnp.where(lane == i1, -BIG, le)
    v2 = jnp.max(le2, axis=1, keepdims=True)
    i2 = first_argmax(le2, v2)
    e21 = jnp.exp(v2 - v1)
    w1 = p_top / (1.0 + e21)
    w2 = p_top * e21 / (1.0 + e21)
    return (jnp.where(lane == 0.0, i1 - MOE_GROUPS, 0.0) + jnp.where(lane == 1.0, i2 - MOE_GROUPS, 0.0)
            + jnp.where(lane == 2.0, w1, 0.0) + jnp.where(lane == 3.0, w2, 0.0))


def _router_weights(w_rg, b_rg, w_re, b_re):
    n_exp = MOE_GROUPS * MOE_EPG
    wr = jnp.concatenate([w_rg.astype(F32), w_re.astype(F32).reshape(D_MODEL, n_exp)], axis=1)
    wr = jnp.pad(wr, ((0, 0), (0, 128 - wr.shape[1])))
    wrh, wrl = _split(wr)
    br = jnp.concatenate([b_rg.astype(F32), b_re.astype(F32).reshape(-1)])
    br = jnp.pad(br, (0, 128 - br.shape[0])).reshape(1, 128)
    return wrh, wrl, br


DISPATCH_TILE = 256


def _dispatch_kernel(pos_ref, fill_ref, x_ref, xs_hbm, zbuf, sem):
    tm = DISPATCH_TILE
    lb = LANE_BLOCKS
    i = pl.program_id(0)

    @pl.when(i == 0)
    def _():
        zbuf[...] = jnp.zeros_like(zbuf)

        def fill(e):
            dst = pl.multiple_of(jnp.maximum(fill_ref[e], 0) * lb, lb)
            return pltpu.make_async_copy(zbuf, xs_hbm.at[pl.ds(dst, MOE_TILE * lb)], sem)

        def start(e, c):
            @pl.when(fill_ref[e] >= 0)
            def _():
                fill(e).start()
            return c

        def wait(e, c):
            @pl.when(fill_ref[e] >= 0)
            def _():
                fill(e).wait()
            return c

        lax.fori_loop(0, fill_ref.shape[0], start, 0)
        lax.fori_loop(0, fill_ref.shape[0], wait, 0)

    base = i * (2 * tm)
    for r in range(tm):
        for k in range(2):
            dst = pl.multiple_of(pos_ref[base + 2 * r + k] * lb, lb)
            pltpu.make_async_copy(x_ref.at[pl.ds(r * lb, lb)], xs_hbm.at[pl.ds(dst, lb)],
                                  sem).start(priority=k)
    for _ in range(2):
        pltpu.make_async_copy(x_ref, xs_hbm.at[pl.ds(0, tm * lb)], sem).wait()


def _dispatch(pos, fill, h3, n_sorted):
    t_rows = h3.shape[0] // LANE_BLOCKS
    tm = DISPATCH_TILE
    return pl.pallas_call(
        _dispatch_kernel,
        grid_spec=pltpu.PrefetchScalarGridSpec(
            num_scalar_prefetch=2,
            grid=(t_rows // tm,),
            in_specs=[pl.BlockSpec((tm * LANE_BLOCKS, 128), lambda i, pos, fill: (i, 0))],
            out_specs=pl.BlockSpec(memory_space=pl.ANY),
            scratch_shapes=[pltpu.VMEM((MOE_TILE * LANE_BLOCKS, 128), F32),
                            pltpu.SemaphoreType.DMA(())]),
        out_shape=jax.ShapeDtypeStruct((n_sorted * LANE_BLOCKS, 128), F32),
        compiler_params=_cparams(1),
        name="moe_dispatch",
    )(pos, fill, h3)


def _moe_kernel(nv_ref, dst_ref, te_ref, x_ref, wg_ref, wu_ref, wd_ref, y_hbm,
                ybuf, xmat, wgb, wub, wdb, ssem):
    tm = MOE_TILE
    lb = LANE_BLOCKS
    i = pl.program_id(0)
    nv = nv_ref[0]
    slot = i % 2
    other = 1 - slot

    def scatter_copy(tile, row, buf_slot):
        dst = y_hbm.at[pl.ds(pl.multiple_of(dst_ref[tile * tm + row] * lb, lb), lb)]
        src_rows = pl.ds(row * lb if isinstance(row, int) else pl.multiple_of(row * lb, lb), lb)
        return pltpu.make_async_copy(ybuf.at[buf_slot, src_rows], dst, ssem.at[buf_slot])

    def wait_scatter(buf_slot):
        pltpu.make_async_copy(ybuf.at[buf_slot], y_hbm.at[pl.ds(0, tm * lb)], ssem.at[buf_slot]).wait()

    @pl.when((i == 0) | (te_ref[i] != te_ref[jnp.maximum(i - 1, 0)]))
    def _():
        wgb[...] = wg_ref[0].astype(BF16)
        wub[...] = wu_ref[0].astype(BF16)
        wdb[...] = wd_ref[0].astype(BF16)

    def compute(buf_slot):
        for c in range(lb):
            xmat[:, c * 128:(c + 1) * 128] = x_ref[pl.ds(c, tm, stride=lb), :]
        xb = xmat[...].astype(BF16)
        hg = _dot(xb, wgb[...])
        hu = _dot(xb, wub[...])
        hh = (hg * _sigmoid(hg)) * hu
        y = _dot(hh.astype(BF16), wdb[...])
        for c in range(lb):
            ybuf[buf_slot, pl.ds(c, tm, stride=lb), :] = y[:, c * 128:(c + 1) * 128]

    @pl.when(i == 0)
    def _():
        ybuf[1] = jnp.zeros((tm * lb, 128), F32)
        fill = pltpu.make_async_copy(ybuf.at[1], y_hbm.at[pl.ds(y_hbm.shape[0] - tm * lb, tm * lb)], ssem.at[1])
        fill.start()
        fill.wait()
        compute(0)

    @pl.when((i > 0) & (i < nv))
    def _():
        @pl.when(i >= 2)
        def _():
            wait_scatter(slot)

        for r in range(tm):
            scatter_copy(i - 1, r, other).start(priority=r % 2)
        compute(slot)

    @pl.when(i == nv)
    def _():
        @pl.when(i >= 2)
        def _():
            wait_scatter(slot)

        def body(r, c):
            scatter_copy(i - 1, r, other).start()
            return c
        lax.fori_loop(0, tm, body, 0)
        wait_scatter(other)


def _moe(nv, dst, te, xs, wg, wu, wd, n_out_rows):
    n_tiles = dst.shape[0] // MOE_TILE
    tm = MOE_TILE
    ex = lambda i, nv, dst, te: (te[i], 0, 0)
    return pl.pallas_call(
        _moe_kernel,
        grid_spec=pltpu.PrefetchScalarGridSpec(
            num_scalar_prefetch=3,
            grid=(n_tiles,),
            in_specs=[pl.BlockSpec((tm * LANE_BLOCKS, 128),
                                   lambda i, nv, dst, te: (jnp.minimum(i, nv[0] - 1), 0)),
                      pl.BlockSpec((1, D_MODEL, MOE_FF), ex),
                      pl.BlockSpec((1, D_MODEL, MOE_FF), ex),
                      pl.BlockSpec((1, MOE_FF, D_MODEL), ex)],
            out_specs=pl.BlockSpec(memory_space=pl.ANY),
            scratch_shapes=[pltpu.VMEM((2, tm * LANE_BLOCKS, 128), F32),
                            pltpu.VMEM((tm, D_MODEL), F32),
                            pltpu.VMEM((D_MODEL, MOE_FF), BF16),
                            pltpu.VMEM((D_MODEL, MOE_FF), BF16),
                            pltpu.VMEM((MOE_FF, D_MODEL), BF16),
                            pltpu.SemaphoreType.DMA((2,))]),
        out_shape=jax.ShapeDtypeStruct((n_out_rows * LANE_BLOCKS, 128), F32),
        compiler_params=_cparams(1),
        name="moe_experts",
    )(nv, dst, te, xs, wg, wu, wd)


MOE_LN_TILE = 256


def _moe_ln_kernel(h_ref, y0_ref, y1_ref, r_ref, g_ref, b_ref, o_ref, acc_ref):
    r = r_ref[...]
    w0 = r[:, 2:3]
    w1 = r[:, 3:4]
    tm = h_ref.shape[0]
    for c in range(LANE_BLOCKS):
        cols = slice(c * 128, (c + 1) * 128)
        rows = pl.ds(c, tm, stride=LANE_BLOCKS)
        acc_ref[:, cols] = DN_ALPHA * h_ref[:, cols] + (w0 * y0_ref[rows, :] + w1 * y1_ref[rows, :])
    o_ref[...] = _layer_norm(acc_ref[...], g_ref[...], b_ref[...])


def _moe_ln(h, y2, rinfo, g, b):
    t_rows = h.shape[0]
    tm = MOE_LN_TILE
    nb = t_rows // tm
    const = lambda i: (0, 0)
    return pl.pallas_call(
        _moe_ln_kernel,
        grid=(nb,),
        in_specs=[pl.BlockSpec((tm, D_MODEL), lambda i: (i, 0)),
                  pl.BlockSpec((tm * LANE_BLOCKS, 128), lambda i: (i, 0)),
                  pl.BlockSpec((tm * LANE_BLOCKS, 128), lambda i: (nb + i, 0)),
                  pl.BlockSpec((tm, 128), lambda i: (i, 0)),
                  pl.BlockSpec((1, D_MODEL), const),
                  pl.BlockSpec((1, D_MODEL), const)],
        out_specs=pl.BlockSpec((tm, D_MODEL), lambda i: (i, 0)),
        out_shape=jax.ShapeDtypeStruct((t_rows, D_MODEL), F32),
        scratch_shapes=[pltpu.VMEM((tm, D_MODEL), F32)],
        compiler_params=_cparams(1),
        name="moe_ln",
    )(h, y2, y2, rinfo, g, b)


def _hier_moe_ln(h, rinfo, h3, counts, layer, w_gate, w_up, w_down, ln_g, ln_b):
    t_rows = h.shape[0]
    n_exp = MOE_GROUPS * MOE_EPG

    n_asg = 2 * t_rows
    tm = MOE_TILE
    e_a = rinfo[:, 0:2].astype(jnp.int32).reshape(n_asg)
    rank_a = rinfo[:, 4:6].astype(jnp.int32).reshape(n_asg)
    asg = jnp.arange(n_asg, dtype=jnp.int32)
    total = counts[0, :n_exp].astype(jnp.int32)
    tiles = (total + tm - 1) // tm
    tile_end = jnp.cumsum(tiles)
    tile_start = tile_end - tiles
    pos = (jnp.take(tile_start * tm, e_a) + rank_a).astype(jnp.int32)
    n_tiles = n_asg // tm + n_exp + 1
    n_sorted = n_tiles * tm
    nv = tile_end[-1:].astype(jnp.int32)
    slot_asg = jnp.full((n_sorted,), -1, jnp.int32).at[pos].set(asg)
    is_pad = slot_asg < 0
    dump = n_asg + jnp.arange(n_sorted, dtype=jnp.int32) % tm
    dst = jnp.where(is_pad, dump, (slot_asg & 1) * t_rows + (slot_asg >> 1))
    te = jnp.minimum(jnp.sum(jnp.arange(n_tiles)[:, None] >= tile_end[None, :], axis=1), n_exp - 1)
    te = (te + layer * n_exp).astype(jnp.int32)
    spare = tile_end[-1] + jnp.arange(n_tiles - n_asg // tm)
    fill = jnp.concatenate([jnp.where(tiles > 0, (tile_end - 1) * tm, -1),
                            jnp.where(spare < n_tiles, spare * tm, -1)]).astype(jnp.int32)

    xs = _dispatch(pos, fill, h3, n_sorted)
    y2 = _moe(nv, dst, te, xs,
              w_gate.astype(F32).reshape(-1, D_MODEL, MOE_FF),
              w_up.astype(F32).reshape(-1, D_MODEL, MOE_FF),
              w_down.astype(F32).reshape(-1, MOE_FF, D_MODEL), n_asg + tm)
    return _moe_ln(h, y2, rinfo, ln_g.reshape(1, D_MODEL).astype(F32), ln_b.reshape(1, D_MODEL).astype(F32))


def _layer_ab(h, batch, lp, w_in, s5, w_glu, b_glu, gla_w_gate, gla_b_gate, gla_norm, w_out, ln_g, ln_b,
              router):
    hk = GLA_HEADS * GLA_DK
    hv = GLA_HEADS * GLA_DV
    o_q = S5_WIDTH
    o_g = o_q + 2 * hk + hv
    o_r = o_g + GLA_RANK
    w_in = w_in.astype(F32)
    w_u = w_in[:, :S5_WIDTH].astype(BF16)
    w_main = jnp.concatenate([w_in[:, o_q:o_g], w_in[:, o_r:o_r + hv]], axis=1).astype(BF16)
    w_low = jnp.pad(w_in[:, o_g:o_r], ((0, 0), (0, 128 - GLA_RANK))).astype(BF16)
    u = _s5_uproj(h, w_u)
    z = _s5_core(u, s5, batch)
    p_s5 = _s5_out(z, w_glu.astype(BF16), b_glu.reshape(1, -1).astype(F32), w_out[:S5_WIDTH].astype(BF16))
    proj, g_low = _in_proj(h, w_main, w_low)
    wg = jnp.pad(gla_w_gate.astype(F32), ((0, 128 - GLA_RANK), (0, 0)))
    wgh, wgl = _split(wg)
    y_gla = _gla(proj, g_low, wgh, wgl, gla_b_gate.reshape(1, -1).astype(F32),
                 gla_norm.reshape(1, -1).astype(F32), batch, lp)
    return _out_ln(h, p_s5, [(y_gla, w_out[S5_WIDTH:].astype(BF16))],
                   ln_g.reshape(1, -1).astype(F32), ln_b.reshape(1, -1).astype(F32), router)


def _layer_cd(h, batch, lp, w_in, conv_w, a_log, dt_bias, gdn_norm, lru_conv_w, lru_conv_b,
              lru_w_a, lru_b_a, lru_w_x, lru_b_x, lru_lambda, w_out, ln_g, ln_b, router):
    nq = GDN_QKV
    hv = GDN_HEADS * GDN_DV
    w_in = w_in.astype(F32)
    o_z = nq
    o_b = o_z + hv
    o_a = o_b + GDN_HEADS
    o_x = o_a + GDN_HEADS
    o_gt = o_x + LRU_WIDTH
    w_main = jnp.concatenate([w_in[:, :o_b], w_in[:, o_x:o_gt + LRU_WIDTH]], axis=1).astype(BF16)
    w_ba = jnp.pad(w_in[:, o_b:o_x], ((0, 0), (0, 128 - 2 * GDN_HEADS))).astype(BF16)
    proj, ba = _in_proj(h, w_main, w_ba)

    cw = jnp.pad(conv_w.astype(F32), ((0, 8 - CONV_K), (0, 0)))
    hp = jnp.zeros((8, 128), F32)
    hp = hp.at[0, GDN_HEADS:2 * GDN_HEADS].set(-jnp.exp(a_log.astype(F32)))
    hp = hp.at[1, GDN_HEADS:2 * GDN_HEADS].set(dt_bias.astype(F32))
    y_gdn = _gdn(proj, ba, cw, hp, gdn_norm.reshape(1, -1).astype(F32), batch, lp)

    lcw = jnp.pad(lru_conv_w.astype(F32), ((0, 8 - CONV_K), (0, 0)))
    wax = jnp.concatenate([lru_w_a.astype(F32), lru_w_x.astype(F32)], axis=2).astype(BF16)
    bax = jnp.concatenate([lru_b_a.astype(F32).reshape(LRU_BLOCKS, 1, LRU_BW),
                           lru_b_x.astype(F32).reshape(LRU_BLOCKS, 1, LRU_BW)], axis=2)
    sp = (-LRU_C * jax.nn.softplus(-lru_lambda.astype(F32))).reshape(1, LRU_WIDTH)
    y_lru = _lru(proj, lcw, lru_conv_b.reshape(1, -1).astype(F32), wax, bax, sp, batch, lp)

    return _out_ln(h, None, [(y_gdn, w_out[:hv].astype(BF16)), (y_lru, w_out[hv:].astype(BF16))],
                   ln_g.reshape(1, -1).astype(F32), ln_b.reshape(1, -1).astype(F32), router)


def kernel(x, meta_tokens, ab_w_in, ab_s5_a_re, ab_s5_a_im, ab_s5_log_dt, ab_s5_b_re, ab_s5_b_im,
           ab_s5_c_re, ab_s5_c_im, ab_s5_d, ab_s5_w_glu, ab_s5_b_glu, ab_gla_w_gate, ab_gla_b_gate,
           ab_gla_norm, ab_w_out, cd_w_in, cd_conv_w, cd_gdn_a_log, cd_gdn_dt_bias, cd_gdn_norm,
           cd_lru_conv_w, cd_lru_conv_b, cd_lru_w_a, cd_lru_b_a, cd_lru_w_x, cd_lru_b_x, cd_lru_lambda,
           cd_w_out, moe_w_router_g, moe_b_router_g, moe_w_router_e, moe_b_router_e, moe_w_gate,
           moe_w_up, moe_w_down, ln_mix_g, ln_mix_b, ln_ffn_g, ln_ffn_b):
    batch, seq, _ = x.shape
    length = seq + N_META
    lp = -(-length // ROW_ALIGN) * ROW_ALIGN
    meta = jnp.broadcast_to(meta_tokens.astype(x.dtype)[None], (batch, N_META, D_MODEL))
    h = jnp.concatenate([meta, x, jnp.zeros((batch, lp - length, D_MODEL), x.dtype)], axis=1)
    h = h.reshape(batch * lp, D_MODEL).astype(F32)

    for layer in range(DEPTH):
        j = layer // 2
        router = _router_weights(moe_w_router_g[layer], moe_b_router_g[layer], moe_w_router_e[layer],
                                 moe_b_router_e[layer])
        if layer % 2 == 0:
            s5 = _s5_params(ab_s5_a_re[j], ab_s5_a_im[j], ab_s5_log_dt[j], ab_s5_b_re[j], ab_s5_b_im[j],
                            ab_s5_c_re[j], ab_s5_c_im[j], ab_s5_d[j])
            h, rinfo, h3, counts = _layer_ab(h, batch, lp, ab_w_in[j], s5, ab_s5_w_glu[j], ab_s5_b_glu[j],
                                 ab_gla_w_gate[j], ab_gla_b_gate[j], ab_gla_norm[j], ab_w_out[j],
                                 ln_mix_g[layer], ln_mix_b[layer], router)
        else:
            h, rinfo, h3, counts = _layer_cd(h, batch, lp, cd_w_in[j], cd_conv_w[j], cd_gdn_a_log[j], cd_gdn_dt_bias[j],
                                 cd_gdn_norm[j], cd_lru_conv_w[j], cd_lru_conv_b[j], cd_lru_w_a[j],
                                 cd_lru_b_a[j], cd_lru_w_x[j], cd_lru_b_x[j], cd_lru_lambda[j], cd_w_out[j],
                                 ln_mix_g[layer], ln_mix_b[layer], router)
        h = _hier_moe_ln(h, rinfo, h3, counts, layer, moe_w_gate, moe_w_up, moe_w_down,
                         ln_ffn_g[layer], ln_ffn_b[layer])
    return h.reshape(batch, lp, D_MODEL)[:, N_META:length].astype(x.dtype)
```

```python
import functools
import math

import jax
import jax.numpy as jnp
from jax import lax
from jax.experimental import pallas as pl
from jax.experimental.pallas import tpu as pltpu

F32 = jnp.float32
BF16 = jnp.bfloat16
HIGHEST = lax.Precision.HIGHEST

D_MODEL = 2048
N_META = 16
CONV_K = 4
DEPTH = 2
DN_ALPHA = (2.0 * DEPTH) ** 0.25
LN_EPS = 1e-5
NORM_EPS = 1e-6

S5_WIDTH = 1024
S5_GROUP = 16
S5_GROUPS = 64
S5_STATE = 64
S5_LC = 16
S5_SLABS = 8

GLA_HEADS = 4
GLA_DK = 128
GLA_DV = 256
GLA_RANK = 16
GLA_TAU = 16.0

GDN_HEADS = 8
GDN_DK = 128
GDN_DV = 128

LRU_WIDTH = 1024
LRU_BLOCKS = 8
LRU_BW = 128
LRU_C = 8.0

MOE_GROUPS = 4
MOE_EPG = 8
MOE_FF = 256

LANE_BLOCKS = D_MODEL // 128
CHUNK = 64
ROW_ALIGN = 768
VMEM_LIMIT = 56 * 1024 * 1024

NT_DIMS = (((1,), (1,)), ((), ()))
TN_DIMS = (((0,), (0,)), ((), ()))


def _cparams(n_axes):
    return pltpu.CompilerParams(dimension_semantics=("arbitrary",) * n_axes,
                                vmem_limit_bytes=VMEM_LIMIT)


def _dot(a, b):
    return jnp.dot(a, b, preferred_element_type=F32)


def _dotg(a, b, dims):
    return lax.dot_general(a, b, dims, preferred_element_type=F32)


def _split(a):
    hi = a.astype(BF16)
    lo = (a - hi.astype(F32)).astype(BF16)
    return hi, lo


def _sigmoid(x):
    return 1.0 / (1.0 + jnp.exp(-x))


def _softplus(x):
    return jnp.maximum(x, 0.0) + jnp.log1p(jnp.exp(-jnp.abs(x)))


def _gelu(x):
    return 0.5 * x * (1.0 + jnp.tanh(math.sqrt(2.0 / math.pi) * (x + 0.044715 * (x * x * x))))


def _layer_norm(x, g, b):
    mu = jnp.mean(x, axis=-1, keepdims=True)
    xc = x - mu
    var = jnp.mean(xc * xc, axis=-1, keepdims=True)
    return xc * lax.rsqrt(var + LN_EPS) * g + b


PROJ_TILE = 512
PROJ_COLS = 1024


def _in_proj_kernel(x_ref, w_ref, wg_ref, o_ref, g_ref):
    xb = x_ref[...].astype(BF16)
    for j in range(w_ref.shape[1] // PROJ_COLS):
        cols = slice(j * PROJ_COLS, (j + 1) * PROJ_COLS)
        o_ref[:, cols] = _dot(xb, w_ref[:, cols]).astype(o_ref.dtype)
    g_ref[...] = _dot(xb, wg_ref[...])


def _in_proj(x, w, w_gates):
    m, k = x.shape
    n = w.shape[1]
    tm = PROJ_TILE
    const = lambda i: (0, 0)
    return pl.pallas_call(
        _in_proj_kernel,
        grid=(m // tm,),
        in_specs=[pl.BlockSpec((tm, k), lambda i: (i, 0)),
                  pl.BlockSpec((k, n), const, pipeline_mode=pl.Buffered(1)),
                  pl.BlockSpec((k, 128), const, pipeline_mode=pl.Buffered(1))],
        out_specs=[pl.BlockSpec((tm, n), lambda i: (i, 0)),
                   pl.BlockSpec((tm, 128), lambda i: (i, 0))],
        out_shape=[jax.ShapeDtypeStruct((m, n), BF16),
                   jax.ShapeDtypeStruct((m, 128), F32)],
        compiler_params=_cparams(1),
        name="in_proj",
    )(x, w, w_gates)


S5_TILE_CHUNKS = 32


def _s5_row_perm():
    m = S5_TILE_CHUNKS
    r = jnp.arange(S5_LC * m)
    nat = (r % m) * S5_LC + r // m
    return (nat[:, None] == jnp.arange(S5_LC * m)[None, :]).astype(BF16)


def _s5_uproj_kernel(h_ref, perm_ref, w_ref, u_ref):
    m = S5_TILE_CHUNKS
    lhs = _dot(perm_ref[...], h_ref[...].astype(BF16)).astype(BF16)
    z = _dot(lhs, w_ref[...])
    for t in range(S5_LC):
        for s in range(S5_SLABS):
            u_ref[s, :, t * 128:(t + 1) * 128] = z[t * m:(t + 1) * m, s * 128:(s + 1) * 128]


def _s5_uproj(h, w_u):
    t_rows = h.shape[0]
    n_chunks = t_rows // S5_LC
    m = S5_TILE_CHUNKS
    return pl.pallas_call(
        _s5_uproj_kernel,
        grid=(n_chunks // m,),
        in_specs=[pl.BlockSpec((m * S5_LC, D_MODEL), lambda i: (i, 0)),
                  pl.BlockSpec((m * S5_LC, m * S5_LC), lambda i: (0, 0)),
                  pl.BlockSpec((D_MODEL, S5_WIDTH), lambda i: (0, 0))],
        out_specs=pl.BlockSpec((S5_SLABS, m, S5_LC * 128), lambda i: (0, i, 0)),
        out_shape=jax.ShapeDtypeStruct((S5_SLABS, n_chunks, S5_LC * 128), F32),
        compiler_params=_cparams(1),
        name="s5_uproj",
    )(h, _s5_row_perm(), w_u)


def _s5_core_kernel(u_ref, stack_ref, ms_ref, mot_ref, ak_ref, pre_ref, pim_ref, d_ref, z_ref,
                    sre_ref, sim_ref, xre_ref, xim_ref):
    n = u_ref.shape[1]
    half = 8 * S5_STATE
    u = u_ref[0]
    ub = u.astype(BF16)
    s = _dot(ub, ms_ref[0])
    sre_ref[...] = s[:, :half]
    sim_ref[...] = s[:, half:]

    ak = ak_ref[0]
    pre = pre_ref[0]
    pim = pim_ref[0]
    row = lax.broadcasted_iota(jnp.int32, (8, half), 0)

    def body(rb, carry):
        cre, cim = carry
        r0 = pl.multiple_of(rb * 8, 8)
        xr = sre_ref[pl.ds(r0, 8), :]
        xi = sim_ref[pl.ds(r0, 8), :]
        for idx, k in enumerate((1, 2, 4)):
            are = ak[2 * idx:2 * idx + 1, :]
            aim = ak[2 * idx + 1:2 * idx + 2, :]
            keep = row >= k
            shr = jnp.where(keep, pltpu.roll(xr, k, 0), 0.0)
            shi = jnp.where(keep, pltpu.roll(xi, k, 0), 0.0)
            xr, xi = xr + are * shr - aim * shi, xi + are * shi + aim * shr
        xr = xr + pre * cre - pim * cim
        xi = xi + pre * cim + pim * cre
        first = row == 0
        xre_ref[pl.ds(r0, 8), :] = jnp.where(first, cre, pltpu.roll(xr, 1, 0))
        xim_ref[pl.ds(r0, 8), :] = jnp.where(first, cim, pltpu.roll(xi, 1, 0))
        return xr[7:8, :], xi[7:8, :]

    zero = jnp.zeros((1, half), F32)
    lax.fori_loop(0, n // 8, body, (zero, zero))

    xprev = jnp.concatenate([xre_ref[...], xim_ref[...]], axis=1).astype(BF16)
    y_inter = _dotg(xprev, mot_ref[0], NT_DIMS)
    d = d_ref[0]
    npair = S5_LC // 2
    for q in range(npair):
        yq = _dot(ub[:, :(q + 1) * 256], stack_ref[0, (npair - 1 - q) * 256:, :])
        sl = slice(q * 256, (q + 1) * 256)
        y = yq + y_inter[:, sl] + d[:, sl] * u[:, sl]
        z_ref[0, :, sl] = _gelu(y)


def _s5_core(u, prm, batch):
    n_tot = u.shape[1]
    n = n_tot // batch
    w = S5_LC * 128
    half = 8 * S5_STATE
    slab = lambda s, b: (s, 0, 0)
    return pl.pallas_call(
        _s5_core_kernel,
        grid=(S5_SLABS, batch),
        in_specs=[pl.BlockSpec((1, n, w), lambda s, b: (s, b, 0)),
                  pl.BlockSpec((1, w, 256), slab),
                  pl.BlockSpec((1, w, 2 * half), slab),
                  pl.BlockSpec((1, w, 2 * half), slab),
                  pl.BlockSpec((1, 8, half), slab),
                  pl.BlockSpec((1, 8, half), slab),
                  pl.BlockSpec((1, 8, half), slab),
                  pl.BlockSpec((1, 1, w), slab)],
        out_specs=pl.BlockSpec((1, n, w), lambda s, b: (s, b, 0)),
        out_shape=jax.ShapeDtypeStruct(u.shape, F32),
        scratch_shapes=[pltpu.VMEM((n, half), F32)] * 4,
        compiler_params=_cparams(2),
        name="s5_core",
    )(u, prm["stack"], prm["ms"], prm["mot"], prm["ak"], prm["pre"], prm["pim"], prm["d"])


def _s5_out_kernel(z_ref, perm_ref, wglu_ref, bglu_ref, wout_ref, p_ref, zf_ref):
    m = S5_TILE_CHUNKS
    for t in range(S5_LC):
        for s in range(S5_SLABS):
            zf_ref[t * m:(t + 1) * m, s * 128:(s + 1) * 128] = z_ref[s, :, t * 128:(t + 1) * 128]
    zf = zf_ref[...]
    gate = _sigmoid(_dot(zf.astype(BF16), wglu_ref[...]) + bglu_ref[...])
    gated = (zf * gate).astype(BF16)
    nat = _dotg(perm_ref[...], gated, TN_DIMS).astype(BF16)
    p_ref[...] = _dot(nat, wout_ref[...])


def _s5_out(z, w_glu, b_glu, w_out_top):
    n_chunks = z.shape[1]
    m = S5_TILE_CHUNKS
    return pl.pallas_call(
        _s5_out_kernel,
        grid=(n_chunks // m,),
        in_specs=[pl.BlockSpec((S5_SLABS, m, S5_LC * 128), lambda i: (0, i, 0)),
                  pl.BlockSpec((m * S5_LC, m * S5_LC), lambda i: (0, 0)),
                  pl.BlockSpec((S5_WIDTH, S5_WIDTH), lambda i: (0, 0)),
                  pl.BlockSpec((1, S5_WIDTH), lambda i: (0, 0)),
                  pl.BlockSpec((S5_WIDTH, D_MODEL), lambda i: (0, 0))],
        out_specs=pl.BlockSpec((m * S5_LC, D_MODEL), lambda i: (i, 0)),
        out_shape=jax.ShapeDtypeStruct((n_chunks * S5_LC, D_MODEL), F32),
        scratch_shapes=[pltpu.VMEM((S5_LC * m, S5_WIDTH), F32)],
        compiler_params=_cparams(1),
        name="s5_out",
    )(z, _s5_row_perm(), w_glu, b_glu, w_out_top)


def _s5_params(a_re, a_im, log_dt, b_re, b_im, c_re, c_im, d):
    g, p, c, lc = S5_GROUPS, S5_STATE, S5_GROUP, S5_LC
    a = lax.complex(a_re.astype(F32), a_im.astype(F32))
    adt = a * jnp.exp(log_dt.astype(F32))[:, None]
    zoh = (jnp.exp(adt) - 1.0) / a
    bz = lax.complex(b_re.astype(F32), b_im.astype(F32)) * zoh[:, :, None]
    cc = lax.complex(c_re.astype(F32), c_im.astype(F32))
    jj = jnp.arange(lc + 1, dtype=F32)
    apow = jnp.exp(adt[None] * jj[:, None, None])

    def expand(compact, width):
        rows = compact.shape[-2]
        row_grp = (jnp.arange(rows) % 128) // S5_GROUP
        lane_grp = jnp.arange(8 * width) // width
        tiled = jnp.tile(compact.astype(BF16), (1,) * (compact.ndim - 1) + (8,))
        return jnp.where(row_grp[:, None] == lane_grp[None, :], tiled, jnp.zeros((), BF16))

    kre = jnp.einsum("gop,jgp,gpc->jgco", cc, apow[:lc], bz, precision=HIGHEST).real
    kblk = expand(kre.reshape(lc, S5_SLABS, 128, c), c)
    kpad = jnp.concatenate([jnp.zeros_like(kblk[:1]), kblk], axis=0)
    npair = lc // 2
    dd = jnp.arange(npair)[:, None, None]
    lo_in = jnp.arange(2)[None, :, None]
    lo_out = jnp.arange(2)[None, None, :]
    lag = 2 * dd + lo_out - lo_in
    tiles = kpad[lag + 1]
    tiles = tiles.transpose(3, 0, 1, 4, 2, 5).reshape(S5_SLABS, npair, 256, 256)
    stack = tiles[:, ::-1].reshape(S5_SLABS, npair * 256, 256)

    wst = apow[lc - 1 - jnp.arange(lc)][:, :, :, None] * bz[None]
    wst = wst.reshape(lc, S5_SLABS, 8, p, c).transpose(1, 0, 2, 4, 3).reshape(S5_SLABS, lc * 128, p)
    ms = jnp.concatenate([expand(wst.real, p), expand(wst.imag, p)], axis=2)

    zc = cc[None] * apow[1:lc + 1][:, :, None, :]
    zc = zc.reshape(lc, S5_SLABS, 8 * c, p).transpose(1, 0, 2, 3).reshape(S5_SLABS, lc * 128, p)
    mot = jnp.concatenate([expand(zc.real, p), expand(-zc.imag, p)], axis=2)

    def slab_layout(x):
        return x.reshape(x.shape[:-2] + (S5_SLABS, 8 * p))

    kk = jnp.array([1.0, 2.0, 4.0], F32)
    a_k = slab_layout(jnp.exp(adt[None] * (lc * kk)[:, None, None]))
    ak = jnp.stack([a_k[0].real, a_k[0].imag, a_k[1].real, a_k[1].imag, a_k[2].real, a_k[2].imag,
                    jnp.zeros_like(a_k[0].real), jnp.zeros_like(a_k[0].real)], axis=1)
    rr = jnp.arange(1, 9, dtype=F32)
    a_r = slab_layout(jnp.exp(adt[None] * (lc * rr)[:, None, None]))
    pre = a_r.real.transpose(1, 0, 2)
    pim = a_r.imag.transpose(1, 0, 2)
    dsk = jnp.tile(d.astype(F32).reshape(S5_SLABS, 1, 128), (1, 1, lc))
    return {"stack": stack.astype(BF16), "ms": ms.astype(BF16), "mot": mot.astype(BF16),
            "ak": ak, "pre": pre, "pim": pim, "d": dsk}


GLA_TILE = 256


def _gla_kernel(q_ref, k_ref, v_ref, r_ref, gl_ref, wgh_ref, wgl_ref, bg_ref, ng_ref, o_ref, st_ref):
    @pl.when(pl.program_id(1) == 0)
    def _():
        st_ref[...] = jnp.zeros_like(st_ref)

    rowi = lax.broadcasted_iota(jnp.int32, (CHUNK, CHUNK), 0)
    coli = lax.broadcasted_iota(jnp.int32, (CHUNK, CHUNK), 1)
    tri = rowi >= coli
    wgh = wgh_ref[...]
    wgl = wgl_ref[...]
    bg = bg_ref[...]
    ng = ng_ref[...]
    scale = GLA_DK ** -0.5

    nch = GLA_TILE // CHUNK
    ti = lax.broadcasted_iota(jnp.int32, (GLA_TILE, GLA_TILE), 0)
    tj = lax.broadcasted_iota(jnp.int32, (GLA_TILE, GLA_TILE), 1)
    tri_t = jnp.where((ti >= tj) & ((ti // CHUNK) == (tj // CHUNK)), 1.0, 0.0).astype(BF16)
    glh, gll = _split(gl_ref[...])
    pre = _dot(glh, wgh) + _dot(gll, wgh) + _dot(glh, wgl) + bg
    logf = (jnp.minimum(pre, 0.0) - jnp.log1p(jnp.exp(-jnp.abs(pre)))) * (1.0 / GLA_TAU)
    lh, ll = _split(logf)
    bcum = _dot(tri_t, lh) + _dot(tri_t, ll)
    ebc = jnp.exp(bcum)
    kk = k_ref[...].astype(F32)
    q_in = (q_ref[...].astype(F32) * scale * ebc).astype(BF16)
    k_in = (kk * jnp.exp(-bcum)).astype(BF16)

    heads = range(GLA_HEADS)
    chunks = range(nch)
    items = [(c, h) for c in chunks for h in heads]
    rws = [slice(c * CHUNK, (c + 1) * CHUNK) for c in chunks]
    sks = [slice(h * GLA_DK, (h + 1) * GLA_DK) for h in heads]
    svs = [slice(h * GLA_DV, (h + 1) * GLA_DV) for h in heads]
    blast = [bcum[(c + 1) * CHUNK - 1:(c + 1) * CHUNK, :] for c in chunks]
    k_st = [(kk[rws[c], :] * jnp.exp(blast[c] - bcum[rws[c], :])).astype(BF16) for c in chunks]
    dec = [jnp.exp(blast[c]) for c in chunks]
    vh = {(c, h): v_ref[rws[c], svs[h]].astype(BF16) for c, h in items}
    att = {(c, h): _dotg(q_in[rws[c], sks[h]], k_in[rws[c], sks[h]], NT_DIMS) for c, h in items}
    kv = {(c, h): _dotg(vh[c, h], k_st[c][:, sks[h]], TN_DIMS) for c, h in items}
    o_intra = {it: _dot(jnp.where(tri, att[it], 0.0).astype(BF16), vh[it]) for it in items}
    st = [st_ref[h] for h in heads]
    for c in chunks:
        o_inter = [_dotg(q_in[rws[c], sks[h]], st[h].astype(BF16), NT_DIMS) for h in heads]
        for h in heads:
            o = o_intra[c, h] + o_inter[h]
            on = o * lax.rsqrt(jnp.mean(o * o, axis=1, keepdims=True) + NORM_EPS) * ng
            rr = r_ref[rws[c], svs[h]].astype(F32)
            o_ref[rws[c], svs[h]] = (on * (rr * _sigmoid(rr))).astype(o_ref.dtype)
        st = [st[h] * dec[c][:, sks[h]] + kv[c, h] for h in heads]
    for h in heads:
        st_ref[h] = st[h]


def _gla(proj, g_low, wgh, wgl, bg, ng, batch, lp):
    t_rows = proj.shape[0]
    nt = lp // GLA_TILE
    hk = GLA_HEADS * GLA_DK
    hv = GLA_HEADS * GLA_DV
    rowmap = lambda col: (lambda b, t: (b * nt + t, col))
    const = lambda b, t: (0, 0)
    return pl.pallas_call(
        _gla_kernel,
        grid=(batch, nt),
        in_specs=[pl.BlockSpec((GLA_TILE, hk), rowmap(0)),
                  pl.BlockSpec((GLA_TILE, hk), rowmap(1)),
                  pl.BlockSpec((GLA_TILE, hv), rowmap(1)),
                  pl.BlockSpec((GLA_TILE, hv), rowmap(2)),
                  pl.BlockSpec((GLA_TILE, 128), rowmap(0)),
                  pl.BlockSpec((128, hk), const),
                  pl.BlockSpec((128, hk), const),
                  pl.BlockSpec((1, hk), const),
                  pl.BlockSpec((1, GLA_DV), const)],
        out_specs=pl.BlockSpec((GLA_TILE, hv), rowmap(0)),
        out_shape=jax.ShapeDtypeStruct((t_rows, hv), BF16),
        scratch_shapes=[pltpu.VMEM((GLA_HEADS, GLA_DV, GLA_DK), F32)],
        compiler_params=_cparams(2),
        name="gla",
    )(proj, proj, proj, proj, g_low, wgh, wgl, bg, ng)


GDN_TILE = 256
GDN_GROUP = 4
GDN_QKV = 3 * GDN_HEADS * GDN_DK


def _gdn_kernel(qkv_ref, z_ref, ba_ref, cw_ref, hp_ref, ng_ref, o_ref,
                ext_ref, act_ref, beta_ref, g_ref, s_ref):
    tile = GDN_TILE

    @pl.when(pl.program_id(1) == 0)
    def _():
        ext_ref[0:8, :] = jnp.zeros((8, GDN_QKV), F32)
        s_ref[...] = jnp.zeros_like(s_ref)

    @pl.when(pl.program_id(1) > 0)
    def _():
        ext_ref[0:8, :] = ext_ref[tile:tile + 8, :]

    ext_ref[8:tile + 8, :] = qkv_ref[...].astype(F32)
    qscale = GDN_DK ** -0.5
    for cb in range(GDN_QKV // 128):
        cols = slice(cb * 128, (cb + 1) * 128)
        w = cw_ref[:, cols]
        y = (w[3:4] * ext_ref[8:tile + 8, cols] + w[2:3] * ext_ref[7:tile + 7, cols]
             + w[1:2] * ext_ref[6:tile + 6, cols] + w[0:1] * ext_ref[5:tile + 5, cols])
        a = y * _sigmoid(y)
        if cb < 2 * GDN_HEADS:
            a = a * lax.rsqrt(jnp.sum(a * a, axis=1, keepdims=True) + NORM_EPS)
            if cb < GDN_HEADS:
                a = a * qscale
        act_ref[:, cols] = a

    ba = ba_ref[...]
    beta_ref[...] = _sigmoid(ba)
    g_ref[...] = hp_ref[0:1, :] * _softplus(ba + hp_ref[1:2, :])

    rowi = lax.broadcasted_iota(jnp.int32, (CHUNK, CHUNK), 0)
    coli = lax.broadcasted_iota(jnp.int32, (CHUNK, CHUNK), 1)
    tri = rowi >= coli
    strict = rowi > coli
    tri_b = jnp.where(tri, 1.0, 0.0).astype(BF16)
    upper_b = jnp.where(rowi <= coli, 1.0, 0.0).astype(BF16)
    eye = jnp.where(rowi == coli, 1.0, 0.0)
    level_masks = []
    for lvl in range(6):
        bi = rowi >> lvl
        bj = coli >> lvl
        level_masks.append(((bi & 1) == 1) & (bj == bi - 1))
    ng = ng_ref[...]

    heads = range(GDN_HEADS)
    grp = range(GDN_GROUP)
    items = [(j, h) for j in grp for h in heads]
    hcol = lambda t, h: t[:, GDN_HEADS + h:GDN_HEADS + h + 1]

    def group(gi, carry):
        base = gi * (GDN_GROUP * CHUNK)
        rows = [pl.ds(pl.multiple_of(base + j * CHUNK, CHUNK), CHUNK) for j in grp]
        gsp = [_split(g_ref[rows[j], :]) for j in grp]
        gc = [_dot(tri_b, gsp[j][0]) + _dot(tri_b, gsp[j][1]) for j in grp]
        gct = [_dotg(gsp[j][0], upper_b, TN_DIMS) + _dotg(gsp[j][1], upper_b, TN_DIMS) for j in grp]
        eg = [jnp.exp(gc[j]) for j in grp]
        elast = [jnp.exp(gc[j][CHUNK - 1:CHUNK, :] - gc[j]) for j in grp]
        dlast = [jnp.exp(gc[j][CHUNK - 1:CHUNK, :]) for j in grp]
        beta = [beta_ref[rows[j], :] for j in grp]
        qs = {(j, h): act_ref[rows[j], h * GDN_DK:(h + 1) * GDN_DK] for j, h in items}
        ks = {(j, h): act_ref[rows[j], (GDN_HEADS + h) * GDN_DK:(GDN_HEADS + h + 1) * GDN_DK] for j, h in items}
        vs = {(j, h): act_ref[rows[j], (2 * GDN_HEADS + h) * GDN_DK:(2 * GDN_HEADS + h + 1) * GDN_DK]
              for j, h in items}
        bcols = {(j, h): beta[j][:, h:h + 1] for j, h in items}
        kbs = {it: ks[it] * bcols[it] for it in items}
        khb = {it: ks[it].astype(BF16) for it in items}
        gams = {(j, h): jnp.exp(jnp.where(tri, hcol(gc[j], h) - gct[j][GDN_HEADS + h:GDN_HEADS + h + 1, :], -1e30))
                for j, h in items}
        kk = {it: _dotg(kbs[it].astype(BF16), khb[it], NT_DIMS) for it in items}
        qk = {it: _dotg(qs[it].astype(BF16), khb[it], NT_DIMS) for it in items}
        mm = {it: jnp.where(strict, kk[it] * gams[it], 0.0) for it in items}
        tinv = {it: eye - jnp.where(level_masks[0], mm[it], 0.0) for it in items}
        for lvl in range(1, 6):
            tb = {it: tinv[it].astype(BF16) for it in items}
            xs = {it: _dot(jnp.where(level_masks[lvl], mm[it], 0.0).astype(BF16), tb[it]).astype(BF16)
                  for it in items}
            tinv = {it: tinv[it] - _dot(tb[it], xs[it]) for it in items}
        rhs = {(j, h): jnp.concatenate([vs[j, h] * bcols[j, h], kbs[j, h] * hcol(eg[j], h)], axis=1).astype(BF16)
               for j, h in items}
        sol = {it: _dot(tinv[it].astype(BF16), rhs[it]) for it in items}
        aqk = {it: jnp.where(tri, qk[it] * gams[it], 0.0).astype(BF16) for it in items}
        qdec = {(j, h): (qs[j, h] * hcol(eg[j], h)).astype(BF16) for j, h in items}
        kst = {(j, h): (ks[j, h] * hcol(elast[j], h)).astype(BF16) for j, h in items}
        for j in grp:
            s_old = [s_ref[h] for h in heads]
            sb = [s_old[h].astype(BF16) for h in heads]
            ws = [_dot(sol[j, h][:, GDN_DV:].astype(BF16), sb[h]) for h in heads]
            qsd = [_dot(qdec[j, h], sb[h]) for h in heads]
            vnb = [(sol[j, h][:, :GDN_DV] - ws[h]).astype(BF16) for h in heads]
            av = [_dot(aqk[j, h], vnb[h]) for h in heads]
            kv = [_dotg(kst[j, h], vnb[h], TN_DIMS) for h in heads]
            for h in heads:
                o = qsd[h] + av[h]
                on = o * lax.rsqrt(jnp.mean(o * o, axis=1, keepdims=True) + NORM_EPS) * ng
                zz = z_ref[rows[j], h * GDN_DV:(h + 1) * GDN_DV].astype(F32)
                o_ref[rows[j], h * GDN_DV:(h + 1) * GDN_DV] = (on * (zz * _sigmoid(zz))).astype(o_ref.dtype)
                s_ref[h] = s_old[h] * hcol(dlast[j], h) + kv[h]
        return carry

    lax.fori_loop(0, tile // (GDN_GROUP * CHUNK), group, 0)


def _gdn(proj, ba, cw, hp, ng, batch, lp):
    t_rows = proj.shape[0]
    nt = lp // GDN_TILE
    hv = GDN_HEADS * GDN_DV
    rowmap = lambda col: (lambda b, t: (b * nt + t, col))
    const = lambda b, t: (0, 0)
    return pl.pallas_call(
        _gdn_kernel,
        grid=(batch, nt),
        in_specs=[pl.BlockSpec((GDN_TILE, GDN_QKV), rowmap(0)),
                  pl.BlockSpec((GDN_TILE, hv), rowmap(3)),
                  pl.BlockSpec((GDN_TILE, 128), rowmap(0)),
                  pl.BlockSpec((8, GDN_QKV), const),
                  pl.BlockSpec((8, 128), const),
                  pl.BlockSpec((1, GDN_DV), const)],
        out_specs=pl.BlockSpec((GDN_TILE, hv), rowmap(0)),
        out_shape=jax.ShapeDtypeStruct((t_rows, hv), BF16),
        scratch_shapes=[pltpu.VMEM((GDN_TILE + 8, GDN_QKV), F32),
                        pltpu.VMEM((GDN_TILE, GDN_QKV), F32),
                        pltpu.VMEM((GDN_TILE, 128), F32),
                        pltpu.VMEM((GDN_TILE, 128), F32),
                        pltpu.VMEM((GDN_HEADS, GDN_DK, GDN_DV), F32)],
        compiler_params=_cparams(2),
        name="gdn",
    )(proj, proj, ba, cw, hp, ng)


LRU_TILE = 256


def _lru_kernel(x_ref, gate_ref, cw_ref, cb_ref, wax_ref, bax_ref, sp_ref, o_ref,
                ext_ref, a_ref, b_ref, h_ref):
    tile = LRU_TILE

    @pl.when(pl.program_id(1) == 0)
    def _():
        ext_ref[0:8, :] = jnp.zeros((8, LRU_WIDTH), F32)
        h_ref[...] = jnp.zeros_like(h_ref)

    @pl.when(pl.program_id(1) > 0)
    def _():
        ext_ref[0:8, :] = ext_ref[tile:tile + 8, :]

    ext_ref[8:tile + 8, :] = x_ref[...].astype(F32)
    for blk in range(LRU_BLOCKS):
        cols = slice(blk * LRU_BW, (blk + 1) * LRU_BW)
        w = cw_ref[:, cols]
        xc = (w[3:4] * ext_ref[8:tile + 8, cols] + w[2:3] * ext_ref[7:tile + 7, cols]
              + w[1:2] * ext_ref[6:tile + 6, cols] + w[0:1] * ext_ref[5:tile + 5, cols]
              + cb_ref[:, cols])
        pre = _dot(xc.astype(BF16), wax_ref[blk]) + bax_ref[blk]
        r = _sigmoid(pre[:, :LRU_BW])
        i = _sigmoid(pre[:, LRU_BW:])
        log_a = sp_ref[:, cols] * r
        a_ref[:, cols] = jnp.exp(log_a)
        th = jnp.tanh(log_a)
        b_ref[:, cols] = jnp.sqrt(-2.0 * th / (1.0 - th)) * (i * xc)

    row = lax.broadcasted_iota(jnp.int32, (8, LRU_WIDTH), 0)

    def body(rb, carry):
        r0 = pl.multiple_of(rb * 8, 8)
        aa = a_ref[pl.ds(r0, 8), :]
        bb = b_ref[pl.ds(r0, 8), :]
        for k in (1, 2, 4):
            keep = row >= k
            a_sh = jnp.where(keep, pltpu.roll(aa, k, 0), 1.0)
            b_sh = jnp.where(keep, pltpu.roll(bb, k, 0), 0.0)
            bb = aa * b_sh + bb
            aa = aa * a_sh
        hs = bb + aa * carry
        gt = gate_ref[pl.ds(r0, 8), :].astype(F32)
        o_ref[pl.ds(r0, 8), :] = (hs * _gelu(gt)).astype(o_ref.dtype)
        return hs[7:8, :]

    h_ref[...] = lax.fori_loop(0, tile // 8, body, h_ref[...])


def _lru(proj, cw, cb, wax, bax, sp, batch, lp):
    t_rows = proj.shape[0]
    nt = lp // LRU_TILE
    rowmap = lambda col: (lambda b, t: (b * nt + t, col))
    const = lambda b, t: (0, 0)
    return pl.pallas_call(
        _lru_kernel,
        grid=(batch, nt),
        in_specs=[pl.BlockSpec((LRU_TILE, LRU_WIDTH), rowmap(4)),
                  pl.BlockSpec((LRU_TILE, LRU_WIDTH), rowmap(5)),
                  pl.BlockSpec((8, LRU_WIDTH), const),
                  pl.BlockSpec((1, LRU_WIDTH), const),
                  pl.BlockSpec((LRU_BLOCKS, LRU_BW, 2 * LRU_BW), lambda b, t: (0, 0, 0)),
                  pl.BlockSpec((LRU_BLOCKS, 1, 2 * LRU_BW), lambda b, t: (0, 0, 0)),
                  pl.BlockSpec((1, LRU_WIDTH), const)],
        out_specs=pl.BlockSpec((LRU_TILE, LRU_WIDTH), rowmap(0)),
        out_shape=jax.ShapeDtypeStruct((t_rows, LRU_WIDTH), BF16),
        scratch_shapes=[pltpu.VMEM((LRU_TILE + 8, LRU_WIDTH), F32),
                        pltpu.VMEM((LRU_TILE, LRU_WIDTH), F32),
                        pltpu.VMEM((LRU_TILE, LRU_WIDTH), F32),
                        pltpu.VMEM((1, LRU_WIDTH), F32)],
        compiler_params=_cparams(2),
        name="lru",
    )(proj, proj, cw, cb, wax, bax, sp)


OUT_TILE = 256


def _out_ln_kernel(n_add, h_ref, *refs):
    adds = refs[:n_add]
    (a1_ref, w1_ref, a2_ref, w2_ref, g_ref, b_ref, wrh_ref, wrl_ref, br_ref,
     o_ref, r_ref, o3_ref, cnt_ref, run_ref) = refs[n_add:]

    @pl.when(pl.program_id(0) == 0)
    def _():
        run_ref[...] = jnp.zeros_like(run_ref)

    acc = DN_ALPHA * h_ref[...]
    for r in adds:
        acc = acc + r[...]
    acc = acc + _dot(a1_ref[...], w1_ref[...])
    if a2_ref is not None:
        acc = acc + _dot(a2_ref[...], w2_ref[...])
    hn = _layer_norm(acc, g_ref[...], b_ref[...])
    o_ref[...] = hn
    tm = hn.shape[0]
    for c in range(LANE_BLOCKS):
        o3_ref[pl.ds(c, tm, stride=LANE_BLOCKS), :] = hn[:, c * 128:(c + 1) * 128]
    r = _route(hn, wrh_ref[...], wrl_ref[...], br_ref[...])
    lane = lax.broadcasted_iota(jnp.int32, r.shape, 1).astype(F32)
    oh1 = lane == r[:, 0:1]
    oh2 = lane == r[:, 1:2]
    picked = jnp.where(oh1 | oh2, 1.0, 0.0)
    ri = lax.broadcasted_iota(jnp.int32, (tm, tm), 0)
    ci = lax.broadcasted_iota(jnp.int32, (tm, tm), 1)
    before = _dot(jnp.where(ri > ci, 1.0, 0.0).astype(BF16), picked.astype(BF16)) + run_ref[0:1, :]
    rank1 = jnp.sum(jnp.where(oh1, before, 0.0), axis=1, keepdims=True)
    rank2 = jnp.sum(jnp.where(oh2, before, 0.0), axis=1, keepdims=True)
    r_ref[...] = r + jnp.where(lane == 4.0, rank1, 0.0) + jnp.where(lane == 5.0, rank2, 0.0)
    total = run_ref[0:1, :] + jnp.sum(picked, axis=0, keepdims=True)
    run_ref[...] = jnp.broadcast_to(total, run_ref.shape)
    cnt_ref[...] = jnp.broadcast_to(total, cnt_ref.shape)


def _out_ln_kernel_1(h_ref, p_ref, a1_ref, w1_ref, *rest):
    _out_ln_kernel(1, h_ref, p_ref, a1_ref, w1_ref, None, None, *rest)


def _out_ln_kernel_2(h_ref, a1_ref, w1_ref, a2_ref, w2_ref, *rest):
    _out_ln_kernel(0, h_ref, a1_ref, w1_ref, a2_ref, w2_ref, *rest)


def _out_ln(h, addend, pairs, g, b, router):
    t_rows = h.shape[0]
    tm = OUT_TILE
    row = lambda i: (i, 0)
    const = lambda i: (0, 0)
    ins = [h]
    specs = [pl.BlockSpec((tm, D_MODEL), row)]
    if addend is not None:
        ins.append(addend)
        specs.append(pl.BlockSpec((tm, D_MODEL), row))
    for a, w in pairs:
        ins += [a, w]
        specs += [pl.BlockSpec((tm, a.shape[1]), row), pl.BlockSpec(w.shape, const)]
    ins += [g, b]
    specs += [pl.BlockSpec((1, D_MODEL), const)] * 2
    ins += list(router)
    specs += [pl.BlockSpec((D_MODEL, 256), const), pl.BlockSpec((D_MODEL, 128), const),
              pl.BlockSpec((1, 128), const)]
    body = _out_ln_kernel_1 if addend is not None else _out_ln_kernel_2
    return pl.pallas_call(
        body,
        grid=(t_rows // tm,),
        in_specs=specs,
        out_specs=[pl.BlockSpec((tm, D_MODEL), row), pl.BlockSpec((tm, 128), row),
                   pl.BlockSpec((tm * LANE_BLOCKS, 128), row),
                   pl.BlockSpec((8, 128), const)],
        out_shape=[jax.ShapeDtypeStruct((t_rows, D_MODEL), F32),
                   jax.ShapeDtypeStruct((t_rows, 128), F32),
                   jax.ShapeDtypeStruct((t_rows * LANE_BLOCKS, 128), F32),
                   jax.ShapeDtypeStruct((8, 128), F32)],
        scratch_shapes=[pltpu.VMEM((8, 128), F32)],
        compiler_params=_cparams(1),
        name="out_ln",
    )(*ins)


ROUTER_TILE = 768
MOE_TILE = 256
BIG = 1e30


def _route(h, whl, wh, b):
    hh, hl = _split(h)
    both = _dot(hh, whl)
    logits = both[:, :128] + both[:, 128:] + _dot(hl, wh) + b
    lane = lax.broadcasted_iota(jnp.int32, logits.shape, 1).astype(F32)

    def first_argmax(vals, vmax):
        return jnp.min(jnp.where(vals == vmax, lane, 4096.0), axis=1, keepdims=True)

    is_g = lane < MOE_GROUPS
    lg = jnp.where(is_g, logits, -BIG)
    m = jnp.max(lg, axis=1, keepdims=True)
    gidx = first_argmax(lg, m)
    denom = jnp.sum(jnp.where(is_g, jnp.exp(logits - m), 0.0), axis=1, keepdims=True)
    p_top = 1.0 / denom
    base = MOE_GROUPS + MOE_EPG * gidx
    in_grp = (lane >= base) & (lane < base + MOE_EPG)
    le = jnp.where(in_grp, logits, -BIG)
    v1 = jnp.max(le, axis=1, keepdims=True)
    i1 = first_argmax(le, v1)
    le2 = jnp.where(lane == i1, -BIG, le)
    v2 = jnp.max(le2, axis=1, keepdims=True)
    i2 = first_argmax(le2, v2)
    e21 = jnp.exp(v2 - v1)
    w1 = p_top / (1.0 + e21)
    w2 = p_top * e21 / (1.0 + e21)
    return (jnp.where(lane == 0.0, i1 - MOE_GROUPS, 0.0) + jnp.where(lane == 1.0, i2 - MOE_GROUPS, 0.0)
            + jnp.where(lane == 2.0, w1, 0.0) + jnp.where(lane == 3.0, w2, 0.0))


def _router_weights(w_rg, b_rg, w_re, b_re):
    n_exp = MOE_GROUPS * MOE_EPG
    wr = jnp.concatenate([w_rg.astype(F32), w_re.astype(F32).reshape(D_MODEL, n_exp)], axis=1)
    wr = jnp.pad(wr, ((0, 0), (0, 128 - wr.shape[1])))
    wrh, wrl = _split(wr)
    br = jnp.concatenate([b_rg.astype(F32), b_re.astype(F32).reshape(-1)])
    br = jnp.pad(br, (0, 128 - br.shape[0])).reshape(1, 128)
    return jnp.concatenate([wrh, wrl], axis=1), wrh, br


DISPATCH_TILE = 256


def _dispatch_kernel(pos_ref, fill_ref, x_ref, xs_hbm, zbuf, sbuf, sem, rsem):
    tm = DISPATCH_TILE
    lb = LANE_BLOCKS
    i = pl.program_id(0)

    @pl.when(i == 0)
    def _():
        zbuf[...] = jnp.zeros_like(zbuf)

        def fill(e):
            dst = pl.multiple_of(jnp.maximum(fill_ref[e], 0) * lb, lb)
            return pltpu.make_async_copy(zbuf, xs_hbm.at[pl.ds(dst, MOE_TILE * lb)], sem)

        def start(e, c):
            @pl.when(fill_ref[e] >= 0)
            def _():
                fill(e).start()
            return c

        def wait(e, c):
            @pl.when(fill_ref[e] >= 0)
            def _():
                fill(e).wait()
            return c

        lax.fori_loop(0, fill_ref.shape[0], start, 0)
        lax.fori_loop(0, fill_ref.shape[0], wait, 0)

    slot = i % 2
    sbuf[slot] = x_ref[...]
    base = i * (2 * tm)
    for r in range(tm):
        for k in range(2):
            dst = pl.multiple_of(pos_ref[base + 2 * r + k] * lb, lb)
            pltpu.make_async_copy(sbuf.at[slot, pl.ds(r * lb, lb)], xs_hbm.at[pl.ds(dst, lb)],
                                  rsem.at[slot]).start(priority=k)

    def wait_rows(s):
        for _ in range(2):
            pltpu.make_async_copy(sbuf.at[s], xs_hbm.at[pl.ds(0, tm * lb)], rsem.at[s]).wait()

    @pl.when(i > 0)
    def _():
        wait_rows(1 - slot)

    @pl.when(i == pl.num_programs(0) - 1)
    def _():
        wait_rows(slot)


def _dispatch(pos, fill, h3, n_sorted):
    t_rows = h3.shape[0] // LANE_BLOCKS
    tm = DISPATCH_TILE
    return pl.pallas_call(
        _dispatch_kernel,
        grid_spec=pltpu.PrefetchScalarGridSpec(
            num_scalar_prefetch=2,
            grid=(t_rows // tm,),
            in_specs=[pl.BlockSpec((tm * LANE_BLOCKS, 128), lambda i, pos, fill: (i, 0))],
            out_specs=pl.BlockSpec(memory_space=pl.ANY),
            scratch_shapes=[pltpu.VMEM((MOE_TILE * LANE_BLOCKS, 128), F32),
                            pltpu.VMEM((2, tm * LANE_BLOCKS, 128), F32),
                            pltpu.SemaphoreType.DMA(()),
                            pltpu.SemaphoreType.DMA((2,))]),
        out_shape=jax.ShapeDtypeStruct((n_sorted * LANE_BLOCKS, 128), F32),
        compiler_params=_cparams(1),
        name="moe_dispatch",
    )(pos, fill, h3)


def _moe_kernel(nv_ref, dst_ref, te_ref, x_ref, wg_ref, wu_ref, wd_ref, y_hbm,
                ybuf, xmat, wgb, wub, wdb, ssem):
    tm = MOE_TILE
    lb = LANE_BLOCKS
    i = pl.program_id(0)
    nv = nv_ref[0]
    slot = i % 2
    other = 1 - slot

    def scatter_copy(tile, row, buf_slot):
        dst = y_hbm.at[pl.ds(pl.multiple_of(dst_ref[tile * tm + row] * lb, lb), lb)]
        src_rows = pl.ds(row * lb if isinstance(row, int) else pl.multiple_of(row * lb, lb), lb)
        return pltpu.make_async_copy(ybuf.at[buf_slot, src_rows], dst, ssem.at[buf_slot])

    def wait_scatter(buf_slot):
        pltpu.make_async_copy(ybuf.at[buf_slot], y_hbm.at[pl.ds(0, tm * lb)], ssem.at[buf_slot]).wait()

    @pl.when((i == 0) | (te_ref[i] != te_ref[jnp.maximum(i - 1, 0)]))
    def _():
        wgb[...] = wg_ref[0].astype(BF16)
        wub[...] = wu_ref[0].astype(BF16)
        wdb[...] = wd_ref[0].astype(BF16)

    def compute(buf_slot):
        for c in range(lb):
            xmat[:, c * 128:(c + 1) * 128] = x_ref[pl.ds(c, tm, stride=lb), :]
        xb = xmat[...].astype(BF16)
        hg = _dot(xb, wgb[...])
        hu = _dot(xb, wub[...])
        hh = (hg * _sigmoid(hg)) * hu
        y = _dot(hh.astype(BF16), wdb[...])
        for c in range(lb):
            ybuf[buf_slot, pl.ds(c, tm, stride=lb), :] = y[:, c * 128:(c + 1) * 128]

    @pl.when(i == 0)
    def _():
        ybuf[1] = jnp.zeros((tm * lb, 128), F32)
        fill = pltpu.make_async_copy(ybuf.at[1], y_hbm.at[pl.ds(y_hbm.shape[0] - tm * lb, tm * lb)], ssem.at[1])
        fill.start()
        fill.wait()
        compute(0)

    @pl.when((i > 0) & (i < nv))
    def _():
        @pl.when(i >= 2)
        def _():
            wait_scatter(slot)

        for r in range(tm):
            scatter_copy(i - 1, r, other).start(priority=r % 2)
        compute(slot)

    @pl.when(i == nv)
    def _():
        @pl.when(i >= 2)
        def _():
            wait_scatter(slot)

        def body(r, c):
            scatter_copy(i - 1, r, other).start()
            return c
        lax.fori_loop(0, tm, body, 0)
        wait_scatter(other)


def _moe(nv, dst, te, xs, wg, wu, wd, n_out_rows):
    n_tiles = dst.shape[0] // MOE_TILE
    tm = MOE_TILE
    ex = lambda i, nv, dst, te: (te[i], 0, 0)
    return pl.pallas_call(
        _moe_kernel,
        grid_spec=pltpu.PrefetchScalarGridSpec(
            num_scalar_prefetch=3,
            grid=(n_tiles,),
            in_specs=[pl.BlockSpec((tm * LANE_BLOCKS, 128),
                                   lambda i, nv, dst, te: (jnp.minimum(i, nv[0] - 1), 0)),
                      pl.BlockSpec((1, D_MODEL, MOE_FF), ex),
                      pl.BlockSpec((1, D_MODEL, MOE_FF), ex),
                      pl.BlockSpec((1, MOE_FF, D_MODEL), ex)],
            out_specs=pl.BlockSpec(memory_space=pl.ANY),
            scratch_shapes=[pltpu.VMEM((2, tm * LANE_BLOCKS, 128), F32),
                            pltpu.VMEM((tm, D_MODEL), F32),
                            pltpu.VMEM((D_MODEL, MOE_FF), BF16),
                            pltpu.VMEM((D_MODEL, MOE_FF), BF16),
                            pltpu.VMEM((MOE_FF, D_MODEL), BF16),
                            pltpu.SemaphoreType.DMA((2,))]),
        out_shape=jax.ShapeDtypeStruct((n_out_rows * LANE_BLOCKS, 128), F32),
        compiler_params=_cparams(1),
        name="moe_experts",
    )(nv, dst, te, xs, wg, wu, wd)


MOE_LN_TILE = 256


def _moe_ln_kernel(h_ref, y0_ref, y1_ref, r_ref, g_ref, b_ref, o_ref, acc_ref):
    r = r_ref[...]
    w0 = r[:, 2:3]
    w1 = r[:, 3:4]
    tm = h_ref.shape[0]
    for c in range(LANE_BLOCKS):
        cols = slice(c * 128, (c + 1) * 128)
        rows = pl.ds(c, tm, stride=LANE_BLOCKS)
        acc_ref[:, cols] = DN_ALPHA * h_ref[:, cols] + (w0 * y0_ref[rows, :] + w1 * y1_ref[rows, :])
    o_ref[...] = _layer_norm(acc_ref[...], g_ref[...], b_ref[...])


def _moe_ln_final(h, y2, rinfo, g, b, batch, lp, seq):
    t_rows = h.shape[0]
    tm = MOE_LN_TILE
    lb = LANE_BLOCKS
    nj = seq // tm
    first = lambda bb, j: pl.multiple_of(bb * lp + N_META + j * tm, 8)
    const = lambda bb, j: (0, 0)
    return pl.pallas_call(
        _moe_ln_kernel,
        grid=(batch, nj),
        in_specs=[pl.BlockSpec((pl.Element(tm), pl.Element(D_MODEL)), lambda bb, j: (first(bb, j), 0)),
                  pl.BlockSpec((pl.Element(tm * lb), pl.Element(128)), lambda bb, j: (first(bb, j) * lb, 0)),
                  pl.BlockSpec((pl.Element(tm * lb), pl.Element(128)),
                               lambda bb, j: ((t_rows + first(bb, j)) * lb, 0)),
                  pl.BlockSpec((pl.Element(tm), pl.Element(128)), lambda bb, j: (first(bb, j), 0)),
                  pl.BlockSpec((1, D_MODEL), const),
                  pl.BlockSpec((1, D_MODEL), const)],
        out_specs=pl.BlockSpec((tm, D_MODEL), lambda bb, j: (bb * nj + j, 0)),
        out_shape=jax.ShapeDtypeStruct((batch * seq, D_MODEL), F32),
        scratch_shapes=[pltpu.VMEM((tm, D_MODEL), F32)],
        compiler_params=_cparams(2),
        name="moe_ln_final",
    )(h, y2, y2, rinfo, g, b)


def _moe_ln(h, y2, rinfo, g, b):
    t_rows = h.shape[0]
    tm = MOE_LN_TILE
    nb = t_rows // tm
    const = lambda i: (0, 0)
    return pl.pallas_call(
        _moe_ln_kernel,
        grid=(nb,),
        in_specs=[pl.BlockSpec((tm, D_MODEL), lambda i: (i, 0)),
                  pl.BlockSpec((tm * LANE_BLOCKS, 128), lambda i: (i, 0)),
                  pl.BlockSpec((tm * LANE_BLOCKS, 128), lambda i: (nb + i, 0)),
                  pl.BlockSpec((tm, 128), lambda i: (i, 0)),
                  pl.BlockSpec((1, D_MODEL), const),
                  pl.BlockSpec((1, D_MODEL), const)],
        out_specs=pl.BlockSpec((tm, D_MODEL), lambda i: (i, 0)),
        out_shape=jax.ShapeDtypeStruct((t_rows, D_MODEL), F32),
        scratch_shapes=[pltpu.VMEM((tm, D_MODEL), F32)],
        compiler_params=_cparams(1),
        name="moe_ln",
    )(h, y2, y2, rinfo, g, b)


def _hier_moe_ln(h, rinfo, h3, counts, layer, w_gate, w_up, w_down, ln_g, ln_b, final=None):
    t_rows = h.shape[0]
    n_exp = MOE_GROUPS * MOE_EPG

    n_asg = 2 * t_rows
    tm = MOE_TILE
    e_a = rinfo[:, 0:2].astype(jnp.int32).reshape(n_asg)
    rank_a = rinfo[:, 4:6].astype(jnp.int32).reshape(n_asg)
    asg = jnp.arange(n_asg, dtype=jnp.int32)
    total = counts[0, :n_exp].astype(jnp.int32)
    tiles = (total + tm - 1) // tm
    tile_end = jnp.cumsum(tiles)
    tile_start = tile_end - tiles
    pos = (jnp.take(tile_start * tm, e_a) + rank_a).astype(jnp.int32)
    n_tiles = n_asg // tm + n_exp + 1
    n_sorted = n_tiles * tm
    nv = tile_end[-1:].astype(jnp.int32)
    slot_asg = jnp.full((n_sorted,), -1, jnp.int32).at[pos].set(asg)
    is_pad = slot_asg < 0
    dump = n_asg + jnp.arange(n_sorted, dtype=jnp.int32) % tm
    dst = jnp.where(is_pad, dump, (slot_asg & 1) * t_rows + (slot_asg >> 1))
    te = jnp.minimum(jnp.sum(jnp.arange(n_tiles)[:, None] >= tile_end[None, :], axis=1), n_exp - 1)
    te = (te + layer * n_exp).astype(jnp.int32)
    spare = tile_end[-1] + jnp.arange(n_tiles - n_asg // tm)
    fill = jnp.concatenate([jnp.where(tiles > 0, (tile_end - 1) * tm, -1),
                            jnp.where(spare < n_tiles, spare * tm, -1)]).astype(jnp.int32)

    xs = _dispatch(pos, fill, h3, n_sorted)
    y2 = _moe(nv, dst, te, xs,
              w_gate.astype(F32).reshape(-1, D_MODEL, MOE_FF),
              w_up.astype(F32).reshape(-1, D_MODEL, MOE_FF),
              w_down.astype(F32).reshape(-1, MOE_FF, D_MODEL), n_asg + tm)
    g2 = ln_g.reshape(1, D_MODEL).astype(F32)
    b2 = ln_b.reshape(1, D_MODEL).astype(F32)
    if final is not None:
        return _moe_ln_final(h, y2, rinfo, g2, b2, *final)
    return _moe_ln(h, y2, rinfo, g2, b2)


def _layer_ab(h, batch, lp, w_in, s5, w_glu, b_glu, gla_w_gate, gla_b_gate, gla_norm, w_out, ln_g, ln_b,
              router):
    hk = GLA_HEADS * GLA_DK
    hv = GLA_HEADS * GLA_DV
    o_q = S5_WIDTH
    o_g = o_q + 2 * hk + hv
    o_r = o_g + GLA_RANK
    w_in = w_in.astype(F32)
    w_u = w_in[:, :S5_WIDTH].astype(BF16)
    w_main = jnp.concatenate([w_in[:, o_q:o_g], w_in[:, o_r:o_r + hv]], axis=1).astype(BF16)
    w_low = jnp.pad(w_in[:, o_g:o_r], ((0, 0), (0, 128 - GLA_RANK))).astype(BF16)
    u = _s5_uproj(h, w_u)
    z = _s5_core(u, s5, batch)
    p_s5 = _s5_out(z, w_glu.astype(BF16), b_glu.reshape(1, -1).astype(F32), w_out[:S5_WIDTH].astype(BF16))
    proj, g_low = _in_proj(h, w_main, w_low)
    wg = jnp.pad(gla_w_gate.astype(F32), ((0, 128 - GLA_RANK), (0, 0)))
    wgh, wgl = _split(wg)
    y_gla = _gla(proj, g_low, wgh, wgl, gla_b_gate.reshape(1, -1).astype(F32),
                 gla_norm.reshape(1, -1).astype(F32), batch, lp)
    return _out_ln(h, p_s5, [(y_gla, w_out[S5_WIDTH:].astype(BF16))],
                   ln_g.reshape(1, -1).astype(F32), ln_b.reshape(1, -1).astype(F32), router)


def _layer_cd(h, batch, lp, w_in, conv_w, a_log, dt_bias, gdn_norm, lru_conv_w, lru_conv_b,
              lru_w_a, lru_b_a, lru_w_x, lru_b_x, lru_lambda, w_out, ln_g, ln_b, router):
    nq = GDN_QKV
    hv = GDN_HEADS * GDN_DV
    w_in = w_in.astype(F32)
    o_z = nq
    o_b = o_z + hv
    o_a = o_b + GDN_HEADS
    o_x = o_a + GDN_HEADS
    o_gt = o_x + LRU_WIDTH
    w_main = jnp.concatenate([w_in[:, :o_b], w_in[:, o_x:o_gt + LRU_WIDTH]], axis=1).astype(BF16)
    w_ba = jnp.pad(w_in[:, o_b:o_x], ((0, 0), (0, 128 - 2 * GDN_HEADS))).astype(BF16)
    proj, ba = _in_proj(h, w_main, w_ba)

    cw = jnp.pad(conv_w.astype(F32), ((0, 8 - CONV_K), (0, 0)))
    hp = jnp.zeros((8, 128), F32)
    hp = hp.at[0, GDN_HEADS:2 * GDN_HEADS].set(-jnp.exp(a_log.astype(F32)))
    hp = hp.at[1, GDN_HEADS:2 * GDN_HEADS].set(dt_bias.astype(F32))
    y_gdn = _gdn(proj, ba, cw, hp, gdn_norm.reshape(1, -1).astype(F32), batch, lp)

    lcw = jnp.pad(lru_conv_w.astype(F32), ((0, 8 - CONV_K), (0, 0)))
    wax = jnp.concatenate([lru_w_a.astype(F32), lru_w_x.astype(F32)], axis=2).astype(BF16)
    bax = jnp.concatenate([lru_b_a.astype(F32).reshape(LRU_BLOCKS, 1, LRU_BW),
                           lru_b_x.astype(F32).reshape(LRU_BLOCKS, 1, LRU_BW)], axis=2)
    sp = (-LRU_C * jax.nn.softplus(-lru_lambda.astype(F32))).reshape(1, LRU_WIDTH)
    y_lru = _lru(proj, lcw, lru_conv_b.reshape(1, -1).astype(F32), wax, bax, sp, batch, lp)

    return _out_ln(h, None, [(y_gdn, w_out[:hv].astype(BF16)), (y_lru, w_out[hv:].astype(BF16))],
                   ln_g.reshape(1, -1).astype(F32), ln_b.reshape(1, -1).astype(F32), router)


def kernel(x, meta_tokens, ab_w_in, ab_s5_a_re, ab_s5_a_im, ab_s5_log_dt, ab_s5_b_re, ab_s5_b_im,
           ab_s5_c_re, ab_s5_c_im, ab_s5_d, ab_s5_w_glu, ab_s5_b_glu, ab_gla_w_gate, ab_gla_b_gate,
           ab_gla_norm, ab_w_out, cd_w_in, cd_conv_w, cd_gdn_a_log, cd_gdn_dt_bias, cd_gdn_norm,
           cd_lru_conv_w, cd_lru_conv_b, cd_lru_w_a, cd_lru_b_a, cd_lru_w_x, cd_lru_b_x, cd_lru_lambda,
           cd_w_out, moe_w_router_g, moe_b_router_g, moe_w_router_e, moe_b_router_e, moe_w_gate,
           moe_w_up, moe_w_down, ln_mix_g, ln_mix_b, ln_ffn_g, ln_ffn_b):
    batch, seq, _ = x.shape
    length = seq + N_META
    lp = -(-length // ROW_ALIGN) * ROW_ALIGN
    meta = jnp.broadcast_to(meta_tokens.astype(x.dtype)[None], (batch, N_META, D_MODEL))
    h = jnp.concatenate([meta, x, jnp.zeros((batch, lp - length, D_MODEL), x.dtype)], axis=1)
    h = h.reshape(batch * lp, D_MODEL).astype(F32)

    for layer in range(DEPTH):
        j = layer // 2
        router = _router_weights(moe_w_router_g[layer], moe_b_router_g[layer], moe_w_router_e[layer],
                                 moe_b_router_e[layer])
        if layer % 2 == 0:
            s5 = _s5_params(ab_s5_a_re[j], ab_s5_a_im[j], ab_s5_log_dt[j], ab_s5_b_re[j], ab_s5_b_im[j],
                            ab_s5_c_re[j], ab_s5_c_im[j], ab_s5_d[j])
            h, rinfo, h3, counts = _layer_ab(h, batch, lp, ab_w_in[j], s5, ab_s5_w_glu[j], ab_s5_b_glu[j],
                                 ab_gla_w_gate[j], ab_gla_b_gate[j], ab_gla_norm[j], ab_w_out[j],
                                 ln_mix_g[layer], ln_mix_b[layer], router)
        else:
            h, rinfo, h3, counts = _layer_cd(h, batch, lp, cd_w_in[j], cd_conv_w[j], cd_gdn_a_log[j], cd_gdn_dt_bias[j],
                                 cd_gdn_norm[j], cd_lru_conv_w[j], cd_lru_conv_b[j], cd_lru_w_a[j],
                                 cd_lru_b_a[j], cd_lru_w_x[j], cd_lru_b_x[j], cd_lru_lambda[j], cd_w_out[j],
                                 ln_mix_g[layer], ln_mix_b[layer], router)
        last = layer == DEPTH - 1 and seq % MOE_LN_TILE == 0
        h = _hier_moe_ln(h, rinfo, h3, counts, layer, moe_w_gate, moe_w_up, moe_w_down,
                         ln_ffn_g[layer], ln_ffn_b[layer], final=(batch, lp, seq) if last else None)
        if last:
            return h.reshape(batch, seq, D_MODEL).astype(x.dtype)
    return h.reshape(batch, lp, D_MODEL)[:, N_META:length].astype(x.dtype)
```

```python
import functools
import math

import jax
import jax.numpy as jnp
from jax import lax
from jax.experimental import pallas as pl
from jax.experimental.pallas import tpu as pltpu

F32 = jnp.float32
BF16 = jnp.bfloat16
HIGHEST = lax.Precision.HIGHEST

D_MODEL = 2048
N_META = 16
CONV_K = 4
DEPTH = 2
DN_ALPHA = (2.0 * DEPTH) ** 0.25
LN_EPS = 1e-5
NORM_EPS = 1e-6

S5_WIDTH = 1024
S5_GROUP = 16
S5_GROUPS = 64
S5_STATE = 64
S5_LC = 16
S5_SLABS = 8

GLA_HEADS = 4
GLA_DK = 128
GLA_DV = 256
GLA_RANK = 16
GLA_TAU = 16.0

GDN_HEADS = 8
GDN_DK = 128
GDN_DV = 128

LRU_WIDTH = 1024
LRU_BLOCKS = 8
LRU_BW = 128
LRU_C = 8.0

MOE_GROUPS = 4
MOE_EPG = 8
MOE_FF = 256

LANE_BLOCKS = D_MODEL // 128
CHUNK = 64
ROW_ALIGN = 768
VMEM_LIMIT = 56 * 1024 * 1024

NT_DIMS = (((1,), (1,)), ((), ()))
TN_DIMS = (((0,), (0,)), ((), ()))


def _cparams(n_axes):
    return pltpu.CompilerParams(dimension_semantics=("arbitrary",) * n_axes,
                                vmem_limit_bytes=VMEM_LIMIT)


def _dot(a, b):
    return jnp.dot(a, b, preferred_element_type=F32)


def _dotg(a, b, dims):
    return lax.dot_general(a, b, dims, preferred_element_type=F32)


def _split(a):
    hi = a.astype(BF16)
    lo = (a - hi.astype(F32)).astype(BF16)
    return hi, lo


def _sigmoid(x):
    return 1.0 / (1.0 + jnp.exp(-x))


def _softplus(x):
    return jnp.maximum(x, 0.0) + jnp.log1p(jnp.exp(-jnp.abs(x)))


def _gelu(x):
    return 0.5 * x * (1.0 + jnp.tanh(math.sqrt(2.0 / math.pi) * (x + 0.044715 * (x * x * x))))


def _layer_norm(x, g, b):
    mu = jnp.mean(x, axis=-1, keepdims=True)
    xc = x - mu
    var = jnp.mean(xc * xc, axis=-1, keepdims=True)
    return xc * lax.rsqrt(var + LN_EPS) * g + b


PROJ_TILE = 512
PROJ_COLS = 1024


def _in_proj_kernel(x_ref, w_ref, wg_ref, o_ref, g_ref):
    xb = x_ref[...].astype(BF16)
    for j in range(w_ref.shape[1] // PROJ_COLS):
        cols = slice(j * PROJ_COLS, (j + 1) * PROJ_COLS)
        o_ref[:, cols] = _dot(xb, w_ref[:, cols]).astype(o_ref.dtype)
    g_ref[...] = _dot(xb, wg_ref[...])


def _in_proj(x, w, w_gates):
    m, k = x.shape
    n = w.shape[1]
    tm = PROJ_TILE
    const = lambda i: (0, 0)
    return pl.pallas_call(
        _in_proj_kernel,
        grid=(m // tm,),
        in_specs=[pl.BlockSpec((tm, k), lambda i: (i, 0)),
                  pl.BlockSpec((k, n), const, pipeline_mode=pl.Buffered(1)),
                  pl.BlockSpec((k, 128), const, pipeline_mode=pl.Buffered(1))],
        out_specs=[pl.BlockSpec((tm, n), lambda i: (i, 0)),
                   pl.BlockSpec((tm, 128), lambda i: (i, 0))],
        out_shape=[jax.ShapeDtypeStruct((m, n), BF16),
                   jax.ShapeDtypeStruct((m, 128), F32)],
        compiler_params=_cparams(1),
        name="in_proj",
    )(x, w, w_gates)


S5_TILE_CHUNKS = 32


def _s5_row_perm():
    m = S5_TILE_CHUNKS
    r = jnp.arange(S5_LC * m)
    nat = (r % m) * S5_LC + r // m
    return (nat[:, None] == jnp.arange(S5_LC * m)[None, :]).astype(BF16)


def _s5_uproj_kernel(h_ref, perm_ref, w_ref, u_ref):
    m = S5_TILE_CHUNKS
    lhs = _dot(perm_ref[...], h_ref[...].astype(BF16)).astype(BF16)
    z = _dot(lhs, w_ref[...])
    for t in range(S5_LC):
        for s in range(S5_SLABS):
            u_ref[s, :, t * 128:(t + 1) * 128] = z[t * m:(t + 1) * m, s * 128:(s + 1) * 128]


def _s5_uproj(h, w_u):
    t_rows = h.shape[0]
    n_chunks = t_rows // S5_LC
    m = S5_TILE_CHUNKS
    return pl.pallas_call(
        _s5_uproj_kernel,
        grid=(n_chunks // m,),
        in_specs=[pl.BlockSpec((m * S5_LC, D_MODEL), lambda i: (i, 0)),
                  pl.BlockSpec((m * S5_LC, m * S5_LC), lambda i: (0, 0)),
                  pl.BlockSpec((D_MODEL, S5_WIDTH), lambda i: (0, 0))],
        out_specs=pl.BlockSpec((S5_SLABS, m, S5_LC * 128), lambda i: (0, i, 0)),
        out_shape=jax.ShapeDtypeStruct((S5_SLABS, n_chunks, S5_LC * 128), F32),
        compiler_params=_cparams(1),
        name="s5_uproj",
    )(h, _s5_row_perm(), w_u)


def _s5_core_kernel(u_ref, stack_ref, ms_ref, mot_ref, ak_ref, pre_ref, pim_ref, d_ref, z_ref,
                    sre_ref, sim_ref, xre_ref, xim_ref):
    n = u_ref.shape[1]
    half = 8 * S5_STATE
    u = u_ref[0]
    ub = u.astype(BF16)
    s = _dot(ub, ms_ref[0])
    sre_ref[...] = s[:, :half]
    sim_ref[...] = s[:, half:]

    ak = ak_ref[0]
    pre = pre_ref[0]
    pim = pim_ref[0]
    row = lax.broadcasted_iota(jnp.int32, (8, half), 0)

    def body(rb, carry):
        cre, cim = carry
        r0 = pl.multiple_of(rb * 8, 8)
        xr = sre_ref[pl.ds(r0, 8), :]
        xi = sim_ref[pl.ds(r0, 8), :]
        for idx, k in enumerate((1, 2, 4)):
            are = ak[2 * idx:2 * idx + 1, :]
            aim = ak[2 * idx + 1:2 * idx + 2, :]
            keep = row >= k
            shr = jnp.where(keep, pltpu.roll(xr, k, 0), 0.0)
            shi = jnp.where(keep, pltpu.roll(xi, k, 0), 0.0)
            xr, xi = xr + are * shr - aim * shi, xi + are * shi + aim * shr
        xr = xr + pre * cre - pim * cim
        xi = xi + pre * cim + pim * cre
        first = row == 0
        xre_ref[pl.ds(r0, 8), :] = jnp.where(first, cre, pltpu.roll(xr, 1, 0))
        xim_ref[pl.ds(r0, 8), :] = jnp.where(first, cim, pltpu.roll(xi, 1, 0))
        return xr[7:8, :], xi[7:8, :]

    zero = jnp.zeros((1, half), F32)
    lax.fori_loop(0, n // 8, body, (zero, zero))

    xprev = jnp.concatenate([xre_ref[...], xim_ref[...]], axis=1).astype(BF16)
    y_inter = _dotg(xprev, mot_ref[0], NT_DIMS)
    d = d_ref[0]
    npair = S5_LC // 2
    for q in range(npair):
        yq = _dot(ub[:, :(q + 1) * 256], stack_ref[0, (npair - 1 - q) * 256:, :])
        sl = slice(q * 256, (q + 1) * 256)
        y = yq + y_inter[:, sl] + d[:, sl] * u[:, sl]
        z_ref[0, :, sl] = _gelu(y)


def _s5_core(u, prm, batch):
    n_tot = u.shape[1]
    n = n_tot // batch
    w = S5_LC * 128
    half = 8 * S5_STATE
    slab = lambda s, b: (s, 0, 0)
    return pl.pallas_call(
        _s5_core_kernel,
        grid=(S5_SLABS, batch),
        in_specs=[pl.BlockSpec((1, n, w), lambda s, b: (s, b, 0)),
                  pl.BlockSpec((1, w, 256), slab),
                  pl.BlockSpec((1, w, 2 * half), slab),
                  pl.BlockSpec((1, w, 2 * half), slab),
                  pl.BlockSpec((1, 8, half), slab),
                  pl.BlockSpec((1, 8, half), slab),
                  pl.BlockSpec((1, 8, half), slab),
                  pl.BlockSpec((1, 1, w), slab)],
        out_specs=pl.BlockSpec((1, n, w), lambda s, b: (s, b, 0)),
        out_shape=jax.ShapeDtypeStruct(u.shape, F32),
        scratch_shapes=[pltpu.VMEM((n, half), F32)] * 4,
        compiler_params=_cparams(2),
        name="s5_core",
    )(u, prm["stack"], prm["ms"], prm["mot"], prm["ak"], prm["pre"], prm["pim"], prm["d"])


def _s5_out_kernel(z_ref, perm_ref, wglu_ref, bglu_ref, wout_ref, p_ref, zf_ref):
    m = S5_TILE_CHUNKS
    for t in range(S5_LC):
        for s in range(S5_SLABS):
            zf_ref[t * m:(t + 1) * m, s * 128:(s + 1) * 128] = z_ref[s, :, t * 128:(t + 1) * 128]
    zf = zf_ref[...]
    gate = _sigmoid(_dot(zf.astype(BF16), wglu_ref[...]) + bglu_ref[...])
    gated = (zf * gate).astype(BF16)
    nat = _dotg(perm_ref[...], gated, TN_DIMS).astype(BF16)
    p_ref[...] = _dot(nat, wout_ref[...])


def _s5_out(z, w_glu, b_glu, w_out_top):
    n_chunks = z.shape[1]
    m = S5_TILE_CHUNKS
    return pl.pallas_call(
        _s5_out_kernel,
        grid=(n_chunks // m,),
        in_specs=[pl.BlockSpec((S5_SLABS, m, S5_LC * 128), lambda i: (0, i, 0)),
                  pl.BlockSpec((m * S5_LC, m * S5_LC), lambda i: (0, 0)),
                  pl.BlockSpec((S5_WIDTH, S5_WIDTH), lambda i: (0, 0)),
                  pl.BlockSpec((1, S5_WIDTH), lambda i: (0, 0)),
                  pl.BlockSpec((S5_WIDTH, D_MODEL), lambda i: (0, 0))],
        out_specs=pl.BlockSpec((m * S5_LC, D_MODEL), lambda i: (i, 0)),
        out_shape=jax.ShapeDtypeStruct((n_chunks * S5_LC, D_MODEL), F32),
        scratch_shapes=[pltpu.VMEM((S5_LC * m, S5_WIDTH), F32)],
        compiler_params=_cparams(1),
        name="s5_out",
    )(z, _s5_row_perm(), w_glu, b_glu, w_out_top)


def _s5_params(a_re, a_im, log_dt, b_re, b_im, c_re, c_im, d):
    g, p, c, lc = S5_GROUPS, S5_STATE, S5_GROUP, S5_LC
    a = lax.complex(a_re.astype(F32), a_im.astype(F32))
    adt = a * jnp.exp(log_dt.astype(F32))[:, None]
    zoh = (jnp.exp(adt) - 1.0) / a
    bz = lax.complex(b_re.astype(F32), b_im.astype(F32)) * zoh[:, :, None]
    cc = lax.complex(c_re.astype(F32), c_im.astype(F32))
    jj = jnp.arange(lc + 1, dtype=F32)
    apow = jnp.exp(adt[None] * jj[:, None, None])

    def expand(compact, width):
        rows = compact.shape[-2]
        row_grp = (jnp.arange(rows) % 128) // S5_GROUP
        lane_grp = jnp.arange(8 * width) // width
        tiled = jnp.tile(compact.astype(BF16), (1,) * (compact.ndim - 1) + (8,))
        return jnp.where(row_grp[:, None] == lane_grp[None, :], tiled, jnp.zeros((), BF16))

    kre = jnp.einsum("gop,jgp,gpc->jgco", cc, apow[:lc], bz, precision=HIGHEST).real
    kblk = expand(kre.reshape(lc, S5_SLABS, 128, c), c)
    kpad = jnp.concatenate([jnp.zeros_like(kblk[:1]), kblk], axis=0)
    npair = lc // 2
    dd = jnp.arange(npair)[:, None, None]
    lo_in = jnp.arange(2)[None, :, None]
    lo_out = jnp.arange(2)[None, None, :]
    lag = 2 * dd + lo_out - lo_in
    tiles = kpad[lag + 1]
    tiles = tiles.transpose(3, 0, 1, 4, 2, 5).reshape(S5_SLABS, npair, 256, 256)
    stack = tiles[:, ::-1].reshape(S5_SLABS, npair * 256, 256)

    wst = apow[lc - 1 - jnp.arange(lc)][:, :, :, None] * bz[None]
    wst = wst.reshape(lc, S5_SLABS, 8, p, c).transpose(1, 0, 2, 4, 3).reshape(S5_SLABS, lc * 128, p)
    ms = jnp.concatenate([expand(wst.real, p), expand(wst.imag, p)], axis=2)

    zc = cc[None] * apow[1:lc + 1][:, :, None, :]
    zc = zc.reshape(lc, S5_SLABS, 8 * c, p).transpose(1, 0, 2, 3).reshape(S5_SLABS, lc * 128, p)
    mot = jnp.concatenate([expand(zc.real, p), expand(-zc.imag, p)], axis=2)

    def slab_layout(x):
        return x.reshape(x.shape[:-2] + (S5_SLABS, 8 * p))

    kk = jnp.array([1.0, 2.0, 4.0], F32)
    a_k = slab_layout(jnp.exp(adt[None] * (lc * kk)[:, None, None]))
    ak = jnp.stack([a_k[0].real, a_k[0].imag, a_k[1].real, a_k[1].imag, a_k[2].real, a_k[2].imag,
                    jnp.zeros_like(a_k[0].real), jnp.zeros_like(a_k[0].real)], axis=1)
    rr = jnp.arange(1, 9, dtype=F32)
    a_r = slab_layout(jnp.exp(adt[None] * (lc * rr)[:, None, None]))
    pre = a_r.real.transpose(1, 0, 2)
    pim = a_r.imag.transpose(1, 0, 2)
    dsk = jnp.tile(d.astype(F32).reshape(S5_SLABS, 1, 128), (1, 1, lc))
    return {"stack": stack.astype(BF16), "ms": ms.astype(BF16), "mot": mot.astype(BF16),
            "ak": ak, "pre": pre, "pim": pim, "d": dsk}


GLA_TILE = 256


def _gla_kernel(q_ref, k_ref, v_ref, r_ref, gl_ref, wgh_ref, wgl_ref, bg_ref, ng_ref, o_ref, st_ref):
    @pl.when(pl.program_id(1) == 0)
    def _():
        st_ref[...] = jnp.zeros_like(st_ref)

    rowi = lax.broadcasted_iota(jnp.int32, (CHUNK, CHUNK), 0)
    coli = lax.broadcasted_iota(jnp.int32, (CHUNK, CHUNK), 1)
    tri = rowi >= coli
    wgh = wgh_ref[...]
    wgl = wgl_ref[...]
    bg = bg_ref[...]
    ng = ng_ref[...]
    scale = GLA_DK ** -0.5

    nch = GLA_TILE // CHUNK
    ti = lax.broadcasted_iota(jnp.int32, (GLA_TILE, GLA_TILE), 0)
    tj = lax.broadcasted_iota(jnp.int32, (GLA_TILE, GLA_TILE), 1)
    tri_t = jnp.where((ti >= tj) & ((ti // CHUNK) == (tj // CHUNK)), 1.0, 0.0).astype(BF16)
    glh, gll = _split(gl_ref[...])
    pre = _dot(glh, wgh) + _dot(gll, wgh) + _dot(glh, wgl) + bg
    logf = (jnp.minimum(pre, 0.0) - jnp.log1p(jnp.exp(-jnp.abs(pre)))) * (1.0 / GLA_TAU)
    lh, ll = _split(logf)
    bcum = _dot(tri_t, lh) + _dot(tri_t, ll)
    ebc = jnp.exp(bcum)
    kk = k_ref[...].astype(F32)
    q_in = (q_ref[...].astype(F32) * scale * ebc).astype(BF16)
    k_in = (kk * jnp.exp(-bcum)).astype(BF16)

    heads = range(GLA_HEADS)
    chunks = range(nch)
    items = [(c, h) for c in chunks for h in heads]
    rws = [slice(c * CHUNK, (c + 1) * CHUNK) for c in chunks]
    sks = [slice(h * GLA_DK, (h + 1) * GLA_DK) for h in heads]
    svs = [slice(h * GLA_DV, (h + 1) * GLA_DV) for h in heads]
    blast = [bcum[(c + 1) * CHUNK - 1:(c + 1) * CHUNK, :] for c in chunks]
    k_st = [(kk[rws[c], :] * jnp.exp(blast[c] - bcum[rws[c], :])).astype(BF16) for c in chunks]
    dec = [jnp.exp(blast[c]) for c in chunks]
    vh = {(c, h): v_ref[rws[c], svs[h]].astype(BF16) for c, h in items}
    att = {(c, h): _dotg(q_in[rws[c], sks[h]], k_in[rws[c], sks[h]], NT_DIMS) for c, h in items}
    kv = {(c, h): _dotg(vh[c, h], k_st[c][:, sks[h]], TN_DIMS) for c, h in items}
    o_intra = {it: _dot(jnp.where(tri, att[it], 0.0).astype(BF16), vh[it]) for it in items}
    st = [st_ref[h] for h in heads]
    for c in chunks:
        o_inter = [_dotg(q_in[rws[c], sks[h]], st[h].astype(BF16), NT_DIMS) for h in heads]
        for h in heads:
            o = o_intra[c, h] + o_inter[h]
            on = o * lax.rsqrt(jnp.mean(o * o, axis=1, keepdims=True) + NORM_EPS) * ng
            rr = r_ref[rws[c], svs[h]].astype(F32)
            o_ref[rws[c], svs[h]] = (on * (rr * _sigmoid(rr))).astype(o_ref.dtype)
        st = [st[h] * dec[c][:, sks[h]] + kv[c, h] for h in heads]
    for h in heads:
        st_ref[h] = st[h]


def _gla(proj, g_low, wgh, wgl, bg, ng, batch, lp):
    t_rows = proj.shape[0]
    nt = lp // GLA_TILE
    hk = GLA_HEADS * GLA_DK
    hv = GLA_HEADS * GLA_DV
    rowmap = lambda col: (lambda b, t: (b * nt + t, col))
    const = lambda b, t: (0, 0)
    return pl.pallas_call(
        _gla_kernel,
        grid=(batch, nt),
        in_specs=[pl.BlockSpec((GLA_TILE, hk), rowmap(0)),
                  pl.BlockSpec((GLA_TILE, hk), rowmap(1)),
                  pl.BlockSpec((GLA_TILE, hv), rowmap(1)),
                  pl.BlockSpec((GLA_TILE, hv), rowmap(2)),
                  pl.BlockSpec((GLA_TILE, 128), rowmap(0)),
                  pl.BlockSpec((128, hk), const),
                  pl.BlockSpec((128, hk), const),
                  pl.BlockSpec((1, hk), const),
                  pl.BlockSpec((1, GLA_DV), const)],
        out_specs=pl.BlockSpec((GLA_TILE, hv), rowmap(0)),
        out_shape=jax.ShapeDtypeStruct((t_rows, hv), BF16),
        scratch_shapes=[pltpu.VMEM((GLA_HEADS, GLA_DV, GLA_DK), F32)],
        compiler_params=_cparams(2),
        name="gla",
    )(proj, proj, proj, proj, g_low, wgh, wgl, bg, ng)


GDN_TILE = 256
GDN_GROUP = 4
GDN_QKV = 3 * GDN_HEADS * GDN_DK


def _gdn_kernel(qkv_ref, z_ref, ba_ref, cw_ref, hp_ref, ng_ref, o_ref,
                ext_ref, act_ref, beta_ref, g_ref, s_ref):
    tile = GDN_TILE

    @pl.when(pl.program_id(1) == 0)
    def _():
        ext_ref[0:8, :] = jnp.zeros((8, GDN_QKV), F32)
        s_ref[...] = jnp.zeros_like(s_ref)

    @pl.when(pl.program_id(1) > 0)
    def _():
        ext_ref[0:8, :] = ext_ref[tile:tile + 8, :]

    ext_ref[8:tile + 8, :] = qkv_ref[...].astype(F32)
    qscale = GDN_DK ** -0.5
    for cb in range(GDN_QKV // 128):
        cols = slice(cb * 128, (cb + 1) * 128)
        w = cw_ref[:, cols]
        y = (w[3:4] * ext_ref[8:tile + 8, cols] + w[2:3] * ext_ref[7:tile + 7, cols]
             + w[1:2] * ext_ref[6:tile + 6, cols] + w[0:1] * ext_ref[5:tile + 5, cols])
        a = y * _sigmoid(y)
        if cb < 2 * GDN_HEADS:
            a = a * lax.rsqrt(jnp.sum(a * a, axis=1, keepdims=True) + NORM_EPS)
            if cb < GDN_HEADS:
                a = a * qscale
        act_ref[:, cols] = a

    ba = ba_ref[...]
    beta_ref[...] = _sigmoid(ba)
    g_ref[...] = hp_ref[0:1, :] * _softplus(ba + hp_ref[1:2, :])

    rowi = lax.broadcasted_iota(jnp.int32, (CHUNK, CHUNK), 0)
    coli = lax.broadcasted_iota(jnp.int32, (CHUNK, CHUNK), 1)
    tri = rowi >= coli
    strict = rowi > coli
    tri_b = jnp.where(tri, 1.0, 0.0).astype(BF16)
    upper_b = jnp.where(rowi <= coli, 1.0, 0.0).astype(BF16)
    eye = jnp.where(rowi == coli, 1.0, 0.0)
    level_masks = []
    for lvl in range(6):
        bi = rowi >> lvl
        bj = coli >> lvl
        level_masks.append(((bi & 1) == 1) & (bj == bi - 1))
    ng = ng_ref[...]

    heads = range(GDN_HEADS)
    grp = range(GDN_GROUP)
    items = [(j, h) for j in grp for h in heads]
    hcol = lambda t, h: t[:, GDN_HEADS + h:GDN_HEADS + h + 1]

    def group(gi, carry):
        base = gi * (GDN_GROUP * CHUNK)
        rows = [pl.ds(pl.multiple_of(base + j * CHUNK, CHUNK), CHUNK) for j in grp]
        gsp = [_split(g_ref[rows[j], :]) for j in grp]
        gc = [_dot(tri_b, gsp[j][0]) + _dot(tri_b, gsp[j][1]) for j in grp]
        gct = [_dotg(gsp[j][0], upper_b, TN_DIMS) + _dotg(gsp[j][1], upper_b, TN_DIMS) for j in grp]
        eg = [jnp.exp(gc[j]) for j in grp]
        elast = [jnp.exp(gc[j][CHUNK - 1:CHUNK, :] - gc[j]) for j in grp]
        dlast = [jnp.exp(gc[j][CHUNK - 1:CHUNK, :]) for j in grp]
        beta = [beta_ref[rows[j], :] for j in grp]
        qs = {(j, h): act_ref[rows[j], h * GDN_DK:(h + 1) * GDN_DK] for j, h in items}
        ks = {(j, h): act_ref[rows[j], (GDN_HEADS + h) * GDN_DK:(GDN_HEADS + h + 1) * GDN_DK] for j, h in items}
        vs = {(j, h): act_ref[rows[j], (2 * GDN_HEADS + h) * GDN_DK:(2 * GDN_HEADS + h + 1) * GDN_DK]
              for j, h in items}
        bcols = {(j, h): beta[j][:, h:h + 1] for j, h in items}
        kbs = {it: ks[it] * bcols[it] for it in items}
        khb = {it: ks[it].astype(BF16) for it in items}
        gams = {(j, h): jnp.exp(jnp.where(tri, hcol(gc[j], h) - gct[j][GDN_HEADS + h:GDN_HEADS + h + 1, :], -1e30))
                for j, h in items}
        kk = {it: _dotg(kbs[it].astype(BF16), khb[it], NT_DIMS) for it in items}
        qk = {it: _dotg(qs[it].astype(BF16), khb[it], NT_DIMS) for it in items}
        mm = {it: jnp.where(strict, kk[it] * gams[it], 0.0) for it in items}
        tinv = {it: eye - jnp.where(level_masks[0], mm[it], 0.0) for it in items}
        for lvl in range(1, 6):
            tb = {it: tinv[it].astype(BF16) for it in items}
            xs = {it: _dot(jnp.where(level_masks[lvl], mm[it], 0.0).astype(BF16), tb[it]).astype(BF16)
                  for it in items}
            tinv = {it: tinv[it] - _dot(tb[it], xs[it]) for it in items}
        rhs = {(j, h): jnp.concatenate([vs[j, h] * bcols[j, h], kbs[j, h] * hcol(eg[j], h)], axis=1).astype(BF16)
               for j, h in items}
        sol = {it: _dot(tinv[it].astype(BF16), rhs[it]) for it in items}
        aqk = {it: jnp.where(tri, qk[it] * gams[it], 0.0).astype(BF16) for it in items}
        qdec = {(j, h): (qs[j, h] * hcol(eg[j], h)).astype(BF16) for j, h in items}
        kst = {(j, h): (ks[j, h] * hcol(elast[j], h)).astype(BF16) for j, h in items}
        for j in grp:
            s_old = [s_ref[h] for h in heads]
            sb = [s_old[h].astype(BF16) for h in heads]
            ws = [_dot(sol[j, h][:, GDN_DV:].astype(BF16), sb[h]) for h in heads]
            qsd = [_dot(qdec[j, h], sb[h]) for h in heads]
            vnb = [(sol[j, h][:, :GDN_DV] - ws[h]).astype(BF16) for h in heads]
            av = [_dot(aqk[j, h], vnb[h]) for h in heads]
            kv = [_dotg(kst[j, h], vnb[h], TN_DIMS) for h in heads]
            for h in heads:
                o = qsd[h] + av[h]
                on = o * lax.rsqrt(jnp.mean(o * o, axis=1, keepdims=True) + NORM_EPS) * ng
                zz = z_ref[rows[j], h * GDN_DV:(h + 1) * GDN_DV].astype(F32)
                o_ref[rows[j], h * GDN_DV:(h + 1) * GDN_DV] = (on * (zz * _sigmoid(zz))).astype(o_ref.dtype)
                s_ref[h] = s_old[h] * hcol(dlast[j], h) + kv[h]
        return carry

    lax.fori_loop(0, tile // (GDN_GROUP * CHUNK), group, 0)


def _gdn(proj, ba, cw, hp, ng, batch, lp):
    t_rows = proj.shape[0]
    nt = lp // GDN_TILE
    hv = GDN_HEADS * GDN_DV
    rowmap = lambda col: (lambda b, t: (b * nt + t, col))
    const = lambda b, t: (0, 0)
    return pl.pallas_call(
        _gdn_kernel,
        grid=(batch, nt),
        in_specs=[pl.BlockSpec((GDN_TILE, GDN_QKV), rowmap(0)),
                  pl.BlockSpec((GDN_TILE, hv), rowmap(3)),
                  pl.BlockSpec((GDN_TILE, 128), rowmap(0)),
                  pl.BlockSpec((8, GDN_QKV), const),
                  pl.BlockSpec((8, 128), const),
                  pl.BlockSpec((1, GDN_DV), const)],
        out_specs=pl.BlockSpec((GDN_TILE, hv), rowmap(0)),
        out_shape=jax.ShapeDtypeStruct((t_rows, hv), BF16),
        scratch_shapes=[pltpu.VMEM((GDN_TILE + 8, GDN_QKV), F32),
                        pltpu.VMEM((GDN_TILE, GDN_QKV), F32),
                        pltpu.VMEM((GDN_TILE, 128), F32),
                        pltpu.VMEM((GDN_TILE, 128), F32),
                        pltpu.VMEM((GDN_HEADS, GDN_DK, GDN_DV), F32)],
        compiler_params=_cparams(2),
        name="gdn",
    )(proj, proj, ba, cw, hp, ng)


LRU_TILE = 256


def _lru_kernel(x_ref, gate_ref, cw_ref, cb_ref, wax_ref, bax_ref, sp_ref, o_ref,
                ext_ref, a_ref, b_ref, h_ref):
    tile = LRU_TILE

    @pl.when(pl.program_id(1) == 0)
    def _():
        ext_ref[0:8, :] = jnp.zeros((8, LRU_WIDTH), F32)
        h_ref[...] = jnp.zeros_like(h_ref)

    @pl.when(pl.program_id(1) > 0)
    def _():
        ext_ref[0:8, :] = ext_ref[tile:tile + 8, :]

    ext_ref[8:tile + 8, :] = x_ref[...].astype(F32)
    for blk in range(LRU_BLOCKS):
        cols = slice(blk * LRU_BW, (blk + 1) * LRU_BW)
        w = cw_ref[:, cols]
        xc = (w[3:4] * ext_ref[8:tile + 8, cols] + w[2:3] * ext_ref[7:tile + 7, cols]
              + w[1:2] * ext_ref[6:tile + 6, cols] + w[0:1] * ext_ref[5:tile + 5, cols]
              + cb_ref[:, cols])
        pre = _dot(xc.astype(BF16), wax_ref[blk]) + bax_ref[blk]
        r = _sigmoid(pre[:, :LRU_BW])
        i = _sigmoid(pre[:, LRU_BW:])
        log_a = sp_ref[:, cols] * r
        a_ref[:, cols] = jnp.exp(log_a)
        th = jnp.tanh(log_a)
        b_ref[:, cols] = jnp.sqrt(-2.0 * th / (1.0 - th)) * (i * xc)

    row = lax.broadcasted_iota(jnp.int32, (8, LRU_WIDTH), 0)

    def body(rb, carry):
        r0 = pl.multiple_of(rb * 8, 8)
        aa = a_ref[pl.ds(r0, 8), :]
        bb = b_ref[pl.ds(r0, 8), :]
        for k in (1, 2, 4):
            keep = row >= k
            a_sh = jnp.where(keep, pltpu.roll(aa, k, 0), 1.0)
            b_sh = jnp.where(keep, pltpu.roll(bb, k, 0), 0.0)
            bb = aa * b_sh + bb
            aa = aa * a_sh
        hs = bb + aa * carry
        gt = gate_ref[pl.ds(r0, 8), :].astype(F32)
        o_ref[pl.ds(r0, 8), :] = (hs * _gelu(gt)).astype(o_ref.dtype)
        return hs[7:8, :]

    h_ref[...] = lax.fori_loop(0, tile // 8, body, h_ref[...])


def _lru(proj, cw, cb, wax, bax, sp, batch, lp):
    t_rows = proj.shape[0]
    nt = lp // LRU_TILE
    rowmap = lambda col: (lambda b, t: (b * nt + t, col))
    const = lambda b, t: (0, 0)
    return pl.pallas_call(
        _lru_kernel,
        grid=(batch, nt),
        in_specs=[pl.BlockSpec((LRU_TILE, LRU_WIDTH), rowmap(4)),
                  pl.BlockSpec((LRU_TILE, LRU_WIDTH), rowmap(5)),
                  pl.BlockSpec((8, LRU_WIDTH), const),
                  pl.BlockSpec((1, LRU_WIDTH), const),
                  pl.BlockSpec((LRU_BLOCKS, LRU_BW, 2 * LRU_BW), lambda b, t: (0, 0, 0)),
                  pl.BlockSpec((LRU_BLOCKS, 1, 2 * LRU_BW), lambda b, t: (0, 0, 0)),
                  pl.BlockSpec((1, LRU_WIDTH), const)],
        out_specs=pl.BlockSpec((LRU_TILE, LRU_WIDTH), rowmap(0)),
        out_shape=jax.ShapeDtypeStruct((t_rows, LRU_WIDTH), BF16),
        scratch_shapes=[pltpu.VMEM((LRU_TILE + 8, LRU_WIDTH), F32),
                        pltpu.VMEM((LRU_TILE, LRU_WIDTH), F32),
                        pltpu.VMEM((LRU_TILE, LRU_WIDTH), F32),
                        pltpu.VMEM((1, LRU_WIDTH), F32)],
        compiler_params=_cparams(2),
        name="lru",
    )(proj, proj, cw, cb, wax, bax, sp)


OUT_TILE = 512


def _out_ln_kernel(n_add, h_ref, *refs):
    adds = refs[:n_add]
    (a1_ref, w1_ref, a2_ref, w2_ref, g_ref, b_ref, wrh_ref, wrl_ref, br_ref,
     o_ref, r_ref, o3_ref, cnt_ref, run_ref) = refs[n_add:]

    @pl.when(pl.program_id(0) == 0)
    def _():
        run_ref[...] = jnp.zeros_like(run_ref)

    acc = DN_ALPHA * h_ref[...]
    for r in adds:
        acc = acc + r[...]
    acc = acc + _dot(a1_ref[...], w1_ref[...])
    if a2_ref is not None:
        acc = acc + _dot(a2_ref[...], w2_ref[...])
    hn = _layer_norm(acc, g_ref[...], b_ref[...])
    o_ref[...] = hn
    tm = hn.shape[0]
    for c in range(LANE_BLOCKS):
        o3_ref[pl.ds(c, tm, stride=LANE_BLOCKS), :] = hn[:, c * 128:(c + 1) * 128]
    r = _route(hn, wrh_ref[...], wrl_ref[...], br_ref[...])
    lane = lax.broadcasted_iota(jnp.int32, r.shape, 1).astype(F32)
    oh1 = lane == r[:, 0:1]
    oh2 = lane == r[:, 1:2]
    picked = jnp.where(oh1 | oh2, 1.0, 0.0)
    ri = lax.broadcasted_iota(jnp.int32, (tm, tm), 0)
    ci = lax.broadcasted_iota(jnp.int32, (tm, tm), 1)
    before = _dot(jnp.where(ri > ci, 1.0, 0.0).astype(BF16), picked.astype(BF16)) + run_ref[0:1, :]
    rank1 = jnp.sum(jnp.where(oh1, before, 0.0), axis=1, keepdims=True)
    rank2 = jnp.sum(jnp.where(oh2, before, 0.0), axis=1, keepdims=True)
    r_ref[...] = r + jnp.where(lane == 4.0, rank1, 0.0) + jnp.where(lane == 5.0, rank2, 0.0)
    total = run_ref[0:1, :] + jnp.sum(picked, axis=0, keepdims=True)
    run_ref[...] = jnp.broadcast_to(total, run_ref.shape)
    cnt_ref[...] = jnp.broadcast_to(total, cnt_ref.shape)


def _out_ln_kernel_1(h_ref, p_ref, a1_ref, w1_ref, *rest):
    _out_ln_kernel(1, h_ref, p_ref, a1_ref, w1_ref, None, None, *rest)


def _out_ln_kernel_2(h_ref, a1_ref, w1_ref, a2_ref, w2_ref, *rest):
    _out_ln_kernel(0, h_ref, a1_ref, w1_ref, a2_ref, w2_ref, *rest)


def _out_ln(h, addend, pairs, g, b, router):
    t_rows = h.shape[0]
    tm = OUT_TILE
    row = lambda i: (i, 0)
    const = lambda i: (0, 0)
    ins = [h]
    specs = [pl.BlockSpec((tm, D_MODEL), row)]
    if addend is not None:
        ins.append(addend)
        specs.append(pl.BlockSpec((tm, D_MODEL), row))
    for a, w in pairs:
        ins += [a, w]
        specs += [pl.BlockSpec((tm, a.shape[1]), row), pl.BlockSpec(w.shape, const)]
    ins += [g, b]
    specs += [pl.BlockSpec((1, D_MODEL), const)] * 2
    ins += list(router)
    specs += [pl.BlockSpec((D_MODEL, 256), const), pl.BlockSpec((D_MODEL, 128), const),
              pl.BlockSpec((1, 128), const)]
    body = _out_ln_kernel_1 if addend is not None else _out_ln_kernel_2
    return pl.pallas_call(
        body,
        grid=(t_rows // tm,),
        in_specs=specs,
        out_specs=[pl.BlockSpec((tm, D_MODEL), row), pl.BlockSpec((tm, 128), row),
                   pl.BlockSpec((tm * LANE_BLOCKS, 128), row),
                   pl.BlockSpec((8, 128), const)],
        out_shape=[jax.ShapeDtypeStruct((t_rows, D_MODEL), F32),
                   jax.ShapeDtypeStruct((t_rows, 128), F32),
                   jax.ShapeDtypeStruct((t_rows * LANE_BLOCKS, 128), F32),
                   jax.ShapeDtypeStruct((8, 128), F32)],
        scratch_shapes=[pltpu.VMEM((8, 128), F32)],
        compiler_params=_cparams(1),
        name="out_ln",
    )(*ins)


ROUTER_TILE = 768
MOE_TILE = 256
BIG = 1e30


def _route(h, whl, wh, b):
    hh, hl = _split(h)
    both = _dot(hh, whl)
    logits = both[:, :128] + both[:, 128:] + _dot(hl, wh) + b
    lane = lax.broadcasted_iota(jnp.int32, logits.shape, 1).astype(F32)

    def first_argmax(vals, vmax):
        return jnp.min(jnp.where(vals == vmax, lane, 4096.0), axis=1, keepdims=True)

    is_g = lane < MOE_GROUPS
    lg = jnp.where(is_g, logits, -BIG)
    m = jnp.max(lg, axis=1, keepdims=True)
    gidx = first_argmax(lg, m)
    denom = jnp.sum(jnp.where(is_g, jnp.exp(logits - m), 0.0), axis=1, keepdims=True)
    p_top = 1.0 / denom
    base = MOE_GROUPS + MOE_EPG * gidx
    in_grp = (lane >= base) & (lane < base + MOE_EPG)
    le = jnp.where(in_grp, logits, -BIG)
    v1 = jnp.max(le, axis=1, keepdims=True)
    i1 = first_argmax(le, v1)
    le2 = jnp.where(lane == i1, -BIG, le)
    v2 = jnp.max(le2, axis=1, keepdims=True)
    i2 = first_argmax(le2, v2)
    e21 = jnp.exp(v2 - v1)
    w1 = p_top / (1.0 + e21)
    w2 = p_top * e21 / (1.0 + e21)
    return (jnp.where(lane == 0.0, i1 - MOE_GROUPS, 0.0) + jnp.where(lane == 1.0, i2 - MOE_GROUPS, 0.0)
            + jnp.where(lane == 2.0, w1, 0.0) + jnp.where(lane == 3.0, w2, 0.0))


def _router_weights(w_rg, b_rg, w_re, b_re):
    n_exp = MOE_GROUPS * MOE_EPG
    wr = jnp.concatenate([w_rg.astype(F32), w_re.astype(F32).reshape(D_MODEL, n_exp)], axis=1)
    wr = jnp.pad(wr, ((0, 0), (0, 128 - wr.shape[1])))
    wrh, wrl = _split(wr)
    br = jnp.concatenate([b_rg.astype(F32), b_re.astype(F32).reshape(-1)])
    br = jnp.pad(br, (0, 128 - br.shape[0])).reshape(1, 128)
    return jnp.concatenate([wrh, wrl], axis=1), wrh, br


DISPATCH_TILE = 256


def _dispatch_kernel(pos_ref, fill_ref, x_ref, xs_hbm, zbuf, sbuf, sem, rsem):
    tm = DISPATCH_TILE
    lb = LANE_BLOCKS
    i = pl.program_id(0)

    @pl.when(i == 0)
    def _():
        zbuf[...] = jnp.zeros_like(zbuf)

        def fill(e):
            dst = pl.multiple_of(jnp.maximum(fill_ref[e], 0) * lb, lb)
            return pltpu.make_async_copy(zbuf, xs_hbm.at[pl.ds(dst, MOE_TILE * lb)], sem)

        def start(e, c):
            @pl.when(fill_ref[e] >= 0)
            def _():
                fill(e).start()
            return c

        def wait(e, c):
            @pl.when(fill_ref[e] >= 0)
            def _():
                fill(e).wait()
            return c

        lax.fori_loop(0, fill_ref.shape[0], start, 0)
        lax.fori_loop(0, fill_ref.shape[0], wait, 0)

    slot = i % 2
    sbuf[slot] = x_ref[...]
    base = i * (2 * tm)
    for r in range(tm):
        for k in range(2):
            dst = pl.multiple_of(pos_ref[base + 2 * r + k] * lb, lb)
            pltpu.make_async_copy(sbuf.at[slot, pl.ds(r * lb, lb)], xs_hbm.at[pl.ds(dst, lb)],
                                  rsem.at[slot]).start(priority=k)

    def wait_rows(s):
        for _ in range(2):
            pltpu.make_async_copy(sbuf.at[s], xs_hbm.at[pl.ds(0, tm * lb)], rsem.at[s]).wait()

    @pl.when(i > 0)
    def _():
        wait_rows(1 - slot)

    @pl.when(i == pl.num_programs(0) - 1)
    def _():
        wait_rows(slot)


def _dispatch(pos, fill, h3, n_sorted):
    t_rows = h3.shape[0] // LANE_BLOCKS
    tm = DISPATCH_TILE
    return pl.pallas_call(
        _dispatch_kernel,
        grid_spec=pltpu.PrefetchScalarGridSpec(
            num_scalar_prefetch=2,
            grid=(t_rows // tm,),
            in_specs=[pl.BlockSpec((tm * LANE_BLOCKS, 128), lambda i, pos, fill: (i, 0))],
            out_specs=pl.BlockSpec(memory_space=pl.ANY),
            scratch_shapes=[pltpu.VMEM((MOE_TILE * LANE_BLOCKS, 128), F32),
                            pltpu.VMEM((2, tm * LANE_BLOCKS, 128), F32),
                            pltpu.SemaphoreType.DMA(()),
                            pltpu.SemaphoreType.DMA((2,))]),
        out_shape=jax.ShapeDtypeStruct((n_sorted * LANE_BLOCKS, 128), F32),
        compiler_params=_cparams(1),
        name="moe_dispatch",
    )(pos, fill, h3)


def _moe_kernel(nv_ref, dst_ref, te_ref, x_ref, wg_ref, wu_ref, wd_ref, y_hbm,
                ybuf, xmat, wgb, wub, wdb, ssem):
    tm = MOE_TILE
    lb = LANE_BLOCKS
    i = pl.program_id(0)
    nv = nv_ref[0]
    slot = i % 2
    other = 1 - slot

    def scatter_copy(tile, row, buf_slot):
        dst = y_hbm.at[pl.ds(pl.multiple_of(dst_ref[tile * tm + row] * lb, lb), lb)]
        src_rows = pl.ds(row * lb if isinstance(row, int) else pl.multiple_of(row * lb, lb), lb)
        return pltpu.make_async_copy(ybuf.at[buf_slot, src_rows], dst, ssem.at[buf_slot])

    def wait_scatter(buf_slot):
        pltpu.make_async_copy(ybuf.at[buf_slot], y_hbm.at[pl.ds(0, tm * lb)], ssem.at[buf_slot]).wait()

    @pl.when((i == 0) | (te_ref[i] != te_ref[jnp.maximum(i - 1, 0)]))
    def _():
        wgb[...] = wg_ref[0].astype(BF16)
        wub[...] = wu_ref[0].astype(BF16)
        wdb[...] = wd_ref[0].astype(BF16)

    def compute(buf_slot):
        for c in range(lb):
            xmat[:, c * 128:(c + 1) * 128] = x_ref[pl.ds(c, tm, stride=lb), :]
        xb = xmat[...].astype(BF16)
        hg = _dot(xb, wgb[...])
        hu = _dot(xb, wub[...])
        hh = (hg * _sigmoid(hg)) * hu
        y = _dot(hh.astype(BF16), wdb[...])
        for c in range(lb):
            ybuf[buf_slot, pl.ds(c, tm, stride=lb), :] = y[:, c * 128:(c + 1) * 128]

    @pl.when(i == 0)
    def _():
        ybuf[1] = jnp.zeros((tm * lb, 128), F32)
        fill = pltpu.make_async_copy(ybuf.at[1], y_hbm.at[pl.ds(y_hbm.shape[0] - tm * lb, tm * lb)], ssem.at[1])
        fill.start()
        fill.wait()
        compute(0)

    @pl.when((i > 0) & (i < nv))
    def _():
        @pl.when(i >= 2)
        def _():
            wait_scatter(slot)

        for r in range(tm):
            scatter_copy(i - 1, r, other).start(priority=r % 2)
        compute(slot)

    @pl.when(i == nv)
    def _():
        @pl.when(i >= 2)
        def _():
            wait_scatter(slot)

        def body(r, c):
            scatter_copy(i - 1, r, other).start()
            return c
        lax.fori_loop(0, tm, body, 0)
        wait_scatter(other)


def _moe(nv, dst, te, xs, wg, wu, wd, n_out_rows):
    n_tiles = dst.shape[0] // MOE_TILE
    tm = MOE_TILE
    ex = lambda i, nv, dst, te: (te[i], 0, 0)
    return pl.pallas_call(
        _moe_kernel,
        grid_spec=pltpu.PrefetchScalarGridSpec(
            num_scalar_prefetch=3,
            grid=(n_tiles,),
            in_specs=[pl.BlockSpec((tm * LANE_BLOCKS, 128),
                                   lambda i, nv, dst, te: (jnp.minimum(i, nv[0] - 1), 0)),
                      pl.BlockSpec((1, D_MODEL, MOE_FF), ex),
                      pl.BlockSpec((1, D_MODEL, MOE_FF), ex),
                      pl.BlockSpec((1, MOE_FF, D_MODEL), ex)],
            out_specs=pl.BlockSpec(memory_space=pl.ANY),
            scratch_shapes=[pltpu.VMEM((2, tm * LANE_BLOCKS, 128), F32),
                            pltpu.VMEM((tm, D_MODEL), F32),
                            pltpu.VMEM((D_MODEL, MOE_FF), BF16),
                            pltpu.VMEM((D_MODEL, MOE_FF), BF16),
                            pltpu.VMEM((MOE_FF, D_MODEL), BF16),
                            pltpu.SemaphoreType.DMA((2,))]),
        out_shape=jax.ShapeDtypeStruct((n_out_rows * LANE_BLOCKS, 128), F32),
        compiler_params=_cparams(1),
        name="moe_experts",
    )(nv, dst, te, xs, wg, wu, wd)


MOE_LN_TILE = 256


def _moe_ln_kernel(h_ref, y0_ref, y1_ref, r_ref, g_ref, b_ref, o_ref, acc_ref):
    r = r_ref[...]
    w0 = r[:, 2:3]
    w1 = r[:, 3:4]
    tm = h_ref.shape[0]
    for c in range(LANE_BLOCKS):
        cols = slice(c * 128, (c + 1) * 128)
        rows = pl.ds(c, tm, stride=LANE_BLOCKS)
        acc_ref[:, cols] = DN_ALPHA * h_ref[:, cols] + (w0 * y0_ref[rows, :] + w1 * y1_ref[rows, :])
    o_ref[...] = _layer_norm(acc_ref[...], g_ref[...], b_ref[...])


def _moe_ln_final(h, y2, rinfo, g, b, batch, lp, seq):
    t_rows = h.shape[0]
    tm = MOE_LN_TILE
    lb = LANE_BLOCKS
    nj = seq // tm
    first = lambda bb, j: pl.multiple_of(bb * lp + N_META + j * tm, 8)
    const = lambda bb, j: (0, 0)
    return pl.pallas_call(
        _moe_ln_kernel,
        grid=(batch, nj),
        in_specs=[pl.BlockSpec((pl.Element(tm), pl.Element(D_MODEL)), lambda bb, j: (first(bb, j), 0)),
                  pl.BlockSpec((pl.Element(tm * lb), pl.Element(128)), lambda bb, j: (first(bb, j) * lb, 0)),
                  pl.BlockSpec((pl.Element(tm * lb), pl.Element(128)),
                               lambda bb, j: ((t_rows + first(bb, j)) * lb, 0)),
                  pl.BlockSpec((pl.Element(tm), pl.Element(128)), lambda bb, j: (first(bb, j), 0)),
                  pl.BlockSpec((1, D_MODEL), const),
                  pl.BlockSpec((1, D_MODEL), const)],
        out_specs=pl.BlockSpec((tm, D_MODEL), lambda bb, j: (bb * nj + j, 0)),
        out_shape=jax.ShapeDtypeStruct((batch * seq, D_MODEL), F32),
        scratch_shapes=[pltpu.VMEM((tm, D_MODEL), F32)],
        compiler_params=_cparams(2),
        name="moe_ln_final",
    )(h, y2, y2, rinfo, g, b)


def _moe_ln(h, y2, rinfo, g, b):
    t_rows = h.shape[0]
    tm = MOE_LN_TILE
    nb = t_rows // tm
    const = lambda i: (0, 0)
    return pl.pallas_call(
        _moe_ln_kernel,
        grid=(nb,),
        in_specs=[pl.BlockSpec((tm, D_MODEL), lambda i: (i, 0)),
                  pl.BlockSpec((tm * LANE_BLOCKS, 128), lambda i: (i, 0)),
                  pl.BlockSpec((tm * LANE_BLOCKS, 128), lambda i: (nb + i, 0)),
                  pl.BlockSpec((tm, 128), lambda i: (i, 0)),
                  pl.BlockSpec((1, D_MODEL), const),
                  pl.BlockSpec((1, D_MODEL), const)],
        out_specs=pl.BlockSpec((tm, D_MODEL), lambda i: (i, 0)),
        out_shape=jax.ShapeDtypeStruct((t_rows, D_MODEL), F32),
        scratch_shapes=[pltpu.VMEM((tm, D_MODEL), F32)],
        compiler_params=_cparams(1),
        name="moe_ln",
    )(h, y2, y2, rinfo, g, b)


def _hier_moe_ln(h, rinfo, h3, counts, layer, w_gate, w_up, w_down, ln_g, ln_b, final=None):
    t_rows = h.shape[0]
    n_exp = MOE_GROUPS * MOE_EPG

    n_asg = 2 * t_rows
    tm = MOE_TILE
    e_a = rinfo[:, 0:2].astype(jnp.int32).reshape(n_asg)
    rank_a = rinfo[:, 4:6].astype(jnp.int32).reshape(n_asg)
    asg = jnp.arange(n_asg, dtype=jnp.int32)
    total = counts[0, :n_exp].astype(jnp.int32)
    tiles = (total + tm - 1) // tm
    tile_end = jnp.cumsum(tiles)
    tile_start = tile_end - tiles
    pos = (jnp.take(tile_start * tm, e_a) + rank_a).astype(jnp.int32)
    n_tiles = n_asg // tm + n_exp + 1
    n_sorted = n_tiles * tm
    nv = tile_end[-1:].astype(jnp.int32)
    slot_asg = jnp.full((n_sorted,), -1, jnp.int32).at[pos].set(asg)
    is_pad = slot_asg < 0
    dump = n_asg + jnp.arange(n_sorted, dtype=jnp.int32) % tm
    dst = jnp.where(is_pad, dump, (slot_asg & 1) * t_rows + (slot_asg >> 1))
    te = jnp.minimum(jnp.sum(jnp.arange(n_tiles)[:, None] >= tile_end[None, :], axis=1), n_exp - 1)
    te = (te + layer * n_exp).astype(jnp.int32)
    spare = tile_end[-1] + jnp.arange(n_tiles - n_asg // tm)
    fill = jnp.concatenate([jnp.where(tiles > 0, (tile_end - 1) * tm, -1),
                            jnp.where(spare < n_tiles, spare * tm, -1)]).astype(jnp.int32)

    xs = _dispatch(pos, fill, h3, n_sorted)
    y2 = _moe(nv, dst, te, xs,
              w_gate.astype(F32).reshape(-1, D_MODEL, MOE_FF),
              w_up.astype(F32).reshape(-1, D_MODEL, MOE_FF),
              w_down.astype(F32).reshape(-1, MOE_FF, D_MODEL), n_asg + tm)
    g2 = ln_g.reshape(1, D_MODEL).astype(F32)
    b2 = ln_b.reshape(1, D_MODEL).astype(F32)
    if final is not None:
        return _moe_ln_final(h, y2, rinfo, g2, b2, *final)
    return _moe_ln(h, y2, rinfo, g2, b2)


def _layer_ab(h, batch, lp, w_in, s5, w_glu, b_glu, gla_w_gate, gla_b_gate, gla_norm, w_out, ln_g, ln_b,
              router):
    hk = GLA_HEADS * GLA_DK
    hv = GLA_HEADS * GLA_DV
    o_q = S5_WIDTH
    o_g = o_q + 2 * hk + hv
    o_r = o_g + GLA_RANK
    w_in = w_in.astype(F32)
    w_u = w_in[:, :S5_WIDTH].astype(BF16)
    w_main = jnp.concatenate([w_in[:, o_q:o_g], w_in[:, o_r:o_r + hv]], axis=1).astype(BF16)
    w_low = jnp.pad(w_in[:, o_g:o_r], ((0, 0), (0, 128 - GLA_RANK))).astype(BF16)
    u = _s5_uproj(h, w_u)
    z = _s5_core(u, s5, batch)
    p_s5 = _s5_out(z, w_glu.astype(BF16), b_glu.reshape(1, -1).astype(F32), w_out[:S5_WIDTH].astype(BF16))
    proj, g_low = _in_proj(h, w_main, w_low)
    wg = jnp.pad(gla_w_gate.astype(F32), ((0, 128 - GLA_RANK), (0, 0)))
    wgh, wgl = _split(wg)
    y_gla = _gla(proj, g_low, wgh, wgl, gla_b_gate.reshape(1, -1).astype(F32),
                 gla_norm.reshape(1, -1).astype(F32), batch, lp)
    return _out_ln(h, p_s5, [(y_gla, w_out[S5_WIDTH:].astype(BF16))],
                   ln_g.reshape(1, -1).astype(F32), ln_b.reshape(1, -1).astype(F32), router)


def _layer_cd(h, batch, lp, w_in, conv_w, a_log, dt_bias, gdn_norm, lru_conv_w, lru_conv_b,
              lru_w_a, lru_b_a, lru_w_x, lru_b_x, lru_lambda, w_out, ln_g, ln_b, router):
    nq = GDN_QKV
    hv = GDN_HEADS * GDN_DV
    w_in = w_in.astype(F32)
    o_z = nq
    o_b = o_z + hv
    o_a = o_b + GDN_HEADS
    o_x = o_a + GDN_HEADS
    o_gt = o_x + LRU_WIDTH
    w_main = jnp.concatenate([w_in[:, :o_b], w_in[:, o_x:o_gt + LRU_WIDTH]], axis=1).astype(BF16)
    w_ba = jnp.pad(w_in[:, o_b:o_x], ((0, 0), (0, 128 - 2 * GDN_HEADS))).astype(BF16)
    proj, ba = _in_proj(h, w_main, w_ba)

    cw = jnp.pad(conv_w.astype(F32), ((0, 8 - CONV_K), (0, 0)))
    hp = jnp.zeros((8, 128), F32)
    hp = hp.at[0, GDN_HEADS:2 * GDN_HEADS].set(-jnp.exp(a_log.astype(F32)))
    hp = hp.at[1, GDN_HEADS:2 * GDN_HEADS].set(dt_bias.astype(F32))
    y_gdn = _gdn(proj, ba, cw, hp, gdn_norm.reshape(1, -1).astype(F32), batch, lp)

    lcw = jnp.pad(lru_conv_w.astype(F32), ((0, 8 - CONV_K), (0, 0)))
    wax = jnp.concatenate([lru_w_a.astype(F32), lru_w_x.astype(F32)], axis=2).astype(BF16)
    bax = jnp.concatenate([lru_b_a.astype(F32).reshape(LRU_BLOCKS, 1, LRU_BW),
                           lru_b_x.astype(F32).reshape(LRU_BLOCKS, 1, LRU_BW)], axis=2)
    sp = (-LRU_C * jax.nn.softplus(-lru_lambda.astype(F32))).reshape(1, LRU_WIDTH)
    y_lru = _lru(proj, lcw, lru_conv_b.reshape(1, -1).astype(F32), wax, bax, sp, batch, lp)

    return _out_ln(h, None, [(y_gdn, w_out[:hv].astype(BF16)), (y_lru, w_out[hv:].astype(BF16))],
                   ln_g.reshape(1, -1).astype(F32), ln_b.reshape(1, -1).astype(F32), router)


def kernel(x, meta_tokens, ab_w_in, ab_s5_a_re, ab_s5_a_im, ab_s5_log_dt, ab_s5_b_re, ab_s5_b_im,
           ab_s5_c_re, ab_s5_c_im, ab_s5_d, ab_s5_w_glu, ab_s5_b_glu, ab_gla_w_gate, ab_gla_b_gate,
           ab_gla_norm, ab_w_out, cd_w_in, cd_conv_w, cd_gdn_a_log, cd_gdn_dt_bias, cd_gdn_norm,
           cd_lru_conv_w, cd_lru_conv_b, cd_lru_w_a, cd_lru_b_a, cd_lru_w_x, cd_lru_b_x, cd_lru_lambda,
           cd_w_out, moe_w_router_g, moe_b_router_g, moe_w_router_e, moe_b_router_e, moe_w_gate,
           moe_w_up, moe_w_down, ln_mix_g, ln_mix_b, ln_ffn_g, ln_ffn_b):
    batch, seq, _ = x.shape
    length = seq + N_META
    lp = -(-length // ROW_ALIGN) * ROW_ALIGN
    meta = jnp.broadcast_to(meta_tokens.astype(F32)[None], (batch, N_META, D_MODEL))
    h = jnp.zeros((batch, lp, D_MODEL), F32)
    h = lax.dynamic_update_slice(h, meta, (0, 0, 0))
    h = lax.dynamic_update_slice(h, x.astype(F32), (0, N_META, 0))
    h = h.reshape(batch * lp, D_MODEL)

    for layer in range(DEPTH):
        j = layer // 2
        router = _router_weights(moe_w_router_g[layer], moe_b_router_g[layer], moe_w_router_e[layer],
                                 moe_b_router_e[layer])
        if layer % 2 == 0:
            s5 = _s5_params(ab_s5_a_re[j], ab_s5_a_im[j], ab_s5_log_dt[j], ab_s5_b_re[j], ab_s5_b_im[j],
                            ab_s5_c_re[j], ab_s5_c_im[j], ab_s5_d[j])
            h, rinfo, h3, counts = _layer_ab(h, batch, lp, ab_w_in[j], s5, ab_s5_w_glu[j], ab_s5_b_glu[j],
                                 ab_gla_w_gate[j], ab_gla_b_gate[j], ab_gla_norm[j], ab_w_out[j],
                                 ln_mix_g[layer], ln_mix_b[layer], router)
        else:
            h, rinfo, h3, counts = _layer_cd(h, batch, lp, cd_w_in[j], cd_conv_w[j], cd_gdn_a_log[j], cd_gdn_dt_bias[j],
                                 cd_gdn_norm[j], cd_lru_conv_w[j], cd_lru_conv_b[j], cd_lru_w_a[j],
                                 cd_lru_b_a[j], cd_lru_w_x[j], cd_lru_b_x[j], cd_lru_lambda[j], cd_w_out[j],
                                 ln_mix_g[layer], ln_mix_b[layer], router)
        last = layer == DEPTH - 1 and seq % MOE_LN_TILE == 0
        h = _hier_moe_ln(h, rinfo, h3, counts, layer, moe_w_gate, moe_w_up, moe_w_down,
                         ln_ffn_g[layer], ln_ffn_b[layer], final=(batch, lp, seq) if last else None)
        if last:
            return h.reshape(batch, seq, D_MODEL).astype(x.dtype)
    return h.reshape(batch, lp, D_MODEL)[:, N_META:length].astype(x.dtype)
```

```python
import math

import jax
import jax.numpy as jnp
from jax import lax
from jax.experimental import pallas as pl
from jax.experimental.pallas import tpu as pltpu

F32 = jnp.float32
BF16 = jnp.bfloat16
HIGHEST = lax.Precision.HIGHEST

D_MODEL = 2048
N_META = 16
CONV_K = 4
DEPTH = 2
DN_ALPHA = (2.0 * DEPTH) ** 0.25
LN_EPS = 1e-5
NORM_EPS = 1e-6

S5_WIDTH = 1024
S5_GROUP = 16
S5_GROUPS = 64
S5_STATE = 64
S5_LC = 16
S5_SLABS = 8

GLA_HEADS = 4
GLA_DK = 128
GLA_DV = 256
GLA_RANK = 16
GLA_TAU = 16.0

GDN_HEADS = 8
GDN_DK = 128
GDN_DV = 128

LRU_WIDTH = 1024
LRU_BLOCKS = 8
LRU_BW = 128
LRU_C = 8.0

MOE_GROUPS = 4
MOE_EPG = 8
MOE_FF = 256

LANE_BLOCKS = D_MODEL // 128
CHUNK = 64
ROW_ALIGN = 768
VMEM_LIMIT = 56 * 1024 * 1024

NT_DIMS = (((1,), (1,)), ((), ()))
TN_DIMS = (((0,), (0,)), ((), ()))


def _cparams(n_axes):
    return pltpu.CompilerParams(dimension_semantics=("arbitrary",) * n_axes,
                                vmem_limit_bytes=VMEM_LIMIT)


def _dot(a, b):
    return jnp.dot(a, b, preferred_element_type=F32)


def _dotg(a, b, dims):
    return lax.dot_general(a, b, dims, preferred_element_type=F32)


def _split(a):
    hi = a.astype(BF16)
    lo = (a - hi.astype(F32)).astype(BF16)
    return hi, lo


def _sigmoid(x):
    return 1.0 / (1.0 + jnp.exp(-x))


def _softplus(x):
    return jnp.maximum(x, 0.0) + jnp.log1p(jnp.exp(-jnp.abs(x)))


def _gelu(x):
    return 0.5 * x * (1.0 + jnp.tanh(math.sqrt(2.0 / math.pi) * (x + 0.044715 * (x * x * x))))


def _layer_norm(x, g, b):
    mu = jnp.mean(x, axis=-1, keepdims=True)
    xc = x - mu
    var = jnp.mean(xc * xc, axis=-1, keepdims=True)
    return xc * lax.rsqrt(var + LN_EPS) * g + b


PROJ_TILE = 512
PROJ_COLS = 1024


def _in_proj_kernel(x_ref, w_ref, wg_ref, o_ref, g_ref):
    xb = x_ref[...].astype(BF16)
    for j in range(w_ref.shape[1] // PROJ_COLS):
        cols = slice(j * PROJ_COLS, (j + 1) * PROJ_COLS)
        o_ref[:, cols] = _dot(xb, w_ref[:, cols]).astype(o_ref.dtype)
    g_ref[...] = _dot(xb, wg_ref[...])


def _in_proj(x, w, w_gates):
    m, k = x.shape
    n = w.shape[1]
    tm = PROJ_TILE
    const = lambda i: (0, 0)
    return pl.pallas_call(
        _in_proj_kernel,
        grid=(m // tm,),
        in_specs=[pl.BlockSpec((tm, k), lambda i: (i, 0)),
                  pl.BlockSpec((k, n), const, pipeline_mode=pl.Buffered(1)),
                  pl.BlockSpec((k, 128), const, pipeline_mode=pl.Buffered(1))],
        out_specs=[pl.BlockSpec((tm, n), lambda i: (i, 0)),
                   pl.BlockSpec((tm, 128), lambda i: (i, 0))],
        out_shape=[jax.ShapeDtypeStruct((m, n), BF16),
                   jax.ShapeDtypeStruct((m, 128), F32)],
        compiler_params=_cparams(1),
        name="in_proj",
    )(x, w, w_gates)


S5_TILE_CHUNKS = 32


def _s5_row_perm():
    m = S5_TILE_CHUNKS
    r = jnp.arange(S5_LC * m)
    nat = (r % m) * S5_LC + r // m
    return (nat[:, None] == jnp.arange(S5_LC * m)[None, :]).astype(BF16)


def _s5_uproj_kernel(h_ref, perm_ref, w_ref, u_ref):
    m = S5_TILE_CHUNKS
    lhs = _dot(perm_ref[...], h_ref[...].astype(BF16)).astype(BF16)
    z = _dot(lhs, w_ref[...])
    for t in range(S5_LC):
        for s in range(S5_SLABS):
            u_ref[s, :, t * 128:(t + 1) * 128] = z[t * m:(t + 1) * m, s * 128:(s + 1) * 128]


def _s5_uproj(h, w_u):
    t_rows = h.shape[0]
    n_chunks = t_rows // S5_LC
    m = S5_TILE_CHUNKS
    return pl.pallas_call(
        _s5_uproj_kernel,
        grid=(n_chunks // m,),
        in_specs=[pl.BlockSpec((m * S5_LC, D_MODEL), lambda i: (i, 0)),
                  pl.BlockSpec((m * S5_LC, m * S5_LC), lambda i: (0, 0)),
                  pl.BlockSpec((D_MODEL, S5_WIDTH), lambda i: (0, 0))],
        out_specs=pl.BlockSpec((S5_SLABS, m, S5_LC * 128), lambda i: (0, i, 0)),
        out_shape=jax.ShapeDtypeStruct((S5_SLABS, n_chunks, S5_LC * 128), F32),
        compiler_params=_cparams(1),
        name="s5_uproj",
    )(h, _s5_row_perm(), w_u)


def _s5_core_kernel(u_ref, stack_ref, ms_ref, mot_ref, ak_ref, pre_ref, pim_ref, d_ref, z_ref,
                    sre_ref, sim_ref, xre_ref, xim_ref):
    n = u_ref.shape[1]
    half = 8 * S5_STATE
    u = u_ref[0]
    ub = u.astype(BF16)
    s = _dot(ub, ms_ref[0])
    sre_ref[...] = s[:, :half]
    sim_ref[...] = s[:, half:]

    ak = ak_ref[0]
    pre = pre_ref[0]
    pim = pim_ref[0]
    row = lax.broadcasted_iota(jnp.int32, (8, half), 0)

    def body(rb, carry):
        cre, cim = carry
        r0 = pl.multiple_of(rb * 8, 8)
        xr = sre_ref[pl.ds(r0, 8), :]
        xi = sim_ref[pl.ds(r0, 8), :]
        for idx, k in enumerate((1, 2, 4)):
            are = ak[2 * idx:2 * idx + 1, :]
            aim = ak[2 * idx + 1:2 * idx + 2, :]
            keep = row >= k
            shr = jnp.where(keep, pltpu.roll(xr, k, 0), 0.0)
            shi = jnp.where(keep, pltpu.roll(xi, k, 0), 0.0)
            xr, xi = xr + are * shr - aim * shi, xi + are * shi + aim * shr
        xr = xr + pre * cre - pim * cim
        xi = xi + pre * cim + pim * cre
        first = row == 0
        xre_ref[pl.ds(r0, 8), :] = jnp.where(first, cre, pltpu.roll(xr, 1, 0))
        xim_ref[pl.ds(r0, 8), :] = jnp.where(first, cim, pltpu.roll(xi, 1, 0))
        return xr[7:8, :], xi[7:8, :]

    zero = jnp.zeros((1, half), F32)
    lax.fori_loop(0, n // 8, body, (zero, zero))

    xprev = jnp.concatenate([xre_ref[...], xim_ref[...]], axis=1).astype(BF16)
    y_inter = _dotg(xprev, mot_ref[0], NT_DIMS)
    d = d_ref[0]
    npair = S5_LC // 2
    for q in range(npair):
        yq = _dot(ub[:, :(q + 1) * 256], stack_ref[0, (npair - 1 - q) * 256:, :])
        sl = slice(q * 256, (q + 1) * 256)
        y = yq + y_inter[:, sl] + d[:, sl] * u[:, sl]
        z_ref[0, :, sl] = _gelu(y)


def _s5_core(u, prm, batch):
    n_tot = u.shape[1]
    n = n_tot // batch
    w = S5_LC * 128
    half = 8 * S5_STATE
    slab = lambda s, b: (s, 0, 0)
    return pl.pallas_call(
        _s5_core_kernel,
        grid=(S5_SLABS, batch),
        in_specs=[pl.BlockSpec((1, n, w), lambda s, b: (s, b, 0)),
                  pl.BlockSpec((1, w, 256), slab),
                  pl.BlockSpec((1, w, 2 * half), slab),
                  pl.BlockSpec((1, w, 2 * half), slab),
                  pl.BlockSpec((1, 8, half), slab),
                  pl.BlockSpec((1, 8, half), slab),
                  pl.BlockSpec((1, 8, half), slab),
                  pl.BlockSpec((1, 1, w), slab)],
        out_specs=pl.BlockSpec((1, n, w), lambda s, b: (s, b, 0)),
        out_shape=jax.ShapeDtypeStruct(u.shape, F32),
        scratch_shapes=[pltpu.VMEM((n, half), F32)] * 4,
        compiler_params=_cparams(2),
        name="s5_core",
    )(u, prm["stack"], prm["ms"], prm["mot"], prm["ak"], prm["pre"], prm["pim"], prm["d"])


def _s5_out_kernel(z_ref, perm_ref, wglu_ref, bglu_ref, wout_ref, p_ref, zf_ref):
    m = S5_TILE_CHUNKS
    for t in range(S5_LC):
        for s in range(S5_SLABS):
            zf_ref[t * m:(t + 1) * m, s * 128:(s + 1) * 128] = z_ref[s, :, t * 128:(t + 1) * 128]
    zf = zf_ref[...]
    gate = _sigmoid(_dot(zf.astype(BF16), wglu_ref[...]) + bglu_ref[...])
    gated = (zf * gate).astype(BF16)
    nat = _dotg(perm_ref[...], gated, TN_DIMS).astype(BF16)
    p_ref[...] = _dot(nat, wout_ref[...])


def _s5_out(z, w_glu, b_glu, w_out_top):
    n_chunks = z.shape[1]
    m = S5_TILE_CHUNKS
    return pl.pallas_call(
        _s5_out_kernel,
        grid=(n_chunks // m,),
        in_specs=[pl.BlockSpec((S5_SLABS, m, S5_LC * 128), lambda i: (0, i, 0)),
                  pl.BlockSpec((m * S5_LC, m * S5_LC), lambda i: (0, 0)),
                  pl.BlockSpec((S5_WIDTH, S5_WIDTH), lambda i: (0, 0)),
                  pl.BlockSpec((1, S5_WIDTH), lambda i: (0, 0)),
                  pl.BlockSpec((S5_WIDTH, D_MODEL), lambda i: (0, 0))],
        out_specs=pl.BlockSpec((m * S5_LC, D_MODEL), lambda i: (i, 0)),
        out_shape=jax.ShapeDtypeStruct((n_chunks * S5_LC, D_MODEL), F32),
        scratch_shapes=[pltpu.VMEM((S5_LC * m, S5_WIDTH), F32)],
        compiler_params=_cparams(1),
        name="s5_out",
    )(z, _s5_row_perm(), w_glu, b_glu, w_out_top)


def _s5_params(a_re, a_im, log_dt, b_re, b_im, c_re, c_im, d):
    g, p, c, lc = S5_GROUPS, S5_STATE, S5_GROUP, S5_LC
    a = lax.complex(a_re.astype(F32), a_im.astype(F32))
    adt = a * jnp.exp(log_dt.astype(F32))[:, None]
    zoh = (jnp.exp(adt) - 1.0) / a
    bz = lax.complex(b_re.astype(F32), b_im.astype(F32)) * zoh[:, :, None]
    cc = lax.complex(c_re.astype(F32), c_im.astype(F32))
    jj = jnp.arange(lc + 1, dtype=F32)
    apow = jnp.exp(adt[None] * jj[:, None, None])

    def expand(compact, width):
        rows = compact.shape[-2]
        row_grp = (jnp.arange(rows) % 128) // S5_GROUP
        lane_grp = jnp.arange(8 * width) // width
        tiled = jnp.tile(compact.astype(BF16), (1,) * (compact.ndim - 1) + (8,))
        return jnp.where(row_grp[:, None] == lane_grp[None, :], tiled, jnp.zeros((), BF16))

    kre = jnp.einsum("gop,jgp,gpc->jgco", cc, apow[:lc], bz, precision=HIGHEST).real
    kblk = expand(kre.reshape(lc, S5_SLABS, 128, c), c)
    kpad = jnp.concatenate([jnp.zeros_like(kblk[:1]), kblk], axis=0)
    npair = lc // 2
    dd = jnp.arange(npair)[:, None, None]
    lo_in = jnp.arange(2)[None, :, None]
    lo_out = jnp.arange(2)[None, None, :]
    lag = 2 * dd + lo_out - lo_in
    tiles = kpad[lag + 1]
    tiles = tiles.transpose(3, 0, 1, 4, 2, 5).reshape(S5_SLABS, npair, 256, 256)
    stack = tiles[:, ::-1].reshape(S5_SLABS, npair * 256, 256)

    wst = apow[lc - 1 - jnp.arange(lc)][:, :, :, None] * bz[None]
    wst = wst.reshape(lc, S5_SLABS, 8, p, c).transpose(1, 0, 2, 4, 3).reshape(S5_SLABS, lc * 128, p)
    ms = jnp.concatenate([expand(wst.real, p), expand(wst.imag, p)], axis=2)

    zc = cc[None] * apow[1:lc + 1][:, :, None, :]
    zc = zc.reshape(lc, S5_SLABS, 8 * c, p).transpose(1, 0, 2, 3).reshape(S5_SLABS, lc * 128, p)
    mot = jnp.concatenate([expand(zc.real, p), expand(-zc.imag, p)], axis=2)

    def slab_layout(x):
        return x.reshape(x.shape[:-2] + (S5_SLABS, 8 * p))

    kk = jnp.array([1.0, 2.0, 4.0], F32)
    a_k = slab_layout(jnp.exp(adt[None] * (lc * kk)[:, None, None]))
    ak = jnp.stack([a_k[0].real, a_k[0].imag, a_k[1].real, a_k[1].imag, a_k[2].real, a_k[2].imag,
                    jnp.zeros_like(a_k[0].real), jnp.zeros_like(a_k[0].real)], axis=1)
    rr = jnp.arange(1, 9, dtype=F32)
    a_r = slab_layout(jnp.exp(adt[None] * (lc * rr)[:, None, None]))
    pre = a_r.real.transpose(1, 0, 2)
    pim = a_r.imag.transpose(1, 0, 2)
    dsk = jnp.tile(d.astype(F32).reshape(S5_SLABS, 1, 128), (1, 1, lc))
    return {"stack": stack.astype(BF16), "ms": ms.astype(BF16), "mot": mot.astype(BF16),
            "ak": ak, "pre": pre, "pim": pim, "d": dsk}


GLA_TILE = 256


def _gla_kernel(q_ref, k_ref, v_ref, r_ref, gl_ref, wgh_ref, wgl_ref, bg_ref, ng_ref, o_ref, st_ref):
    @pl.when(pl.program_id(1) == 0)
    def _():
        st_ref[...] = jnp.zeros_like(st_ref)

    rowi = lax.broadcasted_iota(jnp.int32, (CHUNK, CHUNK), 0)
    coli = lax.broadcasted_iota(jnp.int32, (CHUNK, CHUNK), 1)
    tri = rowi >= coli
    wgh = wgh_ref[...]
    wgl = wgl_ref[...]
    bg = bg_ref[...]
    ng = ng_ref[...]
    scale = GLA_DK ** -0.5

    nch = GLA_TILE // CHUNK
    ti = lax.broadcasted_iota(jnp.int32, (GLA_TILE, GLA_TILE), 0)
    tj = lax.broadcasted_iota(jnp.int32, (GLA_TILE, GLA_TILE), 1)
    tri_t = jnp.where((ti >= tj) & ((ti // CHUNK) == (tj // CHUNK)), 1.0, 0.0).astype(BF16)
    glh, gll = _split(gl_ref[...])
    pre = _dot(glh, wgh) + _dot(gll, wgh) + _dot(glh, wgl) + bg
    logf = (jnp.minimum(pre, 0.0) - jnp.log1p(jnp.exp(-jnp.abs(pre)))) * (1.0 / GLA_TAU)
    lh, ll = _split(logf)
    bcum = _dot(tri_t, lh) + _dot(tri_t, ll)
    ebc = jnp.exp(bcum)
    kk = k_ref[...].astype(F32)
    q_in = (q_ref[...].astype(F32) * scale * ebc).astype(BF16)
    k_in = (kk * jnp.exp(-bcum)).astype(BF16)

    heads = range(GLA_HEADS)
    chunks = range(nch)
    items = [(c, h) for c in chunks for h in heads]
    rws = [slice(c * CHUNK, (c + 1) * CHUNK) for c in chunks]
    sks = [slice(h * GLA_DK, (h + 1) * GLA_DK) for h in heads]
    svs = [slice(h * GLA_DV, (h + 1) * GLA_DV) for h in heads]
    blast = [bcum[(c + 1) * CHUNK - 1:(c + 1) * CHUNK, :] for c in chunks]
    k_st = [(kk[rws[c], :] * jnp.exp(blast[c] - bcum[rws[c], :])).astype(BF16) for c in chunks]
    dec = [jnp.exp(blast[c]) for c in chunks]
    vh = {(c, h): v_ref[rws[c], svs[h]].astype(BF16) for c, h in items}
    att = {(c, h): _dotg(q_in[rws[c], sks[h]], k_in[rws[c], sks[h]], NT_DIMS) for c, h in items}
    kv = {(c, h): _dotg(vh[c, h], k_st[c][:, sks[h]], TN_DIMS) for c, h in items}
    o_intra = {it: _dot(jnp.where(tri, att[it], 0.0).astype(BF16), vh[it]) for it in items}
    st = [st_ref[h] for h in heads]
    for c in chunks:
        o_inter = [_dotg(q_in[rws[c], sks[h]], st[h].astype(BF16), NT_DIMS) for h in heads]
        for h in heads:
            o = o_intra[c, h] + o_inter[h]
            on = o * lax.rsqrt(jnp.mean(o * o, axis=1, keepdims=True) + NORM_EPS) * ng
            rr = r_ref[rws[c], svs[h]].astype(F32)
            o_ref[rws[c], svs[h]] = (on * (rr * _sigmoid(rr))).astype(o_ref.dtype)
        st = [st[h] * dec[c][:, sks[h]] + kv[c, h] for h in heads]
    for h in heads:
        st_ref[h] = st[h]


def _gla(proj, g_low, wgh, wgl, bg, ng, batch, lp):
    t_rows = proj.shape[0]
    nt = lp // GLA_TILE
    hk = GLA_HEADS * GLA_DK
    hv = GLA_HEADS * GLA_DV
    rowmap = lambda col: (lambda b, t: (b * nt + t, col))
    const = lambda b, t: (0, 0)
    return pl.pallas_call(
        _gla_kernel,
        grid=(batch, nt),
        in_specs=[pl.BlockSpec((GLA_TILE, hk), rowmap(0)),
                  pl.BlockSpec((GLA_TILE, hk), rowmap(1)),
                  pl.BlockSpec((GLA_TILE, hv), rowmap(1)),
                  pl.BlockSpec((GLA_TILE, hv), rowmap(2)),
                  pl.BlockSpec((GLA_TILE, 128), rowmap(0)),
                  pl.BlockSpec((128, hk), const),
                  pl.BlockSpec((128, hk), const),
                  pl.BlockSpec((1, hk), const),
                  pl.BlockSpec((1, GLA_DV), const)],
        out_specs=pl.BlockSpec((GLA_TILE, hv), rowmap(0)),
        out_shape=jax.ShapeDtypeStruct((t_rows, hv), BF16),
        scratch_shapes=[pltpu.VMEM((GLA_HEADS, GLA_DV, GLA_DK), F32)],
        compiler_params=_cparams(2),
        name="gla",
    )(proj, proj, proj, proj, g_low, wgh, wgl, bg, ng)


GDN_TILE = 256
GDN_GROUP = 4
GDN_QKV = 3 * GDN_HEADS * GDN_DK


def _gdn_kernel(qkv_ref, z_ref, ba_ref, cw_ref, hp_ref, ng_ref, o_ref,
                ext_ref, act_ref, beta_ref, g_ref, s_ref):
    tile = GDN_TILE

    @pl.when(pl.program_id(1) == 0)
    def _():
        ext_ref[0:8, :] = jnp.zeros((8, GDN_QKV), F32)
        s_ref[...] = jnp.zeros_like(s_ref)

    @pl.when(pl.program_id(1) > 0)
    def _():
        ext_ref[0:8, :] = ext_ref[tile:tile + 8, :]

    ext_ref[8:tile + 8, :] = qkv_ref[...].astype(F32)
    qscale = GDN_DK ** -0.5
    for cb in range(GDN_QKV // 128):
        cols = slice(cb * 128, (cb + 1) * 128)
        w = cw_ref[:, cols]
        y = (w[3:4] * ext_ref[8:tile + 8, cols] + w[2:3] * ext_ref[7:tile + 7, cols]
             + w[1:2] * ext_ref[6:tile + 6, cols] + w[0:1] * ext_ref[5:tile + 5, cols])
        a = y * _sigmoid(y)
        if cb < 2 * GDN_HEADS:
            a = a * lax.rsqrt(jnp.sum(a * a, axis=1, keepdims=True) + NORM_EPS)
            if cb < GDN_HEADS:
                a = a * qscale
        act_ref[:, cols] = a

    ba = ba_ref[...]
    beta_ref[...] = _sigmoid(ba)
    g_ref[...] = hp_ref[0:1, :] * _softplus(ba + hp_ref[1:2, :])

    rowi = lax.broadcasted_iota(jnp.int32, (CHUNK, CHUNK), 0)
    coli = lax.broadcasted_iota(jnp.int32, (CHUNK, CHUNK), 1)
    tri = rowi >= coli
    strict = rowi > coli
    tri_b = jnp.where(tri, 1.0, 0.0).astype(BF16)
    upper_b = jnp.where(rowi <= coli, 1.0, 0.0).astype(BF16)
    eye = jnp.where(rowi == coli, 1.0, 0.0)
    level_masks = []
    for lvl in range(6):
        bi = rowi >> lvl
        bj = coli >> lvl
        level_masks.append(((bi & 1) == 1) & (bj == bi - 1))
    ng = ng_ref[...]

    heads = range(GDN_HEADS)
    grp = range(GDN_GROUP)
    items = [(j, h) for j in grp for h in heads]
    hcol = lambda t, h: t[:, GDN_HEADS + h:GDN_HEADS + h + 1]

    def group(gi, carry):
        base = gi * (GDN_GROUP * CHUNK)
        rows = [pl.ds(pl.multiple_of(base + j * CHUNK, CHUNK), CHUNK) for j in grp]
        gsp = [_split(g_ref[rows[j], :]) for j in grp]
        gc = [_dot(tri_b, gsp[j][0]) + _dot(tri_b, gsp[j][1]) for j in grp]
        gct = [_dotg(gsp[j][0], upper_b, TN_DIMS) + _dotg(gsp[j][1], upper_b, TN_DIMS) for j in grp]
        eg = [jnp.exp(gc[j]) for j in grp]
        elast = [jnp.exp(gc[j][CHUNK - 1:CHUNK, :] - gc[j]) for j in grp]
        dlast = [jnp.exp(gc[j][CHUNK - 1:CHUNK, :]) for j in grp]
        beta = [beta_ref[rows[j], :] for j in grp]
        qs = {(j, h): act_ref[rows[j], h * GDN_DK:(h + 1) * GDN_DK] for j, h in items}
        ks = {(j, h): act_ref[rows[j], (GDN_HEADS + h) * GDN_DK:(GDN_HEADS + h + 1) * GDN_DK] for j, h in items}
        vs = {(j, h): act_ref[rows[j], (2 * GDN_HEADS + h) * GDN_DK:(2 * GDN_HEADS + h + 1) * GDN_DK]
              for j, h in items}
        bcols = {(j, h): beta[j][:, h:h + 1] for j, h in items}
        kbs = {it: ks[it] * bcols[it] for it in items}
        khb = {it: ks[it].astype(BF16) for it in items}
        gams = {(j, h): jnp.exp(jnp.where(tri, hcol(gc[j], h) - gct[j][GDN_HEADS + h:GDN_HEADS + h + 1, :], -1e30))
                for j, h in items}
        kk = {it: _dotg(kbs[it].astype(BF16), khb[it], NT_DIMS) for it in items}
        qk = {it: _dotg(qs[it].astype(BF16), khb[it], NT_DIMS) for it in items}
        mm = {it: jnp.where(strict, kk[it] * gams[it], 0.0) for it in items}
        tinv = {it: eye - jnp.where(level_masks[0], mm[it], 0.0) for it in items}
        for lvl in range(1, 6):
            tb = {it: tinv[it].astype(BF16) for it in items}
            xs = {it: _dot(jnp.where(level_masks[lvl], mm[it], 0.0).astype(BF16), tb[it]).astype(BF16)
                  for it in items}
            tinv = {it: tinv[it] - _dot(tb[it], xs[it]) for it in items}
        rhs = {(j, h): jnp.concatenate([vs[j, h] * bcols[j, h], kbs[j, h] * hcol(eg[j], h)], axis=1).astype(BF16)
               for j, h in items}
        sol = {it: _dot(tinv[it].astype(BF16), rhs[it]) for it in items}
        aqk = {it: jnp.where(tri, qk[it] * gams[it], 0.0).astype(BF16) for it in items}
        qdec = {(j, h): (qs[j, h] * hcol(eg[j], h)).astype(BF16) for j, h in items}
        kst = {(j, h): (ks[j, h] * hcol(elast[j], h)).astype(BF16) for j, h in items}
        for j in grp:
            s_old = [s_ref[h] for h in heads]
            sb = [s_old[h].astype(BF16) for h in heads]
            ws = [_dot(sol[j, h][:, GDN_DV:].astype(BF16), sb[h]) for h in heads]
            qsd = [_dot(qdec[j, h], sb[h]) for h in heads]
            vnb = [(sol[j, h][:, :GDN_DV] - ws[h]).astype(BF16) for h in heads]
            av = [_dot(aqk[j, h], vnb[h]) for h in heads]
            kv = [_dotg(kst[j, h], vnb[h], TN_DIMS) for h in heads]
            for h in heads:
                o = qsd[h] + av[h]
                on = o * lax.rsqrt(jnp.mean(o * o, axis=1, keepdims=True) + NORM_EPS) * ng
                zz = z_ref[rows[j], h * GDN_DV:(h + 1) * GDN_DV].astype(F32)
                o_ref[rows[j], h * GDN_DV:(h + 1) * GDN_DV] = (on * (zz * _sigmoid(zz))).astype(o_ref.dtype)
                s_ref[h] = s_old[h] * hcol(dlast[j], h) + kv[h]
        return carry

    lax.fori_loop(0, tile // (GDN_GROUP * CHUNK), group, 0)


def _gdn(proj, ba, cw, hp, ng, batch, lp):
    t_rows = proj.shape[0]
    nt = lp // GDN_TILE
    hv = GDN_HEADS * GDN_DV
    rowmap = lambda col: (lambda b, t: (b * nt + t, col))
    const = lambda b, t: (0, 0)
    return pl.pallas_call(
        _gdn_kernel,
        grid=(batch, nt),
        in_specs=[pl.BlockSpec((GDN_TILE, GDN_QKV), rowmap(0)),
                  pl.BlockSpec((GDN_TILE, hv), rowmap(3)),
                  pl.BlockSpec((GDN_TILE, 128), rowmap(0)),
                  pl.BlockSpec((8, GDN_QKV), const),
                  pl.BlockSpec((8, 128), const),
                  pl.BlockSpec((1, GDN_DV), const)],
        out_specs=pl.BlockSpec((GDN_TILE, hv), rowmap(0)),
        out_shape=jax.ShapeDtypeStruct((t_rows, hv), BF16),
        scratch_shapes=[pltpu.VMEM((GDN_TILE + 8, GDN_QKV), F32),
                        pltpu.VMEM((GDN_TILE, GDN_QKV), F32),
                        pltpu.VMEM((GDN_TILE, 128), F32),
                        pltpu.VMEM((GDN_TILE, 128), F32),
                        pltpu.VMEM((GDN_HEADS, GDN_DK, GDN_DV), F32)],
        compiler_params=_cparams(2),
        name="gdn",
    )(proj, proj, ba, cw, hp, ng)


LRU_TILE = 256


def _lru_kernel(x_ref, gate_ref, cw_ref, cb_ref, wax_ref, bax_ref, sp_ref, o_ref,
                ext_ref, a_ref, b_ref, h_ref):
    tile = LRU_TILE

    @pl.when(pl.program_id(1) == 0)
    def _():
        ext_ref[0:8, :] = jnp.zeros((8, LRU_WIDTH), F32)
        h_ref[...] = jnp.zeros_like(h_ref)

    @pl.when(pl.program_id(1) > 0)
    def _():
        ext_ref[0:8, :] = ext_ref[tile:tile + 8, :]

    ext_ref[8:tile + 8, :] = x_ref[...].astype(F32)
    for blk in range(LRU_BLOCKS):
        cols = slice(blk * LRU_BW, (blk + 1) * LRU_BW)
        w = cw_ref[:, cols]
        xc = (w[3:4] * ext_ref[8:tile + 8, cols] + w[2:3] * ext_ref[7:tile + 7, cols]
              + w[1:2] * ext_ref[6:tile + 6, cols] + w[0:1] * ext_ref[5:tile + 5, cols]
              + cb_ref[:, cols])
        pre = _dot(xc.astype(BF16), wax_ref[blk]) + bax_ref[blk]
        r = _sigmoid(pre[:, :LRU_BW])
        i = _sigmoid(pre[:, LRU_BW:])
        log_a = sp_ref[:, cols] * r
        a_ref[:, cols] = jnp.exp(log_a)
        th = jnp.tanh(log_a)
        b_ref[:, cols] = jnp.sqrt(-2.0 * th / (1.0 - th)) * (i * xc)

    row = lax.broadcasted_iota(jnp.int32, (8, LRU_WIDTH), 0)

    def body(rb, carry):
        r0 = pl.multiple_of(rb * 8, 8)
        aa = a_ref[pl.ds(r0, 8), :]
        bb = b_ref[pl.ds(r0, 8), :]
        for k in (1, 2, 4):
            keep = row >= k
            a_sh = jnp.where(keep, pltpu.roll(aa, k, 0), 1.0)
            b_sh = jnp.where(keep, pltpu.roll(bb, k, 0), 0.0)
            bb = aa * b_sh + bb
            aa = aa * a_sh
        hs = bb + aa * carry
        gt = gate_ref[pl.ds(r0, 8), :].astype(F32)
        o_ref[pl.ds(r0, 8), :] = (hs * _gelu(gt)).astype(o_ref.dtype)
        return hs[7:8, :]

    h_ref[...] = lax.fori_loop(0, tile // 8, body, h_ref[...])


def _lru(proj, cw, cb, wax, bax, sp, batch, lp):
    t_rows = proj.shape[0]
    nt = lp // LRU_TILE
    rowmap = lambda col: (lambda b, t: (b * nt + t, col))
    const = lambda b, t: (0, 0)
    return pl.pallas_call(
        _lru_kernel,
        grid=(batch, nt),
        in_specs=[pl.BlockSpec((LRU_TILE, LRU_WIDTH), rowmap(4)),
                  pl.BlockSpec((LRU_TILE, LRU_WIDTH), rowmap(5)),
                  pl.BlockSpec((8, LRU_WIDTH), const),
                  pl.BlockSpec((1, LRU_WIDTH), const),
                  pl.BlockSpec((LRU_BLOCKS, LRU_BW, 2 * LRU_BW), lambda b, t: (0, 0, 0)),
                  pl.BlockSpec((LRU_BLOCKS, 1, 2 * LRU_BW), lambda b, t: (0, 0, 0)),
                  pl.BlockSpec((1, LRU_WIDTH), const)],
        out_specs=pl.BlockSpec((LRU_TILE, LRU_WIDTH), rowmap(0)),
        out_shape=jax.ShapeDtypeStruct((t_rows, LRU_WIDTH), BF16),
        scratch_shapes=[pltpu.VMEM((LRU_TILE + 8, LRU_WIDTH), F32),
                        pltpu.VMEM((LRU_TILE, LRU_WIDTH), F32),
                        pltpu.VMEM((LRU_TILE, LRU_WIDTH), F32),
                        pltpu.VMEM((1, LRU_WIDTH), F32)],
        compiler_params=_cparams(2),
        name="lru",
    )(proj, proj, cw, cb, wax, bax, sp)


OUT_TILE = 512


def _out_ln_kernel(n_add, h_ref, *refs):
    adds = refs[:n_add]
    (a1_ref, w1_ref, a2_ref, w2_ref, g_ref, b_ref, wrh_ref, wrl_ref, br_ref,
     o_ref, r_ref, o3_ref, cnt_ref, run_ref) = refs[n_add:]

    @pl.when(pl.program_id(0) == 0)
    def _():
        run_ref[...] = jnp.zeros_like(run_ref)

    acc = DN_ALPHA * h_ref[...]
    for r in adds:
        acc = acc + r[...]
    acc = acc + _dot(a1_ref[...], w1_ref[...])
    if a2_ref is not None:
        acc = acc + _dot(a2_ref[...], w2_ref[...])
    hn = _layer_norm(acc, g_ref[...], b_ref[...])
    o_ref[...] = hn
    tm = hn.shape[0]
    for c in range(LANE_BLOCKS):
        o3_ref[pl.ds(c, tm, stride=LANE_BLOCKS), :] = hn[:, c * 128:(c + 1) * 128]
    r = _route(hn, wrh_ref[...], wrl_ref[...], br_ref[...])
    lane = lax.broadcasted_iota(jnp.int32, r.shape, 1).astype(F32)
    oh1 = lane == r[:, 0:1]
    oh2 = lane == r[:, 1:2]
    picked = jnp.where(oh1 | oh2, 1.0, 0.0)
    ri = lax.broadcasted_iota(jnp.int32, (tm, tm), 0)
    ci = lax.broadcasted_iota(jnp.int32, (tm, tm), 1)
    before = _dot(jnp.where(ri > ci, 1.0, 0.0).astype(BF16), picked.astype(BF16)) + run_ref[0:1, :]
    rank1 = jnp.sum(jnp.where(oh1, before, 0.0), axis=1, keepdims=True)
    rank2 = jnp.sum(jnp.where(oh2, before, 0.0), axis=1, keepdims=True)
    r_ref[...] = r + jnp.where(lane == 4.0, rank1, 0.0) + jnp.where(lane == 5.0, rank2, 0.0)
    total = run_ref[0:1, :] + jnp.sum(picked, axis=0, keepdims=True)
    run_ref[...] = jnp.broadcast_to(total, run_ref.shape)
    cnt_ref[...] = jnp.broadcast_to(total, cnt_ref.shape)


def _out_ln_kernel_1(h_ref, p_ref, a1_ref, w1_ref, *rest):
    _out_ln_kernel(1, h_ref, p_ref, a1_ref, w1_ref, None, None, *rest)


def _out_ln_kernel_2(h_ref, a1_ref, w1_ref, a2_ref, w2_ref, *rest):
    _out_ln_kernel(0, h_ref, a1_ref, w1_ref, a2_ref, w2_ref, *rest)


def _out_ln(h, addend, pairs, g, b, router):
    t_rows = h.shape[0]
    tm = OUT_TILE
    row = lambda i: (i, 0)
    const = lambda i: (0, 0)
    ins = [h]
    specs = [pl.BlockSpec((tm, D_MODEL), row)]
    if addend is not None:
        ins.append(addend)
        specs.append(pl.BlockSpec((tm, D_MODEL), row))
    for a, w in pairs:
        ins += [a, w]
        specs += [pl.BlockSpec((tm, a.shape[1]), row), pl.BlockSpec(w.shape, const)]
    ins += [g, b]
    specs += [pl.BlockSpec((1, D_MODEL), const)] * 2
    ins += list(router)
    specs += [pl.BlockSpec((D_MODEL, 256), const), pl.BlockSpec((D_MODEL, 128), const),
              pl.BlockSpec((1, 128), const)]
    body = _out_ln_kernel_1 if addend is not None else _out_ln_kernel_2
    return pl.pallas_call(
        body,
        grid=(t_rows // tm,),
        in_specs=specs,
        out_specs=[pl.BlockSpec((tm, D_MODEL), row), pl.BlockSpec((tm, 128), row),
                   pl.BlockSpec((tm * LANE_BLOCKS, 128), row),
                   pl.BlockSpec((8, 128), const)],
        out_shape=[jax.ShapeDtypeStruct((t_rows, D_MODEL), F32),
                   jax.ShapeDtypeStruct((t_rows, 128), F32),
                   jax.ShapeDtypeStruct((t_rows * LANE_BLOCKS, 128), F32),
                   jax.ShapeDtypeStruct((8, 128), F32)],
        scratch_shapes=[pltpu.VMEM((8, 128), F32)],
        compiler_params=_cparams(1),
        name="out_ln",
    )(*ins)


MOE_TILE = 256
BIG = 1e30


def _route(h, whl, wh, b):
    hh, hl = _split(h)
    both = _dot(hh, whl)
    logits = both[:, :128] + both[:, 128:] + _dot(hl, wh) + b
    lane = lax.broadcasted_iota(jnp.int32, logits.shape, 1).astype(F32)

    def first_argmax(vals, vmax):
        return jnp.min(jnp.where(vals == vmax, lane, 4096.0), axis=1, keepdims=True)

    is_g = lane < MOE_GROUPS
    lg = jnp.where(is_g, logits, -BIG)
    m = jnp.max(lg, axis=1, keepdims=True)
    gidx = first_argmax(lg, m)
    denom = jnp.sum(jnp.where(is_g, jnp.exp(logits - m), 0.0), axis=1, keepdims=True)
    p_top = 1.0 / denom
    base = MOE_GROUPS + MOE_EPG * gidx
    in_grp = (lane >= base) & (lane < base + MOE_EPG)
    le = jnp.where(in_grp, logits, -BIG)
    v1 = jnp.max(le, axis=1, keepdims=True)
    i1 = first_argmax(le, v1)
    le2 = jnp.where(lane == i1, -BIG, le)
    v2 = jnp.max(le2, axis=1, keepdims=True)
    i2 = first_argmax(le2, v2)
    e21 = jnp.exp(v2 - v1)
    w1 = p_top / (1.0 + e21)
    w2 = p_top * e21 / (1.0 + e21)
    return (jnp.where(lane == 0.0, i1 - MOE_GROUPS, 0.0) + jnp.where(lane == 1.0, i2 - MOE_GROUPS, 0.0)
            + jnp.where(lane == 2.0, w1, 0.0) + jnp.where(lane == 3.0, w2, 0.0))


def _router_weights(w_rg, b_rg, w_re, b_re):
    n_exp = MOE_GROUPS * MOE_EPG
    wr = jnp.concatenate([w_rg.astype(F32), w_re.astype(F32).reshape(D_MODEL, n_exp)], axis=1)
    wr = jnp.pad(wr, ((0, 0), (0, 128 - wr.shape[1])))
    wrh, wrl = _split(wr)
    br = jnp.concatenate([b_rg.astype(F32), b_re.astype(F32).reshape(-1)])
    br = jnp.pad(br, (0, 128 - br.shape[0])).reshape(1, 128)
    return jnp.concatenate([wrh, wrl], axis=1), wrh, br


DISPATCH_TILE = 256


def _dispatch_kernel(pos_ref, fill_ref, x_ref, xs_hbm, zbuf, sbuf, sem, rsem):
    tm = DISPATCH_TILE
    lb = LANE_BLOCKS
    i = pl.program_id(0)

    @pl.when(i == 0)
    def _():
        zbuf[...] = jnp.zeros_like(zbuf)

        def fill(e):
            dst = pl.multiple_of(jnp.maximum(fill_ref[e], 0) * lb, lb)
            return pltpu.make_async_copy(zbuf, xs_hbm.at[pl.ds(dst, MOE_TILE * lb)], sem)

        def start(e, c):
            @pl.when(fill_ref[e] >= 0)
            def _():
                fill(e).start()
            return c

        def wait(e, c):
            @pl.when(fill_ref[e] >= 0)
            def _():
                fill(e).wait()
            return c

        lax.fori_loop(0, fill_ref.shape[0], start, 0)
        lax.fori_loop(0, fill_ref.shape[0], wait, 0)

    slot = i % 2
    sbuf[slot] = x_ref[...]
    base = i * (2 * tm)
    for r in range(tm):
        for k in range(2):
            dst = pl.multiple_of(pos_ref[base + 2 * r + k] * lb, lb)
            pltpu.make_async_copy(sbuf.at[slot, pl.ds(r * lb, lb)], xs_hbm.at[pl.ds(dst, lb)],
                                  rsem.at[slot]).start(priority=k)

    def wait_rows(s):
        for _ in range(2):
            pltpu.make_async_copy(sbuf.at[s], xs_hbm.at[pl.ds(0, tm * lb)], rsem.at[s]).wait()

    @pl.when(i > 0)
    def _():
        wait_rows(1 - slot)

    @pl.when(i == pl.num_programs(0) - 1)
    def _():
        wait_rows(slot)


def _dispatch(pos, fill, h3, n_sorted):
    t_rows = h3.shape[0] // LANE_BLOCKS
    tm = DISPATCH_TILE
    return pl.pallas_call(
        _dispatch_kernel,
        grid_spec=pltpu.PrefetchScalarGridSpec(
            num_scalar_prefetch=2,
            grid=(t_rows // tm,),
            in_specs=[pl.BlockSpec((tm * LANE_BLOCKS, 128), lambda i, pos, fill: (i, 0))],
            out_specs=pl.BlockSpec(memory_space=pl.ANY),
            scratch_shapes=[pltpu.VMEM((MOE_TILE * LANE_BLOCKS, 128), F32),
                            pltpu.VMEM((2, tm * LANE_BLOCKS, 128), F32),
                            pltpu.SemaphoreType.DMA(()),
                            pltpu.SemaphoreType.DMA((2,))]),
        out_shape=jax.ShapeDtypeStruct((n_sorted * LANE_BLOCKS, 128), F32),
        compiler_params=_cparams(1),
        name="moe_dispatch",
    )(pos, fill, h3)


def _moe_kernel(nv_ref, dst_ref, te_ref, x_ref, wg_ref, wu_ref, wd_ref, y_hbm,
                ybuf, xmat, wgb, wub, wdb, ssem):
    tm = MOE_TILE
    lb = LANE_BLOCKS
    i = pl.program_id(0)
    nv = nv_ref[0]
    slot = i % 2
    other = 1 - slot

    def scatter_copy(tile, row, buf_slot):
        dst = y_hbm.at[pl.ds(pl.multiple_of(dst_ref[tile * tm + row] * lb, lb), lb)]
        src_rows = pl.ds(row * lb if isinstance(row, int) else pl.multiple_of(row * lb, lb), lb)
        return pltpu.make_async_copy(ybuf.at[buf_slot, src_rows], dst, ssem.at[buf_slot])

    def wait_scatter(buf_slot):
        pltpu.make_async_copy(ybuf.at[buf_slot], y_hbm.at[pl.ds(0, tm * lb)], ssem.at[buf_slot]).wait()

    @pl.when((i == 0) | (te_ref[i] != te_ref[jnp.maximum(i - 1, 0)]))
    def _():
        wgb[...] = wg_ref[0].astype(BF16)
        wub[...] = wu_ref[0].astype(BF16)
        wdb[...] = wd_ref[0].astype(BF16)

    def compute(buf_slot):
        for c in range(lb):
            xmat[:, c * 128:(c + 1) * 128] = x_ref[pl.ds(c, tm, stride=lb), :]
        xb = xmat[...].astype(BF16)
        hg = _dot(xb, wgb[...])
        hu = _dot(xb, wub[...])
        hh = (hg * _sigmoid(hg)) * hu
        y = _dot(hh.astype(BF16), wdb[...])
        for c in range(lb):
            ybuf[buf_slot, pl.ds(c, tm, stride=lb), :] = y[:, c * 128:(c + 1) * 128]

    @pl.when(i == 0)
    def _():
        ybuf[1] = jnp.zeros((tm * lb, 128), F32)
        fill = pltpu.make_async_copy(ybuf.at[1], y_hbm.at[pl.ds(y_hbm.shape[0] - tm * lb, tm * lb)], ssem.at[1])
        fill.start()
        fill.wait()
        compute(0)

    @pl.when((i > 0) & (i < nv))
    def _():
        @pl.when(i >= 2)
        def _():
            wait_scatter(slot)

        for r in range(tm):
            scatter_copy(i - 1, r, other).start(priority=r % 2)
        compute(slot)

    @pl.when(i == nv)
    def _():
        @pl.when(i >= 2)
        def _():
            wait_scatter(slot)

        def body(r, c):
            scatter_copy(i - 1, r, other).start()
            return c
        lax.fori_loop(0, tm, body, 0)
        wait_scatter(other)


def _moe(nv, dst, te, xs, wg, wu, wd, n_out_rows):
    n_tiles = dst.shape[0] // MOE_TILE
    tm = MOE_TILE
    ex = lambda i, nv, dst, te: (te[i], 0, 0)
    return pl.pallas_call(
        _moe_kernel,
        grid_spec=pltpu.PrefetchScalarGridSpec(
            num_scalar_prefetch=3,
            grid=(n_tiles,),
            in_specs=[pl.BlockSpec((tm * LANE_BLOCKS, 128),
                                   lambda i, nv, dst, te: (jnp.minimum(i, nv[0] - 1), 0)),
                      pl.BlockSpec((1, D_MODEL, MOE_FF), ex),
                      pl.BlockSpec((1, D_MODEL, MOE_FF), ex),
                      pl.BlockSpec((1, MOE_FF, D_MODEL), ex)],
            out_specs=pl.BlockSpec(memory_space=pl.ANY),
            scratch_shapes=[pltpu.VMEM((2, tm * LANE_BLOCKS, 128), F32),
                            pltpu.VMEM((tm, D_MODEL), F32),
                            pltpu.VMEM((D_MODEL, MOE_FF), BF16),
                            pltpu.VMEM((D_MODEL, MOE_FF), BF16),
                            pltpu.VMEM((MOE_FF, D_MODEL), BF16),
                            pltpu.SemaphoreType.DMA((2,))]),
        out_shape=jax.ShapeDtypeStruct((n_out_rows * LANE_BLOCKS, 128), F32),
        compiler_params=_cparams(1),
        name="moe_experts",
    )(nv, dst, te, xs, wg, wu, wd)


MOE_LN_TILE = 512


def _moe_ln_kernel(h_ref, y0_ref, y1_ref, r_ref, g_ref, b_ref, o_ref, acc_ref):
    r = r_ref[...]
    w0 = r[:, 2:3]
    w1 = r[:, 3:4]
    tm = h_ref.shape[0]
    for c in range(LANE_BLOCKS):
        cols = slice(c * 128, (c + 1) * 128)
        rows = pl.ds(c, tm, stride=LANE_BLOCKS)
        acc_ref[:, cols] = DN_ALPHA * h_ref[:, cols] + (w0 * y0_ref[rows, :] + w1 * y1_ref[rows, :])
    o_ref[...] = _layer_norm(acc_ref[...], g_ref[...], b_ref[...])


def _moe_ln_final(h, y2, rinfo, g, b, batch, lp, seq):
    t_rows = h.shape[0]
    tm = MOE_LN_TILE
    lb = LANE_BLOCKS
    nj = seq // tm
    first = lambda bb, j: pl.multiple_of(bb * lp + N_META + j * tm, 8)
    const = lambda bb, j: (0, 0)
    return pl.pallas_call(
        _moe_ln_kernel,
        grid=(batch, nj),
        in_specs=[pl.BlockSpec((pl.Element(tm), pl.Element(D_MODEL)), lambda bb, j: (first(bb, j), 0)),
                  pl.BlockSpec((pl.Element(tm * lb), pl.Element(128)), lambda bb, j: (first(bb, j) * lb, 0)),
                  pl.BlockSpec((pl.Element(tm * lb), pl.Element(128)),
                               lambda bb, j: ((t_rows + first(bb, j)) * lb, 0)),
                  pl.BlockSpec((pl.Element(tm), pl.Element(128)), lambda bb, j: (first(bb, j), 0)),
                  pl.BlockSpec((1, D_MODEL), const),
                  pl.BlockSpec((1, D_MODEL), const)],
        out_specs=pl.BlockSpec((tm, D_MODEL), lambda bb, j: (bb * nj + j, 0)),
        out_shape=jax.ShapeDtypeStruct((batch * seq, D_MODEL), F32),
        scratch_shapes=[pltpu.VMEM((tm, D_MODEL), F32)],
        compiler_params=_cparams(2),
        name="moe_ln_final",
    )(h, y2, y2, rinfo, g, b)


def _moe_ln(h, y2, rinfo, g, b):
    t_rows = h.shape[0]
    tm = MOE_LN_TILE
    nb = t_rows // tm
    const = lambda i: (0, 0)
    return pl.pallas_call(
        _moe_ln_kernel,
        grid=(nb,),
        in_specs=[pl.BlockSpec((tm, D_MODEL), lambda i: (i, 0)),
                  pl.BlockSpec((tm * LANE_BLOCKS, 128), lambda i: (i, 0)),
                  pl.BlockSpec((tm * LANE_BLOCKS, 128), lambda i: (nb + i, 0)),
                  pl.BlockSpec((tm, 128), lambda i: (i, 0)),
                  pl.BlockSpec((1, D_MODEL), const),
                  pl.BlockSpec((1, D_MODEL), const)],
        out_specs=pl.BlockSpec((tm, D_MODEL), lambda i: (i, 0)),
        out_shape=jax.ShapeDtypeStruct((t_rows, D_MODEL), F32),
        scratch_shapes=[pltpu.VMEM((tm, D_MODEL), F32)],
        compiler_params=_cparams(1),
        name="moe_ln",
    )(h, y2, y2, rinfo, g, b)


def _hier_moe_ln(h, rinfo, h3, counts, layer, w_gate, w_up, w_down, ln_g, ln_b, final=None):
    t_rows = h.shape[0]
    n_exp = MOE_GROUPS * MOE_EPG

    n_asg = 2 * t_rows
    tm = MOE_TILE
    e_a = rinfo[:, 0:2].astype(jnp.int32).reshape(n_asg)
    rank_a = rinfo[:, 4:6].astype(jnp.int32).reshape(n_asg)
    asg = jnp.arange(n_asg, dtype=jnp.int32)
    total = counts[0, :n_exp].astype(jnp.int32)
    tiles = (total + tm - 1) // tm
    tile_end = jnp.cumsum(tiles)
    tile_start = tile_end - tiles
    pos = (jnp.take(tile_start * tm, e_a) + rank_a).astype(jnp.int32)
    n_tiles = n_asg // tm + n_exp + 1
    n_sorted = n_tiles * tm
    nv = tile_end[-1:].astype(jnp.int32)
    slot_asg = jnp.full((n_sorted,), -1, jnp.int32).at[pos].set(asg)
    is_pad = slot_asg < 0
    dump = n_asg + jnp.arange(n_sorted, dtype=jnp.int32) % tm
    dst = jnp.where(is_pad, dump, (slot_asg & 1) * t_rows + (slot_asg >> 1))
    te = jnp.minimum(jnp.sum(jnp.arange(n_tiles)[:, None] >= tile_end[None, :], axis=1), n_exp - 1)
    te = (te + layer * n_exp).astype(jnp.int32)
    spare = tile_end[-1] + jnp.arange(n_tiles - n_asg // tm)
    fill = jnp.concatenate([jnp.where(tiles > 0, (tile_end - 1) * tm, -1),
                            jnp.where(spare < n_tiles, spare * tm, -1)]).astype(jnp.int32)

    xs = _dispatch(pos, fill, h3, n_sorted)
    y2 = _moe(nv, dst, te, xs,
              w_gate.astype(F32).reshape(-1, D_MODEL, MOE_FF),
              w_up.astype(F32).reshape(-1, D_MODEL, MOE_FF),
              w_down.astype(F32).reshape(-1, MOE_FF, D_MODEL), n_asg + tm)
    g2 = ln_g.reshape(1, D_MODEL).astype(F32)
    b2 = ln_b.reshape(1, D_MODEL).astype(F32)
    if final is not None:
        return _moe_ln_final(h, y2, rinfo, g2, b2, *final)
    return _moe_ln(h, y2, rinfo, g2, b2)


def _layer_ab(h, batch, lp, w_in, s5, w_glu, b_glu, gla_w_gate, gla_b_gate, gla_norm, w_out, ln_g, ln_b,
              router):
    hk = GLA_HEADS * GLA_DK
    hv = GLA_HEADS * GLA_DV
    o_q = S5_WIDTH
    o_g = o_q + 2 * hk + hv
    o_r = o_g + GLA_RANK
    w_in = w_in.astype(F32)
    w_u = w_in[:, :S5_WIDTH].astype(BF16)
    w_main = jnp.concatenate([w_in[:, o_q:o_g], w_in[:, o_r:o_r + hv]], axis=1).astype(BF16)
    w_low = jnp.pad(w_in[:, o_g:o_r], ((0, 0), (0, 128 - GLA_RANK))).astype(BF16)
    u = _s5_uproj(h, w_u)
    z = _s5_core(u, s5, batch)
    p_s5 = _s5_out(z, w_glu.astype(BF16), b_glu.reshape(1, -1).astype(F32), w_out[:S5_WIDTH].astype(BF16))
    proj, g_low = _in_proj(h, w_main, w_low)
    wg = jnp.pad(gla_w_gate.astype(F32), ((0, 128 - GLA_RANK), (0, 0)))
    wgh, wgl = _split(wg)
    y_gla = _gla(proj, g_low, wgh, wgl, gla_b_gate.reshape(1, -1).astype(F32),
                 gla_norm.reshape(1, -1).astype(F32), batch, lp)
    return _out_ln(h, p_s5, [(y_gla, w_out[S5_WIDTH:].astype(BF16))],
                   ln_g.reshape(1, -1).astype(F32), ln_b.reshape(1, -1).astype(F32), router)


def _layer_cd(h, batch, lp, w_in, conv_w, a_log, dt_bias, gdn_norm, lru_conv_w, lru_conv_b,
              lru_w_a, lru_b_a, lru_w_x, lru_b_x, lru_lambda, w_out, ln_g, ln_b, router):
    nq = GDN_QKV
    hv = GDN_HEADS * GDN_DV
    w_in = w_in.astype(F32)
    o_z = nq
    o_b = o_z + hv
    o_a = o_b + GDN_HEADS
    o_x = o_a + GDN_HEADS
    o_gt = o_x + LRU_WIDTH
    w_main = jnp.concatenate([w_in[:, :o_b], w_in[:, o_x:o_gt + LRU_WIDTH]], axis=1).astype(BF16)
    w_ba = jnp.pad(w_in[:, o_b:o_x], ((0, 0), (0, 128 - 2 * GDN_HEADS))).astype(BF16)
    proj, ba = _in_proj(h, w_main, w_ba)

    cw = jnp.pad(conv_w.astype(F32), ((0, 8 - CONV_K), (0, 0)))
    hp = jnp.zeros((8, 128), F32)
    hp = hp.at[0, GDN_HEADS:2 * GDN_HEADS].set(-jnp.exp(a_log.astype(F32)))
    hp = hp.at[1, GDN_HEADS:2 * GDN_HEADS].set(dt_bias.astype(F32))
    y_gdn = _gdn(proj, ba, cw, hp, gdn_norm.reshape(1, -1).astype(F32), batch, lp)

    lcw = jnp.pad(lru_conv_w.astype(F32), ((0, 8 - CONV_K), (0, 0)))
    wax = jnp.concatenate([lru_w_a.astype(F32), lru_w_x.astype(F32)], axis=2).astype(BF16)
    bax = jnp.concatenate([lru_b_a.astype(F32).reshape(LRU_BLOCKS, 1, LRU_BW),
                           lru_b_x.astype(F32).reshape(LRU_BLOCKS, 1, LRU_BW)], axis=2)
    sp = (-LRU_C * jax.nn.softplus(-lru_lambda.astype(F32))).reshape(1, LRU_WIDTH)
    y_lru = _lru(proj, lcw, lru_conv_b.reshape(1, -1).astype(F32), wax, bax, sp, batch, lp)

    return _out_ln(h, None, [(y_gdn, w_out[:hv].astype(BF16)), (y_lru, w_out[hv:].astype(BF16))],
                   ln_g.reshape(1, -1).astype(F32), ln_b.reshape(1, -1).astype(F32), router)


def kernel(x, meta_tokens, ab_w_in, ab_s5_a_re, ab_s5_a_im, ab_s5_log_dt, ab_s5_b_re, ab_s5_b_im,
           ab_s5_c_re, ab_s5_c_im, ab_s5_d, ab_s5_w_glu, ab_s5_b_glu, ab_gla_w_gate, ab_gla_b_gate,
           ab_gla_norm, ab_w_out, cd_w_in, cd_conv_w, cd_gdn_a_log, cd_gdn_dt_bias, cd_gdn_norm,
           cd_lru_conv_w, cd_lru_conv_b, cd_lru_w_a, cd_lru_b_a, cd_lru_w_x, cd_lru_b_x, cd_lru_lambda,
           cd_w_out, moe_w_router_g, moe_b_router_g, moe_w_router_e, moe_b_router_e, moe_w_gate,
           moe_w_up, moe_w_down, ln_mix_g, ln_mix_b, ln_ffn_g, ln_ffn_b):
    batch, seq, _ = x.shape
    length = seq + N_META
    lp = -(-length // ROW_ALIGN) * ROW_ALIGN
    meta = jnp.broadcast_to(meta_tokens.astype(F32)[None], (batch, N_META, D_MODEL))
    h = jnp.zeros((batch, lp, D_MODEL), F32)
    h = lax.dynamic_update_slice(h, meta, (0, 0, 0))
    h = lax.dynamic_update_slice(h, x.astype(F32), (0, N_META, 0))
    h = h.reshape(batch * lp, D_MODEL)

    for layer in range(DEPTH):
        j = layer // 2
        router = _router_weights(moe_w_router_g[layer], moe_b_router_g[layer], moe_w_router_e[layer],
                                 moe_b_router_e[layer])
        if layer % 2 == 0:
            s5 = _s5_params(ab_s5_a_re[j], ab_s5_a_im[j], ab_s5_log_dt[j], ab_s5_b_re[j], ab_s5_b_im[j],
                            ab_s5_c_re[j], ab_s5_c_im[j], ab_s5_d[j])
            h, rinfo, h3, counts = _layer_ab(h, batch, lp, ab_w_in[j], s5, ab_s5_w_glu[j], ab_s5_b_glu[j],
                                 ab_gla_w_gate[j], ab_gla_b_gate[j], ab_gla_norm[j], ab_w_out[j],
                                 ln_mix_g[layer], ln_mix_b[layer], router)
        else:
            h, rinfo, h3, counts = _layer_cd(h, batch, lp, cd_w_in[j], cd_conv_w[j], cd_gdn_a_log[j], cd_gdn_dt_bias[j],
                                 cd_gdn_norm[j], cd_lru_conv_w[j], cd_lru_conv_b[j], cd_lru_w_a[j],
                                 cd_lru_b_a[j], cd_lru_w_x[j], cd_lru_b_x[j], cd_lru_lambda[j], cd_w_out[j],
                                 ln_mix_g[layer], ln_mix_b[layer], router)
        last = layer == DEPTH - 1 and seq % MOE_LN_TILE == 0
        h = _hier_moe_ln(h, rinfo, h3, counts, layer, moe_w_gate, moe_w_up, moe_w_down,
                         ln_ffn_g[layer], ln_ffn_b[layer], final=(batch, lp, seq) if last else None)
        if last:
            return h.reshape(batch, seq, D_MODEL).astype(x.dtype)
    return h.reshape(batch, lp, D_MODEL)[:, N_META:length].astype(x.dtype)
```

```python
import math

import jax
import jax.numpy as jnp
from jax import lax
from jax.experimental import pallas as pl
from jax.experimental.pallas import tpu as pltpu

F32 = jnp.float32
BF16 = jnp.bfloat16
HIGHEST = lax.Precision.HIGHEST

D_MODEL = 2048
N_META = 16
CONV_K = 4
DEPTH = 2
DN_ALPHA = (2.0 * DEPTH) ** 0.25
LN_EPS = 1e-5
NORM_EPS = 1e-6

S5_WIDTH = 1024
S5_GROUP = 16
S5_GROUPS = 64
S5_STATE = 64
S5_LC = 16
S5_SLABS = 8

GLA_HEADS = 4
GLA_DK = 128
GLA_DV = 256
GLA_RANK = 16
GLA_TAU = 16.0

GDN_HEADS = 8
GDN_DK = 128
GDN_DV = 128

LRU_WIDTH = 1024
LRU_BLOCKS = 8
LRU_BW = 128
LRU_C = 8.0

MOE_GROUPS = 4
MOE_EPG = 8
MOE_FF = 256

LANE_BLOCKS = D_MODEL // 128
CHUNK = 64
ROW_ALIGN = 768
VMEM_LIMIT = 56 * 1024 * 1024

NT_DIMS = (((1,), (1,)), ((), ()))
TN_DIMS = (((0,), (0,)), ((), ()))


def _cparams(n_axes):
    return pltpu.CompilerParams(dimension_semantics=("arbitrary",) * n_axes,
                                vmem_limit_bytes=VMEM_LIMIT)


def _dot(a, b):
    return jnp.dot(a, b, preferred_element_type=F32)


def _dotg(a, b, dims):
    return lax.dot_general(a, b, dims, preferred_element_type=F32)


def _split(a):
    hi = a.astype(BF16)
    lo = (a - hi.astype(F32)).astype(BF16)
    return hi, lo


def _sigmoid(x):
    return 1.0 / (1.0 + jnp.exp(-x))


def _softplus(x):
    return jnp.maximum(x, 0.0) + jnp.log1p(jnp.exp(-jnp.abs(x)))


def _gelu(x):
    return 0.5 * x * (1.0 + jnp.tanh(math.sqrt(2.0 / math.pi) * (x + 0.044715 * (x * x * x))))


def _layer_norm(x, g, b):
    mu = jnp.mean(x, axis=-1, keepdims=True)
    xc = x - mu
    var = jnp.mean(xc * xc, axis=-1, keepdims=True)
    return xc * lax.rsqrt(var + LN_EPS) * g + b


PROJ_TILE = 512
PROJ_COLS = 1024


def _in_proj_kernel(x_ref, w_ref, wg_ref, o_ref, g_ref):
    xb = x_ref[...].astype(BF16)
    for j in range(w_ref.shape[1] // PROJ_COLS):
        cols = slice(j * PROJ_COLS, (j + 1) * PROJ_COLS)
        o_ref[:, cols] = _dot(xb, w_ref[:, cols]).astype(o_ref.dtype)
    g_ref[...] = _dot(xb, wg_ref[...])


def _in_proj(x, w, w_gates):
    m, k = x.shape
    n = w.shape[1]
    tm = PROJ_TILE
    const = lambda i: (0, 0)
    return pl.pallas_call(
        _in_proj_kernel,
        grid=(m // tm,),
        in_specs=[pl.BlockSpec((tm, k), lambda i: (i, 0)),
                  pl.BlockSpec((k, n), const, pipeline_mode=pl.Buffered(1)),
                  pl.BlockSpec((k, 128), const, pipeline_mode=pl.Buffered(1))],
        out_specs=[pl.BlockSpec((tm, n), lambda i: (i, 0)),
                   pl.BlockSpec((tm, 128), lambda i: (i, 0))],
        out_shape=[jax.ShapeDtypeStruct((m, n), BF16),
                   jax.ShapeDtypeStruct((m, 128), F32)],
        compiler_params=_cparams(1),
        name="in_proj",
    )(x, w, w_gates)


S5_TILE_CHUNKS = 32


def _s5_row_perm():
    m = S5_TILE_CHUNKS
    r = jnp.arange(S5_LC * m)
    nat = (r % m) * S5_LC + r // m
    return (nat[:, None] == jnp.arange(S5_LC * m)[None, :]).astype(BF16)


def _s5_uproj_kernel(h_ref, perm_ref, w_ref, u_ref):
    m = S5_TILE_CHUNKS
    lhs = _dot(perm_ref[...], h_ref[...].astype(BF16)).astype(BF16)
    z = _dot(lhs, w_ref[...])
    for t in range(S5_LC):
        for s in range(S5_SLABS):
            u_ref[s, :, t * 128:(t + 1) * 128] = z[t * m:(t + 1) * m, s * 128:(s + 1) * 128]


def _s5_uproj(h, w_u):
    t_rows = h.shape[0]
    n_chunks = t_rows // S5_LC
    m = S5_TILE_CHUNKS
    return pl.pallas_call(
        _s5_uproj_kernel,
        grid=(n_chunks // m,),
        in_specs=[pl.BlockSpec((m * S5_LC, D_MODEL), lambda i: (i, 0)),
                  pl.BlockSpec((m * S5_LC, m * S5_LC), lambda i: (0, 0)),
                  pl.BlockSpec((D_MODEL, S5_WIDTH), lambda i: (0, 0))],
        out_specs=pl.BlockSpec((S5_SLABS, m, S5_LC * 128), lambda i: (0, i, 0)),
        out_shape=jax.ShapeDtypeStruct((S5_SLABS, n_chunks, S5_LC * 128), F32),
        compiler_params=_cparams(1),
        name="s5_uproj",
    )(h, _s5_row_perm(), w_u)


def _s5_core_kernel(u_ref, stack_ref, ms_ref, mot_ref, ak_ref, pre_ref, pim_ref, d_ref, z_ref,
                    sre_ref, sim_ref, xre_ref, xim_ref):
    n = u_ref.shape[1]
    half = 8 * S5_STATE
    u = u_ref[0]
    ub = u.astype(BF16)
    s = _dot(ub, ms_ref[0])
    sre_ref[...] = s[:, :half]
    sim_ref[...] = s[:, half:]

    ak = ak_ref[0]
    pre = pre_ref[0]
    pim = pim_ref[0]
    row = lax.broadcasted_iota(jnp.int32, (8, half), 0)

    def body(rb, carry):
        cre, cim = carry
        r0 = pl.multiple_of(rb * 8, 8)
        xr = sre_ref[pl.ds(r0, 8), :]
        xi = sim_ref[pl.ds(r0, 8), :]
        for idx, k in enumerate((1, 2, 4)):
            are = ak[2 * idx:2 * idx + 1, :]
            aim = ak[2 * idx + 1:2 * idx + 2, :]
            keep = row >= k
            shr = jnp.where(keep, pltpu.roll(xr, k, 0), 0.0)
            shi = jnp.where(keep, pltpu.roll(xi, k, 0), 0.0)
            xr, xi = xr + are * shr - aim * shi, xi + are * shi + aim * shr
        xr = xr + pre * cre - pim * cim
        xi = xi + pre * cim + pim * cre
        first = row == 0
        xre_ref[pl.ds(r0, 8), :] = jnp.where(first, cre, pltpu.roll(xr, 1, 0))
        xim_ref[pl.ds(r0, 8), :] = jnp.where(first, cim, pltpu.roll(xi, 1, 0))
        return xr[7:8, :], xi[7:8, :]

    zero = jnp.zeros((1, half), F32)
    lax.fori_loop(0, n // 8, body, (zero, zero))

    xprev = jnp.concatenate([xre_ref[...], xim_ref[...]], axis=1).astype(BF16)
    y_inter = _dotg(xprev, mot_ref[0], NT_DIMS)
    d = d_ref[0]
    npair = S5_LC // 2
    for q in range(npair):
        yq = _dot(ub[:, :(q + 1) * 256], stack_ref[0, (npair - 1 - q) * 256:, :])
        sl = slice(q * 256, (q + 1) * 256)
        y = yq + y_inter[:, sl] + d[:, sl] * u[:, sl]
        z_ref[0, :, sl] = _gelu(y)


def _s5_core(u, prm, batch):
    n_tot = u.shape[1]
    n = n_tot // batch
    w = S5_LC * 128
    half = 8 * S5_STATE
    slab = lambda s, b: (s, 0, 0)
    return pl.pallas_call(
        _s5_core_kernel,
        grid=(S5_SLABS, batch),
        in_specs=[pl.BlockSpec((1, n, w), lambda s, b: (s, b, 0)),
                  pl.BlockSpec((1, w, 256), slab),
                  pl.BlockSpec((1, w, 2 * half), slab),
                  pl.BlockSpec((1, w, 2 * half), slab),
                  pl.BlockSpec((1, 8, half), slab),
                  pl.BlockSpec((1, 8, half), slab),
                  pl.BlockSpec((1, 8, half), slab),
                  pl.BlockSpec((1, 1, w), slab)],
        out_specs=pl.BlockSpec((1, n, w), lambda s, b: (s, b, 0)),
        out_shape=jax.ShapeDtypeStruct(u.shape, F32),
        scratch_shapes=[pltpu.VMEM((n, half), F32)] * 4,
        compiler_params=_cparams(2),
        name="s5_core",
    )(u, prm["stack"], prm["ms"], prm["mot"], prm["ak"], prm["pre"], prm["pim"], prm["d"])


def _s5_out_kernel(z_ref, perm_ref, wglu_ref, bglu_ref, wout_ref, p_ref, zf_ref):
    m = S5_TILE_CHUNKS
    for t in range(S5_LC):
        for s in range(S5_SLABS):
            zf_ref[t * m:(t + 1) * m, s * 128:(s + 1) * 128] = z_ref[s, :, t * 128:(t + 1) * 128]
    zf = zf_ref[...]
    gate = _sigmoid(_dot(zf.astype(BF16), wglu_ref[...]) + bglu_ref[...])
    gated = (zf * gate).astype(BF16)
    nat = _dotg(perm_ref[...], gated, TN_DIMS).astype(BF16)
    p_ref[...] = _dot(nat, wout_ref[...])


def _s5_out(z, w_glu, b_glu, w_out_top):
    n_chunks = z.shape[1]
    m = S5_TILE_CHUNKS
    return pl.pallas_call(
        _s5_out_kernel,
        grid=(n_chunks // m,),
        in_specs=[pl.BlockSpec((S5_SLABS, m, S5_LC * 128), lambda i: (0, i, 0)),
                  pl.BlockSpec((m * S5_LC, m * S5_LC), lambda i: (0, 0)),
                  pl.BlockSpec((S5_WIDTH, S5_WIDTH), lambda i: (0, 0)),
                  pl.BlockSpec((1, S5_WIDTH), lambda i: (0, 0)),
                  pl.BlockSpec((S5_WIDTH, D_MODEL), lambda i: (0, 0))],
        out_specs=pl.BlockSpec((m * S5_LC, D_MODEL), lambda i: (i, 0)),
        out_shape=jax.ShapeDtypeStruct((n_chunks * S5_LC, D_MODEL), F32),
        scratch_shapes=[pltpu.VMEM((S5_LC * m, S5_WIDTH), F32)],
        compiler_params=_cparams(1),
        name="s5_out",
    )(z, _s5_row_perm(), w_glu, b_glu, w_out_top)


def _s5_params(a_re, a_im, log_dt, b_re, b_im, c_re, c_im, d):
    g, p, c, lc = S5_GROUPS, S5_STATE, S5_GROUP, S5_LC
    a = lax.complex(a_re.astype(F32), a_im.astype(F32))
    adt = a * jnp.exp(log_dt.astype(F32))[:, None]
    zoh = (jnp.exp(adt) - 1.0) / a
    bz = lax.complex(b_re.astype(F32), b_im.astype(F32)) * zoh[:, :, None]
    cc = lax.complex(c_re.astype(F32), c_im.astype(F32))
    jj = jnp.arange(lc + 1, dtype=F32)
    apow = jnp.exp(adt[None] * jj[:, None, None])

    def expand(compact, width):
        rows = compact.shape[-2]
        row_grp = (jnp.arange(rows) % 128) // S5_GROUP
        lane_grp = jnp.arange(8 * width) // width
        tiled = jnp.tile(compact.astype(BF16), (1,) * (compact.ndim - 1) + (8,))
        return jnp.where(row_grp[:, None] == lane_grp[None, :], tiled, jnp.zeros((), BF16))

    kre = jnp.einsum("gop,jgp,gpc->jgco", cc, apow[:lc], bz, precision=HIGHEST).real
    kblk = expand(kre.reshape(lc, S5_SLABS, 128, c), c)
    kpad = jnp.concatenate([jnp.zeros_like(kblk[:1]), kblk], axis=0)
    npair = lc // 2
    dd = jnp.arange(npair)[:, None, None]
    lo_in = jnp.arange(2)[None, :, None]
    lo_out = jnp.arange(2)[None, None, :]
    lag = 2 * dd + lo_out - lo_in
    tiles = kpad[lag + 1]
    tiles = tiles.transpose(3, 0, 1, 4, 2, 5).reshape(S5_SLABS, npair, 256, 256)
    stack = tiles[:, ::-1].reshape(S5_SLABS, npair * 256, 256)

    wst = apow[lc - 1 - jnp.arange(lc)][:, :, :, None] * bz[None]
    wst = wst.reshape(lc, S5_SLABS, 8, p, c).transpose(1, 0, 2, 4, 3).reshape(S5_SLABS, lc * 128, p)
    ms = jnp.concatenate([expand(wst.real, p), expand(wst.imag, p)], axis=2)

    zc = cc[None] * apow[1:lc + 1][:, :, None, :]
    zc = zc.reshape(lc, S5_SLABS, 8 * c, p).transpose(1, 0, 2, 3).reshape(S5_SLABS, lc * 128, p)
    mot = jnp.concatenate([expand(zc.real, p), expand(-zc.imag, p)], axis=2)

    def slab_layout(x):
        return x.reshape(x.shape[:-2] + (S5_SLABS, 8 * p))

    kk = jnp.array([1.0, 2.0, 4.0], F32)
    a_k = slab_layout(jnp.exp(adt[None] * (lc * kk)[:, None, None]))
    ak = jnp.stack([a_k[0].real, a_k[0].imag, a_k[1].real, a_k[1].imag, a_k[2].real, a_k[2].imag,
                    jnp.zeros_like(a_k[0].real), jnp.zeros_like(a_k[0].real)], axis=1)
    rr = jnp.arange(1, 9, dtype=F32)
    a_r = slab_layout(jnp.exp(adt[None] * (lc * rr)[:, None, None]))
    pre = a_r.real.transpose(1, 0, 2)
    pim = a_r.imag.transpose(1, 0, 2)
    dsk = jnp.tile(d.astype(F32).reshape(S5_SLABS, 1, 128), (1, 1, lc))
    return {"stack": stack.astype(BF16), "ms": ms.astype(BF16), "mot": mot.astype(BF16),
            "ak": ak, "pre": pre, "pim": pim, "d": dsk}


GLA_TILE = 256


def _gla_kernel(q_ref, k_ref, v_ref, r_ref, gl_ref, wgh_ref, wgl_ref, bg_ref, ng_ref, o_ref, st_ref):
    @pl.when(pl.program_id(1) == 0)
    def _():
        st_ref[...] = jnp.zeros_like(st_ref)

    rowi = lax.broadcasted_iota(jnp.int32, (CHUNK, CHUNK), 0)
    coli = lax.broadcasted_iota(jnp.int32, (CHUNK, CHUNK), 1)
    tri = rowi >= coli
    wgh = wgh_ref[...]
    wgl = wgl_ref[...]
    bg = bg_ref[...]
    ng = ng_ref[...]
    scale = GLA_DK ** -0.5

    nch = GLA_TILE // CHUNK
    ti = lax.broadcasted_iota(jnp.int32, (GLA_TILE, GLA_TILE), 0)
    tj = lax.broadcasted_iota(jnp.int32, (GLA_TILE, GLA_TILE), 1)
    tri_t = jnp.where((ti >= tj) & ((ti // CHUNK) == (tj // CHUNK)), 1.0, 0.0).astype(BF16)
    glh, gll = _split(gl_ref[...])
    pre = _dot(glh, wgh) + _dot(gll, wgh) + _dot(glh, wgl) + bg
    logf = (jnp.minimum(pre, 0.0) - jnp.log1p(jnp.exp(-jnp.abs(pre)))) * (1.0 / GLA_TAU)
    lh, ll = _split(logf)
    bcum = _dot(tri_t, lh) + _dot(tri_t, ll)
    ebc = jnp.exp(bcum)
    kk = k_ref[...].astype(F32)
    q_in = (q_ref[...].astype(F32) * scale * ebc).astype(BF16)
    k_in = (kk * jnp.exp(-bcum)).astype(BF16)

    heads = range(GLA_HEADS)
    chunks = range(nch)
    items = [(c, h) for c in chunks for h in heads]
    rws = [slice(c * CHUNK, (c + 1) * CHUNK) for c in chunks]
    sks = [slice(h * GLA_DK, (h + 1) * GLA_DK) for h in heads]
    svs = [slice(h * GLA_DV, (h + 1) * GLA_DV) for h in heads]
    blast = [bcum[(c + 1) * CHUNK - 1:(c + 1) * CHUNK, :] for c in chunks]
    k_st = [(kk[rws[c], :] * jnp.exp(blast[c] - bcum[rws[c], :])).astype(BF16) for c in chunks]
    dec = [jnp.exp(blast[c]) for c in chunks]
    vh = {(c, h): v_ref[rws[c], svs[h]].astype(BF16) for c, h in items}
    att = {(c, h): _dotg(q_in[rws[c], sks[h]], k_in[rws[c], sks[h]], NT_DIMS) for c, h in items}
    kv = {(c, h): _dotg(vh[c, h], k_st[c][:, sks[h]], TN_DIMS) for c, h in items}
    o_intra = {it: _dot(jnp.where(tri, att[it], 0.0).astype(BF16), vh[it]) for it in items}
    st = [st_ref[h] for h in heads]
    for c in chunks:
        o_inter = [_dotg(q_in[rws[c], sks[h]], st[h].astype(BF16), NT_DIMS) for h in heads]
        for h in heads:
            o = o_intra[c, h] + o_inter[h]
            on = o * lax.rsqrt(jnp.mean(o * o, axis=1, keepdims=True) + NORM_EPS) * ng
            rr = r_ref[rws[c], svs[h]].astype(F32)
            o_ref[rws[c], svs[h]] = (on * (rr * _sigmoid(rr))).astype(o_ref.dtype)
        st = [st[h] * dec[c][:, sks[h]] + kv[c, h] for h in heads]
    for h in heads:
        st_ref[h] = st[h]


def _gla(proj, g_low, wgh, wgl, bg, ng, batch, lp):
    t_rows = proj.shape[0]
    nt = lp // GLA_TILE
    hk = GLA_HEADS * GLA_DK
    hv = GLA_HEADS * GLA_DV
    rowmap = lambda col: (lambda b, t: (b * nt + t, col))
    const = lambda b, t: (0, 0)
    return pl.pallas_call(
        _gla_kernel,
        grid=(batch, nt),
        in_specs=[pl.BlockSpec((GLA_TILE, hk), rowmap(0)),
                  pl.BlockSpec((GLA_TILE, hk), rowmap(1)),
                  pl.BlockSpec((GLA_TILE, hv), rowmap(1)),
                  pl.BlockSpec((GLA_TILE, hv), rowmap(2)),
                  pl.BlockSpec((GLA_TILE, 128), rowmap(0)),
                  pl.BlockSpec((128, hk), const),
                  pl.BlockSpec((128, hk), const),
                  pl.BlockSpec((1, hk), const),
                  pl.BlockSpec((1, GLA_DV), const)],
        out_specs=pl.BlockSpec((GLA_TILE, hv), rowmap(0)),
        out_shape=jax.ShapeDtypeStruct((t_rows, hv), BF16),
        scratch_shapes=[pltpu.VMEM((GLA_HEADS, GLA_DV, GLA_DK), F32)],
        compiler_params=_cparams(2),
        name="gla",
    )(proj, proj, proj, proj, g_low, wgh, wgl, bg, ng)


GDN_TILE = 256
GDN_GROUP = 4
GDN_QKV = 3 * GDN_HEADS * GDN_DK


def _gdn_kernel(qkv_ref, z_ref, ba_ref, cw_ref, hp_ref, ng_ref, o_ref,
                ext_ref, act_ref, beta_ref, g_ref, s_ref):
    tile = GDN_TILE

    @pl.when(pl.program_id(1) == 0)
    def _():
        ext_ref[0:8, :] = jnp.zeros((8, GDN_QKV), F32)
        s_ref[...] = jnp.zeros_like(s_ref)

    @pl.when(pl.program_id(1) > 0)
    def _():
        ext_ref[0:8, :] = ext_ref[tile:tile + 8, :]

    ext_ref[8:tile + 8, :] = qkv_ref[...].astype(F32)
    qscale = GDN_DK ** -0.5
    for cb in range(GDN_QKV // 128):
        cols = slice(cb * 128, (cb + 1) * 128)
        w = cw_ref[:, cols]
        y = (w[3:4] * ext_ref[8:tile + 8, cols] + w[2:3] * ext_ref[7:tile + 7, cols]
             + w[1:2] * ext_ref[6:tile + 6, cols] + w[0:1] * ext_ref[5:tile + 5, cols])
        a = y * _sigmoid(y)
        if cb < 2 * GDN_HEADS:
            a = a * lax.rsqrt(jnp.sum(a * a, axis=1, keepdims=True) + NORM_EPS)
            if cb < GDN_HEADS:
                a = a * qscale
        act_ref[:, cols] = a

    ba = ba_ref[...]
    beta_ref[...] = _sigmoid(ba)
    g_ref[...] = hp_ref[0:1, :] * _softplus(ba + hp_ref[1:2, :])

    rowi = lax.broadcasted_iota(jnp.int32, (CHUNK, CHUNK), 0)
    coli = lax.broadcasted_iota(jnp.int32, (CHUNK, CHUNK), 1)
    tri = rowi >= coli
    strict = rowi > coli
    tri_b = jnp.where(tri, 1.0, 0.0).astype(BF16)
    upper_b = jnp.where(rowi <= coli, 1.0, 0.0).astype(BF16)
    eye = jnp.where(rowi == coli, 1.0, 0.0)
    level_masks = []
    for lvl in range(6):
        bi = rowi >> lvl
        bj = coli >> lvl
        level_masks.append(((bi & 1) == 1) & (bj == bi - 1))
    ng = ng_ref[...]

    heads = range(GDN_HEADS)
    grp = range(GDN_GROUP)
    items = [(j, h) for j in grp for h in heads]
    hcol = lambda t, h: t[:, GDN_HEADS + h:GDN_HEADS + h + 1]

    def group(gi, carry):
        base = gi * (GDN_GROUP * CHUNK)
        rows = [pl.ds(pl.multiple_of(base + j * CHUNK, CHUNK), CHUNK) for j in grp]
        gsp = [_split(g_ref[rows[j], :]) for j in grp]
        gc = [_dot(tri_b, gsp[j][0]) + _dot(tri_b, gsp[j][1]) for j in grp]
        gct = [_dotg(gsp[j][0], upper_b, TN_DIMS) + _dotg(gsp[j][1], upper_b, TN_DIMS) for j in grp]
        eg = [jnp.exp(gc[j]) for j in grp]
        elast = [jnp.exp(gc[j][CHUNK - 1:CHUNK, :] - gc[j]) for j in grp]
        dlast = [jnp.exp(gc[j][CHUNK - 1:CHUNK, :]) for j in grp]
        beta = [beta_ref[rows[j], :] for j in grp]
        qs = {(j, h): act_ref[rows[j], h * GDN_DK:(h + 1) * GDN_DK] for j, h in items}
        ks = {(j, h): act_ref[rows[j], (GDN_HEADS + h) * GDN_DK:(GDN_HEADS + h + 1) * GDN_DK] for j, h in items}
        vs = {(j, h): act_ref[rows[j], (2 * GDN_HEADS + h) * GDN_DK:(2 * GDN_HEADS + h + 1) * GDN_DK]
              for j, h in items}
        bcols = {(j, h): beta[j][:, h:h + 1] for j, h in items}
        kbs = {it: ks[it] * bcols[it] for it in items}
        khb = {it: ks[it].astype(BF16) for it in items}
        gams = {(j, h): jnp.exp(jnp.where(tri, hcol(gc[j], h) - gct[j][GDN_HEADS + h:GDN_HEADS + h + 1, :], -1e30))
                for j, h in items}
        kk = {it: _dotg(kbs[it].astype(BF16), khb[it], NT_DIMS) for it in items}
        qk = {it: _dotg(qs[it].astype(BF16), khb[it], NT_DIMS) for it in items}
        mm = {it: jnp.where(strict, kk[it] * gams[it], 0.0) for it in items}
        tinv = {it: eye - jnp.where(level_masks[0], mm[it], 0.0) for it in items}
        for lvl in range(1, 6):
            tb = {it: tinv[it].astype(BF16) for it in items}
            xs = {it: _dot(jnp.where(level_masks[lvl], mm[it], 0.0).astype(BF16), tb[it]).astype(BF16)
                  for it in items}
            tinv = {it: tinv[it] - _dot(tb[it], xs[it]) for it in items}
        rhs = {(j, h): jnp.concatenate([vs[j, h] * bcols[j, h], kbs[j, h] * hcol(eg[j], h)], axis=1).astype(BF16)
               for j, h in items}
        sol = {it: _dot(tinv[it].astype(BF16), rhs[it]) for it in items}
        aqk = {it: jnp.where(tri, qk[it] * gams[it], 0.0).astype(BF16) for it in items}
        qdec = {(j, h): (qs[j, h] * hcol(eg[j], h)).astype(BF16) for j, h in items}
        kst = {(j, h): (ks[j, h] * hcol(elast[j], h)).astype(BF16) for j, h in items}
        for j in grp:
            s_old = [s_ref[h] for h in heads]
            sb = [s_old[h].astype(BF16) for h in heads]
            ws = [_dot(sol[j, h][:, GDN_DV:].astype(BF16), sb[h]) for h in heads]
            qsd = [_dot(qdec[j, h], sb[h]) for h in heads]
            vnb = [(sol[j, h][:, :GDN_DV] - ws[h]).astype(BF16) for h in heads]
            av = [_dot(aqk[j, h], vnb[h]) for h in heads]
            kv = [_dotg(kst[j, h], vnb[h], TN_DIMS) for h in heads]
            for h in heads:
                o = qsd[h] + av[h]
                on = o * lax.rsqrt(jnp.mean(o * o, axis=1, keepdims=True) + NORM_EPS) * ng
                zz = z_ref[rows[j], h * GDN_DV:(h + 1) * GDN_DV].astype(F32)
                o_ref[rows[j], h * GDN_DV:(h + 1) * GDN_DV] = (on * (zz * _sigmoid(zz))).astype(o_ref.dtype)
                s_ref[h] = s_old[h] * hcol(dlast[j], h) + kv[h]
        return carry

    lax.fori_loop(0, tile // (GDN_GROUP * CHUNK), group, 0)


def _gdn(proj, ba, cw, hp, ng, batch, lp):
    t_rows = proj.shape[0]
    nt = lp // GDN_TILE
    hv = GDN_HEADS * GDN_DV
    rowmap = lambda col: (lambda b, t: (b * nt + t, col))
    const = lambda b, t: (0, 0)
    return pl.pallas_call(
        _gdn_kernel,
        grid=(batch, nt),
        in_specs=[pl.BlockSpec((GDN_TILE, GDN_QKV), rowmap(0)),
                  pl.BlockSpec((GDN_TILE, hv), rowmap(3)),
                  pl.BlockSpec((GDN_TILE, 128), rowmap(0)),
                  pl.BlockSpec((8, GDN_QKV), const),
                  pl.BlockSpec((8, 128), const),
                  pl.BlockSpec((1, GDN_DV), const)],
        out_specs=pl.BlockSpec((GDN_TILE, hv), rowmap(0)),
        out_shape=jax.ShapeDtypeStruct((t_rows, hv), BF16),
        scratch_shapes=[pltpu.VMEM((GDN_TILE + 8, GDN_QKV), F32),
                        pltpu.VMEM((GDN_TILE, GDN_QKV), F32),
                        pltpu.VMEM((GDN_TILE, 128), F32),
                        pltpu.VMEM((GDN_TILE, 128), F32),
                        pltpu.VMEM((GDN_HEADS, GDN_DK, GDN_DV), F32)],
        compiler_params=_cparams(2),
        name="gdn",
    )(proj, proj, ba, cw, hp, ng)


LRU_TILE = 256


def _lru_kernel(x_ref, gate_ref, cw_ref, cb_ref, wax_ref, bax_ref, sp_ref, o_ref,
                ext_ref, a_ref, b_ref, h_ref):
    tile = LRU_TILE

    @pl.when(pl.program_id(1) == 0)
    def _():
        ext_ref[0:8, :] = jnp.zeros((8, LRU_WIDTH), F32)
        h_ref[...] = jnp.zeros_like(h_ref)

    @pl.when(pl.program_id(1) > 0)
    def _():
        ext_ref[0:8, :] = ext_ref[tile:tile + 8, :]

    ext_ref[8:tile + 8, :] = x_ref[...].astype(F32)
    for blk in range(LRU_BLOCKS):
        cols = slice(blk * LRU_BW, (blk + 1) * LRU_BW)
        w = cw_ref[:, cols]
        xc = (w[3:4] * ext_ref[8:tile + 8, cols] + w[2:3] * ext_ref[7:tile + 7, cols]
              + w[1:2] * ext_ref[6:tile + 6, cols] + w[0:1] * ext_ref[5:tile + 5, cols]
              + cb_ref[:, cols])
        pre = _dot(xc.astype(BF16), wax_ref[blk]) + bax_ref[blk]
        r = _sigmoid(pre[:, :LRU_BW])
        i = _sigmoid(pre[:, LRU_BW:])
        log_a = sp_ref[:, cols] * r
        a_ref[:, cols] = jnp.exp(log_a)
        th = jnp.tanh(log_a)
        b_ref[:, cols] = jnp.sqrt(-2.0 * th / (1.0 - th)) * (i * xc)

    row = lax.broadcasted_iota(jnp.int32, (8, LRU_WIDTH), 0)

    def body(rb, carry):
        r0 = pl.multiple_of(rb * 8, 8)
        aa = a_ref[pl.ds(r0, 8), :]
        bb = b_ref[pl.ds(r0, 8), :]
        for k in (1, 2, 4):
            keep = row >= k
            a_sh = jnp.where(keep, pltpu.roll(aa, k, 0), 1.0)
            b_sh = jnp.where(keep, pltpu.roll(bb, k, 0), 0.0)
            bb = aa * b_sh + bb
            aa = aa * a_sh
        hs = bb + aa * carry
        gt = gate_ref[pl.ds(r0, 8), :].astype(F32)
        o_ref[pl.ds(r0, 8), :] = (hs * _gelu(gt)).astype(o_ref.dtype)
        return hs[7:8, :]

    h_ref[...] = lax.fori_loop(0, tile // 8, body, h_ref[...])


def _lru(proj, cw, cb, wax, bax, sp, batch, lp):
    t_rows = proj.shape[0]
    nt = lp // LRU_TILE
    rowmap = lambda col: (lambda b, t: (b * nt + t, col))
    const = lambda b, t: (0, 0)
    return pl.pallas_call(
        _lru_kernel,
        grid=(batch, nt),
        in_specs=[pl.BlockSpec((LRU_TILE, LRU_WIDTH), rowmap(4)),
                  pl.BlockSpec((LRU_TILE, LRU_WIDTH), rowmap(5)),
                  pl.BlockSpec((8, LRU_WIDTH), const),
                  pl.BlockSpec((1, LRU_WIDTH), const),
                  pl.BlockSpec((LRU_BLOCKS, LRU_BW, 2 * LRU_BW), lambda b, t: (0, 0, 0)),
                  pl.BlockSpec((LRU_BLOCKS, 1, 2 * LRU_BW), lambda b, t: (0, 0, 0)),
                  pl.BlockSpec((1, LRU_WIDTH), const)],
        out_specs=pl.BlockSpec((LRU_TILE, LRU_WIDTH), rowmap(0)),
        out_shape=jax.ShapeDtypeStruct((t_rows, LRU_WIDTH), BF16),
        scratch_shapes=[pltpu.VMEM((LRU_TILE + 8, LRU_WIDTH), F32),
                        pltpu.VMEM((LRU_TILE, LRU_WIDTH), F32),
                        pltpu.VMEM((LRU_TILE, LRU_WIDTH), F32),
                        pltpu.VMEM((1, LRU_WIDTH), F32)],
        compiler_params=_cparams(2),
        name="lru",
    )(proj, proj, cw, cb, wax, bax, sp)


OUT_TILE = 512


def _out_ln_kernel(n_add, h_ref, *refs):
    adds = refs[:n_add]
    (a1_ref, w1_ref, a2_ref, w2_ref, g_ref, b_ref, wrh_ref, wrl_ref, br_ref,
     o_ref, r_ref, cnt_ref, run_ref) = refs[n_add:]

    @pl.when(pl.program_id(0) == 0)
    def _():
        run_ref[...] = jnp.zeros_like(run_ref)

    acc = DN_ALPHA * h_ref[...]
    for r in adds:
        acc = acc + r[...]
    acc = acc + _dot(a1_ref[...], w1_ref[...])
    if a2_ref is not None:
        acc = acc + _dot(a2_ref[...], w2_ref[...])
    hn = _layer_norm(acc, g_ref[...], b_ref[...])
    o_ref[...] = hn
    tm = hn.shape[0]
    r = _route(hn, wrh_ref[...], wrl_ref[...], br_ref[...])
    lane = lax.broadcasted_iota(jnp.int32, r.shape, 1).astype(F32)
    oh1 = lane == r[:, 0:1]
    oh2 = lane == r[:, 1:2]
    picked = jnp.where(oh1 | oh2, 1.0, 0.0)
    ri = lax.broadcasted_iota(jnp.int32, (tm, tm), 0)
    ci = lax.broadcasted_iota(jnp.int32, (tm, tm), 1)
    before = _dot(jnp.where(ri > ci, 1.0, 0.0).astype(BF16), picked.astype(BF16)) + run_ref[0:1, :]
    rank1 = jnp.sum(jnp.where(oh1, before, 0.0), axis=1, keepdims=True)
    rank2 = jnp.sum(jnp.where(oh2, before, 0.0), axis=1, keepdims=True)
    r_ref[...] = r + jnp.where(lane == 4.0, rank1, 0.0) + jnp.where(lane == 5.0, rank2, 0.0)
    total = run_ref[0:1, :] + jnp.sum(picked, axis=0, keepdims=True)
    run_ref[...] = jnp.broadcast_to(total, run_ref.shape)
    cnt_ref[...] = jnp.broadcast_to(total, cnt_ref.shape)


def _out_ln_kernel_1(h_ref, p_ref, a1_ref, w1_ref, *rest):
    _out_ln_kernel(1, h_ref, p_ref, a1_ref, w1_ref, None, None, *rest)


def _out_ln_kernel_2(h_ref, a1_ref, w1_ref, a2_ref, w2_ref, *rest):
    _out_ln_kernel(0, h_ref, a1_ref, w1_ref, a2_ref, w2_ref, *rest)


def _out_ln(h, addend, pairs, g, b, router):
    t_rows = h.shape[0]
    tm = OUT_TILE
    row = lambda i: (i, 0)
    const = lambda i: (0, 0)
    ins = [h]
    specs = [pl.BlockSpec((tm, D_MODEL), row)]
    if addend is not None:
        ins.append(addend)
        specs.append(pl.BlockSpec((tm, D_MODEL), row))
    for a, w in pairs:
        ins += [a, w]
        specs += [pl.BlockSpec((tm, a.shape[1]), row), pl.BlockSpec(w.shape, const)]
    ins += [g, b]
    specs += [pl.BlockSpec((1, D_MODEL), const)] * 2
    ins += list(router)
    specs += [pl.BlockSpec((D_MODEL, 256), const), pl.BlockSpec((D_MODEL, 128), const),
              pl.BlockSpec((1, 128), const)]
    body = _out_ln_kernel_1 if addend is not None else _out_ln_kernel_2
    return pl.pallas_call(
        body,
        grid=(t_rows // tm,),
        in_specs=specs,
        out_specs=[pl.BlockSpec((tm, D_MODEL), row), pl.BlockSpec((tm, 128), row),
                   pl.BlockSpec((8, 128), const)],
        out_shape=[jax.ShapeDtypeStruct((t_rows, D_MODEL), F32),
                   jax.ShapeDtypeStruct((t_rows, 128), F32),
                   jax.ShapeDtypeStruct((8, 128), F32)],
        scratch_shapes=[pltpu.VMEM((8, 128), F32)],
        compiler_params=_cparams(1),
        name="out_ln",
    )(*ins)


MOE_TILE = 256
BIG = 1e30


def _route(h, whl, wh, b):
    hh, hl = _split(h)
    both = _dot(hh, whl)
    logits = both[:, :128] + both[:, 128:] + _dot(hl, wh) + b
    lane = lax.broadcasted_iota(jnp.int32, logits.shape, 1).astype(F32)

    def first_argmax(vals, vmax):
        return jnp.min(jnp.where(vals == vmax, lane, 4096.0), axis=1, keepdims=True)

    is_g = lane < MOE_GROUPS
    lg = jnp.where(is_g, logits, -BIG)
    m = jnp.max(lg, axis=1, keepdims=True)
    gidx = first_argmax(lg, m)
    denom = jnp.sum(jnp.where(is_g, jnp.exp(logits - m), 0.0), axis=1, keepdims=True)
    p_top = 1.0 / denom
    base = MOE_GROUPS + MOE_EPG * gidx
    in_grp = (lane >= base) & (lane < base + MOE_EPG)
    le = jnp.where(in_grp, logits, -BIG)
    v1 = jnp.max(le, axis=1, keepdims=True)
    i1 = first_argmax(le, v1)
    le2 = jnp.where(lane == i1, -BIG, le)
    v2 = jnp.max(le2, axis=1, keepdims=True)
    i2 = first_argmax(le2, v2)
    e21 = jnp.exp(v2 - v1)
    w1 = p_top / (1.0 + e21)
    w2 = p_top * e21 / (1.0 + e21)
    return (jnp.where(lane == 0.0, i1 - MOE_GROUPS, 0.0) + jnp.where(lane == 1.0, i2 - MOE_GROUPS, 0.0)
            + jnp.where(lane == 2.0, w1, 0.0) + jnp.where(lane == 3.0, w2, 0.0))


def _router_weights(w_rg, b_rg, w_re, b_re):
    n_exp = MOE_GROUPS * MOE_EPG
    wr = jnp.concatenate([w_rg.astype(F32), w_re.astype(F32).reshape(D_MODEL, n_exp)], axis=1)
    wr = jnp.pad(wr, ((0, 0), (0, 128 - wr.shape[1])))
    wrh, wrl = _split(wr)
    br = jnp.concatenate([b_rg.astype(F32), b_re.astype(F32).reshape(-1)])
    br = jnp.pad(br, (0, 128 - br.shape[0])).reshape(1, 128)
    return jnp.concatenate([wrh, wrl], axis=1), wrh, br


DISPATCH_TILE = 256


def _dispatch_kernel(pos_ref, fill_ref, x_ref, xs_hbm, zbuf, sbuf, sem, rsem):
    tm = DISPATCH_TILE
    lb = LANE_BLOCKS
    i = pl.program_id(0)

    @pl.when(i == 0)
    def _():
        zbuf[...] = jnp.zeros_like(zbuf)

        def fill(e):
            dst = pl.multiple_of(jnp.maximum(fill_ref[e], 0) * lb, lb)
            return pltpu.make_async_copy(zbuf, xs_hbm.at[pl.ds(dst, MOE_TILE * lb)], sem)

        def start(e, c):
            @pl.when(fill_ref[e] >= 0)
            def _():
                fill(e).start()
            return c

        def wait(e, c):
            @pl.when(fill_ref[e] >= 0)
            def _():
                fill(e).wait()
            return c

        lax.fori_loop(0, fill_ref.shape[0], start, 0)
        lax.fori_loop(0, fill_ref.shape[0], wait, 0)

    slot = i % 2
    for c in range(lb):
        sbuf[slot, pl.ds(c, tm, stride=lb), :] = x_ref[:, c * 128:(c + 1) * 128]
    base = i * (2 * tm)
    for r in range(tm):
        for k in range(2):
            dst = pl.multiple_of(pos_ref[base + 2 * r + k] * lb, lb)
            pltpu.make_async_copy(sbuf.at[slot, pl.ds(r * lb, lb)], xs_hbm.at[pl.ds(dst, lb)],
                                  rsem.at[slot]).start(priority=k)

    def wait_rows(s):
        for _ in range(2):
            pltpu.make_async_copy(sbuf.at[s], xs_hbm.at[pl.ds(0, tm * lb)], rsem.at[s]).wait()

    @pl.when(i > 0)
    def _():
        wait_rows(1 - slot)

    @pl.when(i == pl.num_programs(0) - 1)
    def _():
        wait_rows(slot)


def _dispatch(pos, fill, h, n_sorted):
    t_rows = h.shape[0]
    tm = DISPATCH_TILE
    return pl.pallas_call(
        _dispatch_kernel,
        grid_spec=pltpu.PrefetchScalarGridSpec(
            num_scalar_prefetch=2,
            grid=(t_rows // tm,),
            in_specs=[pl.BlockSpec((tm, D_MODEL), lambda i, pos, fill: (i, 0))],
            out_specs=pl.BlockSpec(memory_space=pl.ANY),
            scratch_shapes=[pltpu.VMEM((MOE_TILE * LANE_BLOCKS, 128), F32),
                            pltpu.VMEM((2, tm * LANE_BLOCKS, 128), F32),
                            pltpu.SemaphoreType.DMA(()),
                            pltpu.SemaphoreType.DMA((2,))]),
        out_shape=jax.ShapeDtypeStruct((n_sorted * LANE_BLOCKS, 128), F32),
        compiler_params=_cparams(1),
        name="moe_dispatch",
    )(pos, fill, h)


def _moe_kernel(nv_ref, dst_ref, te_ref, x_ref, wg_ref, wu_ref, wd_ref, y_hbm,
                ybuf, xmat, wgb, wub, wdb, ssem):
    tm = MOE_TILE
    lb = LANE_BLOCKS
    i = pl.program_id(0)
    nv = nv_ref[0]
    slot = i % 2
    other = 1 - slot

    def scatter_copy(tile, row, buf_slot):
        dst = y_hbm.at[pl.ds(pl.multiple_of(dst_ref[tile * tm + row] * lb, lb), lb)]
        src_rows = pl.ds(row * lb if isinstance(row, int) else pl.multiple_of(row * lb, lb), lb)
        return pltpu.make_async_copy(ybuf.at[buf_slot, src_rows], dst, ssem.at[buf_slot])

    def wait_scatter(buf_slot):
        pltpu.make_async_copy(ybuf.at[buf_slot], y_hbm.at[pl.ds(0, tm * lb)], ssem.at[buf_slot]).wait()

    @pl.when((i == 0) | (te_ref[i] != te_ref[jnp.maximum(i - 1, 0)]))
    def _():
        wgb[...] = wg_ref[0].astype(BF16)
        wub[...] = wu_ref[0].astype(BF16)
        wdb[...] = wd_ref[0].astype(BF16)

    def compute(buf_slot):
        for c in range(lb):
            xmat[:, c * 128:(c + 1) * 128] = x_ref[pl.ds(c, tm, stride=lb), :]
        xb = xmat[...].astype(BF16)
        hg = _dot(xb, wgb[...])
        hu = _dot(xb, wub[...])
        hh = (hg * _sigmoid(hg)) * hu
        y = _dot(hh.astype(BF16), wdb[...])
        for c in range(lb):
            ybuf[buf_slot, pl.ds(c, tm, stride=lb), :] = y[:, c * 128:(c + 1) * 128]

    @pl.when(i == 0)
    def _():
        ybuf[1] = jnp.zeros((tm * lb, 128), F32)
        fill = pltpu.make_async_copy(ybuf.at[1], y_hbm.at[pl.ds(y_hbm.shape[0] - tm * lb, tm * lb)], ssem.at[1])
        fill.start()
        fill.wait()
        compute(0)

    @pl.when((i > 0) & (i < nv))
    def _():
        @pl.when(i >= 2)
        def _():
            wait_scatter(slot)

        for r in range(tm):
            scatter_copy(i - 1, r, other).start(priority=r % 2)
        compute(slot)

    @pl.when(i == nv)
    def _():
        @pl.when(i >= 2)
        def _():
            wait_scatter(slot)

        def body(r, c):
            scatter_copy(i - 1, r, other).start()
            return c
        lax.fori_loop(0, tm, body, 0)
        wait_scatter(other)


def _moe(nv, dst, te, xs, wg, wu, wd, n_out_rows):
    n_tiles = dst.shape[0] // MOE_TILE
    tm = MOE_TILE
    ex = lambda i, nv, dst, te: (te[i], 0, 0)
    return pl.pallas_call(
        _moe_kernel,
        grid_spec=pltpu.PrefetchScalarGridSpec(
            num_scalar_prefetch=3,
            grid=(n_tiles,),
            in_specs=[pl.BlockSpec((tm * LANE_BLOCKS, 128),
                                   lambda i, nv, dst, te: (jnp.minimum(i, nv[0] - 1), 0)),
                      pl.BlockSpec((1, D_MODEL, MOE_FF), ex),
                      pl.BlockSpec((1, D_MODEL, MOE_FF), ex),
                      pl.BlockSpec((1, MOE_FF, D_MODEL), ex)],
            out_specs=pl.BlockSpec(memory_space=pl.ANY),
            scratch_shapes=[pltpu.VMEM((2, tm * LANE_BLOCKS, 128), F32),
                            pltpu.VMEM((tm, D_MODEL), F32),
                            pltpu.VMEM((D_MODEL, MOE_FF), BF16),
                            pltpu.VMEM((D_MODEL, MOE_FF), BF16),
                            pltpu.VMEM((MOE_FF, D_MODEL), BF16),
                            pltpu.SemaphoreType.DMA((2,))]),
        out_shape=jax.ShapeDtypeStruct((n_out_rows * LANE_BLOCKS, 128), F32),
        compiler_params=_cparams(1),
        name="moe_experts",
    )(nv, dst, te, xs, wg, wu, wd)


MOE_LN_TILE = 512


def _moe_ln_kernel(h_ref, y0_ref, y1_ref, r_ref, g_ref, b_ref, o_ref, acc_ref):
    r = r_ref[...]
    w0 = r[:, 2:3]
    w1 = r[:, 3:4]
    tm = h_ref.shape[0]
    for c in range(LANE_BLOCKS):
        cols = slice(c * 128, (c + 1) * 128)
        rows = pl.ds(c, tm, stride=LANE_BLOCKS)
        acc_ref[:, cols] = DN_ALPHA * h_ref[:, cols] + (w0 * y0_ref[rows, :] + w1 * y1_ref[rows, :])
    o_ref[...] = _layer_norm(acc_ref[...], g_ref[...], b_ref[...])


def _moe_ln_final(h, y2, rinfo, g, b, batch, lp, seq):
    t_rows = h.shape[0]
    tm = MOE_LN_TILE
    lb = LANE_BLOCKS
    nj = seq // tm
    first = lambda bb, j: pl.multiple_of(bb * lp + N_META + j * tm, 8)
    const = lambda bb, j: (0, 0)
    return pl.pallas_call(
        _moe_ln_kernel,
        grid=(batch, nj),
        in_specs=[pl.BlockSpec((pl.Element(tm), pl.Element(D_MODEL)), lambda bb, j: (first(bb, j), 0)),
                  pl.BlockSpec((pl.Element(tm * lb), pl.Element(128)), lambda bb, j: (first(bb, j) * lb, 0)),
                  pl.BlockSpec((pl.Element(tm * lb), pl.Element(128)),
                               lambda bb, j: ((t_rows + first(bb, j)) * lb, 0)),
                  pl.BlockSpec((pl.Element(tm), pl.Element(128)), lambda bb, j: (first(bb, j), 0)),
                  pl.BlockSpec((1, D_MODEL), const),
                  pl.BlockSpec((1, D_MODEL), const)],
        out_specs=pl.BlockSpec((tm, D_MODEL), lambda bb, j: (bb * nj + j, 0)),
        out_shape=jax.ShapeDtypeStruct((batch * seq, D_MODEL), F32),
        scratch_shapes=[pltpu.VMEM((tm, D_MODEL), F32)],
        compiler_params=_cparams(2),
        name="moe_ln_final",
    )(h, y2, y2, rinfo, g, b)


def _moe_ln(h, y2, rinfo, g, b):
    t_rows = h.shape[0]
    tm = MOE_LN_TILE
    nb = t_rows // tm
    const = lambda i: (0, 0)
    return pl.pallas_call(
        _moe_ln_kernel,
        grid=(nb,),
        in_specs=[pl.BlockSpec((tm, D_MODEL), lambda i: (i, 0)),
                  pl.BlockSpec((tm * LANE_BLOCKS, 128), lambda i: (i, 0)),
                  pl.BlockSpec((tm * LANE_BLOCKS, 128), lambda i: (nb + i, 0)),
                  pl.BlockSpec((tm, 128), lambda i: (i, 0)),
                  pl.BlockSpec((1, D_MODEL), const),
                  pl.BlockSpec((1, D_MODEL), const)],
        out_specs=pl.BlockSpec((tm, D_MODEL), lambda i: (i, 0)),
        out_shape=jax.ShapeDtypeStruct((t_rows, D_MODEL), F32),
        scratch_shapes=[pltpu.VMEM((tm, D_MODEL), F32)],
        compiler_params=_cparams(1),
        name="moe_ln",
    )(h, y2, y2, rinfo, g, b)


def _hier_moe_ln(h, rinfo, counts, layer, w_gate, w_up, w_down, ln_g, ln_b, final=None):
    t_rows = h.shape[0]
    n_exp = MOE_GROUPS * MOE_EPG

    n_asg = 2 * t_rows
    tm = MOE_TILE
    e_a = rinfo[:, 0:2].astype(jnp.int32).reshape(n_asg)
    rank_a = rinfo[:, 4:6].astype(jnp.int32).reshape(n_asg)
    asg = jnp.arange(n_asg, dtype=jnp.int32)
    total = counts[0, :n_exp].astype(jnp.int32)
    tiles = (total + tm - 1) // tm
    tile_end = jnp.cumsum(tiles)
    tile_start = tile_end - tiles
    pos = (jnp.take(tile_start * tm, e_a) + rank_a).astype(jnp.int32)
    n_tiles = n_asg // tm + n_exp + 1
    n_sorted = n_tiles * tm
    nv = tile_end[-1:].astype(jnp.int32)
    slot_asg = jnp.full((n_sorted,), -1, jnp.int32).at[pos].set(asg)
    is_pad = slot_asg < 0
    dump = n_asg + jnp.arange(n_sorted, dtype=jnp.int32) % tm
    dst = jnp.where(is_pad, dump, (slot_asg & 1) * t_rows + (slot_asg >> 1))
    te = jnp.minimum(jnp.sum(jnp.arange(n_tiles)[:, None] >= tile_end[None, :], axis=1), n_exp - 1)
    te = (te + layer * n_exp).astype(jnp.int32)
    spare = tile_end[-1] + jnp.arange(n_tiles - n_asg // tm)
    fill = jnp.concatenate([jnp.where(tiles > 0, (tile_end - 1) * tm, -1),
                            jnp.where(spare < n_tiles, spare * tm, -1)]).astype(jnp.int32)

    xs = _dispatch(pos, fill, h, n_sorted)
    y2 = _moe(nv, dst, te, xs,
              w_gate.astype(F32).reshape(-1, D_MODEL, MOE_FF),
              w_up.astype(F32).reshape(-1, D_MODEL, MOE_FF),
              w_down.astype(F32).reshape(-1, MOE_FF, D_MODEL), n_asg + tm)
    g2 = ln_g.reshape(1, D_MODEL).astype(F32)
    b2 = ln_b.reshape(1, D_MODEL).astype(F32)
    if final is not None:
        return _moe_ln_final(h, y2, rinfo, g2, b2, *final)
    return _moe_ln(h, y2, rinfo, g2, b2)


def _layer_ab(h, batch, lp, w_in, s5, w_glu, b_glu, gla_w_gate, gla_b_gate, gla_norm, w_out, ln_g, ln_b,
              router):
    hk = GLA_HEADS * GLA_DK
    hv = GLA_HEADS * GLA_DV
    o_q = S5_WIDTH
    o_g = o_q + 2 * hk + hv
    o_r = o_g + GLA_RANK
    w_in = w_in.astype(F32)
    w_u = w_in[:, :S5_WIDTH].astype(BF16)
    w_main = jnp.concatenate([w_in[:, o_q:o_g], w_in[:, o_r:o_r + hv]], axis=1).astype(BF16)
    w_low = jnp.pad(w_in[:, o_g:o_r], ((0, 0), (0, 128 - GLA_RANK))).astype(BF16)
    u = _s5_uproj(h, w_u)
    z = _s5_core(u, s5, batch)
    p_s5 = _s5_out(z, w_glu.astype(BF16), b_glu.reshape(1, -1).astype(F32), w_out[:S5_WIDTH].astype(BF16))
    proj, g_low = _in_proj(h, w_main, w_low)
    wg = jnp.pad(gla_w_gate.astype(F32), ((0, 128 - GLA_RANK), (0, 0)))
    wgh, wgl = _split(wg)
    y_gla = _gla(proj, g_low, wgh, wgl, gla_b_gate.reshape(1, -1).astype(F32),
                 gla_norm.reshape(1, -1).astype(F32), batch, lp)
    return _out_ln(h, p_s5, [(y_gla, w_out[S5_WIDTH:].astype(BF16))],
                   ln_g.reshape(1, -1).astype(F32), ln_b.reshape(1, -1).astype(F32), router)


def _layer_cd(h, batch, lp, w_in, conv_w, a_log, dt_bias, gdn_norm, lru_conv_w, lru_conv_b,
              lru_w_a, lru_b_a, lru_w_x, lru_b_x, lru_lambda, w_out, ln_g, ln_b, router):
    nq = GDN_QKV
    hv = GDN_HEADS * GDN_DV
    w_in = w_in.astype(F32)
    o_z = nq
    o_b = o_z + hv
    o_a = o_b + GDN_HEADS
    o_x = o_a + GDN_HEADS
    o_gt = o_x + LRU_WIDTH
    w_main = jnp.concatenate([w_in[:, :o_b], w_in[:, o_x:o_gt + LRU_WIDTH]], axis=1).astype(BF16)
    w_ba = jnp.pad(w_in[:, o_b:o_x], ((0, 0), (0, 128 - 2 * GDN_HEADS))).astype(BF16)
    proj, ba = _in_proj(h, w_main, w_ba)

    cw = jnp.pad(conv_w.astype(F32), ((0, 8 - CONV_K), (0, 0)))
    hp = jnp.zeros((8, 128), F32)
    hp = hp.at[0, GDN_HEADS:2 * GDN_HEADS].set(-jnp.exp(a_log.astype(F32)))
    hp = hp.at[1, GDN_HEADS:2 * GDN_HEADS].set(dt_bias.astype(F32))
    y_gdn = _gdn(proj, ba, cw, hp, gdn_norm.reshape(1, -1).astype(F32), batch, lp)

    lcw = jnp.pad(lru_conv_w.astype(F32), ((0, 8 - CONV_K), (0, 0)))
    wax = jnp.concatenate([lru_w_a.astype(F32), lru_w_x.astype(F32)], axis=2).astype(BF16)
    bax = jnp.concatenate([lru_b_a.astype(F32).reshape(LRU_BLOCKS, 1, LRU_BW),
                           lru_b_x.astype(F32).reshape(LRU_BLOCKS, 1, LRU_BW)], axis=2)
    sp = (-LRU_C * jax.nn.softplus(-lru_lambda.astype(F32))).reshape(1, LRU_WIDTH)
    y_lru = _lru(proj, lcw, lru_conv_b.reshape(1, -1).astype(F32), wax, bax, sp, batch, lp)

    return _out_ln(h, None, [(y_gdn, w_out[:hv].astype(BF16)), (y_lru, w_out[hv:].astype(BF16))],
                   ln_g.reshape(1, -1).astype(F32), ln_b.reshape(1, -1).astype(F32), router)


def kernel(x, meta_tokens, ab_w_in, ab_s5_a_re, ab_s5_a_im, ab_s5_log_dt, ab_s5_b_re, ab_s5_b_im,
           ab_s5_c_re, ab_s5_c_im, ab_s5_d, ab_s5_w_glu, ab_s5_b_glu, ab_gla_w_gate, ab_gla_b_gate,
           ab_gla_norm, ab_w_out, cd_w_in, cd_conv_w, cd_gdn_a_log, cd_gdn_dt_bias, cd_gdn_norm,
           cd_lru_conv_w, cd_lru_conv_b, cd_lru_w_a, cd_lru_b_a, cd_lru_w_x, cd_lru_b_x, cd_lru_lambda,
           cd_w_out, moe_w_router_g, moe_b_router_g, moe_w_router_e, moe_b_router_e, moe_w_gate,
           moe_w_up, moe_w_down, ln_mix_g, ln_mix_b, ln_ffn_g, ln_ffn_b):
    batch, seq, _ = x.shape
    length = seq + N_META
    lp = -(-length // ROW_ALIGN) * ROW_ALIGN
    meta = jnp.broadcast_to(meta_tokens.astype(F32)[None], (batch, N_META, D_MODEL))
    h = jnp.zeros((batch, lp, D_MODEL), F32)
    h = lax.dynamic_update_slice(h, meta, (0, 0, 0))
    h = lax.dynamic_update_slice(h, x.astype(F32), (0, N_META, 0))
    h = h.reshape(batch * lp, D_MODEL)

    for layer in range(DEPTH):
        j = layer // 2
        router = _router_weights(moe_w_router_g[layer], moe_b_router_g[layer], moe_w_router_e[layer],
                                 moe_b_router_e[layer])
        if layer % 2 == 0:
            s5 = _s5_params(ab_s5_a_re[j], ab_s5_a_im[j], ab_s5_log_dt[j], ab_s5_b_re[j], ab_s5_b_im[j],
                            ab_s5_c_re[j], ab_s5_c_im[j], ab_s5_d[j])
            h, rinfo, counts = _layer_ab(h, batch, lp, ab_w_in[j], s5, ab_s5_w_glu[j], ab_s5_b_glu[j],
                                 ab_gla_w_gate[j], ab_gla_b_gate[j], ab_gla_norm[j], ab_w_out[j],
                                 ln_mix_g[layer], ln_mix_b[layer], router)
        else:
            h, rinfo, counts = _layer_cd(h, batch, lp, cd_w_in[j], cd_conv_w[j], cd_gdn_a_log[j], cd_gdn_dt_bias[j],
                                 cd_gdn_norm[j], cd_lru_conv_w[j], cd_lru_conv_b[j], cd_lru_w_a[j],
                                 cd_lru_b_a[j], cd_lru_w_x[j], cd_lru_b_x[j], cd_lru_lambda[j], cd_w_out[j],
                                 ln_mix_g[layer], ln_mix_b[layer], router)
        last = layer == DEPTH - 1 and seq % MOE_LN_TILE == 0
        h = _hier_moe_ln(h, rinfo, counts, layer, moe_w_gate, moe_w_up, moe_w_down,
                         ln_ffn_g[layer], ln_ffn_b[layer], final=(batch, lp, seq) if last else None)
        if last:
            return h.reshape(batch, seq, D_MODEL).astype(x.dtype)
    return h.reshape(batch, lp, D_MODEL)[:, N_META:length].astype(x.dtype)
```

```python
import math

import jax
import jax.numpy as jnp
from jax import lax
from jax.experimental import pallas as pl
from jax.experimental.pallas import tpu as pltpu

F32 = jnp.float32
BF16 = jnp.bfloat16
HIGHEST = lax.Precision.HIGHEST

D_MODEL = 2048
N_META = 16
CONV_K = 4
DEPTH = 2
DN_ALPHA = (2.0 * DEPTH) ** 0.25
LN_EPS = 1e-5
NORM_EPS = 1e-6

S5_WIDTH = 1024
S5_GROUP = 16
S5_GROUPS = 64
S5_STATE = 64
S5_LC = 16
S5_SLABS = 8

GLA_HEADS = 4
GLA_DK = 128
GLA_DV = 256
GLA_RANK = 16
GLA_TAU = 16.0

GDN_HEADS = 8
GDN_DK = 128
GDN_DV = 128

LRU_WIDTH = 1024
LRU_BLOCKS = 8
LRU_BW = 128
LRU_C = 8.0

MOE_GROUPS = 4
MOE_EPG = 8
MOE_FF = 256

LANE_BLOCKS = D_MODEL // 128
CHUNK = 64
ROW_ALIGN = 768
VMEM_LIMIT = 56 * 1024 * 1024

NT_DIMS = (((1,), (1,)), ((), ()))
TN_DIMS = (((0,), (0,)), ((), ()))


def _cparams(n_axes):
    return pltpu.CompilerParams(dimension_semantics=("arbitrary",) * n_axes,
                                vmem_limit_bytes=VMEM_LIMIT)


def _dot(a, b):
    return jnp.dot(a, b, preferred_element_type=F32)


def _dotg(a, b, dims):
    return lax.dot_general(a, b, dims, preferred_element_type=F32)


def _split(a):
    hi = a.astype(BF16)
    lo = (a - hi.astype(F32)).astype(BF16)
    return hi, lo


def _sigmoid(x):
    return 1.0 / (1.0 + jnp.exp(-x))


def _softplus(x):
    return jnp.maximum(x, 0.0) + jnp.log1p(jnp.exp(-jnp.abs(x)))


def _gelu(x):
    return 0.5 * x * (1.0 + jnp.tanh(math.sqrt(2.0 / math.pi) * (x + 0.044715 * (x * x * x))))


def _layer_norm(x, g, b):
    mu = jnp.mean(x, axis=-1, keepdims=True)
    xc = x - mu
    var = jnp.mean(xc * xc, axis=-1, keepdims=True)
    return xc * lax.rsqrt(var + LN_EPS) * g + b


PROJ_TILE = 512
PROJ_COLS = 1024


def _in_proj_kernel(x_ref, w_ref, wg_ref, o_ref, g_ref):
    xb = x_ref[...].astype(BF16)
    for j in range(w_ref.shape[1] // PROJ_COLS):
        cols = slice(j * PROJ_COLS, (j + 1) * PROJ_COLS)
        o_ref[:, cols] = _dot(xb, w_ref[:, cols]).astype(o_ref.dtype)
    g_ref[...] = _dot(xb, wg_ref[...])


def _in_proj(x, w, w_gates):
    m, k = x.shape
    n = w.shape[1]
    tm = PROJ_TILE
    const = lambda i: (0, 0)
    return pl.pallas_call(
        _in_proj_kernel,
        grid=(m // tm,),
        in_specs=[pl.BlockSpec((tm, k), lambda i: (i, 0)),
                  pl.BlockSpec((k, n), const, pipeline_mode=pl.Buffered(1)),
                  pl.BlockSpec((k, 128), const, pipeline_mode=pl.Buffered(1))],
        out_specs=[pl.BlockSpec((tm, n), lambda i: (i, 0)),
                   pl.BlockSpec((tm, 128), lambda i: (i, 0))],
        out_shape=[jax.ShapeDtypeStruct((m, n), BF16),
                   jax.ShapeDtypeStruct((m, 128), F32)],
        compiler_params=_cparams(1),
        name="in_proj",
    )(x, w, w_gates)


S5_TILE_CHUNKS = 32


def _s5_row_perm():
    m = S5_TILE_CHUNKS
    r = jnp.arange(S5_LC * m)
    nat = (r % m) * S5_LC + r // m
    return (nat[:, None] == jnp.arange(S5_LC * m)[None, :]).astype(BF16)


def _s5_uproj_kernel(h_ref, perm_ref, w_ref, u_ref):
    m = S5_TILE_CHUNKS
    lhs = _dot(perm_ref[...], h_ref[...].astype(BF16)).astype(BF16)
    z = _dot(lhs, w_ref[...])
    for t in range(S5_LC):
        for s in range(S5_SLABS):
            u_ref[s, :, t * 128:(t + 1) * 128] = z[t * m:(t + 1) * m, s * 128:(s + 1) * 128]


def _s5_uproj(h, w_u):
    t_rows = h.shape[0]
    n_chunks = t_rows // S5_LC
    m = S5_TILE_CHUNKS
    return pl.pallas_call(
        _s5_uproj_kernel,
        grid=(n_chunks // m,),
        in_specs=[pl.BlockSpec((m * S5_LC, D_MODEL), lambda i: (i, 0)),
                  pl.BlockSpec((m * S5_LC, m * S5_LC), lambda i: (0, 0)),
                  pl.BlockSpec((D_MODEL, S5_WIDTH), lambda i: (0, 0))],
        out_specs=pl.BlockSpec((S5_SLABS, m, S5_LC * 128), lambda i: (0, i, 0)),
        out_shape=jax.ShapeDtypeStruct((S5_SLABS, n_chunks, S5_LC * 128), F32),
        compiler_params=_cparams(1),
        name="s5_uproj",
    )(h, _s5_row_perm(), w_u)


def _s5_core_kernel(u_ref, stack_ref, ms_ref, mot_ref, ak_ref, pre_ref, pim_ref, d_ref, z_ref,
                    sre_ref, sim_ref, xre_ref, xim_ref):
    n = u_ref.shape[1]
    half = 8 * S5_STATE
    u = u_ref[0]
    ub = u.astype(BF16)
    s = _dot(ub, ms_ref[0])
    sre_ref[...] = s[:, :half]
    sim_ref[...] = s[:, half:]

    ak = ak_ref[0]
    pre = pre_ref[0]
    pim = pim_ref[0]
    row = lax.broadcasted_iota(jnp.int32, (8, half), 0)

    def body(rb, carry):
        cre, cim = carry
        r0 = pl.multiple_of(rb * 8, 8)
        xr = sre_ref[pl.ds(r0, 8), :]
        xi = sim_ref[pl.ds(r0, 8), :]
        for idx, k in enumerate((1, 2, 4)):
            are = ak[2 * idx:2 * idx + 1, :]
            aim = ak[2 * idx + 1:2 * idx + 2, :]
            keep = row >= k
            shr = jnp.where(keep, pltpu.roll(xr, k, 0), 0.0)
            shi = jnp.where(keep, pltpu.roll(xi, k, 0), 0.0)
            xr, xi = xr + are * shr - aim * shi, xi + are * shi + aim * shr
        xr = xr + pre * cre - pim * cim
        xi = xi + pre * cim + pim * cre
        first = row == 0
        xre_ref[pl.ds(r0, 8), :] = jnp.where(first, cre, pltpu.roll(xr, 1, 0))
        xim_ref[pl.ds(r0, 8), :] = jnp.where(first, cim, pltpu.roll(xi, 1, 0))
        return xr[7:8, :], xi[7:8, :]

    zero = jnp.zeros((1, half), F32)
    lax.fori_loop(0, n // 8, body, (zero, zero))

    xprev = jnp.concatenate([xre_ref[...], xim_ref[...]], axis=1).astype(BF16)
    y_inter = _dotg(xprev, mot_ref[0], NT_DIMS)
    d = d_ref[0]
    npair = S5_LC // 2
    for q in range(npair):
        yq = _dot(ub[:, :(q + 1) * 256], stack_ref[0, (npair - 1 - q) * 256:, :])
        sl = slice(q * 256, (q + 1) * 256)
        y = yq + y_inter[:, sl] + d[:, sl] * u[:, sl]
        z_ref[0, :, sl] = _gelu(y)


def _s5_core(u, prm, batch):
    n_tot = u.shape[1]
    n = n_tot // batch
    w = S5_LC * 128
    half = 8 * S5_STATE
    slab = lambda s, b: (s, 0, 0)
    return pl.pallas_call(
        _s5_core_kernel,
        grid=(S5_SLABS, batch),
        in_specs=[pl.BlockSpec((1, n, w), lambda s, b: (s, b, 0)),
                  pl.BlockSpec((1, w, 256), slab),
                  pl.BlockSpec((1, w, 2 * half), slab),
                  pl.BlockSpec((1, w, 2 * half), slab),
                  pl.BlockSpec((1, 8, half), slab),
                  pl.BlockSpec((1, 8, half), slab),
                  pl.BlockSpec((1, 8, half), slab),
                  pl.BlockSpec((1, 1, w), slab)],
        out_specs=pl.BlockSpec((1, n, w), lambda s, b: (s, b, 0)),
        out_shape=jax.ShapeDtypeStruct(u.shape, F32),
        scratch_shapes=[pltpu.VMEM((n, half), F32)] * 4,
        compiler_params=_cparams(2),
        name="s5_core",
    )(u, prm["stack"], prm["ms"], prm["mot"], prm["ak"], prm["pre"], prm["pim"], prm["d"])


def _s5_out_kernel(z_ref, perm_ref, wglu_ref, bglu_ref, wout_ref, p_ref, zf_ref):
    m = S5_TILE_CHUNKS
    for t in range(S5_LC):
        for s in range(S5_SLABS):
            zf_ref[t * m:(t + 1) * m, s * 128:(s + 1) * 128] = z_ref[s, :, t * 128:(t + 1) * 128]
    zf = zf_ref[...]
    gate = _sigmoid(_dot(zf.astype(BF16), wglu_ref[...]) + bglu_ref[...])
    gated = (zf * gate).astype(BF16)
    nat = _dotg(perm_ref[...], gated, TN_DIMS).astype(BF16)
    p_ref[...] = _dot(nat, wout_ref[...])


def _s5_out(z, w_glu, b_glu, w_out_top):
    n_chunks = z.shape[1]
    m = S5_TILE_CHUNKS
    return pl.pallas_call(
        _s5_out_kernel,
        grid=(n_chunks // m,),
        in_specs=[pl.BlockSpec((S5_SLABS, m, S5_LC * 128), lambda i: (0, i, 0)),
                  pl.BlockSpec((m * S5_LC, m * S5_LC), lambda i: (0, 0)),
                  pl.BlockSpec((S5_WIDTH, S5_WIDTH), lambda i: (0, 0)),
                  pl.BlockSpec((1, S5_WIDTH), lambda i: (0, 0)),
                  pl.BlockSpec((S5_WIDTH, D_MODEL), lambda i: (0, 0))],
        out_specs=pl.BlockSpec((m * S5_LC, D_MODEL), lambda i: (i, 0)),
        out_shape=jax.ShapeDtypeStruct((n_chunks * S5_LC, D_MODEL), F32),
        scratch_shapes=[pltpu.VMEM((S5_LC * m, S5_WIDTH), F32)],
        compiler_params=_cparams(1),
        name="s5_out",
    )(z, _s5_row_perm(), w_glu, b_glu, w_out_top)


def _s5_params(a_re, a_im, log_dt, b_re, b_im, c_re, c_im, d):
    g, p, c, lc = S5_GROUPS, S5_STATE, S5_GROUP, S5_LC
    a = lax.complex(a_re.astype(F32), a_im.astype(F32))
    adt = a * jnp.exp(log_dt.astype(F32))[:, None]
    zoh = (jnp.exp(adt) - 1.0) / a
    bz = lax.complex(b_re.astype(F32), b_im.astype(F32)) * zoh[:, :, None]
    cc = lax.complex(c_re.astype(F32), c_im.astype(F32))
    jj = jnp.arange(lc + 1, dtype=F32)
    apow = jnp.exp(adt[None] * jj[:, None, None])

    def expand(compact, width):
        rows = compact.shape[-2]
        row_grp = (jnp.arange(rows) % 128) // S5_GROUP
        lane_grp = jnp.arange(8 * width) // width
        tiled = jnp.tile(compact.astype(BF16), (1,) * (compact.ndim - 1) + (8,))
        return jnp.where(row_grp[:, None] == lane_grp[None, :], tiled, jnp.zeros((), BF16))

    kre = jnp.einsum("gop,jgp,gpc->jgco", cc, apow[:lc], bz, precision=HIGHEST).real
    kblk = expand(kre.reshape(lc, S5_SLABS, 128, c), c)
    kpad = jnp.concatenate([jnp.zeros_like(kblk[:1]), kblk], axis=0)
    npair = lc // 2
    dd = jnp.arange(npair)[:, None, None]
    lo_in = jnp.arange(2)[None, :, None]
    lo_out = jnp.arange(2)[None, None, :]
    lag = 2 * dd + lo_out - lo_in
    tiles = kpad[lag + 1]
    tiles = tiles.transpose(3, 0, 1, 4, 2, 5).reshape(S5_SLABS, npair, 256, 256)
    stack = tiles[:, ::-1].reshape(S5_SLABS, npair * 256, 256)

    wst = apow[lc - 1 - jnp.arange(lc)][:, :, :, None] * bz[None]
    wst = wst.reshape(lc, S5_SLABS, 8, p, c).transpose(1, 0, 2, 4, 3).reshape(S5_SLABS, lc * 128, p)
    ms = jnp.concatenate([expand(wst.real, p), expand(wst.imag, p)], axis=2)

    zc = cc[None] * apow[1:lc + 1][:, :, None, :]
    zc = zc.reshape(lc, S5_SLABS, 8 * c, p).transpose(1, 0, 2, 3).reshape(S5_SLABS, lc * 128, p)
    mot = jnp.concatenate([expand(zc.real, p), expand(-zc.imag, p)], axis=2)

    def slab_layout(x):
        return x.reshape(x.shape[:-2] + (S5_SLABS, 8 * p))

    kk = jnp.array([1.0, 2.0, 4.0], F32)
    a_k = slab_layout(jnp.exp(adt[None] * (lc * kk)[:, None, None]))
    ak = jnp.stack([a_k[0].real, a_k[0].imag, a_k[1].real, a_k[1].imag, a_k[2].real, a_k[2].imag,
                    jnp.zeros_like(a_k[0].real), jnp.zeros_like(a_k[0].real)], axis=1)
    rr = jnp.arange(1, 9, dtype=F32)
    a_r = slab_layout(jnp.exp(adt[None] * (lc * rr)[:, None, None]))
    pre = a_r.real.transpose(1, 0, 2)
    pim = a_r.imag.transpose(1, 0, 2)
    dsk = jnp.tile(d.astype(F32).reshape(S5_SLABS, 1, 128), (1, 1, lc))
    return {"stack": stack.astype(BF16), "ms": ms.astype(BF16), "mot": mot.astype(BF16),
            "ak": ak, "pre": pre, "pim": pim, "d": dsk}


GLA_TILE = 256


def _gla_kernel(q_ref, k_ref, v_ref, r_ref, gl_ref, wgh_ref, wgl_ref, bg_ref, ng_ref, o_ref, st_ref):
    @pl.when(pl.program_id(1) == 0)
    def _():
        st_ref[...] = jnp.zeros_like(st_ref)

    rowi = lax.broadcasted_iota(jnp.int32, (CHUNK, CHUNK), 0)
    coli = lax.broadcasted_iota(jnp.int32, (CHUNK, CHUNK), 1)
    tri = rowi >= coli
    wgh = wgh_ref[...]
    wgl = wgl_ref[...]
    bg = bg_ref[...]
    ng = ng_ref[...]
    scale = GLA_DK ** -0.5

    nch = GLA_TILE // CHUNK
    ti = lax.broadcasted_iota(jnp.int32, (GLA_TILE, GLA_TILE), 0)
    tj = lax.broadcasted_iota(jnp.int32, (GLA_TILE, GLA_TILE), 1)
    tri_t = jnp.where((ti >= tj) & ((ti // CHUNK) == (tj // CHUNK)), 1.0, 0.0).astype(BF16)
    glh, gll = _split(gl_ref[...])
    pre = _dot(glh, wgh) + _dot(gll, wgh) + _dot(glh, wgl) + bg
    logf = (jnp.minimum(pre, 0.0) - jnp.log1p(jnp.exp(-jnp.abs(pre)))) * (1.0 / GLA_TAU)
    lh, ll = _split(logf)
    bcum = _dot(tri_t, lh) + _dot(tri_t, ll)
    ebc = jnp.exp(bcum)
    kk = k_ref[...].astype(F32)
    q_in = (q_ref[...].astype(F32) * scale * ebc).astype(BF16)
    k_in = (kk * jnp.exp(-bcum)).astype(BF16)

    heads = range(GLA_HEADS)
    chunks = range(nch)
    items = [(c, h) for c in chunks for h in heads]
    rws = [slice(c * CHUNK, (c + 1) * CHUNK) for c in chunks]
    sks = [slice(h * GLA_DK, (h + 1) * GLA_DK) for h in heads]
    svs = [slice(h * GLA_DV, (h + 1) * GLA_DV) for h in heads]
    blast = [bcum[(c + 1) * CHUNK - 1:(c + 1) * CHUNK, :] for c in chunks]
    k_st = [(kk[rws[c], :] * jnp.exp(blast[c] - bcum[rws[c], :])).astype(BF16) for c in chunks]
    dec = [jnp.exp(blast[c]) for c in chunks]
    vh = {(c, h): v_ref[rws[c], svs[h]].astype(BF16) for c, h in items}
    att = {(c, h): _dotg(q_in[rws[c], sks[h]], k_in[rws[c], sks[h]], NT_DIMS) for c, h in items}
    kv = {(c, h): _dotg(vh[c, h], k_st[c][:, sks[h]], TN_DIMS) for c, h in items}
    o_intra = {it: _dot(jnp.where(tri, att[it], 0.0).astype(BF16), vh[it]) for it in items}
    st = [st_ref[h] for h in heads]
    for c in chunks:
        o_inter = [_dotg(q_in[rws[c], sks[h]], st[h].astype(BF16), NT_DIMS) for h in heads]
        for h in heads:
            o = o_intra[c, h] + o_inter[h]
            on = o * lax.rsqrt(jnp.mean(o * o, axis=1, keepdims=True) + NORM_EPS) * ng
            rr = r_ref[rws[c], svs[h]].astype(F32)
            o_ref[rws[c], svs[h]] = (on * (rr * _sigmoid(rr))).astype(o_ref.dtype)
        st = [st[h] * dec[c][:, sks[h]] + kv[c, h] for h in heads]
    for h in heads:
        st_ref[h] = st[h]


def _gla(proj, g_low, wgh, wgl, bg, ng, batch, lp):
    t_rows = proj.shape[0]
    nt = lp // GLA_TILE
    hk = GLA_HEADS * GLA_DK
    hv = GLA_HEADS * GLA_DV
    rowmap = lambda col: (lambda b, t: (b * nt + t, col))
    const = lambda b, t: (0, 0)
    return pl.pallas_call(
        _gla_kernel,
        grid=(batch, nt),
        in_specs=[pl.BlockSpec((GLA_TILE, hk), rowmap(0)),
                  pl.BlockSpec((GLA_TILE, hk), rowmap(1)),
                  pl.BlockSpec((GLA_TILE, hv), rowmap(1)),
                  pl.BlockSpec((GLA_TILE, hv), rowmap(2)),
                  pl.BlockSpec((GLA_TILE, 128), rowmap(0)),
                  pl.BlockSpec((128, hk), const),
                  pl.BlockSpec((128, hk), const),
                  pl.BlockSpec((1, hk), const),
                  pl.BlockSpec((1, GLA_DV), const)],
        out_specs=pl.BlockSpec((GLA_TILE, hv), rowmap(0)),
        out_shape=jax.ShapeDtypeStruct((t_rows, hv), BF16),
        scratch_shapes=[pltpu.VMEM((GLA_HEADS, GLA_DV, GLA_DK), F32)],
        compiler_params=_cparams(2),
        name="gla",
    )(proj, proj, proj, proj, g_low, wgh, wgl, bg, ng)


GDN_TILE = 256
GDN_GROUP = 4
GDN_QKV = 3 * GDN_HEADS * GDN_DK


def _gdn_kernel(qkv_ref, z_ref, ba_ref, cw_ref, hp_ref, ng_ref, o_ref,
                ext_ref, act_ref, beta_ref, g_ref, s_ref):
    tile = GDN_TILE

    @pl.when(pl.program_id(1) == 0)
    def _():
        ext_ref[0:8, :] = jnp.zeros((8, GDN_QKV), F32)
        s_ref[...] = jnp.zeros_like(s_ref)

    @pl.when(pl.program_id(1) > 0)
    def _():
        ext_ref[0:8, :] = ext_ref[tile:tile + 8, :]

    ext_ref[8:tile + 8, :] = qkv_ref[...].astype(F32)
    qscale = GDN_DK ** -0.5
    for cb in range(GDN_QKV // 128):
        cols = slice(cb * 128, (cb + 1) * 128)
        w = cw_ref[:, cols]
        y = (w[3:4] * ext_ref[8:tile + 8, cols] + w[2:3] * ext_ref[7:tile + 7, cols]
             + w[1:2] * ext_ref[6:tile + 6, cols] + w[0:1] * ext_ref[5:tile + 5, cols])
        a = y * _sigmoid(y)
        if cb < 2 * GDN_HEADS:
            a = a * lax.rsqrt(jnp.sum(a * a, axis=1, keepdims=True) + NORM_EPS)
            if cb < GDN_HEADS:
                a = a * qscale
        act_ref[:, cols] = a

    ba = ba_ref[...]
    beta_ref[...] = _sigmoid(ba)
    g_ref[...] = hp_ref[0:1, :] * _softplus(ba + hp_ref[1:2, :])

    rowi = lax.broadcasted_iota(jnp.int32, (CHUNK, CHUNK), 0)
    coli = lax.broadcasted_iota(jnp.int32, (CHUNK, CHUNK), 1)
    tri = rowi >= coli
    strict = rowi > coli
    tri_b = jnp.where(tri, 1.0, 0.0).astype(BF16)
    upper_b = jnp.where(rowi <= coli, 1.0, 0.0).astype(BF16)
    eye = jnp.where(rowi == coli, 1.0, 0.0)
    level_masks = []
    for lvl in range(6):
        bi = rowi >> lvl
        bj = coli >> lvl
        level_masks.append(((bi & 1) == 1) & (bj == bi - 1))
    ng = ng_ref[...]

    heads = range(GDN_HEADS)
    grp = range(GDN_GROUP)
    items = [(j, h) for j in grp for h in heads]
    hcol = lambda t, h: t[:, GDN_HEADS + h:GDN_HEADS + h + 1]

    def group(gi, carry):
        base = gi * (GDN_GROUP * CHUNK)
        rows = [pl.ds(pl.multiple_of(base + j * CHUNK, CHUNK), CHUNK) for j in grp]
        gsp = [_split(g_ref[rows[j], :]) for j in grp]
        gc = [_dot(tri_b, gsp[j][0]) + _dot(tri_b, gsp[j][1]) for j in grp]
        gct = [_dotg(gsp[j][0], upper_b, TN_DIMS) + _dotg(gsp[j][1], upper_b, TN_DIMS) for j in grp]
        eg = [jnp.exp(gc[j]) for j in grp]
        elast = [jnp.exp(gc[j][CHUNK - 1:CHUNK, :] - gc[j]) for j in grp]
        dlast = [jnp.exp(gc[j][CHUNK - 1:CHUNK, :]) for j in grp]
        beta = [beta_ref[rows[j], :] for j in grp]
        qs = {(j, h): act_ref[rows[j], h * GDN_DK:(h + 1) * GDN_DK] for j, h in items}
        ks = {(j, h): act_ref[rows[j], (GDN_HEADS + h) * GDN_DK:(GDN_HEADS + h + 1) * GDN_DK] for j, h in items}
        vs = {(j, h): act_ref[rows[j], (2 * GDN_HEADS + h) * GDN_DK:(2 * GDN_HEADS + h + 1) * GDN_DK]
              for j, h in items}
        bcols = {(j, h): beta[j][:, h:h + 1] for j, h in items}
        kbs = {it: ks[it] * bcols[it] for it in items}
        khb = {it: ks[it].astype(BF16) for it in items}
        gams = {(j, h): jnp.exp(jnp.where(tri, hcol(gc[j], h) - gct[j][GDN_HEADS + h:GDN_HEADS + h + 1, :], -1e30))
                for j, h in items}
        kk = {it: _dotg(kbs[it].astype(BF16), khb[it], NT_DIMS) for it in items}
        qk = {it: _dotg(qs[it].astype(BF16), khb[it], NT_DIMS) for it in items}
        mm = {it: jnp.where(strict, kk[it] * gams[it], 0.0) for it in items}
        tinv = {it: eye - jnp.where(level_masks[0], mm[it], 0.0) for it in items}
        for lvl in range(1, 6):
            tb = {it: tinv[it].astype(BF16) for it in items}
            xs = {it: _dot(jnp.where(level_masks[lvl], mm[it], 0.0).astype(BF16), tb[it]).astype(BF16)
                  for it in items}
            tinv = {it: tinv[it] - _dot(tb[it], xs[it]) for it in items}
        rhs = {(j, h): jnp.concatenate([vs[j, h] * bcols[j, h], kbs[j, h] * hcol(eg[j], h)], axis=1).astype(BF16)
               for j, h in items}
        sol = {it: _dot(tinv[it].astype(BF16), rhs[it]) for it in items}
        aqk = {it: jnp.where(tri, qk[it] * gams[it], 0.0).astype(BF16) for it in items}
        qdec = {(j, h): (qs[j, h] * hcol(eg[j], h)).astype(BF16) for j, h in items}
        kst = {(j, h): (ks[j, h] * hcol(elast[j], h)).astype(BF16) for j, h in items}
        for j in grp:
            s_old = [s_ref[h] for h in heads]
            sb = [s_old[h].astype(BF16) for h in heads]
            ws = [_dot(sol[j, h][:, GDN_DV:].astype(BF16), sb[h]) for h in heads]
            qsd = [_dot(qdec[j, h], sb[h]) for h in heads]
            vnb = [(sol[j, h][:, :GDN_DV] - ws[h]).astype(BF16) for h in heads]
            av = [_dot(aqk[j, h], vnb[h]) for h in heads]
            kv = [_dotg(kst[j, h], vnb[h], TN_DIMS) for h in heads]
            for h in heads:
                o = qsd[h] + av[h]
                on = o * lax.rsqrt(jnp.mean(o * o, axis=1, keepdims=True) + NORM_EPS) * ng
                zz = z_ref[rows[j], h * GDN_DV:(h + 1) * GDN_DV].astype(F32)
                o_ref[rows[j], h * GDN_DV:(h + 1) * GDN_DV] = (on * (zz * _sigmoid(zz))).astype(o_ref.dtype)
                s_ref[h] = s_old[h] * hcol(dlast[j], h) + kv[h]
        return carry

    lax.fori_loop(0, tile // (GDN_GROUP * CHUNK), group, 0)


def _gdn(proj, ba, cw, hp, ng, batch, lp):
    t_rows = proj.shape[0]
    nt = lp // GDN_TILE
    hv = GDN_HEADS * GDN_DV
    rowmap = lambda col: (lambda b, t: (b * nt + t, col))
    const = lambda b, t: (0, 0)
    return pl.pallas_call(
        _gdn_kernel,
        grid=(batch, nt),
        in_specs=[pl.BlockSpec((GDN_TILE, GDN_QKV), rowmap(0)),
                  pl.BlockSpec((GDN_TILE, hv), rowmap(3)),
                  pl.BlockSpec((GDN_TILE, 128), rowmap(0)),
                  pl.BlockSpec((8, GDN_QKV), const),
                  pl.BlockSpec((8, 128), const),
                  pl.BlockSpec((1, GDN_DV), const)],
        out_specs=pl.BlockSpec((GDN_TILE, hv), rowmap(0)),
        out_shape=jax.ShapeDtypeStruct((t_rows, hv), BF16),
        scratch_shapes=[pltpu.VMEM((GDN_TILE + 8, GDN_QKV), F32),
                        pltpu.VMEM((GDN_TILE, GDN_QKV), F32),
                        pltpu.VMEM((GDN_TILE, 128), F32),
                        pltpu.VMEM((GDN_TILE, 128), F32),
                        pltpu.VMEM((GDN_HEADS, GDN_DK, GDN_DV), F32)],
        compiler_params=_cparams(2),
        name="gdn",
    )(proj, proj, ba, cw, hp, ng)


LRU_TILE = 256


def _lru_kernel(x_ref, gate_ref, cw_ref, cb_ref, wax_ref, bax_ref, sp_ref, o_ref,
                ext_ref, a_ref, b_ref, h_ref):
    tile = LRU_TILE

    @pl.when(pl.program_id(1) == 0)
    def _():
        ext_ref[0:8, :] = jnp.zeros((8, LRU_WIDTH), F32)
        h_ref[...] = jnp.zeros_like(h_ref)

    @pl.when(pl.program_id(1) > 0)
    def _():
        ext_ref[0:8, :] = ext_ref[tile:tile + 8, :]

    ext_ref[8:tile + 8, :] = x_ref[...].astype(F32)
    for blk in range(LRU_BLOCKS):
        cols = slice(blk * LRU_BW, (blk + 1) * LRU_BW)
        w = cw_ref[:, cols]
        xc = (w[3:4] * ext_ref[8:tile + 8, cols] + w[2:3] * ext_ref[7:tile + 7, cols]
              + w[1:2] * ext_ref[6:tile + 6, cols] + w[0:1] * ext_ref[5:tile + 5, cols]
              + cb_ref[:, cols])
        pre = _dot(xc.astype(BF16), wax_ref[blk]) + bax_ref[blk]
        r = _sigmoid(pre[:, :LRU_BW])
        i = _sigmoid(pre[:, LRU_BW:])
        log_a = sp_ref[:, cols] * r
        a_ref[:, cols] = jnp.exp(log_a)
        th = jnp.tanh(log_a)
        b_ref[:, cols] = jnp.sqrt(-2.0 * th / (1.0 - th)) * (i * xc)

    row = lax.broadcasted_iota(jnp.int32, (8, LRU_WIDTH), 0)

    def body(rb, carry):
        r0 = pl.multiple_of(rb * 8, 8)
        aa = a_ref[pl.ds(r0, 8), :]
        bb = b_ref[pl.ds(r0, 8), :]
        for k in (1, 2, 4):
            keep = row >= k
            a_sh = jnp.where(keep, pltpu.roll(aa, k, 0), 1.0)
            b_sh = jnp.where(keep, pltpu.roll(bb, k, 0), 0.0)
            bb = aa * b_sh + bb
            aa = aa * a_sh
        hs = bb + aa * carry
        gt = gate_ref[pl.ds(r0, 8), :].astype(F32)
        o_ref[pl.ds(r0, 8), :] = (hs * _gelu(gt)).astype(o_ref.dtype)
        return hs[7:8, :]

    h_ref[...] = lax.fori_loop(0, tile // 8, body, h_ref[...])


def _lru(proj, cw, cb, wax, bax, sp, batch, lp):
    t_rows = proj.shape[0]
    nt = lp // LRU_TILE
    rowmap = lambda col: (lambda b, t: (b * nt + t, col))
    const = lambda b, t: (0, 0)
    return pl.pallas_call(
        _lru_kernel,
        grid=(batch, nt),
        in_specs=[pl.BlockSpec((LRU_TILE, LRU_WIDTH), rowmap(4)),
                  pl.BlockSpec((LRU_TILE, LRU_WIDTH), rowmap(5)),
                  pl.BlockSpec((8, LRU_WIDTH), const),
                  pl.BlockSpec((1, LRU_WIDTH), const),
                  pl.BlockSpec((LRU_BLOCKS, LRU_BW, 2 * LRU_BW), lambda b, t: (0, 0, 0)),
                  pl.BlockSpec((LRU_BLOCKS, 1, 2 * LRU_BW), lambda b, t: (0, 0, 0)),
                  pl.BlockSpec((1, LRU_WIDTH), const)],
        out_specs=pl.BlockSpec((LRU_TILE, LRU_WIDTH), rowmap(0)),
        out_shape=jax.ShapeDtypeStruct((t_rows, LRU_WIDTH), BF16),
        scratch_shapes=[pltpu.VMEM((LRU_TILE + 8, LRU_WIDTH), F32),
                        pltpu.VMEM((LRU_TILE, LRU_WIDTH), F32),
                        pltpu.VMEM((LRU_TILE, LRU_WIDTH), F32),
                        pltpu.VMEM((1, LRU_WIDTH), F32)],
        compiler_params=_cparams(2),
        name="lru",
    )(proj, proj, cw, cb, wax, bax, sp)


OUT_TILE = 512


def _out_ln_kernel(n_add, h_ref, *refs):
    adds = refs[:n_add]
    (a1_ref, w1_ref, a2_ref, w2_ref, g_ref, b_ref, wrh_ref, wrl_ref, br_ref,
     o_ref, r_ref, cnt_ref, run_ref) = refs[n_add:]

    @pl.when(pl.program_id(0) == 0)
    def _():
        run_ref[...] = jnp.zeros_like(run_ref)

    acc = DN_ALPHA * h_ref[...]
    for r in adds:
        acc = acc + r[...]
    acc = acc + _dot(a1_ref[...], w1_ref[...])
    if a2_ref is not None:
        acc = acc + _dot(a2_ref[...], w2_ref[...])
    hn = _layer_norm(acc, g_ref[...], b_ref[...])
    o_ref[...] = hn
    tm = hn.shape[0]
    r = _route(hn, wrh_ref[...], wrl_ref[...], br_ref[...])
    lane = lax.broadcasted_iota(jnp.int32, r.shape, 1).astype(F32)
    oh1 = lane == r[:, 0:1]
    oh2 = lane == r[:, 1:2]
    picked = jnp.where(oh1 | oh2, 1.0, 0.0)
    ri = lax.broadcasted_iota(jnp.int32, (tm, tm), 0)
    ci = lax.broadcasted_iota(jnp.int32, (tm, tm), 1)
    before = _dot(jnp.where(ri > ci, 1.0, 0.0).astype(BF16), picked.astype(BF16)) + run_ref[0:1, :]
    rank1 = jnp.sum(jnp.where(oh1, before, 0.0), axis=1, keepdims=True)
    rank2 = jnp.sum(jnp.where(oh2, before, 0.0), axis=1, keepdims=True)
    r_ref[...] = r + jnp.where(lane == 4.0, rank1, 0.0) + jnp.where(lane == 5.0, rank2, 0.0)
    total = run_ref[0:1, :] + jnp.sum(picked, axis=0, keepdims=True)
    run_ref[...] = jnp.broadcast_to(total, run_ref.shape)
    cnt_ref[...] = jnp.broadcast_to(total, cnt_ref.shape)


def _out_ln_kernel_1(h_ref, p_ref, a1_ref, w1_ref, *rest):
    _out_ln_kernel(1, h_ref, p_ref, a1_ref, w1_ref, None, None, *rest)


def _out_ln_kernel_2(h_ref, a1_ref, w1_ref, a2_ref, w2_ref, *rest):
    _out_ln_kernel(0, h_ref, a1_ref, w1_ref, a2_ref, w2_ref, *rest)


def _out_ln(h, addend, pairs, g, b, router):
    t_rows = h.shape[0]
    tm = OUT_TILE
    row = lambda i: (i, 0)
    const = lambda i: (0, 0)
    ins = [h]
    specs = [pl.BlockSpec((tm, D_MODEL), row)]
    if addend is not None:
        ins.append(addend)
        specs.append(pl.BlockSpec((tm, D_MODEL), row))
    for a, w in pairs:
        ins += [a, w]
        specs += [pl.BlockSpec((tm, a.shape[1]), row), pl.BlockSpec(w.shape, const)]
    ins += [g, b]
    specs += [pl.BlockSpec((1, D_MODEL), const)] * 2
    ins += list(router)
    specs += [pl.BlockSpec((D_MODEL, 256), const), pl.BlockSpec((D_MODEL, 128), const),
              pl.BlockSpec((1, 128), const)]
    body = _out_ln_kernel_1 if addend is not None else _out_ln_kernel_2
    return pl.pallas_call(
        body,
        grid=(t_rows // tm,),
        in_specs=specs,
        out_specs=[pl.BlockSpec((tm, D_MODEL), row), pl.BlockSpec((tm, 128), row),
                   pl.BlockSpec((8, 128), const)],
        out_shape=[jax.ShapeDtypeStruct((t_rows, D_MODEL), F32),
                   jax.ShapeDtypeStruct((t_rows, 128), F32),
                   jax.ShapeDtypeStruct((8, 128), F32)],
        scratch_shapes=[pltpu.VMEM((8, 128), F32)],
        compiler_params=_cparams(1),
        name="out_ln",
    )(*ins)


MOE_TILE = 256
BIG = 1e30


def _route(h, whl, wh, b):
    hh, hl = _split(h)
    both = _dot(hh, whl)
    logits = both[:, :128] + both[:, 128:] + _dot(hl, wh) + b
    lane = lax.broadcasted_iota(jnp.int32, logits.shape, 1).astype(F32)

    def first_argmax(vals, vmax):
        return jnp.min(jnp.where(vals == vmax, lane, 4096.0), axis=1, keepdims=True)

    is_g = lane < MOE_GROUPS
    lg = jnp.where(is_g, logits, -BIG)
    m = jnp.max(lg, axis=1, keepdims=True)
    gidx = first_argmax(lg, m)
    denom = jnp.sum(jnp.where(is_g, jnp.exp(logits - m), 0.0), axis=1, keepdims=True)
    p_top = 1.0 / denom
    base = MOE_GROUPS + MOE_EPG * gidx
    in_grp = (lane >= base) & (lane < base + MOE_EPG)
    le = jnp.where(in_grp, logits, -BIG)
    v1 = jnp.max(le, axis=1, keepdims=True)
    i1 = first_argmax(le, v1)
    le2 = jnp.where(lane == i1, -BIG, le)
    v2 = jnp.max(le2, axis=1, keepdims=True)
    i2 = first_argmax(le2, v2)
    e21 = jnp.exp(v2 - v1)
    w1 = p_top / (1.0 + e21)
    w2 = p_top * e21 / (1.0 + e21)
    return (jnp.where(lane == 0.0, i1 - MOE_GROUPS, 0.0) + jnp.where(lane == 1.0, i2 - MOE_GROUPS, 0.0)
            + jnp.where(lane == 2.0, w1, 0.0) + jnp.where(lane == 3.0, w2, 0.0))


def _router_weights(w_rg, b_rg, w_re, b_re):
    n_exp = MOE_GROUPS * MOE_EPG
    wr = jnp.concatenate([w_rg.astype(F32), w_re.astype(F32).reshape(D_MODEL, n_exp)], axis=1)
    wr = jnp.pad(wr, ((0, 0), (0, 128 - wr.shape[1])))
    wrh, wrl = _split(wr)
    br = jnp.concatenate([b_rg.astype(F32), b_re.astype(F32).reshape(-1)])
    br = jnp.pad(br, (0, 128 - br.shape[0])).reshape(1, 128)
    return jnp.concatenate([wrh, wrl], axis=1), wrh, br


DISPATCH_TILE = 256


def _dispatch_kernel(pos_ref, fill_ref, x_ref, xs_hbm, zbuf, sbuf, sem, rsem):
    tm = DISPATCH_TILE
    lb = LANE_BLOCKS
    i = pl.program_id(0)

    @pl.when(i == 0)
    def _():
        zbuf[...] = jnp.zeros_like(zbuf)

        def fill(e):
            dst = pl.multiple_of(jnp.maximum(fill_ref[e], 0) * lb, lb)
            return pltpu.make_async_copy(zbuf, xs_hbm.at[pl.ds(dst, MOE_TILE * lb)], sem)

        def start(e, c):
            @pl.when(fill_ref[e] >= 0)
            def _():
                fill(e).start()
            return c

        def wait(e, c):
            @pl.when(fill_ref[e] >= 0)
            def _():
                fill(e).wait()
            return c

        lax.fori_loop(0, fill_ref.shape[0], start, 0)
        lax.fori_loop(0, fill_ref.shape[0], wait, 0)

    slot = i % 2
    for c in range(lb):
        sbuf[slot, pl.ds(c, tm, stride=lb), :] = x_ref[:, c * 128:(c + 1) * 128]
    base = i * (2 * tm)
    for r in range(tm):
        for k in range(2):
            dst = pl.multiple_of(pos_ref[base + 2 * r + k] * lb, lb)
            pltpu.make_async_copy(sbuf.at[slot, pl.ds(r * lb, lb)], xs_hbm.at[pl.ds(dst, lb)],
                                  rsem.at[slot]).start(priority=k)

    def wait_rows(s):
        for _ in range(2):
            pltpu.make_async_copy(sbuf.at[s], xs_hbm.at[pl.ds(0, tm * lb)], rsem.at[s]).wait()

    @pl.when(i > 0)
    def _():
        wait_rows(1 - slot)

    @pl.when(i == pl.num_programs(0) - 1)
    def _():
        wait_rows(slot)


def _dispatch(pos, fill, h, n_sorted):
    t_rows = h.shape[0]
    tm = DISPATCH_TILE
    return pl.pallas_call(
        _dispatch_kernel,
        grid_spec=pltpu.PrefetchScalarGridSpec(
            num_scalar_prefetch=2,
            grid=(t_rows // tm,),
            in_specs=[pl.BlockSpec((tm, D_MODEL), lambda i, pos, fill: (i, 0))],
            out_specs=pl.BlockSpec(memory_space=pl.ANY),
            scratch_shapes=[pltpu.VMEM((MOE_TILE * LANE_BLOCKS, 128), F32),
                            pltpu.VMEM((2, tm * LANE_BLOCKS, 128), F32),
                            pltpu.SemaphoreType.DMA(()),
                            pltpu.SemaphoreType.DMA((2,))]),
        out_shape=jax.ShapeDtypeStruct((n_sorted * LANE_BLOCKS, 128), F32),
        compiler_params=_cparams(1),
        name="moe_dispatch",
    )(pos, fill, h)


def _moe_kernel(nv_ref, dst_ref, te_ref, x_ref, wg_ref, wu_ref, wd_ref, y_hbm,
                ybuf, xmat, wgb, wub, wdb, ssem):
    tm = MOE_TILE
    lb = LANE_BLOCKS
    i = pl.program_id(0)
    nv = nv_ref[0]
    slot = i % 2
    other = 1 - slot

    def scatter_copy(tile, row, buf_slot):
        dst = y_hbm.at[pl.ds(pl.multiple_of(dst_ref[tile * tm + row] * lb, lb), lb)]
        src_rows = pl.ds(row * lb if isinstance(row, int) else pl.multiple_of(row * lb, lb), lb)
        return pltpu.make_async_copy(ybuf.at[buf_slot, src_rows], dst, ssem.at[buf_slot])

    def wait_scatter(buf_slot):
        pltpu.make_async_copy(ybuf.at[buf_slot], y_hbm.at[pl.ds(0, tm * lb)], ssem.at[buf_slot]).wait()

    @pl.when((i == 0) | (te_ref[i] != te_ref[jnp.maximum(i - 1, 0)]))
    def _():
        wgb[...] = wg_ref[0].astype(BF16)
        wub[...] = wu_ref[0].astype(BF16)
        wdb[...] = wd_ref[0].astype(BF16)

    def compute(buf_slot):
        for c in range(lb):
            xmat[:, c * 128:(c + 1) * 128] = x_ref[pl.ds(c, tm, stride=lb), :]
        xb = xmat[...].astype(BF16)
        hg = _dot(xb, wgb[...])
        hu = _dot(xb, wub[...])
        hh = (hg * _sigmoid(hg)) * hu
        y = _dot(hh.astype(BF16), wdb[...])
        for c in range(lb):
            ybuf[buf_slot, pl.ds(c, tm, stride=lb), :] = y[:, c * 128:(c + 1) * 128]

    @pl.when(i == 0)
    def _():
        ybuf[1] = jnp.zeros((tm * lb, 128), F32)
        fill = pltpu.make_async_copy(ybuf.at[1], y_hbm.at[pl.ds(y_hbm.shape[0] - tm * lb, tm * lb)], ssem.at[1])
        fill.start()
        fill.wait()
        compute(0)

    @pl.when((i > 0) & (i < nv))
    def _():
        @pl.when(i >= 2)
        def _():
            wait_scatter(slot)

        for r in range(tm):
            scatter_copy(i - 1, r, other).start(priority=r % 2)
        compute(slot)

    @pl.when(i == nv)
    def _():
        @pl.when(i >= 2)
        def _():
            wait_scatter(slot)

        def body(r, c):
            scatter_copy(i - 1, r, other).start()
            return c
        lax.fori_loop(0, tm, body, 0)
        wait_scatter(other)


def _moe(nv, dst, te, xs, wg, wu, wd, n_out_rows):
    n_tiles = dst.shape[0] // MOE_TILE
    tm = MOE_TILE
    ex = lambda i, nv, dst, te: (te[i], 0, 0)
    return pl.pallas_call(
        _moe_kernel,
        grid_spec=pltpu.PrefetchScalarGridSpec(
            num_scalar_prefetch=3,
            grid=(n_tiles,),
            in_specs=[pl.BlockSpec((tm * LANE_BLOCKS, 128),
                                   lambda i, nv, dst, te: (jnp.minimum(i, nv[0] - 1), 0)),
                      pl.BlockSpec((1, D_MODEL, MOE_FF), ex),
                      pl.BlockSpec((1, D_MODEL, MOE_FF), ex),
                      pl.BlockSpec((1, MOE_FF, D_MODEL), ex)],
            out_specs=pl.BlockSpec(memory_space=pl.ANY),
            scratch_shapes=[pltpu.VMEM((2, tm * LANE_BLOCKS, 128), F32),
                            pltpu.VMEM((tm, D_MODEL), F32),
                            pltpu.VMEM((D_MODEL, MOE_FF), BF16),
                            pltpu.VMEM((D_MODEL, MOE_FF), BF16),
                            pltpu.VMEM((MOE_FF, D_MODEL), BF16),
                            pltpu.SemaphoreType.DMA((2,))]),
        out_shape=jax.ShapeDtypeStruct((n_out_rows * LANE_BLOCKS, 128), F32),
        compiler_params=_cparams(1),
        name="moe_experts",
    )(nv, dst, te, xs, wg, wu, wd)


MOE_LN_TILE = 512


def _moe_ln_kernel(h_ref, y0_ref, y1_ref, r_ref, g_ref, b_ref, o_ref, acc_ref):
    r = r_ref[...]
    w0 = r[:, 2:3]
    w1 = r[:, 3:4]
    tm = h_ref.shape[0]
    for c in range(LANE_BLOCKS):
        cols = slice(c * 128, (c + 1) * 128)
        rows = pl.ds(c, tm, stride=LANE_BLOCKS)
        acc_ref[:, cols] = DN_ALPHA * h_ref[:, cols] + (w0 * y0_ref[rows, :] + w1 * y1_ref[rows, :])
    o_ref[...] = _layer_norm(acc_ref[...], g_ref[...], b_ref[...])


def _moe_ln_final(h, y2, rinfo, g, b, batch, lp, seq):
    t_rows = h.shape[0]
    tm = MOE_LN_TILE
    lb = LANE_BLOCKS
    nj = seq // tm
    first = lambda bb, j: pl.multiple_of(bb * lp + N_META + j * tm, 8)
    const = lambda bb, j: (0, 0)
    return pl.pallas_call(
        _moe_ln_kernel,
        grid=(batch, nj),
        in_specs=[pl.BlockSpec((pl.Element(tm), pl.Element(D_MODEL)), lambda bb, j: (first(bb, j), 0)),
                  pl.BlockSpec((pl.Element(tm * lb), pl.Element(128)), lambda bb, j: (first(bb, j) * lb, 0)),
                  pl.BlockSpec((pl.Element(tm * lb), pl.Element(128)),
                               lambda bb, j: ((t_rows + first(bb, j)) * lb, 0)),
                  pl.BlockSpec((pl.Element(tm), pl.Element(128)), lambda bb, j: (first(bb, j), 0)),
                  pl.BlockSpec((1, D_MODEL), const),
                  pl.BlockSpec((1, D_MODEL), const)],
        out_specs=pl.BlockSpec((tm, D_MODEL), lambda bb, j: (bb * nj + j, 0)),
        out_shape=jax.ShapeDtypeStruct((batch * seq, D_MODEL), F32),
        scratch_shapes=[pltpu.VMEM((tm, D_MODEL), F32)],
        compiler_params=_cparams(2),
        name="moe_ln_final",
    )(h, y2, y2, rinfo, g, b)


def _moe_ln(h, y2, rinfo, g, b):
    t_rows = h.shape[0]
    tm = MOE_LN_TILE
    nb = t_rows // tm
    const = lambda i: (0, 0)
    return pl.pallas_call(
        _moe_ln_kernel,
        grid=(nb,),
        in_specs=[pl.BlockSpec((tm, D_MODEL), lambda i: (i, 0)),
                  pl.BlockSpec((tm * LANE_BLOCKS, 128), lambda i: (i, 0)),
                  pl.BlockSpec((tm * LANE_BLOCKS, 128), lambda i: (nb + i, 0)),
                  pl.BlockSpec((tm, 128), lambda i: (i, 0)),
                  pl.BlockSpec((1, D_MODEL), const),
                  pl.BlockSpec((1, D_MODEL), const)],
        out_specs=pl.BlockSpec((tm, D_MODEL), lambda i: (i, 0)),
        out_shape=jax.ShapeDtypeStruct((t_rows, D_MODEL), F32),
        scratch_shapes=[pltpu.VMEM((tm, D_MODEL), F32)],
        compiler_params=_cparams(1),
        name="moe_ln",
    )(h, y2, y2, rinfo, g, b)


def _hier_moe_ln(h, rinfo, counts, layer, w_gate, w_up, w_down, ln_g, ln_b, final=None):
    t_rows = h.shape[0]
    n_exp = MOE_GROUPS * MOE_EPG

    n_asg = 2 * t_rows
    tm = MOE_TILE
    e_a = rinfo[:, 0:2].astype(jnp.int32).reshape(n_asg)
    rank_a = rinfo[:, 4:6].astype(jnp.int32).reshape(n_asg)
    asg = jnp.arange(n_asg, dtype=jnp.int32)
    total = counts[0, :n_exp].astype(jnp.int32)
    tiles = (total + tm - 1) // tm
    tile_end = jnp.cumsum(tiles)
    tile_start = tile_end - tiles
    pos = (jnp.take(tile_start * tm, e_a) + rank_a).astype(jnp.int32)
    n_tiles = n_asg // tm + n_exp + 1
    n_sorted = n_tiles * tm
    nv = tile_end[-1:].astype(jnp.int32)
    slot_asg = jnp.full((n_sorted,), -1, jnp.int32).at[pos].set(asg, unique_indices=True,
                                                                mode="promise_in_bounds")
    is_pad = slot_asg < 0
    dump = n_asg + jnp.arange(n_sorted, dtype=jnp.int32) % tm
    dst = jnp.where(is_pad, dump, (slot_asg & 1) * t_rows + (slot_asg >> 1))
    te = jnp.minimum(jnp.sum(jnp.arange(n_tiles)[:, None] >= tile_end[None, :], axis=1), n_exp - 1)
    te = (te + layer * n_exp).astype(jnp.int32)
    spare = tile_end[-1] + jnp.arange(n_tiles - n_asg // tm)
    fill = jnp.concatenate([jnp.where(tiles > 0, (tile_end - 1) * tm, -1),
                            jnp.where(spare < n_tiles, spare * tm, -1)]).astype(jnp.int32)

    xs = _dispatch(pos, fill, h, n_sorted)
    y2 = _moe(nv, dst, te, xs,
              w_gate.astype(F32).reshape(-1, D_MODEL, MOE_FF),
              w_up.astype(F32).reshape(-1, D_MODEL, MOE_FF),
              w_down.astype(F32).reshape(-1, MOE_FF, D_MODEL), n_asg + tm)
    g2 = ln_g.reshape(1, D_MODEL).astype(F32)
    b2 = ln_b.reshape(1, D_MODEL).astype(F32)
    if final is not None:
        return _moe_ln_final(h, y2, rinfo, g2, b2, *final)
    return _moe_ln(h, y2, rinfo, g2, b2)


def _layer_ab(h, batch, lp, w_in, s5, w_glu, b_glu, gla_w_gate, gla_b_gate, gla_norm, w_out, ln_g, ln_b,
              router):
    hk = GLA_HEADS * GLA_DK
    hv = GLA_HEADS * GLA_DV
    o_q = S5_WIDTH
    o_g = o_q + 2 * hk + hv
    o_r = o_g + GLA_RANK
    w_in = w_in.astype(F32)
    w_u = w_in[:, :S5_WIDTH].astype(BF16)
    w_main = jnp.concatenate([w_in[:, o_q:o_g], w_in[:, o_r:o_r + hv]], axis=1).astype(BF16)
    w_low = jnp.pad(w_in[:, o_g:o_r], ((0, 0), (0, 128 - GLA_RANK))).astype(BF16)
    u = _s5_uproj(h, w_u)
    z = _s5_core(u, s5, batch)
    p_s5 = _s5_out(z, w_glu.astype(BF16), b_glu.reshape(1, -1).astype(F32), w_out[:S5_WIDTH].astype(BF16))
    proj, g_low = _in_proj(h, w_main, w_low)
    wg = jnp.pad(gla_w_gate.astype(F32), ((0, 128 - GLA_RANK), (0, 0)))
    wgh, wgl = _split(wg)
    y_gla = _gla(proj, g_low, wgh, wgl, gla_b_gate.reshape(1, -1).astype(F32),
                 gla_norm.reshape(1, -1).astype(F32), batch, lp)
    return _out_ln(h, p_s5, [(y_gla, w_out[S5_WIDTH:].astype(BF16))],
                   ln_g.reshape(1, -1).astype(F32), ln_b.reshape(1, -1).astype(F32), router)


def _layer_cd(h, batch, lp, w_in, conv_w, a_log, dt_bias, gdn_norm, lru_conv_w, lru_conv_b,
              lru_w_a, lru_b_a, lru_w_x, lru_b_x, lru_lambda, w_out, ln_g, ln_b, router):
    nq = GDN_QKV
    hv = GDN_HEADS * GDN_DV
    w_in = w_in.astype(F32)
    o_z = nq
    o_b = o_z + hv
    o_a = o_b + GDN_HEADS
    o_x = o_a + GDN_HEADS
    o_gt = o_x + LRU_WIDTH
    w_main = jnp.concatenate([w_in[:, :o_b], w_in[:, o_x:o_gt + LRU_WIDTH]], axis=1).astype(BF16)
    w_ba = jnp.pad(w_in[:, o_b:o_x], ((0, 0), (0, 128 - 2 * GDN_HEADS))).astype(BF16)
    proj, ba = _in_proj(h, w_main, w_ba)

    cw = jnp.pad(conv_w.astype(F32), ((0, 8 - CONV_K), (0, 0)))
    hp = jnp.zeros((8, 128), F32)
    hp = hp.at[0, GDN_HEADS:2 * GDN_HEADS].set(-jnp.exp(a_log.astype(F32)))
    hp = hp.at[1, GDN_HEADS:2 * GDN_HEADS].set(dt_bias.astype(F32))
    y_gdn = _gdn(proj, ba, cw, hp, gdn_norm.reshape(1, -1).astype(F32), batch, lp)

    lcw = jnp.pad(lru_conv_w.astype(F32), ((0, 8 - CONV_K), (0, 0)))
    wax = jnp.concatenate([lru_w_a.astype(F32), lru_w_x.astype(F32)], axis=2).astype(BF16)
    bax = jnp.concatenate([lru_b_a.astype(F32).reshape(LRU_BLOCKS, 1, LRU_BW),
                           lru_b_x.astype(F32).reshape(LRU_BLOCKS, 1, LRU_BW)], axis=2)
    sp = (-LRU_C * jax.nn.softplus(-lru_lambda.astype(F32))).reshape(1, LRU_WIDTH)
    y_lru = _lru(proj, lcw, lru_conv_b.reshape(1, -1).astype(F32), wax, bax, sp, batch, lp)

    return _out_ln(h, None, [(y_gdn, w_out[:hv].astype(BF16)), (y_lru, w_out[hv:].astype(BF16))],
                   ln_g.reshape(1, -1).astype(F32), ln_b.reshape(1, -1).astype(F32), router)


def kernel(x, meta_tokens, ab_w_in, ab_s5_a_re, ab_s5_a_im, ab_s5_log_dt, ab_s5_b_re, ab_s5_b_im,
           ab_s5_c_re, ab_s5_c_im, ab_s5_d, ab_s5_w_glu, ab_s5_b_glu, ab_gla_w_gate, ab_gla_b_gate,
           ab_gla_norm, ab_w_out, cd_w_in, cd_conv_w, cd_gdn_a_log, cd_gdn_dt_bias, cd_gdn_norm,
           cd_lru_conv_w, cd_lru_conv_b, cd_lru_w_a, cd_lru_b_a, cd_lru_w_x, cd_lru_b_x, cd_lru_lambda,
           cd_w_out, moe_w_router_g, moe_b_router_g, moe_w_router_e, moe_b_router_e, moe_w_gate,
           moe_w_up, moe_w_down, ln_mix_g, ln_mix_b, ln_ffn_g, ln_ffn_b):
    batch, seq, _ = x.shape
    length = seq + N_META
    lp = -(-length // ROW_ALIGN) * ROW_ALIGN
    meta = jnp.broadcast_to(meta_tokens.astype(F32)[None], (batch, N_META, D_MODEL))
    h = jnp.zeros((batch, lp, D_MODEL), F32)
    h = lax.dynamic_update_slice(h, meta, (0, 0, 0))
    h = lax.dynamic_update_slice(h, x.astype(F32), (0, N_META, 0))
    h = h.reshape(batch * lp, D_MODEL)

    for layer in range(DEPTH):
        j = layer // 2
        router = _router_weights(moe_w_router_g[layer], moe_b_router_g[layer], moe_w_router_e[layer],
                                 moe_b_router_e[layer])
        if layer % 2 == 0:
            s5 = _s5_params(ab_s5_a_re[j], ab_s5_a_im[j], ab_s5_log_dt[j], ab_s5_b_re[j], ab_s5_b_im[j],
                            ab_s5_c_re[j], ab_s5_c_im[j], ab_s5_d[j])
            h, rinfo, counts = _layer_ab(h, batch, lp, ab_w_in[j], s5, ab_s5_w_glu[j], ab_s5_b_glu[j],
                                 ab_gla_w_gate[j], ab_gla_b_gate[j], ab_gla_norm[j], ab_w_out[j],
                                 ln_mix_g[layer], ln_mix_b[layer], router)
        else:
            h, rinfo, counts = _layer_cd(h, batch, lp, cd_w_in[j], cd_conv_w[j], cd_gdn_a_log[j], cd_gdn_dt_bias[j],
                                 cd_gdn_norm[j], cd_lru_conv_w[j], cd_lru_conv_b[j], cd_lru_w_a[j],
                                 cd_lru_b_a[j], cd_lru_w_x[j], cd_lru_b_x[j], cd_lru_lambda[j], cd_w_out[j],
                                 ln_mix_g[layer], ln_mix_b[layer], router)
        last = layer == DEPTH - 1 and seq % MOE_LN_TILE == 0
        h = _hier_moe_ln(h, rinfo, counts, layer, moe_w_gate, moe_w_up, moe_w_down,
                         ln_ffn_g[layer], ln_ffn_b[layer], final=(batch, lp, seq) if last else None)
        if last:
            return h.reshape(batch, seq, D_MODEL).astype(x.dtype)
    return h.reshape(batch, lp, D_MODEL)[:, N_META:length].astype(x.dtype)
```

```python
import math

import jax
import jax.numpy as jnp
from jax import lax
from jax.experimental import pallas as pl
from jax.experimental.pallas import tpu as pltpu

F32 = jnp.float32
BF16 = jnp.bfloat16
HIGHEST = lax.Precision.HIGHEST

D_MODEL = 2048
N_META = 16
CONV_K = 4
DEPTH = 2
DN_ALPHA = (2.0 * DEPTH) ** 0.25
LN_EPS = 1e-5
NORM_EPS = 1e-6

S5_WIDTH = 1024
S5_GROUP = 16
S5_GROUPS = 64
S5_STATE = 64
S5_LC = 16
S5_SLABS = 8

GLA_HEADS = 4
GLA_DK = 128
GLA_DV = 256
GLA_RANK = 16
GLA_TAU = 16.0

GDN_HEADS = 8
GDN_DK = 128
GDN_DV = 128

LRU_WIDTH = 1024
LRU_BLOCKS = 8
LRU_BW = 128
LRU_C = 8.0

MOE_GROUPS = 4
MOE_EPG = 8
MOE_FF = 256

LANE_BLOCKS = D_MODEL // 128
CHUNK = 64
ROW_ALIGN = 768
VMEM_LIMIT = 56 * 1024 * 1024

NT_DIMS = (((1,), (1,)), ((), ()))
TN_DIMS = (((0,), (0,)), ((), ()))


def _cparams(n_axes):
    return pltpu.CompilerParams(dimension_semantics=("arbitrary",) * n_axes,
                                vmem_limit_bytes=VMEM_LIMIT)


def _dot(a, b):
    return jnp.dot(a, b, preferred_element_type=F32)


def _dotg(a, b, dims):
    return lax.dot_general(a, b, dims, preferred_element_type=F32)


def _split(a):
    hi = a.astype(BF16)
    lo = (a - hi.astype(F32)).astype(BF16)
    return hi, lo


def _sigmoid(x):
    return 1.0 / (1.0 + jnp.exp(-x))


def _softplus(x):
    return jnp.maximum(x, 0.0) + jnp.log1p(jnp.exp(-jnp.abs(x)))


def _gelu(x):
    return 0.5 * x * (1.0 + jnp.tanh(math.sqrt(2.0 / math.pi) * (x + 0.044715 * (x * x * x))))


def _layer_norm(x, g, b):
    mu = jnp.mean(x, axis=-1, keepdims=True)
    xc = x - mu
    var = jnp.mean(xc * xc, axis=-1, keepdims=True)
    return xc * lax.rsqrt(var + LN_EPS) * g + b


PROJ_TILE = 512
PROJ_COLS = 1024


def _in_proj_kernel(x_ref, w_ref, wg_ref, o_ref, g_ref):
    xb = x_ref[...].astype(BF16)
    for j in range(w_ref.shape[1] // PROJ_COLS):
        cols = slice(j * PROJ_COLS, (j + 1) * PROJ_COLS)
        o_ref[:, cols] = _dot(xb, w_ref[:, cols]).astype(o_ref.dtype)
    g_ref[...] = _dot(xb, wg_ref[...])


def _in_proj(x, w, w_gates):
    m, k = x.shape
    n = w.shape[1]
    tm = PROJ_TILE
    const = lambda i: (0, 0)
    return pl.pallas_call(
        _in_proj_kernel,
        grid=(m // tm,),
        in_specs=[pl.BlockSpec((tm, k), lambda i: (i, 0)),
                  pl.BlockSpec((k, n), const, pipeline_mode=pl.Buffered(1)),
                  pl.BlockSpec((k, 128), const, pipeline_mode=pl.Buffered(1))],
        out_specs=[pl.BlockSpec((tm, n), lambda i: (i, 0)),
                   pl.BlockSpec((tm, 128), lambda i: (i, 0))],
        out_shape=[jax.ShapeDtypeStruct((m, n), BF16),
                   jax.ShapeDtypeStruct((m, 128), F32)],
        compiler_params=_cparams(1),
        name="in_proj",
    )(x, w, w_gates)


S5_TILE_CHUNKS = 32


def _s5_row_perm():
    m = S5_TILE_CHUNKS
    r = jnp.arange(S5_LC * m)
    nat = (r % m) * S5_LC + r // m
    return (nat[:, None] == jnp.arange(S5_LC * m)[None, :]).astype(BF16)


def _s5_uproj_kernel(h_ref, perm_ref, w_ref, u_ref):
    m = S5_TILE_CHUNKS
    lhs = _dot(perm_ref[...], h_ref[...].astype(BF16)).astype(BF16)
    z = _dot(lhs, w_ref[...])
    for t in range(S5_LC):
        for s in range(S5_SLABS):
            u_ref[s, :, t * 128:(t + 1) * 128] = z[t * m:(t + 1) * m, s * 128:(s + 1) * 128]


def _s5_uproj(h, w_u):
    t_rows = h.shape[0]
    n_chunks = t_rows // S5_LC
    m = S5_TILE_CHUNKS
    return pl.pallas_call(
        _s5_uproj_kernel,
        grid=(n_chunks // m,),
        in_specs=[pl.BlockSpec((m * S5_LC, D_MODEL), lambda i: (i, 0)),
                  pl.BlockSpec((m * S5_LC, m * S5_LC), lambda i: (0, 0)),
                  pl.BlockSpec((D_MODEL, S5_WIDTH), lambda i: (0, 0))],
        out_specs=pl.BlockSpec((S5_SLABS, m, S5_LC * 128), lambda i: (0, i, 0)),
        out_shape=jax.ShapeDtypeStruct((S5_SLABS, n_chunks, S5_LC * 128), F32),
        compiler_params=_cparams(1),
        name="s5_uproj",
    )(h, _s5_row_perm(), w_u)


def _s5_core_kernel(u_ref, stack_ref, ms_ref, mot_ref, ak_ref, pre_ref, pim_ref, d_ref, z_ref,
                    sre_ref, sim_ref, xre_ref, xim_ref):
    n = u_ref.shape[1]
    half = 8 * S5_STATE
    u = u_ref[0]
    ub = u.astype(BF16)
    s = _dot(ub, ms_ref[0])
    sre_ref[...] = s[:, :half]
    sim_ref[...] = s[:, half:]

    ak = ak_ref[0]
    pre = pre_ref[0]
    pim = pim_ref[0]
    row = lax.broadcasted_iota(jnp.int32, (8, half), 0)

    def body(rb, carry):
        cre, cim = carry
        r0 = pl.multiple_of(rb * 8, 8)
        xr = sre_ref[pl.ds(r0, 8), :]
        xi = sim_ref[pl.ds(r0, 8), :]
        for idx, k in enumerate((1, 2, 4)):
            are = ak[2 * idx:2 * idx + 1, :]
            aim = ak[2 * idx + 1:2 * idx + 2, :]
            keep = row >= k
            shr = jnp.where(keep, pltpu.roll(xr, k, 0), 0.0)
            shi = jnp.where(keep, pltpu.roll(xi, k, 0), 0.0)
            xr, xi = xr + are * shr - aim * shi, xi + are * shi + aim * shr
        xr = xr + pre * cre - pim * cim
        xi = xi + pre * cim + pim * cre
        first = row == 0
        xre_ref[pl.ds(r0, 8), :] = jnp.where(first, cre, pltpu.roll(xr, 1, 0))
        xim_ref[pl.ds(r0, 8), :] = jnp.where(first, cim, pltpu.roll(xi, 1, 0))
        return xr[7:8, :], xi[7:8, :]

    zero = jnp.zeros((1, half), F32)
    lax.fori_loop(0, n // 8, body, (zero, zero))

    xprev = jnp.concatenate([xre_ref[...], xim_ref[...]], axis=1).astype(BF16)
    y_inter = _dotg(xprev, mot_ref[0], NT_DIMS)
    d = d_ref[0]
    npair = S5_LC // 2
    for q in range(npair):
        yq = _dot(ub[:, :(q + 1) * 256], stack_ref[0, (npair - 1 - q) * 256:, :])
        sl = slice(q * 256, (q + 1) * 256)
        y = yq + y_inter[:, sl] + d[:, sl] * u[:, sl]
        z_ref[0, :, sl] = _gelu(y)


def _s5_core(u, prm, batch):
    n_tot = u.shape[1]
    n = n_tot // batch
    w = S5_LC * 128
    half = 8 * S5_STATE
    slab = lambda s, b: (s, 0, 0)
    return pl.pallas_call(
        _s5_core_kernel,
        grid=(S5_SLABS, batch),
        in_specs=[pl.BlockSpec((1, n, w), lambda s, b: (s, b, 0)),
                  pl.BlockSpec((1, w, 256), slab),
                  pl.BlockSpec((1, w, 2 * half), slab),
                  pl.BlockSpec((1, w, 2 * half), slab),
                  pl.BlockSpec((1, 8, half), slab),
                  pl.BlockSpec((1, 8, half), slab),
                  pl.BlockSpec((1, 8, half), slab),
                  pl.BlockSpec((1, 1, w), slab)],
        out_specs=pl.BlockSpec((1, n, w), lambda s, b: (s, b, 0)),
        out_shape=jax.ShapeDtypeStruct(u.shape, F32),
        scratch_shapes=[pltpu.VMEM((n, half), F32)] * 4,
        compiler_params=_cparams(2),
        name="s5_core",
    )(u, prm["stack"], prm["ms"], prm["mot"], prm["ak"], prm["pre"], prm["pim"], prm["d"])


def _s5_out_kernel(z_ref, perm_ref, wglu_ref, bglu_ref, wout_ref, p_ref, zf_ref):
    m = S5_TILE_CHUNKS
    for t in range(S5_LC):
        for s in range(S5_SLABS):
            zf_ref[t * m:(t + 1) * m, s * 128:(s + 1) * 128] = z_ref[s, :, t * 128:(t + 1) * 128]
    zf = zf_ref[...]
    gate = _sigmoid(_dot(zf.astype(BF16), wglu_ref[...]) + bglu_ref[...])
    gated = (zf * gate).astype(BF16)
    nat = _dotg(perm_ref[...], gated, TN_DIMS).astype(BF16)
    p_ref[...] = _dot(nat, wout_ref[...]).astype(p_ref.dtype)


def _s5_out(z, w_glu, b_glu, w_out_top):
    n_chunks = z.shape[1]
    m = S5_TILE_CHUNKS
    return pl.pallas_call(
        _s5_out_kernel,
        grid=(n_chunks // m,),
        in_specs=[pl.BlockSpec((S5_SLABS, m, S5_LC * 128), lambda i: (0, i, 0)),
                  pl.BlockSpec((m * S5_LC, m * S5_LC), lambda i: (0, 0)),
                  pl.BlockSpec((S5_WIDTH, S5_WIDTH), lambda i: (0, 0)),
                  pl.BlockSpec((1, S5_WIDTH), lambda i: (0, 0)),
                  pl.BlockSpec((S5_WIDTH, D_MODEL), lambda i: (0, 0))],
        out_specs=pl.BlockSpec((m * S5_LC, D_MODEL), lambda i: (i, 0)),
        out_shape=jax.ShapeDtypeStruct((n_chunks * S5_LC, D_MODEL), BF16),
        scratch_shapes=[pltpu.VMEM((S5_LC * m, S5_WIDTH), F32)],
        compiler_params=_cparams(1),
        name="s5_out",
    )(z, _s5_row_perm(), w_glu, b_glu, w_out_top)


def _s5_params(a_re, a_im, log_dt, b_re, b_im, c_re, c_im, d):
    g, p, c, lc = S5_GROUPS, S5_STATE, S5_GROUP, S5_LC
    a = lax.complex(a_re.astype(F32), a_im.astype(F32))
    adt = a * jnp.exp(log_dt.astype(F32))[:, None]
    zoh = (jnp.exp(adt) - 1.0) / a
    bz = lax.complex(b_re.astype(F32), b_im.astype(F32)) * zoh[:, :, None]
    cc = lax.complex(c_re.astype(F32), c_im.astype(F32))
    jj = jnp.arange(lc + 1, dtype=F32)
    apow = jnp.exp(adt[None] * jj[:, None, None])

    def expand(compact, width):
        rows = compact.shape[-2]
        row_grp = (jnp.arange(rows) % 128) // S5_GROUP
        lane_grp = jnp.arange(8 * width) // width
        tiled = jnp.tile(compact.astype(BF16), (1,) * (compact.ndim - 1) + (8,))
        return jnp.where(row_grp[:, None] == lane_grp[None, :], tiled, jnp.zeros((), BF16))

    kre = jnp.einsum("gop,jgp,gpc->jgco", cc, apow[:lc], bz, precision=HIGHEST).real
    kblk = expand(kre.reshape(lc, S5_SLABS, 128, c), c)
    kpad = jnp.concatenate([jnp.zeros_like(kblk[:1]), kblk], axis=0)
    npair = lc // 2
    dd = jnp.arange(npair)[:, None, None]
    lo_in = jnp.arange(2)[None, :, None]
    lo_out = jnp.arange(2)[None, None, :]
    lag = 2 * dd + lo_out - lo_in
    tiles = kpad[lag + 1]
    tiles = tiles.transpose(3, 0, 1, 4, 2, 5).reshape(S5_SLABS, npair, 256, 256)
    stack = tiles[:, ::-1].reshape(S5_SLABS, npair * 256, 256)

    wst = apow[lc - 1 - jnp.arange(lc)][:, :, :, None] * bz[None]
    wst = wst.reshape(lc, S5_SLABS, 8, p, c).transpose(1, 0, 2, 4, 3).reshape(S5_SLABS, lc * 128, p)
    ms = jnp.concatenate([expand(wst.real, p), expand(wst.imag, p)], axis=2)

    zc = cc[None] * apow[1:lc + 1][:, :, None, :]
    zc = zc.reshape(lc, S5_SLABS, 8 * c, p).transpose(1, 0, 2, 3).reshape(S5_SLABS, lc * 128, p)
    mot = jnp.concatenate([expand(zc.real, p), expand(-zc.imag, p)], axis=2)

    def slab_layout(x):
        return x.reshape(x.shape[:-2] + (S5_SLABS, 8 * p))

    kk = jnp.array([1.0, 2.0, 4.0], F32)
    a_k = slab_layout(jnp.exp(adt[None] * (lc * kk)[:, None, None]))
    ak = jnp.stack([a_k[0].real, a_k[0].imag, a_k[1].real, a_k[1].imag, a_k[2].real, a_k[2].imag,
                    jnp.zeros_like(a_k[0].real), jnp.zeros_like(a_k[0].real)], axis=1)
    rr = jnp.arange(1, 9, dtype=F32)
    a_r = slab_layout(jnp.exp(adt[None] * (lc * rr)[:, None, None]))
    pre = a_r.real.transpose(1, 0, 2)
    pim = a_r.imag.transpose(1, 0, 2)
    dsk = jnp.tile(d.astype(F32).reshape(S5_SLABS, 1, 128), (1, 1, lc))
    return {"stack": stack.astype(BF16), "ms": ms.astype(BF16), "mot": mot.astype(BF16),
            "ak": ak, "pre": pre, "pim": pim, "d": dsk}


GLA_TILE = 256


def _gla_kernel(q_ref, k_ref, v_ref, r_ref, gl_ref, wgh_ref, wgl_ref, bg_ref, ng_ref, o_ref, st_ref):
    @pl.when(pl.program_id(1) == 0)
    def _():
        st_ref[...] = jnp.zeros_like(st_ref)

    rowi = lax.broadcasted_iota(jnp.int32, (CHUNK, CHUNK), 0)
    coli = lax.broadcasted_iota(jnp.int32, (CHUNK, CHUNK), 1)
    tri = rowi >= coli
    wgh = wgh_ref[...]
    wgl = wgl_ref[...]
    bg = bg_ref[...]
    ng = ng_ref[...]
    scale = GLA_DK ** -0.5

    nch = GLA_TILE // CHUNK
    ti = lax.broadcasted_iota(jnp.int32, (GLA_TILE, GLA_TILE), 0)
    tj = lax.broadcasted_iota(jnp.int32, (GLA_TILE, GLA_TILE), 1)
    tri_t = jnp.where((ti >= tj) & ((ti // CHUNK) == (tj // CHUNK)), 1.0, 0.0).astype(BF16)
    glh, gll = _split(gl_ref[...])
    pre = _dot(glh, wgh) + _dot(gll, wgh) + _dot(glh, wgl) + bg
    logf = (jnp.minimum(pre, 0.0) - jnp.log1p(jnp.exp(-jnp.abs(pre)))) * (1.0 / GLA_TAU)
    lh, ll = _split(logf)
    bcum = _dot(tri_t, lh) + _dot(tri_t, ll)
    ebc = jnp.exp(bcum)
    kk = k_ref[...].astype(F32)
    q_in = (q_ref[...].astype(F32) * scale * ebc).astype(BF16)
    k_in = (kk * jnp.exp(-bcum)).astype(BF16)

    heads = range(GLA_HEADS)
    chunks = range(nch)
    items = [(c, h) for c in chunks for h in heads]
    rws = [slice(c * CHUNK, (c + 1) * CHUNK) for c in chunks]
    sks = [slice(h * GLA_DK, (h + 1) * GLA_DK) for h in heads]
    svs = [slice(h * GLA_DV, (h + 1) * GLA_DV) for h in heads]
    blast = [bcum[(c + 1) * CHUNK - 1:(c + 1) * CHUNK, :] for c in chunks]
    k_st = [(kk[rws[c], :] * jnp.exp(blast[c] - bcum[rws[c], :])).astype(BF16) for c in chunks]
    dec = [jnp.exp(blast[c]) for c in chunks]
    vh = {(c, h): v_ref[rws[c], svs[h]].astype(BF16) for c, h in items}
    att = {(c, h): _dotg(q_in[rws[c], sks[h]], k_in[rws[c], sks[h]], NT_DIMS) for c, h in items}
    kv = {(c, h): _dotg(vh[c, h], k_st[c][:, sks[h]], TN_DIMS) for c, h in items}
    o_intra = {it: _dot(jnp.where(tri, att[it], 0.0).astype(BF16), vh[it]) for it in items}
    st = [st_ref[h] for h in heads]
    for c in chunks:
        o_inter = [_dotg(q_in[rws[c], sks[h]], st[h].astype(BF16), NT_DIMS) for h in heads]
        for h in heads:
            o = o_intra[c, h] + o_inter[h]
            on = o * lax.rsqrt(jnp.mean(o * o, axis=1, keepdims=True) + NORM_EPS) * ng
            rr = r_ref[rws[c], svs[h]].astype(F32)
            o_ref[rws[c], svs[h]] = (on * (rr * _sigmoid(rr))).astype(o_ref.dtype)
        st = [st[h] * dec[c][:, sks[h]] + kv[c, h] for h in heads]
    for h in heads:
        st_ref[h] = st[h]


def _gla(proj, g_low, wgh, wgl, bg, ng, batch, lp):
    t_rows = proj.shape[0]
    nt = lp // GLA_TILE
    hk = GLA_HEADS * GLA_DK
    hv = GLA_HEADS * GLA_DV
    rowmap = lambda col: (lambda b, t: (b * nt + t, col))
    const = lambda b, t: (0, 0)
    return pl.pallas_call(
        _gla_kernel,
        grid=(batch, nt),
        in_specs=[pl.BlockSpec((GLA_TILE, hk), rowmap(0)),
                  pl.BlockSpec((GLA_TILE, hk), rowmap(1)),
                  pl.BlockSpec((GLA_TILE, hv), rowmap(1)),
                  pl.BlockSpec((GLA_TILE, hv), rowmap(2)),
                  pl.BlockSpec((GLA_TILE, 128), rowmap(0)),
                  pl.BlockSpec((128, hk), const),
                  pl.BlockSpec((128, hk), const),
                  pl.BlockSpec((1, hk), const),
                  pl.BlockSpec((1, GLA_DV), const)],
        out_specs=pl.BlockSpec((GLA_TILE, hv), rowmap(0)),
        out_shape=jax.ShapeDtypeStruct((t_rows, hv), BF16),
        scratch_shapes=[pltpu.VMEM((GLA_HEADS, GLA_DV, GLA_DK), F32)],
        compiler_params=_cparams(2),
        name="gla",
    )(proj, proj, proj, proj, g_low, wgh, wgl, bg, ng)


GDN_TILE = 256
GDN_GROUP = 4
GDN_QKV = 3 * GDN_HEADS * GDN_DK


def _gdn_kernel(qkv_ref, z_ref, ba_ref, cw_ref, hp_ref, ng_ref, o_ref,
                ext_ref, act_ref, beta_ref, g_ref, s_ref):
    tile = GDN_TILE

    @pl.when(pl.program_id(1) == 0)
    def _():
        ext_ref[0:8, :] = jnp.zeros((8, GDN_QKV), F32)
        s_ref[...] = jnp.zeros_like(s_ref)

    @pl.when(pl.program_id(1) > 0)
    def _():
        ext_ref[0:8, :] = ext_ref[tile:tile + 8, :]

    ext_ref[8:tile + 8, :] = qkv_ref[...].astype(F32)
    qscale = GDN_DK ** -0.5
    for cb in range(GDN_QKV // 128):
        cols = slice(cb * 128, (cb + 1) * 128)
        w = cw_ref[:, cols]
        y = (w[3:4] * ext_ref[8:tile + 8, cols] + w[2:3] * ext_ref[7:tile + 7, cols]
             + w[1:2] * ext_ref[6:tile + 6, cols] + w[0:1] * ext_ref[5:tile + 5, cols])
        a = y * _sigmoid(y)
        if cb < 2 * GDN_HEADS:
            a = a * lax.rsqrt(jnp.sum(a * a, axis=1, keepdims=True) + NORM_EPS)
            if cb < GDN_HEADS:
                a = a * qscale
        act_ref[:, cols] = a

    ba = ba_ref[...]
    beta_ref[...] = _sigmoid(ba)
    g_ref[...] = hp_ref[0:1, :] * _softplus(ba + hp_ref[1:2, :])

    rowi = lax.broadcasted_iota(jnp.int32, (CHUNK, CHUNK), 0)
    coli = lax.broadcasted_iota(jnp.int32, (CHUNK, CHUNK), 1)
    tri = rowi >= coli
    strict = rowi > coli
    tri_b = jnp.where(tri, 1.0, 0.0).astype(BF16)
    upper_b = jnp.where(rowi <= coli, 1.0, 0.0).astype(BF16)
    eye = jnp.where(rowi == coli, 1.0, 0.0)
    level_masks = []
    for lvl in range(6):
        bi = rowi >> lvl
        bj = coli >> lvl
        level_masks.append(((bi & 1) == 1) & (bj == bi - 1))
    ng = ng_ref[...]

    heads = range(GDN_HEADS)
    grp = range(GDN_GROUP)
    items = [(j, h) for j in grp for h in heads]
    hcol = lambda t, h: t[:, GDN_HEADS + h:GDN_HEADS + h + 1]

    def group(gi, carry):
        base = gi * (GDN_GROUP * CHUNK)
        rows = [pl.ds(pl.multiple_of(base + j * CHUNK, CHUNK), CHUNK) for j in grp]
        gsp = [_split(g_ref[rows[j], :]) for j in grp]
        gc = [_dot(tri_b, gsp[j][0]) + _dot(tri_b, gsp[j][1]) for j in grp]
        gct = [_dotg(gsp[j][0], upper_b, TN_DIMS) + _dotg(gsp[j][1], upper_b, TN_DIMS) for j in grp]
        eg = [jnp.exp(gc[j]) for j in grp]
        elast = [jnp.exp(gc[j][CHUNK - 1:CHUNK, :] - gc[j]) for j in grp]
        dlast = [jnp.exp(gc[j][CHUNK - 1:CHUNK, :]) for j in grp]
        beta = [beta_ref[rows[j], :] for j in grp]
        qs = {(j, h): act_ref[rows[j], h * GDN_DK:(h + 1) * GDN_DK] for j, h in items}
        ks = {(j, h): act_ref[rows[j], (GDN_HEADS + h) * GDN_DK:(GDN_HEADS + h + 1) * GDN_DK] for j, h in items}
        vs = {(j, h): act_ref[rows[j], (2 * GDN_HEADS + h) * GDN_DK:(2 * GDN_HEADS + h + 1) * GDN_DK]
              for j, h in items}
        bcols = {(j, h): beta[j][:, h:h + 1] for j, h in items}
        kbs = {it: ks[it] * bcols[it] for it in items}
        khb = {it: ks[it].astype(BF16) for it in items}
        gams = {(j, h): jnp.exp(jnp.where(tri, hcol(gc[j], h) - gct[j][GDN_HEADS + h:GDN_HEADS + h + 1, :], -1e30))
                for j, h in items}
        kk = {it: _dotg(kbs[it].astype(BF16), khb[it], NT_DIMS) for it in items}
        qk = {it: _dotg(qs[it].astype(BF16), khb[it], NT_DIMS) for it in items}
        mm = {it: jnp.where(strict, kk[it] * gams[it], 0.0) for it in items}
        tinv = {it: eye - jnp.where(level_masks[0], mm[it], 0.0) for it in items}
        for lvl in range(1, 6):
            tb = {it: tinv[it].astype(BF16) for it in items}
            xs = {it: _dot(jnp.where(level_masks[lvl], mm[it], 0.0).astype(BF16), tb[it]).astype(BF16)
                  for it in items}
            tinv = {it: tinv[it] - _dot(tb[it], xs[it]) for it in items}
        rhs = {(j, h): jnp.concatenate([vs[j, h] * bcols[j, h], kbs[j, h] * hcol(eg[j], h)], axis=1).astype(BF16)
               for j, h in items}
        sol = {it: _dot(tinv[it].astype(BF16), rhs[it]) for it in items}
        aqk = {it: jnp.where(tri, qk[it] * gams[it], 0.0).astype(BF16) for it in items}
        qdec = {(j, h): (qs[j, h] * hcol(eg[j], h)).astype(BF16) for j, h in items}
        kst = {(j, h): (ks[j, h] * hcol(elast[j], h)).astype(BF16) for j, h in items}
        for j in grp:
            s_old = [s_ref[h] for h in heads]
            sb = [s_old[h].astype(BF16) for h in heads]
            ws = [_dot(sol[j, h][:, GDN_DV:].astype(BF16), sb[h]) for h in heads]
            qsd = [_dot(qdec[j, h], sb[h]) for h in heads]
            vnb = [(sol[j, h][:, :GDN_DV] - ws[h]).astype(BF16) for h in heads]
            av = [_dot(aqk[j, h], vnb[h]) for h in heads]
            kv = [_dotg(kst[j, h], vnb[h], TN_DIMS) for h in heads]
            for h in heads:
                o = qsd[h] + av[h]
                on = o * lax.rsqrt(jnp.mean(o * o, axis=1, keepdims=True) + NORM_EPS) * ng
                zz = z_ref[rows[j], h * GDN_DV:(h + 1) * GDN_DV].astype(F32)
                o_ref[rows[j], h * GDN_DV:(h + 1) * GDN_DV] = (on * (zz * _sigmoid(zz))).astype(o_ref.dtype)
                s_ref[h] = s_old[h] * hcol(dlast[j], h) + kv[h]
        return carry

    lax.fori_loop(0, tile // (GDN_GROUP * CHUNK), group, 0)


def _gdn(proj, ba, cw, hp, ng, batch, lp):
    t_rows = proj.shape[0]
    nt = lp // GDN_TILE
    hv = GDN_HEADS * GDN_DV
    rowmap = lambda col: (lambda b, t: (b * nt + t, col))
    const = lambda b, t: (0, 0)
    return pl.pallas_call(
        _gdn_kernel,
        grid=(batch, nt),
        in_specs=[pl.BlockSpec((GDN_TILE, GDN_QKV), rowmap(0)),
                  pl.BlockSpec((GDN_TILE, hv), rowmap(3)),
                  pl.BlockSpec((GDN_TILE, 128), rowmap(0)),
                  pl.BlockSpec((8, GDN_QKV), const),
                  pl.BlockSpec((8, 128), const),
                  pl.BlockSpec((1, GDN_DV), const)],
        out_specs=pl.BlockSpec((GDN_TILE, hv), rowmap(0)),
        out_shape=jax.ShapeDtypeStruct((t_rows, hv), BF16),
        scratch_shapes=[pltpu.VMEM((GDN_TILE + 8, GDN_QKV), F32),
                        pltpu.VMEM((GDN_TILE, GDN_QKV), F32),
                        pltpu.VMEM((GDN_TILE, 128), F32),
                        pltpu.VMEM((GDN_TILE, 128), F32),
                        pltpu.VMEM((GDN_HEADS, GDN_DK, GDN_DV), F32)],
        compiler_params=_cparams(2),
        name="gdn",
    )(proj, proj, ba, cw, hp, ng)


LRU_TILE = 256


def _lru_kernel(x_ref, gate_ref, cw_ref, cb_ref, wax_ref, bax_ref, sp_ref, o_ref,
                ext_ref, a_ref, b_ref, h_ref):
    tile = LRU_TILE

    @pl.when(pl.program_id(1) == 0)
    def _():
        ext_ref[0:8, :] = jnp.zeros((8, LRU_WIDTH), F32)
        h_ref[...] = jnp.zeros_like(h_ref)

    @pl.when(pl.program_id(1) > 0)
    def _():
        ext_ref[0:8, :] = ext_ref[tile:tile + 8, :]

    ext_ref[8:tile + 8, :] = x_ref[...].astype(F32)
    for blk in range(LRU_BLOCKS):
        cols = slice(blk * LRU_BW, (blk + 1) * LRU_BW)
        w = cw_ref[:, cols]
        xc = (w[3:4] * ext_ref[8:tile + 8, cols] + w[2:3] * ext_ref[7:tile + 7, cols]
              + w[1:2] * ext_ref[6:tile + 6, cols] + w[0:1] * ext_ref[5:tile + 5, cols]
              + cb_ref[:, cols])
        pre = _dot(xc.astype(BF16), wax_ref[blk]) + bax_ref[blk]
        r = _sigmoid(pre[:, :LRU_BW])
        i = _sigmoid(pre[:, LRU_BW:])
        log_a = sp_ref[:, cols] * r
        a_ref[:, cols] = jnp.exp(log_a)
        th = jnp.tanh(log_a)
        b_ref[:, cols] = jnp.sqrt(-2.0 * th / (1.0 - th)) * (i * xc)

    row = lax.broadcasted_iota(jnp.int32, (8, LRU_WIDTH), 0)

    def body(rb, carry):
        r0 = pl.multiple_of(rb * 8, 8)
        aa = a_ref[pl.ds(r0, 8), :]
        bb = b_ref[pl.ds(r0, 8), :]
        for k in (1, 2, 4):
            keep = row >= k
            a_sh = jnp.where(keep, pltpu.roll(aa, k, 0), 1.0)
            b_sh = jnp.where(keep, pltpu.roll(bb, k, 0), 0.0)
            bb = aa * b_sh + bb
            aa = aa * a_sh
        hs = bb + aa * carry
        gt = gate_ref[pl.ds(r0, 8), :].astype(F32)
        o_ref[pl.ds(r0, 8), :] = (hs * _gelu(gt)).astype(o_ref.dtype)
        return hs[7:8, :]

    h_ref[...] = lax.fori_loop(0, tile // 8, body, h_ref[...])


def _lru(proj, cw, cb, wax, bax, sp, batch, lp):
    t_rows = proj.shape[0]
    nt = lp // LRU_TILE
    rowmap = lambda col: (lambda b, t: (b * nt + t, col))
    const = lambda b, t: (0, 0)
    return pl.pallas_call(
        _lru_kernel,
        grid=(batch, nt),
        in_specs=[pl.BlockSpec((LRU_TILE, LRU_WIDTH), rowmap(4)),
                  pl.BlockSpec((LRU_TILE, LRU_WIDTH), rowmap(5)),
                  pl.BlockSpec((8, LRU_WIDTH), const),
                  pl.BlockSpec((1, LRU_WIDTH), const),
                  pl.BlockSpec((LRU_BLOCKS, LRU_BW, 2 * LRU_BW), lambda b, t: (0, 0, 0)),
                  pl.BlockSpec((LRU_BLOCKS, 1, 2 * LRU_BW), lambda b, t: (0, 0, 0)),
                  pl.BlockSpec((1, LRU_WIDTH), const)],
        out_specs=pl.BlockSpec((LRU_TILE, LRU_WIDTH), rowmap(0)),
        out_shape=jax.ShapeDtypeStruct((t_rows, LRU_WIDTH), BF16),
        scratch_shapes=[pltpu.VMEM((LRU_TILE + 8, LRU_WIDTH), F32),
                        pltpu.VMEM((LRU_TILE, LRU_WIDTH), F32),
                        pltpu.VMEM((LRU_TILE, LRU_WIDTH), F32),
                        pltpu.VMEM((1, LRU_WIDTH), F32)],
        compiler_params=_cparams(2),
        name="lru",
    )(proj, proj, cw, cb, wax, bax, sp)


OUT_TILE = 512


def _out_ln_kernel(n_add, h_ref, *refs):
    adds = refs[:n_add]
    (a1_ref, w1_ref, a2_ref, w2_ref, g_ref, b_ref, wrh_ref, wrl_ref, br_ref,
     o_ref, r_ref, cnt_ref, run_ref) = refs[n_add:]

    @pl.when(pl.program_id(0) == 0)
    def _():
        run_ref[...] = jnp.zeros_like(run_ref)

    acc = DN_ALPHA * h_ref[...]
    for r in adds:
        acc = acc + r[...]
    acc = acc + _dot(a1_ref[...], w1_ref[...])
    if a2_ref is not None:
        acc = acc + _dot(a2_ref[...], w2_ref[...])
    hn = _layer_norm(acc, g_ref[...], b_ref[...])
    o_ref[...] = hn
    tm = hn.shape[0]
    r = _route(hn, wrh_ref[...], wrl_ref[...], br_ref[...])
    lane = lax.broadcasted_iota(jnp.int32, r.shape, 1).astype(F32)
    oh1 = lane == r[:, 0:1]
    oh2 = lane == r[:, 1:2]
    picked = jnp.where(oh1 | oh2, 1.0, 0.0)
    ri = lax.broadcasted_iota(jnp.int32, (tm, tm), 0)
    ci = lax.broadcasted_iota(jnp.int32, (tm, tm), 1)
    before = _dot(jnp.where(ri > ci, 1.0, 0.0).astype(BF16), picked.astype(BF16)) + run_ref[0:1, :]
    rank1 = jnp.sum(jnp.where(oh1, before, 0.0), axis=1, keepdims=True)
    rank2 = jnp.sum(jnp.where(oh2, before, 0.0), axis=1, keepdims=True)
    r_ref[...] = r + jnp.where(lane == 4.0, rank1, 0.0) + jnp.where(lane == 5.0, rank2, 0.0)
    total = run_ref[0:1, :] + jnp.sum(picked, axis=0, keepdims=True)
    run_ref[...] = jnp.broadcast_to(total, run_ref.shape)
    cnt_ref[...] = jnp.broadcast_to(total, cnt_ref.shape)


def _out_ln_kernel_1(h_ref, p_ref, a1_ref, w1_ref, *rest):
    _out_ln_kernel(1, h_ref, p_ref, a1_ref, w1_ref, None, None, *rest)


def _out_ln_kernel_2(h_ref, a1_ref, w1_ref, a2_ref, w2_ref, *rest):
    _out_ln_kernel(0, h_ref, a1_ref, w1_ref, a2_ref, w2_ref, *rest)


def _out_ln(h, addend, pairs, g, b, router):
    t_rows = h.shape[0]
    tm = OUT_TILE
    row = lambda i: (i, 0)
    const = lambda i: (0, 0)
    ins = [h]
    specs = [pl.BlockSpec((tm, D_MODEL), row)]
    if addend is not None:
        ins.append(addend)
        specs.append(pl.BlockSpec((tm, D_MODEL), row))
    for a, w in pairs:
        ins += [a, w]
        specs += [pl.BlockSpec((tm, a.shape[1]), row), pl.BlockSpec(w.shape, const)]
    ins += [g, b]
    specs += [pl.BlockSpec((1, D_MODEL), const)] * 2
    ins += list(router)
    specs += [pl.BlockSpec((D_MODEL, 256), const), pl.BlockSpec((D_MODEL, 128), const),
              pl.BlockSpec((1, 128), const)]
    body = _out_ln_kernel_1 if addend is not None else _out_ln_kernel_2
    return pl.pallas_call(
        body,
        grid=(t_rows // tm,),
        in_specs=specs,
        out_specs=[pl.BlockSpec((tm, D_MODEL), row), pl.BlockSpec((tm, 128), row),
                   pl.BlockSpec((8, 128), const)],
        out_shape=[jax.ShapeDtypeStruct((t_rows, D_MODEL), F32),
                   jax.ShapeDtypeStruct((t_rows, 128), F32),
                   jax.ShapeDtypeStruct((8, 128), F32)],
        scratch_shapes=[pltpu.VMEM((8, 128), F32)],
        compiler_params=_cparams(1),
        name="out_ln",
    )(*ins)


MOE_TILE = 256
BIG = 1e30


def _route(h, whl, wh, b):
    hh, hl = _split(h)
    both = _dot(hh, whl)
    logits = both[:, :128] + both[:, 128:] + _dot(hl, wh) + b
    lane = lax.broadcasted_iota(jnp.int32, logits.shape, 1).astype(F32)

    def first_argmax(vals, vmax):
        return jnp.min(jnp.where(vals == vmax, lane, 4096.0), axis=1, keepdims=True)

    is_g = lane < MOE_GROUPS
    lg = jnp.where(is_g, logits, -BIG)
    m = jnp.max(lg, axis=1, keepdims=True)
    gidx = first_argmax(lg, m)
    denom = jnp.sum(jnp.where(is_g, jnp.exp(logits - m), 0.0), axis=1, keepdims=True)
    p_top = 1.0 / denom
    base = MOE_GROUPS + MOE_EPG * gidx
    in_grp = (lane >= base) & (lane < base + MOE_EPG)
    le = jnp.where(in_grp, logits, -BIG)
    v1 = jnp.max(le, axis=1, keepdims=True)
    i1 = first_argmax(le, v1)
    le2 = jnp.where(lane == i1, -BIG, le)
    v2 = jnp.max(le2, axis=1, keepdims=True)
    i2 = first_argmax(le2, v2)
    e21 = jnp.exp(v2 - v1)
    w1 = p_top / (1.0 + e21)
    w2 = p_top * e21 / (1.0 + e21)
    return (jnp.where(lane == 0.0, i1 - MOE_GROUPS, 0.0) + jnp.where(lane == 1.0, i2 - MOE_GROUPS, 0.0)
            + jnp.where(lane == 2.0, w1, 0.0) + jnp.where(lane == 3.0, w2, 0.0))


def _router_weights(w_rg, b_rg, w_re, b_re):
    n_exp = MOE_GROUPS * MOE_EPG
    wr = jnp.concatenate([w_rg.astype(F32), w_re.astype(F32).reshape(D_MODEL, n_exp)], axis=1)
    wr = jnp.pad(wr, ((0, 0), (0, 128 - wr.shape[1])))
    wrh, wrl = _split(wr)
    br = jnp.concatenate([b_rg.astype(F32), b_re.astype(F32).reshape(-1)])
    br = jnp.pad(br, (0, 128 - br.shape[0])).reshape(1, 128)
    return jnp.concatenate([wrh, wrl], axis=1), wrh, br


DISPATCH_TILE = 256


def _dispatch_kernel(pos_ref, fill_ref, x_ref, xs_hbm, zbuf, sbuf, sem, rsem):
    tm = DISPATCH_TILE
    lb = LANE_BLOCKS
    i = pl.program_id(0)

    @pl.when(i == 0)
    def _():
        zbuf[...] = jnp.zeros_like(zbuf)

        def fill(e):
            dst = pl.multiple_of(jnp.maximum(fill_ref[e], 0) * lb, lb)
            return pltpu.make_async_copy(zbuf, xs_hbm.at[pl.ds(dst, MOE_TILE * lb)], sem)

        def start(e, c):
            @pl.when(fill_ref[e] >= 0)
            def _():
                fill(e).start()
            return c

        def wait(e, c):
            @pl.when(fill_ref[e] >= 0)
            def _():
                fill(e).wait()
            return c

        lax.fori_loop(0, fill_ref.shape[0], start, 0)
        lax.fori_loop(0, fill_ref.shape[0], wait, 0)

    slot = i % 2
    for c in range(lb):
        sbuf[slot, pl.ds(c, tm, stride=lb), :] = x_ref[:, c * 128:(c + 1) * 128]
    base = i * (2 * tm)
    for r in range(tm):
        for k in range(2):
            dst = pl.multiple_of(pos_ref[base + 2 * r + k] * lb, lb)
            pltpu.make_async_copy(sbuf.at[slot, pl.ds(r * lb, lb)], xs_hbm.at[pl.ds(dst, lb)],
                                  rsem.at[slot]).start(priority=k)

    def wait_rows(s):
        for _ in range(2):
            pltpu.make_async_copy(sbuf.at[s], xs_hbm.at[pl.ds(0, tm * lb)], rsem.at[s]).wait()

    @pl.when(i > 0)
    def _():
        wait_rows(1 - slot)

    @pl.when(i == pl.num_programs(0) - 1)
    def _():
        wait_rows(slot)


def _dispatch(pos, fill, h, n_sorted):
    t_rows = h.shape[0]
    tm = DISPATCH_TILE
    return pl.pallas_call(
        _dispatch_kernel,
        grid_spec=pltpu.PrefetchScalarGridSpec(
            num_scalar_prefetch=2,
            grid=(t_rows // tm,),
            in_specs=[pl.BlockSpec((tm, D_MODEL), lambda i, pos, fill: (i, 0))],
            out_specs=pl.BlockSpec(memory_space=pl.ANY),
            scratch_shapes=[pltpu.VMEM((MOE_TILE * LANE_BLOCKS, 128), F32),
                            pltpu.VMEM((2, tm * LANE_BLOCKS, 128), F32),
                            pltpu.SemaphoreType.DMA(()),
                            pltpu.SemaphoreType.DMA((2,))]),
        out_shape=jax.ShapeDtypeStruct((n_sorted * LANE_BLOCKS, 128), F32),
        compiler_params=_cparams(1),
        name="moe_dispatch",
    )(pos, fill, h)


def _moe_kernel(nv_ref, dst_ref, te_ref, x_ref, wg_ref, wu_ref, wd_ref, y_hbm,
                ybuf, xmat, wgb, wub, wdb, ssem):
    tm = MOE_TILE
    lb = LANE_BLOCKS
    i = pl.program_id(0)
    nv = nv_ref[0]
    slot = i % 2
    other = 1 - slot

    def scatter_copy(tile, row, buf_slot):
        dst = y_hbm.at[pl.ds(pl.multiple_of(dst_ref[tile * tm + row] * lb, lb), lb)]
        src_rows = pl.ds(row * lb if isinstance(row, int) else pl.multiple_of(row * lb, lb), lb)
        return pltpu.make_async_copy(ybuf.at[buf_slot, src_rows], dst, ssem.at[buf_slot])

    def wait_scatter(buf_slot):
        pltpu.make_async_copy(ybuf.at[buf_slot], y_hbm.at[pl.ds(0, tm * lb)], ssem.at[buf_slot]).wait()

    @pl.when((i == 0) | (te_ref[i] != te_ref[jnp.maximum(i - 1, 0)]))
    def _():
        wgb[...] = wg_ref[0].astype(BF16)
        wub[...] = wu_ref[0].astype(BF16)
        wdb[...] = wd_ref[0].astype(BF16)

    def compute(buf_slot):
        for c in range(lb):
            xmat[:, c * 128:(c + 1) * 128] = x_ref[pl.ds(c, tm, stride=lb), :]
        xb = xmat[...].astype(BF16)
        hg = _dot(xb, wgb[...])
        hu = _dot(xb, wub[...])
        hh = (hg * _sigmoid(hg)) * hu
        y = _dot(hh.astype(BF16), wdb[...])
        for c in range(lb):
            ybuf[buf_slot, pl.ds(c, tm, stride=lb), :] = y[:, c * 128:(c + 1) * 128]

    @pl.when(i == 0)
    def _():
        ybuf[1] = jnp.zeros((tm * lb, 128), F32)
        fill = pltpu.make_async_copy(ybuf.at[1], y_hbm.at[pl.ds(y_hbm.shape[0] - tm * lb, tm * lb)], ssem.at[1])
        fill.start()
        fill.wait()
        compute(0)

    @pl.when((i > 0) & (i < nv))
    def _():
        @pl.when(i >= 2)
        def _():
            wait_scatter(slot)

        for r in range(tm):
            scatter_copy(i - 1, r, other).start(priority=r % 2)
        compute(slot)

    @pl.when(i == nv)
    def _():
        @pl.when(i >= 2)
        def _():
            wait_scatter(slot)

        def body(r, c):
            scatter_copy(i - 1, r, other).start()
            return c
        lax.fori_loop(0, tm, body, 0)
        wait_scatter(other)


def _moe(nv, dst, te, xs, wg, wu, wd, n_out_rows):
    n_tiles = dst.shape[0] // MOE_TILE
    tm = MOE_TILE
    ex = lambda i, nv, dst, te: (te[i], 0, 0)
    return pl.pallas_call(
        _moe_kernel,
        grid_spec=pltpu.PrefetchScalarGridSpec(
            num_scalar_prefetch=3,
            grid=(n_tiles,),
            in_specs=[pl.BlockSpec((tm * LANE_BLOCKS, 128),
                                   lambda i, nv, dst, te: (jnp.minimum(i, nv[0] - 1), 0)),
                      pl.BlockSpec((1, D_MODEL, MOE_FF), ex),
                      pl.BlockSpec((1, D_MODEL, MOE_FF), ex),
                      pl.BlockSpec((1, MOE_FF, D_MODEL), ex)],
            out_specs=pl.BlockSpec(memory_space=pl.ANY),
            scratch_shapes=[pltpu.VMEM((2, tm * LANE_BLOCKS, 128), F32),
                            pltpu.VMEM((tm, D_MODEL), F32),
                            pltpu.VMEM((D_MODEL, MOE_FF), BF16),
                            pltpu.VMEM((D_MODEL, MOE_FF), BF16),
                            pltpu.VMEM((MOE_FF, D_MODEL), BF16),
                            pltpu.SemaphoreType.DMA((2,))]),
        out_shape=jax.ShapeDtypeStruct((n_out_rows * LANE_BLOCKS, 128), F32),
        compiler_params=_cparams(1),
        name="moe_experts",
    )(nv, dst, te, xs, wg, wu, wd)


MOE_LN_TILE = 512


def _moe_ln_kernel(h_ref, y0_ref, y1_ref, r_ref, g_ref, b_ref, o_ref, acc_ref):
    r = r_ref[...]
    w0 = r[:, 2:3]
    w1 = r[:, 3:4]
    tm = h_ref.shape[0]
    for c in range(LANE_BLOCKS):
        cols = slice(c * 128, (c + 1) * 128)
        rows = pl.ds(c, tm, stride=LANE_BLOCKS)
        acc_ref[:, cols] = DN_ALPHA * h_ref[:, cols] + (w0 * y0_ref[rows, :] + w1 * y1_ref[rows, :])
    o_ref[...] = _layer_norm(acc_ref[...], g_ref[...], b_ref[...])


def _moe_ln_final(h, y2, rinfo, g, b, batch, lp, seq):
    t_rows = h.shape[0]
    tm = MOE_LN_TILE
    lb = LANE_BLOCKS
    nj = seq // tm
    first = lambda bb, j: pl.multiple_of(bb * lp + N_META + j * tm, 8)
    const = lambda bb, j: (0, 0)
    return pl.pallas_call(
        _moe_ln_kernel,
        grid=(batch, nj),
        in_specs=[pl.BlockSpec((pl.Element(tm), pl.Element(D_MODEL)), lambda bb, j: (first(bb, j), 0)),
                  pl.BlockSpec((pl.Element(tm * lb), pl.Element(128)), lambda bb, j: (first(bb, j) * lb, 0)),
                  pl.BlockSpec((pl.Element(tm * lb), pl.Element(128)),
                               lambda bb, j: ((t_rows + first(bb, j)) * lb, 0)),
                  pl.BlockSpec((pl.Element(tm), pl.Element(128)), lambda bb, j: (first(bb, j), 0)),
                  pl.BlockSpec((1, D_MODEL), const),
                  pl.BlockSpec((1, D_MODEL), const)],
        out_specs=pl.BlockSpec((tm, D_MODEL), lambda bb, j: (bb * nj + j, 0)),
        out_shape=jax.ShapeDtypeStruct((batch * seq, D_MODEL), F32),
        scratch_shapes=[pltpu.VMEM((tm, D_MODEL), F32)],
        compiler_params=_cparams(2),
        name="moe_ln_final",
    )(h, y2, y2, rinfo, g, b)


def _moe_ln(h, y2, rinfo, g, b):
    t_rows = h.shape[0]
    tm = MOE_LN_TILE
    nb = t_rows // tm
    const = lambda i: (0, 0)
    return pl.pallas_call(
        _moe_ln_kernel,
        grid=(nb,),
        in_specs=[pl.BlockSpec((tm, D_MODEL), lambda i: (i, 0)),
                  pl.BlockSpec((tm * LANE_BLOCKS, 128), lambda i: (i, 0)),
                  pl.BlockSpec((tm * LANE_BLOCKS, 128), lambda i: (nb + i, 0)),
                  pl.BlockSpec((tm, 128), lambda i: (i, 0)),
                  pl.BlockSpec((1, D_MODEL), const),
                  pl.BlockSpec((1, D_MODEL), const)],
        out_specs=pl.BlockSpec((tm, D_MODEL), lambda i: (i, 0)),
        out_shape=jax.ShapeDtypeStruct((t_rows, D_MODEL), F32),
        scratch_shapes=[pltpu.VMEM((tm, D_MODEL), F32)],
        compiler_params=_cparams(1),
        name="moe_ln",
    )(h, y2, y2, rinfo, g, b)


def _hier_moe_ln(h, rinfo, counts, layer, w_gate, w_up, w_down, ln_g, ln_b, final=None):
    t_rows = h.shape[0]
    n_exp = MOE_GROUPS * MOE_EPG

    n_asg = 2 * t_rows
    tm = MOE_TILE
    e_a = rinfo[:, 0:2].astype(jnp.int32).reshape(n_asg)
    rank_a = rinfo[:, 4:6].astype(jnp.int32).reshape(n_asg)
    asg = jnp.arange(n_asg, dtype=jnp.int32)
    total = counts[0, :n_exp].astype(jnp.int32)
    tiles = (total + tm - 1) // tm
    tile_end = jnp.cumsum(tiles)
    tile_start = tile_end - tiles
    pos = (jnp.take(tile_start * tm, e_a) + rank_a).astype(jnp.int32)
    n_tiles = n_asg // tm + n_exp + 1
    n_sorted = n_tiles * tm
    nv = tile_end[-1:].astype(jnp.int32)
    slot_asg = jnp.full((n_sorted,), -1, jnp.int32).at[pos].set(asg)
    is_pad = slot_asg < 0
    dump = n_asg + jnp.arange(n_sorted, dtype=jnp.int32) % tm
    dst = jnp.where(is_pad, dump, (slot_asg & 1) * t_rows + (slot_asg >> 1))
    te = jnp.minimum(jnp.sum(jnp.arange(n_tiles)[:, None] >= tile_end[None, :], axis=1), n_exp - 1)
    te = (te + layer * n_exp).astype(jnp.int32)
    spare = tile_end[-1] + jnp.arange(n_tiles - n_asg // tm)
    fill = jnp.concatenate([jnp.where(tiles > 0, (tile_end - 1) * tm, -1),
                            jnp.where(spare < n_tiles, spare * tm, -1)]).astype(jnp.int32)

    xs = _dispatch(pos, fill, h, n_sorted)
    y2 = _moe(nv, dst, te, xs,
              w_gate.astype(F32).reshape(-1, D_MODEL, MOE_FF),
              w_up.astype(F32).reshape(-1, D_MODEL, MOE_FF),
              w_down.astype(F32).reshape(-1, MOE_FF, D_MODEL), n_asg + tm)
    g2 = ln_g.reshape(1, D_MODEL).astype(F32)
    b2 = ln_b.reshape(1, D_MODEL).astype(F32)
    if final is not None:
        return _moe_ln_final(h, y2, rinfo, g2, b2, *final)
    return _moe_ln(h, y2, rinfo, g2, b2)


def _layer_ab(h, batch, lp, w_in, s5, w_glu, b_glu, gla_w_gate, gla_b_gate, gla_norm, w_out, ln_g, ln_b,
              router):
    hk = GLA_HEADS * GLA_DK
    hv = GLA_HEADS * GLA_DV
    o_q = S5_WIDTH
    o_g = o_q + 2 * hk + hv
    o_r = o_g + GLA_RANK
    w_in = w_in.astype(F32)
    w_u = w_in[:, :S5_WIDTH].astype(BF16)
    w_main = jnp.concatenate([w_in[:, o_q:o_g], w_in[:, o_r:o_r + hv]], axis=1).astype(BF16)
    w_low = jnp.pad(w_in[:, o_g:o_r], ((0, 0), (0, 128 - GLA_RANK))).astype(BF16)
    u = _s5_uproj(h, w_u)
    z = _s5_core(u, s5, batch)
    p_s5 = _s5_out(z, w_glu.astype(BF16), b_glu.reshape(1, -1).astype(F32), w_out[:S5_WIDTH].astype(BF16))
    proj, g_low = _in_proj(h, w_main, w_low)
    wg = jnp.pad(gla_w_gate.astype(F32), ((0, 128 - GLA_RANK), (0, 0)))
    wgh, wgl = _split(wg)
    y_gla = _gla(proj, g_low, wgh, wgl, gla_b_gate.reshape(1, -1).astype(F32),
                 gla_norm.reshape(1, -1).astype(F32), batch, lp)
    return _out_ln(h, p_s5, [(y_gla, w_out[S5_WIDTH:].astype(BF16))],
                   ln_g.reshape(1, -1).astype(F32), ln_b.reshape(1, -1).astype(F32), router)


def _layer_cd(h, batch, lp, w_in, conv_w, a_log, dt_bias, gdn_norm, lru_conv_w, lru_conv_b,
              lru_w_a, lru_b_a, lru_w_x, lru_b_x, lru_lambda, w_out, ln_g, ln_b, router):
    nq = GDN_QKV
    hv = GDN_HEADS * GDN_DV
    w_in = w_in.astype(F32)
    o_z = nq
    o_b = o_z + hv
    o_a = o_b + GDN_HEADS
    o_x = o_a + GDN_HEADS
    o_gt = o_x + LRU_WIDTH
    w_main = jnp.concatenate([w_in[:, :o_b], w_in[:, o_x:o_gt + LRU_WIDTH]], axis=1).astype(BF16)
    w_ba = jnp.pad(w_in[:, o_b:o_x], ((0, 0), (0, 128 - 2 * GDN_HEADS))).astype(BF16)
    proj, ba = _in_proj(h, w_main, w_ba)

    cw = jnp.pad(conv_w.astype(F32), ((0, 8 - CONV_K), (0, 0)))
    hp = jnp.zeros((8, 128), F32)
    hp = hp.at[0, GDN_HEADS:2 * GDN_HEADS].set(-jnp.exp(a_log.astype(F32)))
    hp = hp.at[1, GDN_HEADS:2 * GDN_HEADS].set(dt_bias.astype(F32))
    y_gdn = _gdn(proj, ba, cw, hp, gdn_norm.reshape(1, -1).astype(F32), batch, lp)

    lcw = jnp.pad(lru_conv_w.astype(F32), ((0, 8 - CONV_K), (0, 0)))
    wax = jnp.concatenate([lru_w_a.astype(F32), lru_w_x.astype(F32)], axis=2).astype(BF16)
    bax = jnp.concatenate([lru_b_a.astype(F32).reshape(LRU_BLOCKS, 1, LRU_BW),
                           lru_b_x.astype(F32).reshape(LRU_BLOCKS, 1, LRU_BW)], axis=2)
    sp = (-LRU_C * jax.nn.softplus(-lru_lambda.astype(F32))).reshape(1, LRU_WIDTH)
    y_lru = _lru(proj, lcw, lru_conv_b.reshape(1, -1).astype(F32), wax, bax, sp, batch, lp)

    return _out_ln(h, None, [(y_gdn, w_out[:hv].astype(BF16)), (y_lru, w_out[hv:].astype(BF16))],
                   ln_g.reshape(1, -1).astype(F32), ln_b.reshape(1, -1).astype(F32), router)


def kernel(x, meta_tokens, ab_w_in, ab_s5_a_re, ab_s5_a_im, ab_s5_log_dt, ab_s5_b_re, ab_s5_b_im,
           ab_s5_c_re, ab_s5_c_im, ab_s5_d, ab_s5_w_glu, ab_s5_b_glu, ab_gla_w_gate, ab_gla_b_gate,
           ab_gla_norm, ab_w_out, cd_w_in, cd_conv_w, cd_gdn_a_log, cd_gdn_dt_bias, cd_gdn_norm,
           cd_lru_conv_w, cd_lru_conv_b, cd_lru_w_a, cd_lru_b_a, cd_lru_w_x, cd_lru_b_x, cd_lru_lambda,
           cd_w_out, moe_w_router_g, moe_b_router_g, moe_w_router_e, moe_b_router_e, moe_w_gate,
           moe_w_up, moe_w_down, ln_mix_g, ln_mix_b, ln_ffn_g, ln_ffn_b):
    batch, seq, _ = x.shape
    length = seq + N_META
    lp = -(-length // ROW_ALIGN) * ROW_ALIGN
    meta = jnp.broadcast_to(meta_tokens.astype(F32)[None], (batch, N_META, D_MODEL))
    h = jnp.zeros((batch, lp, D_MODEL), F32)
    h = lax.dynamic_update_slice(h, meta, (0, 0, 0))
    h = lax.dynamic_update_slice(h, x.astype(F32), (0, N_META, 0))
    h = h.reshape(batch * lp, D_MODEL)

    for layer in range(DEPTH):
        j = layer // 2
        router = _router_weights(moe_w_router_g[layer], moe_b_router_g[layer], moe_w_router_e[layer],
                                 moe_b_router_e[layer])
        if layer % 2 == 0:
            s5 = _s5_params(ab_s5_a_re[j], ab_s5_a_im[j], ab_s5_log_dt[j], ab_s5_b_re[j], ab_s5_b_im[j],
                            ab_s5_c_re[j], ab_s5_c_im[j], ab_s5_d[j])
            h, rinfo, counts = _layer_ab(h, batch, lp, ab_w_in[j], s5, ab_s5_w_glu[j], ab_s5_b_glu[j],
                                 ab_gla_w_gate[j], ab_gla_b_gate[j], ab_gla_norm[j], ab_w_out[j],
                                 ln_mix_g[layer], ln_mix_b[layer], router)
        else:
            h, rinfo, counts = _layer_cd(h, batch, lp, cd_w_in[j], cd_conv_w[j], cd_gdn_a_log[j], cd_gdn_dt_bias[j],
                                 cd_gdn_norm[j], cd_lru_conv_w[j], cd_lru_conv_b[j], cd_lru_w_a[j],
                                 cd_lru_b_a[j], cd_lru_w_x[j], cd_lru_b_x[j], cd_lru_lambda[j], cd_w_out[j],
                                 ln_mix_g[layer], ln_mix_b[layer], router)
        last = layer == DEPTH - 1 and seq % MOE_LN_TILE == 0
        h = _hier_moe_ln(h, rinfo, counts, layer, moe_w_gate, moe_w_up, moe_w_down,
                         ln_ffn_g[layer], ln_ffn_b[layer], final=(batch, lp, seq) if last else None)
        if last:
            return h.reshape(batch, seq, D_MODEL).astype(x.dtype)
    return h.reshape(batch, lp, D_MODEL)[:, N_META:length].astype(x.dtype)
```
